```python
import jax, jax.numpy as jnp
from jax import lax
import numpy as np

D_MODEL = 2048
BATCH = 8
SEQ = 2048
DEPTH = 1

M_HEADS = 4
M_HEAD_DIM = 256
M_WIDTH = M_HEADS * M_HEAD_DIM
M_CHUNK = 64
CONV_WIDTH = 4
A_HEADS = 16
A_KV_HEADS = 4
A_GROUP = A_HEADS // A_KV_HEADS
A_HEAD_DIM = 64
A_WIDTH = A_HEADS * A_HEAD_DIM
A_KV_WIDTH = A_KV_HEADS * A_HEAD_DIM
WINDOW = 128
ROPE_THETA = 10000.0
N_BRANCHES = 2
D_IN = 4 * M_WIDTH + 2 * M_HEADS + A_WIDTH + 2 * A_KV_WIDTH + N_BRANCHES * D_MODEL
N_GROUPS = 8
EXPERTS_PER_GROUP = 8
N_EXPERTS = N_GROUPS * EXPERTS_PER_GROUP
TOP_K = 2
D_EXPERT = 768
MOE_BLOCK = 128
EPS = 1e-6

kernel_name = "hybrid_mlstm_swa_hmoe_layer"


def rms_norm(x, g):
    xf = x.astype(jnp.float32)
    y = xf * lax.rsqrt(jnp.mean(xf * xf, axis=-1, keepdims=True) + EPS)
    return (y * g.astype(jnp.float32)).astype(x.dtype)


def rope(x, pos):
    half = x.shape[-1] // 2
    freqs = ROPE_THETA ** (-jnp.arange(half, dtype=jnp.float32) / half)
    ang = pos.astype(jnp.float32)[:, None] * freqs[None, :]
    cos = jnp.cos(ang)[None, :, None, :]
    sin = jnp.sin(ang)[None, :, None, :]
    xf = x.astype(jnp.float32)
    x1, x2 = xf[..., :half], xf[..., half:]
    return jnp.concatenate([x1 * cos - x2 * sin, x2 * cos + x1 * sin], axis=-1).astype(x.dtype)


def causal_dwconv(x, w):
    K = w.shape[0]
    S = x.shape[1]
    xp = jnp.pad(x, ((0, 0), (K - 1, 0), (0, 0)))
    y = xp[:, 0:S, :] * w[0]
    for j in range(1, K):
        y = y + xp[:, j:j + S, :] * w[j]
    return y


def mlstm_chunkwise(q, k, v, li, lf):
    B, H, S, dk = q.shape
    dv = v.shape[-1]
    L = M_CHUNK
    NC = S // L

    def to_chunks(t):
        t = t.reshape(B, H, NC, L, *t.shape[3:])
        return jnp.moveaxis(t, 2, 0)

    causal = jnp.tril(jnp.ones((L, L), dtype=bool))

    def step(carry, xs):
        C, n, m = carry
        qc, kc, vc, lic, lfc = xs
        b = jnp.cumsum(lfc, axis=-1)
        a = b + m[..., None]
        D = jnp.where(causal, b[..., :, None] - b[..., None, :] + lic[..., None, :], -jnp.inf)
        m_t = jnp.maximum(a, jnp.max(D, axis=-1))
        w_inter = jnp.exp(a - m_t)
        P = jnp.einsum('bhtd,bhsd->bhts', qc, kc) * jnp.exp(D - m_t[..., None])
        num = w_inter[..., None] * jnp.einsum('bhtd,bhde->bhte', qc, C) + jnp.einsum('bhts,bhse->bhte', P, vc)
        qn = w_inter * jnp.einsum('bhtd,bhd->bht', qc, n) + jnp.sum(P, axis=-1)
        den = jnp.maximum(jnp.abs(qn), jnp.exp(-m_t))
        h = num / den[..., None]
        bL = b[..., -1]
        g = bL[..., None] - b + lic
        m_new = jnp.maximum(bL + m, jnp.max(g, axis=-1))
        decay = jnp.exp(bL + m - m_new)
        wk = jnp.exp(g - m_new[..., None])
        C_new = decay[..., None, None] * C + jnp.einsum('bhs,bhsd,bhse->bhde', wk, kc, vc)
        n_new = decay[..., None] * n + jnp.einsum('bhs,bhsd->bhd', wk, kc)
        return (C_new, n_new, m_new), h

    init = (jnp.zeros((B, H, dk, dv), jnp.float32),
            jnp.zeros((B, H, dk), jnp.float32),
            jnp.zeros((B, H), jnp.float32))
    xs = (to_chunks(q), to_chunks(k), to_chunks(v), to_chunks(li), to_chunks(lf))
    _, h = lax.scan(step, init, xs)
    return jnp.moveaxis(h, 0, 2).reshape(B, H, S, dv)


def sliding_window_attention(q, k, v, sinks):
    B, S, _, dh = q.shape
    Bk = WINDOW
    NB = S // Bk
    qb = q.reshape(B, NB, Bk, A_KV_HEADS, A_GROUP, dh)

    def band(t):
        tb = t.reshape(B, NB, Bk, A_KV_HEADS, dh)
        prev = jnp.pad(tb, ((0, 0), (1, 0), (0, 0), (0, 0), (0, 0)))[:, :NB]
        return jnp.concatenate([prev, tb], axis=2)

    kb, vb = band(k), band(v)
    s = jnp.einsum('bnqhgd,bnkhd->bnhgqk', qb, kb).astype(jnp.float32) * (dh ** -0.5)
    qpos = jnp.arange(Bk)[:, None] + Bk
    kpos = jnp.arange(2 * Bk)[None, :]
    rel = qpos - kpos
    allowed = (rel >= 0) & (rel < WINDOW)
    key_abs = (jnp.arange(NB) * Bk - Bk)[:, None] + kpos
    mask = allowed[None] & (key_abs >= 0)[:, None, :]
    s = jnp.where(mask[None, :, None, None, :, :], s, -jnp.inf)
    sink = sinks.astype(jnp.float32).reshape(A_KV_HEADS, A_GROUP)[None, None, :, :, None]
    mx = jnp.maximum(jnp.max(s, axis=-1), sink)
    p = jnp.exp(s - mx[..., None])
    den = jnp.sum(p, axis=-1) + jnp.exp(sink - mx)
    p = (p / den[..., None]).astype(v.dtype)
    o = jnp.einsum('bnhgqk,bnkhd->bnqhgd', p, vb)
    return o.reshape(B, S, A_HEADS, dh)


def hierarchical_moe(x, w_group, b_group, w_expert, b_expert, w_gate, w_up, w_down):
    T, D = x.shape
    g_logits = jnp.matmul(x, w_group).astype(jnp.float32) + b_group.astype(jnp.float32)
    g_prob = jax.nn.softmax(g_logits, axis=-1)
    g_p, g_idx = lax.top_k(g_prob, 1)
    e_logits = (jnp.matmul(x, w_expert).astype(jnp.float32) + b_expert.astype(jnp.float32))
    e_logits = e_logits.reshape(T, N_GROUPS, EXPERTS_PER_GROUP)[jnp.arange(T), g_idx[:, 0]]
    e_prob = jax.nn.softmax(e_logits, axis=-1)
    e_p, e_loc = lax.top_k(e_prob, TOP_K)
    e_p = e_p / jnp.sum(e_p, axis=-1, keepdims=True)
    gate = g_p * e_p
    eid = g_idx * EXPERTS_PER_GROUP + e_loc

    M = T * TOP_K
    eid_f = eid.reshape(M).astype(jnp.int32)
    tok_f = jnp.repeat(jnp.arange(T, dtype=jnp.int32), TOP_K)
    w_f = gate.reshape(M)
    order = jnp.argsort(eid_f)
    e_s, tok_s, w_s = eid_f[order], tok_f[order], w_f[order]
    counts = jnp.bincount(eid_f, length=N_EXPERTS).astype(jnp.int32)
    padded = (counts + MOE_BLOCK - 1) // MOE_BLOCK * MOE_BLOCK
    start = jnp.cumsum(counts) - counts
    pend = jnp.cumsum(padded)
    pstart = pend - padded
    dest = pstart[e_s] + jnp.arange(M, dtype=jnp.int32) - start[e_s]
    n_blocks = -(-M // MOE_BLOCK) + N_EXPERTS
    P = n_blocks * MOE_BLOCK
    row_tok = jnp.zeros((P,), jnp.int32).at[dest].set(tok_s)
    row_w = jnp.zeros((P,), jnp.float32).at[dest].set(w_s)
    blk_e = jnp.minimum(jnp.searchsorted(pend, jnp.arange(n_blocks, dtype=jnp.int32) * MOE_BLOCK, side='right'),
                        N_EXPERTS - 1)
    xr = x[row_tok].reshape(n_blocks, MOE_BLOCK, D)

    def expert_block(args):
        xb, e = args
        h = jax.nn.silu(jnp.matmul(xb, w_gate[e])) * jnp.matmul(xb, w_up[e])
        return jnp.matmul(h, w_down[e])

    yr = lax.map(expert_block, (xr, blk_e)).reshape(P, D)
    y = jax.ops.segment_sum(yr.astype(jnp.float32) * row_w[:, None], row_tok, num_segments=T)
    return y.astype(x.dtype)


def setup_inputs(seed: int = 0) -> dict:
    key = jax.random.key(seed)
    ks = jax.random.split(key, 24)
    f32 = jnp.float32
    nrm = lambda k, shape, scale: jax.random.normal(k, shape, f32) * scale
    return {
        "x": nrm(ks[0], (BATCH, SEQ, D_MODEL), 1.0),
        "g_mix": 1.0 + nrm(ks[1], (DEPTH, D_MODEL), 0.02),
        "w_in": nrm(ks[2], (DEPTH, D_MODEL, D_IN), D_MODEL ** -0.5),
        "conv_qk": nrm(ks[3], (DEPTH, CONV_WIDTH, 2 * M_WIDTH), CONV_WIDTH ** -0.5),
        "b_igate": nrm(ks[4], (DEPTH, M_HEADS), 0.1),
        "b_fgate": jnp.linspace(3.0, 6.0, M_HEADS, dtype=f32)[None, :] + nrm(ks[5], (DEPTH, M_HEADS), 0.1),
        "g_mlstm": 1.0 + nrm(ks[6], (DEPTH, M_HEADS, M_HEAD_DIM), 0.02),
        "g_q": 1.0 + nrm(ks[7], (DEPTH, A_HEAD_DIM), 0.02),
        "g_k": 1.0 + nrm(ks[8], (DEPTH, A_HEAD_DIM), 0.02),
        "sinks": nrm(ks[9], (DEPTH, A_HEADS), 0.5),
        "w_proj_m": nrm(ks[10], (DEPTH, M_WIDTH, D_MODEL), M_WIDTH ** -0.5),
        "w_proj_a": nrm(ks[11], (DEPTH, A_WIDTH, D_MODEL), A_WIDTH ** -0.5),
        "w_out": nrm(ks[12], (DEPTH, D_MODEL, D_MODEL), D_MODEL ** -0.5),
        "g_ffn": 1.0 + nrm(ks[13], (DEPTH, D_MODEL), 0.02),
        "w_group": nrm(ks[14], (DEPTH, D_MODEL, N_GROUPS), D_MODEL ** -0.5),
        "b_group": nrm(ks[15], (DEPTH, N_GROUPS), 0.01),
        "w_expert": nrm(ks[16], (DEPTH, D_MODEL, N_EXPERTS), D_MODEL ** -0.5),
        "b_expert": nrm(ks[17], (DEPTH, N_EXPERTS), 0.01),
        "w_gate": nrm(ks[18], (DEPTH, N_EXPERTS, D_MODEL, D_EXPERT), D_MODEL ** -0.5),
        "w_up": nrm(ks[19], (DEPTH, N_EXPERTS, D_MODEL, D_EXPERT), D_MODEL ** -0.5),
        "w_down": nrm(ks[20], (DEPTH, N_EXPERTS, D_EXPERT, D_MODEL), D_EXPERT ** -0.5),
    }


def reference(x, g_mix, w_in, conv_qk, b_igate, b_fgate, g_mlstm, g_q, g_k, sinks,
              w_proj_m, w_proj_a, w_out, g_ffn, w_group, b_group, w_expert, b_expert,
              w_gate, w_up, w_down):
    B, S, D = x.shape
    f32 = jnp.float32
    sizes = (M_WIDTH, M_WIDTH, M_WIDTH, M_WIDTH, M_HEADS, M_HEADS,
             A_WIDTH, A_KV_WIDTH, A_KV_WIDTH, D_MODEL, D_MODEL)
    cuts = [int(c) for c in np.cumsum(sizes)[:-1]]
    pos = jnp.arange(S, dtype=jnp.int32)
    for l in range(DEPTH):
        h = rms_norm(x, g_mix[l])
        z = jnp.matmul(h, w_in[l])
        mq, mk, mv, mo, mi, mf, aq, ak, av, gm, ga = jnp.split(z, cuts, axis=-1)

        qk = jax.nn.silu(causal_dwconv(jnp.concatenate([mq, mk], axis=-1), conv_qk[l]))
        mq, mk = jnp.split(qk, 2, axis=-1)
        heads = lambda t: t.reshape(B, S, M_HEADS, M_HEAD_DIM).transpose(0, 2, 1, 3).astype(f32)
        li = (mi + b_igate[l]).astype(f32).transpose(0, 2, 1)
        lf = jax.nn.log_sigmoid((mf + b_fgate[l]).astype(f32)).transpose(0, 2, 1)
        hm = mlstm_chunkwise(heads(mq), heads(mk) * (M_HEAD_DIM ** -0.5), heads(mv), li, lf)
        hm = rms_norm(hm.transpose(0, 2, 1, 3), g_mlstm[l])
        hm = hm.reshape(B, S, M_WIDTH).astype(x.dtype) * jax.nn.sigmoid(mo)

        aq = rope(rms_norm(aq.reshape(B, S, A_HEADS, A_HEAD_DIM), g_q[l]), pos)
        ak = rope(rms_norm(ak.reshape(B, S, A_KV_HEADS, A_HEAD_DIM), g_k[l]), pos)
        av = av.reshape(B, S, A_KV_HEADS, A_HEAD_DIM)
        ha = sliding_window_attention(aq, ak, av, sinks[l]).reshape(B, S, A_WIDTH)

        mixed = (jax.nn.sigmoid(gm) * jnp.matmul(hm, w_proj_m[l])
                 + jax.nn.sigmoid(ga) * jnp.matmul(ha, w_proj_a[l]))
        x = x + jnp.matmul(mixed, w_out[l])

        hf = rms_norm(x, g_ffn[l]).reshape(B * S, D)
        y = hierarchical_moe(hf, w_group[l], b_group[l], w_expert[l], b_expert[l],
                             w_gate[l], w_up[l], w_down[l])
        x = x + y.reshape(B, S, D)
    return x
```

```python
import functools

import jax
import jax.numpy as jnp
from jax import lax
from jax.experimental import pallas as pl
from jax.experimental.pallas import tpu as pltpu

F32 = jnp.float32
BF16 = jnp.bfloat16
EPS = 1e-6
LANES = 128
VMEM_LIMIT = 56 * 1024 * 1024

M_HEADS = 4
M_HEAD_DIM = 256
M_WIDTH = M_HEADS * M_HEAD_DIM
CONV_WIDTH = 4
A_HEADS = 16
A_KV_HEADS = 4
A_GROUP = A_HEADS // A_KV_HEADS
A_HEAD_DIM = 64
A_WIDTH = A_HEADS * A_HEAD_DIM
A_KV_WIDTH = A_KV_HEADS * A_HEAD_DIM
WINDOW = 128
ROPE_THETA = 10000.0
N_GROUPS = 8
EXPERTS_PER_GROUP = 8
N_EXPERTS = N_GROUPS * EXPERTS_PER_GROUP
TOP_K = 2

MLSTM_CHUNK = 128
MOE_ROWS = 256
COMBINE_ROWS = 128


def _sigmoid(v):
    return 1.0 / (1.0 + jnp.exp(-v))


def _params(*sem):
    return pltpu.CompilerParams(dimension_semantics=sem, vmem_limit_bytes=VMEM_LIMIT)


def _inproj_body(x_ref, g_ref, w_ref, wgate_ref, z_ref, zg_ref, hn_ref, *, sub):
    bm = x_ref.shape[0]

    @pl.when(pl.program_id(1) == 0)
    def _():
        def rows(r, carry):
            sl = pl.ds(pl.multiple_of(r * sub, sub), sub)
            xv = x_ref[sl, :]
            ms = jnp.mean(xv * xv, axis=-1, keepdims=True)
            hn_ref[sl, :] = ((xv * lax.rsqrt(ms + EPS)) * g_ref[...]).astype(BF16)
            return carry
        lax.fori_loop(0, bm // sub, rows, 0)
        zg_ref[...] = jnp.dot(hn_ref[...], wgate_ref[...], preferred_element_type=F32)

    z_ref[...] = jnp.dot(hn_ref[...], w_ref[...], preferred_element_type=F32).astype(BF16)


def _in_proj(x2, g, w, wgate, *, bm, bn):
    T, D = x2.shape
    N = w.shape[1]
    return pl.pallas_call(
        functools.partial(_inproj_body, sub=128),
        grid=(T // bm, N // bn),
        in_specs=[
            pl.BlockSpec((bm, D), lambda i, j: (i, 0)),
            pl.BlockSpec((1, D), lambda i, j: (0, 0)),
            pl.BlockSpec((D, bn), lambda i, j: (0, j)),
            pl.BlockSpec((D, LANES), lambda i, j: (0, 0)),
        ],
        out_specs=[
            pl.BlockSpec((bm, bn), lambda i, j: (i, j)),
            pl.BlockSpec((bm, LANES), lambda i, j: (i, 0)),
        ],
        out_shape=[
            jax.ShapeDtypeStruct((T, N), BF16),
            jax.ShapeDtypeStruct((T, LANES), F32),
        ],
        scratch_shapes=[pltpu.VMEM((bm, D), BF16)],
        compiler_params=_params("parallel", "arbitrary"),
        name="in_proj",
    )(x2, g, w, wgate)


def _mlstm_body(q_ref, k_ref, v_ref, o_ref, zg_ref, cq_ref, ck_ref, bias_ref, gn_ref, out_ref,
                qs_ref, ks_ref, colli_ref, colb_ref, rowli_ref, rowb_ref, cli_ref, cb_ref, rli_ref, rb_ref,
                c_ref, n_ref, m_ref, *, L, CB):
    h = pl.program_id(1)
    S, dk = q_ref.shape

    @pl.when(h == 0)
    def _():
        G = zg_ref[...] + bias_ref[...]
        lf = jnp.minimum(G, 0.0) - jnp.log1p(jnp.exp(-jnp.abs(G)))
        pos = lax.broadcasted_iota(jnp.int32, (S, LANES), 0) % L
        bc = lf
        sh = 1
        while sh < L:
            bc = bc + jnp.where(pos >= sh, pltpu.roll(bc, sh, axis=0), 0.0)
            sh *= 2
        colli_ref[...] = G
        colb_ref[...] = bc
        for p in range(S // LANES):
            sl = slice(p * LANES, (p + 1) * LANES)
            rowli_ref[:, sl] = G[sl, :].T[0:8, :]
            rowb_ref[:, sl] = bc[sl, :].T[0:8, :]

    lane = lax.broadcasted_iota(jnp.int32, (S, LANES), 1)
    cli_ref[...] = jnp.sum(jnp.where(lane == h, colli_ref[...], 0.0), axis=1, keepdims=True)
    cb_ref[...] = jnp.sum(jnp.where(lane == h + M_HEADS, colb_ref[...], 0.0), axis=1, keepdims=True)
    sub = lax.broadcasted_iota(jnp.int32, (8, S), 0)
    rli_ref[...] = jnp.sum(jnp.where(sub == h, rowli_ref[...], 0.0), axis=0, keepdims=True)
    rb_ref[...] = jnp.sum(jnp.where(sub == h + M_HEADS, rowb_ref[...], 0.0), axis=0, keepdims=True)

    def conv_silu(src_ref, w_ref, dst_ref, scale):
        w = w_ref[...]

        def taps(xs):
            y = xs[3] * w[0:1, :]
            y = y + xs[2] * w[1:2, :]
            y = y + xs[1] * w[2:3, :]
            y = y + xs[0] * w[3:4, :]
            return ((y * _sigmoid(y)) * scale).astype(BF16)

        def chunk(r, carry):
            r0 = pl.multiple_of(r * CB, CB)
            cur = src_ref[pl.ds(r0, CB), :].astype(F32)
            dst_ref[pl.ds(r0, CB), :] = taps([cur] + [pltpu.roll(cur, d, axis=0) for d in (1, 2, 3)])
            p0 = pl.multiple_of(jnp.maximum(r0 - 16, 0), 16)
            prev = src_ref[pl.ds(p0, 16), :].astype(F32)[8:16, :]
            prev = jnp.where(r > 0, prev, 0.0)
            both = jnp.concatenate([prev, cur[0:16, :]], axis=0)
            dst_ref[pl.ds(r0, 16), :] = taps([both[8:24, :]] + [pltpu.roll(both, d, axis=0)[8:24, :] for d in (1, 2, 3)])
            return carry
        lax.fori_loop(0, S // CB, chunk, 0)

    conv_silu(q_ref, cq_ref, qs_ref, 1.0)
    conv_silu(k_ref, ck_ref, ks_ref, float(dk) ** -0.5)

    c_ref[...] = jnp.zeros_like(c_ref)
    n_ref[...] = jnp.zeros_like(n_ref)
    m_ref[...] = jnp.zeros_like(m_ref)
    t_idx = lax.broadcasted_iota(jnp.int32, (L, L), 0)
    s_idx = lax.broadcasted_iota(jnp.int32, (L, L), 1)
    causal = s_idx <= t_idx
    gn = gn_ref[0]

    def chunk(c, carry):
        r0 = pl.multiple_of(c * L, L)
        rows = pl.ds(r0, L)
        qc = qs_ref[rows, :]
        kc = ks_ref[rows, :]
        vc = v_ref[rows, :]
        b_col = cb_ref[rows, :]
        li_col = cli_ref[rows, :]
        li_row = rli_ref[:, rows]
        b_row = rb_ref[:, rows]
        m = m_ref[...]
        a = b_col + m
        D = jnp.where(causal, b_col - b_row + li_row, -jnp.inf)
        m_t = jnp.maximum(a, jnp.max(D, axis=1, keepdims=True))
        w_inter = jnp.exp(a - m_t)
        P = lax.dot_general(qc, kc, (((1,), (1,)), ((), ())), preferred_element_type=F32) * jnp.exp(D - m_t)
        num = (w_inter * jnp.dot(qc, c_ref[...].astype(BF16), preferred_element_type=F32)
               + jnp.dot(P.astype(BF16), vc, preferred_element_type=F32))
        qn = (w_inter * jnp.sum(qc.astype(F32) * n_ref[...], axis=1, keepdims=True)
              + jnp.sum(P, axis=1, keepdims=True))
        den = jnp.maximum(jnp.abs(qn), jnp.exp(-m_t))
        hh = num / den
        ms = jnp.mean(hh * hh, axis=1, keepdims=True)
        hn = (hh * lax.rsqrt(ms + EPS)) * gn
        out_ref[rows, :] = (hn * _sigmoid(o_ref[rows, :].astype(F32))).astype(BF16)
        bL = b_row[:, L - 1:L]
        g_col = bL - b_col + li_col
        m_new = jnp.maximum(bL + m, jnp.max(g_col, axis=0, keepdims=True))
        decay = jnp.exp(bL + m - m_new)
        kw = kc.astype(F32) * jnp.exp(g_col - m_new)
        c_ref[...] = decay * c_ref[...] + jnp.dot(kw.T.astype(BF16), vc, preferred_element_type=F32)
        n_ref[...] = decay * n_ref[...] + jnp.sum(kw, axis=0, keepdims=True)
        m_ref[...] = m_new
        return carry
    lax.fori_loop(0, S // L, chunk, 0)


def _mlstm(z, zg, conv_qk, gate_bias, g_mlstm3, *, B, S, col_q, col_k, col_v, col_o):
    T = B * S
    dk = M_HEAD_DIM
    L = MLSTM_CHUNK
    zspec = lambda col: pl.BlockSpec((S, dk), lambda b, h: (b, col + h))
    return pl.pallas_call(
        functools.partial(_mlstm_body, L=L, CB=128),
        grid=(B, M_HEADS),
        in_specs=[
            zspec(col_q), zspec(col_k), zspec(col_v), zspec(col_o),
            pl.BlockSpec((S, LANES), lambda b, h: (b, 0)),
            pl.BlockSpec((CONV_WIDTH, dk), lambda b, h: (0, h)),
            pl.BlockSpec((CONV_WIDTH, dk), lambda b, h: (0, M_HEADS + h)),
            pl.BlockSpec((1, LANES), lambda b, h: (0, 0)),
            pl.BlockSpec((1, 1, dk), lambda b, h: (h, 0, 0)),
        ],
        out_specs=pl.BlockSpec((S, dk), lambda b, h: (b, h)),
        out_shape=jax.ShapeDtypeStruct((T, M_WIDTH), BF16),
        scratch_shapes=[
            pltpu.VMEM((S, dk), BF16), pltpu.VMEM((S, dk), BF16),
            pltpu.VMEM((S, LANES), F32), pltpu.VMEM((S, LANES), F32),
            pltpu.VMEM((8, S), F32), pltpu.VMEM((8, S), F32),
            pltpu.VMEM((S, 1), F32), pltpu.VMEM((S, 1), F32),
            pltpu.VMEM((1, S), F32), pltpu.VMEM((1, S), F32),
            pltpu.VMEM((dk, dk), F32), pltpu.VMEM((1, dk), F32), pltpu.VMEM((1, 1), F32),
        ],
        compiler_params=_params("parallel", "arbitrary"),
        name="mlstm",
    )(z, z, z, z, zg, conv_qk, conv_qk, gate_bias, g_mlstm3)


def _swa_body(sink_ref, q_ref, k_ref, v_ref, qa_ref, qb_ref, ka_ref, kb_ref, bd_ref, rep_ref, out_ref,
              kbd_ref, vbd_ref):
    n = pl.program_id(1)
    W = q_ref.shape[0]
    hd = A_HEAD_DIM
    gw = A_GROUP * hd

    def norm_rope(x, ta, tb):
        x2 = x * x
        x2h = x2.astype(BF16)
        x2l = (x2 - x2h.astype(F32)).astype(BF16)
        ss = (jnp.dot(x2h, bd_ref[...], preferred_element_type=F32)
              + jnp.dot(x2l, bd_ref[...], preferred_element_type=F32))
        r = lax.rsqrt(ss * (1.0 / hd) + EPS)
        ln = lax.broadcasted_iota(jnp.int32, x.shape, 1) % hd
        swapped = jnp.where(ln < hd // 2, pltpu.roll(x, LANES - hd // 2, axis=1), pltpu.roll(x, hd // 2, axis=1))
        return r * (x * ta + swapped * tb)

    qa = qa_ref[...]
    qb = qb_ref[...]
    qp = jnp.concatenate(
        [norm_rope(q_ref[:, t * LANES:(t + 1) * LANES].astype(F32), qa, qb) for t in range(A_WIDTH // LANES)],
        axis=1).astype(BF16)
    ka = ka_ref[...]
    kb = kb_ref[...]
    kp = jnp.concatenate(
        [norm_rope(k_ref[:, t * LANES:(t + 1) * LANES].astype(F32), ka, kb) for t in range(A_KV_WIDTH // LANES)],
        axis=1).astype(BF16)
    vv = v_ref[...]

    lane_head = lax.broadcasted_iota(jnp.int32, (W, gw), 1) // hd

    @pl.when(n == 0)
    def _():
        kbd_ref[0] = jnp.zeros(kbd_ref.shape[1:], BF16)
        vbd_ref[0] = jnp.zeros(vbd_ref.shape[1:], BF16)

    @pl.when(n > 0)
    def _():
        kbd_ref[0] = kbd_ref[1]
        vbd_ref[0] = vbd_ref[1]

    for j in range(A_KV_HEADS):
        krep = jnp.dot(kp, rep_ref[j], preferred_element_type=F32).astype(BF16)
        vrep = jnp.dot(vv, rep_ref[j], preferred_element_type=F32).astype(BF16)
        for i in range(A_GROUP):
            kbd_ref[1, j, i * W:(i + 1) * W, :] = jnp.where(lane_head == i, krep, jnp.zeros_like(krep))
            vbd_ref[1, j, i * W:(i + 1) * W, :] = jnp.where(lane_head == i, vrep, jnp.zeros_like(vrep))

    t_idx = lax.broadcasted_iota(jnp.int32, (W, W), 0)
    k_idx = lax.broadcasted_iota(jnp.int32, (W, W), 1)
    mask_cur = k_idx <= t_idx
    mask_prev = jnp.logical_and(k_idx > t_idx, n > 0)
    neg = -jnp.inf

    for j in range(A_KV_HEADS):
        qg = qp[:, j * gw:(j + 1) * gw]
        s_prev = lax.dot_general(qg, kbd_ref[0, j], (((1,), (1,)), ((), ())), preferred_element_type=F32)
        s_cur = lax.dot_general(qg, kbd_ref[1, j], (((1,), (1,)), ((), ())), preferred_element_type=F32)
        pp, pc = [], []
        for i in range(A_GROUP):
            sp = jnp.where(mask_prev, s_prev[:, i * W:(i + 1) * W], neg)
            sc = jnp.where(mask_cur, s_cur[:, i * W:(i + 1) * W], neg)
            sink = sink_ref[j * A_GROUP + i]
            mx = jnp.maximum(jnp.maximum(jnp.max(sp, axis=1, keepdims=True), jnp.max(sc, axis=1, keepdims=True)), sink)
            ep = jnp.exp(sp - mx)
            ec = jnp.exp(sc - mx)
            den = jnp.sum(ep, axis=1, keepdims=True) + jnp.sum(ec, axis=1, keepdims=True) + jnp.exp(sink - mx)
            inv = 1.0 / den
            pp.append((ep * inv).astype(BF16))
            pc.append((ec * inv).astype(BF16))
        o = (jnp.dot(jnp.concatenate(pp, axis=1), vbd_ref[0, j], preferred_element_type=F32)
             + jnp.dot(jnp.concatenate(pc, axis=1), vbd_ref[1, j], preferred_element_type=F32))
        out_ref[:, j * gw:(j + 1) * gw] = o.astype(BF16)


def _swa(z, sinks, qa, qb, ka, kb, bd, rep, *, B, S, col_q, col_k, col_v):
    T = B * S
    W = WINDOW
    NB = S // W
    gw = A_GROUP * A_HEAD_DIM
    tab = lambda: pl.BlockSpec((W, LANES), lambda b, n: (n, 0))
    return pl.pallas_call(
        _swa_body,
        grid=(B, NB),
        in_specs=[
            pl.BlockSpec(memory_space=pltpu.SMEM),
            pl.BlockSpec((W, A_WIDTH), lambda b, n: (b * NB + n, col_q)),
            pl.BlockSpec((W, A_KV_WIDTH), lambda b, n: (b * NB + n, col_k)),
            pl.BlockSpec((W, A_KV_WIDTH), lambda b, n: (b * NB + n, col_v)),
            tab(), tab(), tab(), tab(),
            pl.BlockSpec((LANES, LANES), lambda b, n: (0, 0)),
            pl.BlockSpec((A_KV_HEADS, gw, gw), lambda b, n: (0, 0, 0)),
        ],
        out_specs=pl.BlockSpec((W, A_WIDTH), lambda b, n: (b * NB + n, 0)),
        out_shape=jax.ShapeDtypeStruct((T, A_WIDTH), BF16),
        scratch_shapes=[
            pltpu.VMEM((2, A_KV_HEADS, A_GROUP * W, gw), BF16),
            pltpu.VMEM((2, A_KV_HEADS, A_GROUP * W, gw), BF16),
        ],
        compiler_params=_params("parallel", "arbitrary"),
        name="swa",
    )(sinks, z, z, z, qa, qb, ka, kb, bd, rep)


def _merge_body(hm_ref, ha_ref, gm_ref, ga_ref, x_ref, wm_ref, wa_ref, wo_ref, gf_ref, rh_ref, rl_ref, rb_ref,
                x1_ref, hf_ref, lg_ref):
    pm = jnp.dot(hm_ref[...], wm_ref[...], preferred_element_type=F32)
    pa = jnp.dot(ha_ref[...], wa_ref[...], preferred_element_type=F32)
    mixed = _sigmoid(gm_ref[...].astype(F32)) * pm + _sigmoid(ga_ref[...].astype(F32)) * pa
    x1 = x_ref[...] + jnp.dot(mixed.astype(BF16), wo_ref[...], preferred_element_type=F32)
    x1_ref[...] = x1
    ms = jnp.mean(x1 * x1, axis=-1, keepdims=True)
    hf = (x1 * lax.rsqrt(ms + EPS)) * gf_ref[...]
    hf_ref[...] = hf
    hh = hf.astype(BF16)
    hl = (hf - hh.astype(F32)).astype(BF16)
    lg_ref[...] = (jnp.dot(hh, rh_ref[...], preferred_element_type=F32)
                   + jnp.dot(hl, rh_ref[...], preferred_element_type=F32)
                   + jnp.dot(hh, rl_ref[...], preferred_element_type=F32)) + rb_ref[...]


def _merge(hm, ha, z, x2, wm, wa, wo, gf, rh, rl, rb, *, bm, col_gm, col_ga):
    T, D = x2.shape
    const = lambda shape: pl.BlockSpec(shape, lambda i: (0,) * len(shape), pipeline_mode=pl.Buffered(1))
    return pl.pallas_call(
        _merge_body,
        grid=(T // bm,),
        in_specs=[
            pl.BlockSpec((bm, M_WIDTH), lambda i: (i, 0)),
            pl.BlockSpec((bm, A_WIDTH), lambda i: (i, 0)),
            pl.BlockSpec((bm, D), lambda i: (i, col_gm)),
            pl.BlockSpec((bm, D), lambda i: (i, col_ga)),
            pl.BlockSpec((bm, D), lambda i: (i, 0)),
            const((M_WIDTH, D)), const((A_WIDTH, D)), const((D, D)), const((1, D)),
            const((D, LANES)), const((D, LANES)), const((1, LANES)),
        ],
        out_specs=[
            pl.BlockSpec((bm, D), lambda i: (i, 0)),
            pl.BlockSpec((bm, D), lambda i: (i, 0)),
            pl.BlockSpec((bm, LANES), lambda i: (i, 0)),
        ],
        out_shape=[
            jax.ShapeDtypeStruct((T, D), F32),
            jax.ShapeDtypeStruct((T, D), F32),
            jax.ShapeDtypeStruct((T, LANES), F32),
        ],
        compiler_params=_params("parallel"),
        name="merge",
    )(hm, ha, z, z, x2, wm, wa, wo, gf, rh, rl, rb)


def _moe_body(blk_e_ref, nused_ref, tok_ref, hf_hbm, wg_ref, wu_ref, wd_ref, y_ref, xbuf, sem, *, R):
    i = pl.program_id(0)
    nused = nused_ref[0]

    def start_gather(blk, slot):
        base = blk * R

        def row(r, carry):
            tok = tok_ref[base + r]
            pltpu.make_async_copy(hf_hbm.at[pl.ds(tok, 1)], xbuf.at[slot, pl.ds(r, 1)], sem.at[slot]).start()
            return carry
        lax.fori_loop(0, R, row, 0, unroll=8)

    def wait_gather(slot):
        pltpu.make_async_copy(xbuf.at[slot], xbuf.at[slot], sem.at[slot]).wait()

    slot = i % 2

    @pl.when(i == 0)
    def _():
        start_gather(0, 0)

    @pl.when(i < nused)
    def _():
        wait_gather(slot)

        @pl.when(i + 1 < nused)
        def _():
            start_gather(i + 1, 1 - slot)

        xv = xbuf[slot].astype(BF16)
        g = jnp.dot(xv, wg_ref[...], preferred_element_type=F32)
        u = jnp.dot(xv, wu_ref[...], preferred_element_type=F32)
        hmid = ((g * _sigmoid(g)) * u).astype(BF16)
        y_ref[...] = jnp.dot(hmid, wd_ref[...], preferred_element_type=F32)


def _moe(blk_e, nused, row_tok, hf, wg, wu, wd, *, R):
    T, D = hf.shape
    E, _, F = wg.shape
    n_blocks = blk_e.shape[0]
    wspec = lambda shape: pl.BlockSpec((None,) + shape, lambda i, be, nu, tk: (be[i], 0, 0))
    grid_spec = pltpu.PrefetchScalarGridSpec(
        num_scalar_prefetch=3,
        grid=(n_blocks,),
        in_specs=[
            pl.BlockSpec(memory_space=pl.ANY),
            wspec((D, F)), wspec((D, F)), wspec((F, D)),
        ],
        out_specs=pl.BlockSpec((R, D), lambda i, be, nu, tk: (jnp.minimum(i, nu[0] - 1), 0)),
        scratch_shapes=[pltpu.VMEM((2, R, D), F32), pltpu.SemaphoreType.DMA((2,))],
    )
    return pl.pallas_call(
        functools.partial(_moe_body, R=R),
        grid_spec=grid_spec,
        out_shape=jax.ShapeDtypeStruct((n_blocks * R, D), F32),
        compiler_params=_params("arbitrary"),
        name="moe",
    )(blk_e, nused, row_tok, hf, wg, wu, wd)


def _combine_body(dest_ref, x1_ref, w_ref, yr_hbm, out_ref, ybuf, sem, *, R):
    i = pl.program_id(0)
    nsteps = pl.num_programs(0)

    def start_gather(step, slot):
        base = step * (R * TOP_K)

        def row(r, carry):
            for k in range(TOP_K):
                d = dest_ref[base + r * TOP_K + k]
                pltpu.make_async_copy(yr_hbm.at[pl.ds(d, 1)], ybuf.at[slot, k, pl.ds(r, 1)], sem.at[slot]).start()
            return carry
        lax.fori_loop(0, R, row, 0, unroll=4)

    slot = i % 2

    @pl.when(i == 0)
    def _():
        start_gather(0, 0)

    pltpu.make_async_copy(ybuf.at[slot], ybuf.at[slot], sem.at[slot]).wait()

    @pl.when(i + 1 < nsteps)
    def _():
        start_gather(i + 1, 1 - slot)

    w = w_ref[...]
    y = ybuf[slot, 0] * w[:, 0:1] + ybuf[slot, 1] * w[:, 1:2]
    out_ref[...] = x1_ref[...] + y


def _combine(dest, x1, gate_w, yr, *, R):
    T, D = x1.shape
    grid_spec = pltpu.PrefetchScalarGridSpec(
        num_scalar_prefetch=1,
        grid=(T // R,),
        in_specs=[
            pl.BlockSpec((R, D), lambda i, d: (i, 0)),
            pl.BlockSpec((R, TOP_K), lambda i, d: (i, 0)),
            pl.BlockSpec(memory_space=pl.ANY),
        ],
        out_specs=pl.BlockSpec((R, D), lambda i, d: (i, 0)),
        scratch_shapes=[pltpu.VMEM((2, TOP_K, R, D), F32), pltpu.SemaphoreType.DMA((2,))],
    )
    return pl.pallas_call(
        functools.partial(_combine_body, R=R),
        grid_spec=grid_spec,
        out_shape=jax.ShapeDtypeStruct((T, D), F32),
        compiler_params=_params("arbitrary"),
        name="combine",
    )(dest, x1, gate_w, yr)


def _route(logits, R):
    T = logits.shape[0]
    g_prob = jax.nn.softmax(logits[:, :N_GROUPS], axis=-1)
    g_p, g_idx = lax.top_k(g_prob, 1)
    e_all = logits[:, N_GROUPS:N_GROUPS + N_EXPERTS].reshape(T, N_GROUPS, EXPERTS_PER_GROUP)
    e_logits = jnp.take_along_axis(e_all, g_idx[:, :, None], axis=1)[:, 0]
    e_prob = jax.nn.softmax(e_logits, axis=-1)
    e_p, e_loc = lax.top_k(e_prob, TOP_K)
    e_p = e_p / jnp.sum(e_p, axis=-1, keepdims=True)
    gate = g_p * e_p
    eid = (g_idx * EXPERTS_PER_GROUP + e_loc).astype(jnp.int32)

    M = T * TOP_K
    eid_f = eid.reshape(M)
    onehot = (eid_f[:, None] == jnp.arange(N_EXPERTS, dtype=jnp.int32)[None, :]).astype(jnp.int32)
    csum = jnp.cumsum(onehot, axis=0)
    rank = jnp.sum(csum * onehot, axis=1) - 1
    counts = csum[-1]
    padded = (counts + R - 1) // R * R
    pend = jnp.cumsum(padded)
    pstart = pend - padded
    dest = (pstart[eid_f] + rank).astype(jnp.int32)
    n_blocks = -(-M // R) + N_EXPERTS
    tok_f = jnp.arange(M, dtype=jnp.int32) // TOP_K
    row_tok = jnp.zeros((n_blocks * R,), jnp.int32).at[dest].set(tok_f)
    blk_start = jnp.arange(n_blocks, dtype=jnp.int32) * R
    blk_e = jnp.sum((pend[None, :] <= blk_start[:, None]).astype(jnp.int32), axis=1)
    nused = (pend[-1] // R).astype(jnp.int32)
    last_e = blk_e[jnp.maximum(nused - 1, 0)]
    blk_e = jnp.where(jnp.arange(n_blocks) < nused, blk_e, last_e)
    blk_e = jnp.minimum(blk_e, N_EXPERTS - 1).astype(jnp.int32)
    return gate, dest, row_tok, blk_e, nused.reshape(1)


def _rope_tables(S, gain, scale):
    half = A_HEAD_DIM // 2
    freqs = ROPE_THETA ** (-jnp.arange(half, dtype=F32) / half)
    ang = jnp.arange(S, dtype=F32)[:, None] * freqs[None, :]
    cos = jnp.cos(ang)
    sin = jnp.sin(ang)
    g1, g2 = gain[:half], gain[half:]
    a_head = jnp.concatenate([cos * g1, cos * g2], axis=1)
    b_head = jnp.concatenate([-sin * g2, sin * g1], axis=1)
    reps = LANES // A_HEAD_DIM
    return jnp.tile(a_head, (1, reps)) * scale, jnp.tile(b_head, (1, reps)) * scale


def kernel(x, g_mix, w_in, conv_qk, b_igate, b_fgate, g_mlstm, g_q, g_k, sinks, w_proj_m, w_proj_a, w_out,
           g_ffn, w_group, b_group, w_expert, b_expert, w_gate, w_up, w_down):
    B, S, D = x.shape
    T = B * S
    depth = g_mix.shape[0]
    xf = x.reshape(T, D)

    sizes = (M_WIDTH, M_WIDTH, M_WIDTH, M_WIDTH, M_HEADS, M_HEADS, A_WIDTH, A_KV_WIDTH, A_KV_WIDTH, D, D)
    offs = [0]
    for s_ in sizes:
        offs.append(offs[-1] + s_)
    seg = lambda w, idx: w[:, offs[idx]:offs[idx + 1]]
    order = (9, 10, 0, 1, 2, 3, 6, 7, 8)
    new_off = {}
    acc = 0
    for idx in order:
        new_off[idx] = acc
        acc += sizes[idx]
    dk = M_HEAD_DIM

    lane128 = jnp.arange(LANES)
    bd = (lane128[:, None] // A_HEAD_DIM == lane128[None, :] // A_HEAD_DIM).astype(BF16)
    gw = A_GROUP * A_HEAD_DIM
    r_idx = jnp.arange(gw)
    rep = jnp.stack([(r_idx[:, None] == (j * A_HEAD_DIM + r_idx[None, :] % A_HEAD_DIM)).astype(BF16)
                     for j in range(A_KV_HEADS)])

    for l in range(depth):
        w_rep = jnp.concatenate([seg(w_in[l], idx) for idx in order], axis=1).astype(BF16)
        w_gates = jnp.concatenate([seg(w_in[l], 4), seg(w_in[l], 5)], axis=1)
        w_gates = jnp.pad(w_gates, ((0, 0), (0, LANES - 2 * M_HEADS))).astype(BF16)
        z, zg = _in_proj(xf, g_mix[l][None, :], w_rep, w_gates, bm=1024 if T % 1024 == 0 else T, bn=512)

        gate_bias = jnp.pad(jnp.concatenate([b_igate[l], b_fgate[l]]), (0, LANES - 2 * M_HEADS))[None, :]
        hm = _mlstm(z, zg, conv_qk[l], gate_bias, g_mlstm[l][:, None, :], B=B, S=S,
                    col_q=new_off[0] // dk, col_k=new_off[1] // dk, col_v=new_off[2] // dk, col_o=new_off[3] // dk)

        qa, qb = _rope_tables(S, g_q[l], A_HEAD_DIM ** -0.5)
        ka, kb = _rope_tables(S, g_k[l], 1.0)
        ha = _swa(z, sinks[l], qa, qb, ka, kb, bd, rep, B=B, S=S,
                  col_q=new_off[6] // A_WIDTH, col_k=new_off[7] // A_KV_WIDTH, col_v=new_off[8] // A_KV_WIDTH)

        w_router = jnp.pad(jnp.concatenate([w_group[l], w_expert[l]], axis=1),
                           ((0, 0), (0, LANES - N_GROUPS - N_EXPERTS)))
        r_hi = w_router.astype(BF16)
        r_lo = (w_router - r_hi.astype(F32)).astype(BF16)
        r_b = jnp.pad(jnp.concatenate([b_group[l], b_expert[l]]), (0, LANES - N_GROUPS - N_EXPERTS))[None, :]
        x1, hf, logits = _merge(hm, ha, z, xf, w_proj_m[l].astype(BF16), w_proj_a[l].astype(BF16),
                                w_out[l].astype(BF16), g_ffn[l][None, :], r_hi, r_lo, r_b,
                                bm=256, col_gm=new_off[9] // D, col_ga=new_off[10] // D)

        gate, dest, row_tok, blk_e, nused = _route(logits, MOE_ROWS)
        yr = _moe(blk_e, nused, row_tok, hf, w_gate[l].astype(BF16), w_up[l].astype(BF16),
                  w_down[l].astype(BF16), R=MOE_ROWS)
        xf = _combine(dest, x1, gate, yr, R=COMBINE_ROWS)
    return xf.reshape(B, S, D)
```

```python
import functools

import jax
import jax.numpy as jnp
from jax import lax
from jax.experimental import pallas as pl
from jax.experimental.pallas import tpu as pltpu

F32 = jnp.float32
BF16 = jnp.bfloat16
EPS = 1e-6
LANES = 128
VMEM_LIMIT = 56 * 1024 * 1024

M_HEADS = 4
M_HEAD_DIM = 256
M_WIDTH = M_HEADS * M_HEAD_DIM
CONV_WIDTH = 4
A_HEADS = 16
A_KV_HEADS = 4
A_GROUP = A_HEADS // A_KV_HEADS
A_HEAD_DIM = 64
A_WIDTH = A_HEADS * A_HEAD_DIM
A_KV_WIDTH = A_KV_HEADS * A_HEAD_DIM
WINDOW = 128
ROPE_THETA = 10000.0
N_GROUPS = 8
EXPERTS_PER_GROUP = 8
N_EXPERTS = N_GROUPS * EXPERTS_PER_GROUP
TOP_K = 2

MLSTM_CHUNK = 128
MOE_ROWS = 256
COMBINE_ROWS = 128


def _sigmoid(v):
    return 1.0 / (1.0 + jnp.exp(-v))


def _params(*sem):
    return pltpu.CompilerParams(dimension_semantics=sem, vmem_limit_bytes=VMEM_LIMIT)


def _store_token_rows(ref, val):
    n, d = val.shape
    sub = d // LANES
    for s in range(sub):
        ref[pl.ds(s, n, stride=sub), :] = val[:, s * LANES:(s + 1) * LANES]


def _load_token_rows(ref, n, sub, s):
    return ref[pl.ds(s, n, stride=sub), :]


def _inproj_body(x_ref, g_ref, w_ref, wgate_ref, z_ref, zg_ref, hn_ref, *, sub):
    bm = x_ref.shape[0]

    @pl.when(pl.program_id(1) == 0)
    def _():
        def rows(r, carry):
            sl = pl.ds(pl.multiple_of(r * sub, sub), sub)
            xv = x_ref[sl, :]
            ms = jnp.mean(xv * xv, axis=-1, keepdims=True)
            hn_ref[sl, :] = ((xv * lax.rsqrt(ms + EPS)) * g_ref[...]).astype(BF16)
            return carry
        lax.fori_loop(0, bm // sub, rows, 0)
        zg_ref[...] = jnp.dot(hn_ref[...], wgate_ref[...], preferred_element_type=F32)

    z_ref[...] = jnp.dot(hn_ref[...], w_ref[...], preferred_element_type=F32).astype(BF16)


def _in_proj(x2, g, w, wgate, *, bm, bn):
    T, D = x2.shape
    N = w.shape[1]
    return pl.pallas_call(
        functools.partial(_inproj_body, sub=128),
        grid=(T // bm, N // bn),
        in_specs=[
            pl.BlockSpec((bm, D), lambda i, j: (i, 0)),
            pl.BlockSpec((1, D), lambda i, j: (0, 0)),
            pl.BlockSpec((D, bn), lambda i, j: (0, j)),
            pl.BlockSpec((D, LANES), lambda i, j: (0, 0)),
        ],
        out_specs=[
            pl.BlockSpec((bm, bn), lambda i, j: (i, j)),
            pl.BlockSpec((bm, LANES), lambda i, j: (i, 0)),
        ],
        out_shape=[
            jax.ShapeDtypeStruct((T, N), BF16),
            jax.ShapeDtypeStruct((T, LANES), F32),
        ],
        scratch_shapes=[pltpu.VMEM((bm, D), BF16)],
        compiler_params=_params("parallel", "arbitrary"),
        name="in_proj",
    )(x2, g, w, wgate)


def _mlstm_body(q_ref, k_ref, v_ref, o_ref, zg_ref, cq_ref, ck_ref, bias_ref, gn_ref, out_ref,
                qs_ref, ks_ref, colli_ref, colb_ref, rowli_ref, rowb_ref, cli_ref, cb_ref, rli_ref, rb_ref,
                c_ref, n_ref, m_ref, *, L, CB):
    h = pl.program_id(1)
    S, dk = q_ref.shape

    @pl.when(h == 0)
    def _():
        G = zg_ref[...] + bias_ref[...]
        lf = jnp.minimum(G, 0.0) - jnp.log1p(jnp.exp(-jnp.abs(G)))
        pos = lax.broadcasted_iota(jnp.int32, (S, LANES), 0) % L
        bc = lf
        sh = 1
        while sh < L:
            bc = bc + jnp.where(pos >= sh, pltpu.roll(bc, sh, axis=0), 0.0)
            sh *= 2
        colli_ref[...] = G
        colb_ref[...] = bc
        for p in range(S // LANES):
            sl = slice(p * LANES, (p + 1) * LANES)
            rowli_ref[:, sl] = G[sl, :].T[0:8, :]
            rowb_ref[:, sl] = bc[sl, :].T[0:8, :]

    lane = lax.broadcasted_iota(jnp.int32, (S, LANES), 1)
    cli_ref[...] = jnp.sum(jnp.where(lane == h, colli_ref[...], 0.0), axis=1, keepdims=True)
    cb_ref[...] = jnp.sum(jnp.where(lane == h + M_HEADS, colb_ref[...], 0.0), axis=1, keepdims=True)
    sub = lax.broadcasted_iota(jnp.int32, (8, S), 0)
    rli_ref[...] = jnp.sum(jnp.where(sub == h, rowli_ref[...], 0.0), axis=0, keepdims=True)
    rb_ref[...] = jnp.sum(jnp.where(sub == h + M_HEADS, rowb_ref[...], 0.0), axis=0, keepdims=True)

    def conv_silu(src_ref, w_ref, dst_ref, scale):
        w = w_ref[...]

        def taps(xs):
            y = xs[3] * w[0:1, :]
            y = y + xs[2] * w[1:2, :]
            y = y + xs[1] * w[2:3, :]
            y = y + xs[0] * w[3:4, :]
            return ((y * _sigmoid(y)) * scale).astype(BF16)

        def chunk(r, carry):
            r0 = pl.multiple_of(r * CB, CB)
            cur = src_ref[pl.ds(r0, CB), :].astype(F32)
            dst_ref[pl.ds(r0, CB), :] = taps([cur] + [pltpu.roll(cur, d, axis=0) for d in (1, 2, 3)])
            p0 = pl.multiple_of(jnp.maximum(r0 - 16, 0), 16)
            prev = src_ref[pl.ds(p0, 16), :].astype(F32)[8:16, :]
            prev = jnp.where(r > 0, prev, 0.0)
            both = jnp.concatenate([prev, cur[0:16, :]], axis=0)
            dst_ref[pl.ds(r0, 16), :] = taps([both[8:24, :]] + [pltpu.roll(both, d, axis=0)[8:24, :] for d in (1, 2, 3)])
            return carry
        lax.fori_loop(0, S // CB, chunk, 0)

    conv_silu(q_ref, cq_ref, qs_ref, 1.0)
    conv_silu(k_ref, ck_ref, ks_ref, float(dk) ** -0.5)

    c_ref[...] = jnp.zeros_like(c_ref)
    n_ref[...] = jnp.zeros_like(n_ref)
    m_ref[...] = jnp.zeros_like(m_ref)
    t_idx = lax.broadcasted_iota(jnp.int32, (L, L), 0)
    s_idx = lax.broadcasted_iota(jnp.int32, (L, L), 1)
    causal = s_idx <= t_idx
    gn = gn_ref[0]

    def chunk(c, carry):
        r0 = pl.multiple_of(c * L, L)
        rows = pl.ds(r0, L)
        qc = qs_ref[rows, :]
        kc = ks_ref[rows, :]
        vc = v_ref[rows, :]
        b_col = cb_ref[rows, :]
        li_col = cli_ref[rows, :]
        li_row = rli_ref[:, rows]
        b_row = rb_ref[:, rows]
        m = m_ref[...]
        a = b_col + m
        D = jnp.where(causal, b_col - b_row + li_row, -jnp.inf)
        m_t = jnp.maximum(a, jnp.max(D, axis=1, keepdims=True))
        w_inter = jnp.exp(a - m_t)
        P = lax.dot_general(qc, kc, (((1,), (1,)), ((), ())), preferred_element_type=F32) * jnp.exp(D - m_t)
        num = (w_inter * jnp.dot(qc, c_ref[...].astype(BF16), preferred_element_type=F32)
               + jnp.dot(P.astype(BF16), vc, preferred_element_type=F32))
        qn = (w_inter * jnp.sum(qc.astype(F32) * n_ref[...], axis=1, keepdims=True)
              + jnp.sum(P, axis=1, keepdims=True))
        den = jnp.maximum(jnp.abs(qn), jnp.exp(-m_t))
        hh = num / den
        ms = jnp.mean(hh * hh, axis=1, keepdims=True)
        hn = (hh * lax.rsqrt(ms + EPS)) * gn
        out_ref[rows, :] = (hn * _sigmoid(o_ref[rows, :].astype(F32))).astype(BF16)
        bL = b_row[:, L - 1:L]
        g_col = bL - b_col + li_col
        m_new = jnp.maximum(bL + m, jnp.max(g_col, axis=0, keepdims=True))
        decay = jnp.exp(bL + m - m_new)
        kw = kc.astype(F32) * jnp.exp(g_col - m_new)
        c_ref[...] = decay * c_ref[...] + jnp.dot(kw.T.astype(BF16), vc, preferred_element_type=F32)
        n_ref[...] = decay * n_ref[...] + jnp.sum(kw, axis=0, keepdims=True)
        m_ref[...] = m_new
        return carry
    lax.fori_loop(0, S // L, chunk, 0)


def _mlstm(z, zg, conv_qk, gate_bias, g_mlstm3, *, B, S, col_q, col_k, col_v, col_o):
    T = B * S
    dk = M_HEAD_DIM
    L = MLSTM_CHUNK
    zspec = lambda col: pl.BlockSpec((S, dk), lambda b, h: (b, col + h))
    return pl.pallas_call(
        functools.partial(_mlstm_body, L=L, CB=128),
        grid=(B, M_HEADS),
        in_specs=[
            zspec(col_q), zspec(col_k), zspec(col_v), zspec(col_o),
            pl.BlockSpec((S, LANES), lambda b, h: (b, 0)),
            pl.BlockSpec((CONV_WIDTH, dk), lambda b, h: (0, h)),
            pl.BlockSpec((CONV_WIDTH, dk), lambda b, h: (0, M_HEADS + h)),
            pl.BlockSpec((1, LANES), lambda b, h: (0, 0)),
            pl.BlockSpec((1, 1, dk), lambda b, h: (h, 0, 0)),
        ],
        out_specs=pl.BlockSpec((S, dk), lambda b, h: (b, h)),
        out_shape=jax.ShapeDtypeStruct((T, M_WIDTH), BF16),
        scratch_shapes=[
            pltpu.VMEM((S, dk), BF16), pltpu.VMEM((S, dk), BF16),
            pltpu.VMEM((S, LANES), F32), pltpu.VMEM((S, LANES), F32),
            pltpu.VMEM((8, S), F32), pltpu.VMEM((8, S), F32),
            pltpu.VMEM((S, 1), F32), pltpu.VMEM((S, 1), F32),
            pltpu.VMEM((1, S), F32), pltpu.VMEM((1, S), F32),
            pltpu.VMEM((dk, dk), F32), pltpu.VMEM((1, dk), F32), pltpu.VMEM((1, 1), F32),
        ],
        compiler_params=_params("parallel", "arbitrary"),
        name="mlstm",
    )(z, z, z, z, zg, conv_qk, conv_qk, gate_bias, g_mlstm3)


def _swa_body(sink_ref, q_ref, k_ref, v_ref, qa_ref, qb_ref, ka_ref, kb_ref, bd_ref, rep_ref, out_ref,
              kbd_ref, vbd_ref):
    n = pl.program_id(1)
    W = q_ref.shape[0]
    hd = A_HEAD_DIM
    gw = A_GROUP * hd

    def norm_rope(x, ta, tb):
        x2 = x * x
        x2h = x2.astype(BF16)
        x2l = (x2 - x2h.astype(F32)).astype(BF16)
        ss = (jnp.dot(x2h, bd_ref[...], preferred_element_type=F32)
              + jnp.dot(x2l, bd_ref[...], preferred_element_type=F32))
        r = lax.rsqrt(ss * (1.0 / hd) + EPS)
        ln = lax.broadcasted_iota(jnp.int32, x.shape, 1) % hd
        swapped = jnp.where(ln < hd // 2, pltpu.roll(x, LANES - hd // 2, axis=1), pltpu.roll(x, hd // 2, axis=1))
        return r * (x * ta + swapped * tb)

    qa = qa_ref[...]
    qb = qb_ref[...]
    qp = jnp.concatenate(
        [norm_rope(q_ref[:, t * LANES:(t + 1) * LANES].astype(F32), qa, qb) for t in range(A_WIDTH // LANES)],
        axis=1).astype(BF16)
    ka = ka_ref[...]
    kb = kb_ref[...]
    kp = jnp.concatenate(
        [norm_rope(k_ref[:, t * LANES:(t + 1) * LANES].astype(F32), ka, kb) for t in range(A_KV_WIDTH // LANES)],
        axis=1).astype(BF16)
    vv = v_ref[...]

    lane_head = lax.broadcasted_iota(jnp.int32, (W, gw), 1) // hd

    @pl.when(n == 0)
    def _():
        kbd_ref[0] = jnp.zeros(kbd_ref.shape[1:], BF16)
        vbd_ref[0] = jnp.zeros(vbd_ref.shape[1:], BF16)

    @pl.when(n > 0)
    def _():
        kbd_ref[0] = kbd_ref[1]
        vbd_ref[0] = vbd_ref[1]

    for j in range(A_KV_HEADS):
        krep = jnp.dot(kp, rep_ref[j], preferred_element_type=F32).astype(BF16)
        vrep = jnp.dot(vv, rep_ref[j], preferred_element_type=F32).astype(BF16)
        for i in range(A_GROUP):
            kbd_ref[1, j, i * W:(i + 1) * W, :] = jnp.where(lane_head == i, krep, jnp.zeros_like(krep))
            vbd_ref[1, j, i * W:(i + 1) * W, :] = jnp.where(lane_head == i, vrep, jnp.zeros_like(vrep))

    t_idx = lax.broadcasted_iota(jnp.int32, (W, W), 0)
    k_idx = lax.broadcasted_iota(jnp.int32, (W, W), 1)
    mask_cur = k_idx <= t_idx
    mask_prev = jnp.logical_and(k_idx > t_idx, n > 0)
    neg = -jnp.inf

    for j in range(A_KV_HEADS):
        qg = qp[:, j * gw:(j + 1) * gw]
        s_prev = lax.dot_general(qg, kbd_ref[0, j], (((1,), (1,)), ((), ())), preferred_element_type=F32)
        s_cur = lax.dot_general(qg, kbd_ref[1, j], (((1,), (1,)), ((), ())), preferred_element_type=F32)
        pp, pc = [], []
        for i in range(A_GROUP):
            sp = jnp.where(mask_prev, s_prev[:, i * W:(i + 1) * W], neg)
            sc = jnp.where(mask_cur, s_cur[:, i * W:(i + 1) * W], neg)
            sink = sink_ref[j * A_GROUP + i]
            mx = jnp.maximum(jnp.maximum(jnp.max(sp, axis=1, keepdims=True), jnp.max(sc, axis=1, keepdims=True)), sink)
            ep = jnp.exp(sp - mx)
            ec = jnp.exp(sc - mx)
            den = jnp.sum(ep, axis=1, keepdims=True) + jnp.sum(ec, axis=1, keepdims=True) + jnp.exp(sink - mx)
            inv = 1.0 / den
            pp.append((ep * inv).astype(BF16))
            pc.append((ec * inv).astype(BF16))
        o = (jnp.dot(jnp.concatenate(pp, axis=1), vbd_ref[0, j], preferred_element_type=F32)
             + jnp.dot(jnp.concatenate(pc, axis=1), vbd_ref[1, j], preferred_element_type=F32))
        out_ref[:, j * gw:(j + 1) * gw] = o.astype(BF16)


def _swa(z, sinks, qa, qb, ka, kb, bd, rep, *, B, S, col_q, col_k, col_v):
    T = B * S
    W = WINDOW
    NB = S // W
    gw = A_GROUP * A_HEAD_DIM
    tab = lambda: pl.BlockSpec((W, LANES), lambda b, n: (n, 0))
    return pl.pallas_call(
        _swa_body,
        grid=(B, NB),
        in_specs=[
            pl.BlockSpec(memory_space=pltpu.SMEM),
            pl.BlockSpec((W, A_WIDTH), lambda b, n: (b * NB + n, col_q)),
            pl.BlockSpec((W, A_KV_WIDTH), lambda b, n: (b * NB + n, col_k)),
            pl.BlockSpec((W, A_KV_WIDTH), lambda b, n: (b * NB + n, col_v)),
            tab(), tab(), tab(), tab(),
            pl.BlockSpec((LANES, LANES), lambda b, n: (0, 0)),
            pl.BlockSpec((A_KV_HEADS, gw, gw), lambda b, n: (0, 0, 0)),
        ],
        out_specs=pl.BlockSpec((W, A_WIDTH), lambda b, n: (b * NB + n, 0)),
        out_shape=jax.ShapeDtypeStruct((T, A_WIDTH), BF16),
        scratch_shapes=[
            pltpu.VMEM((2, A_KV_HEADS, A_GROUP * W, gw), BF16),
            pltpu.VMEM((2, A_KV_HEADS, A_GROUP * W, gw), BF16),
        ],
        compiler_params=_params("parallel", "arbitrary"),
        name="swa",
    )(sinks, z, z, z, qa, qb, ka, kb, bd, rep)


def _merge_body(hm_ref, ha_ref, gm_ref, ga_ref, x_ref, wm_ref, wa_ref, wo_ref, gf_ref, rh_ref, rl_ref, rb_ref,
                x1_ref, hf_ref, lg_ref):
    pm = jnp.dot(hm_ref[...], wm_ref[...], preferred_element_type=F32)
    pa = jnp.dot(ha_ref[...], wa_ref[...], preferred_element_type=F32)
    mixed = _sigmoid(gm_ref[...].astype(F32)) * pm + _sigmoid(ga_ref[...].astype(F32)) * pa
    x1 = x_ref[...] + jnp.dot(mixed.astype(BF16), wo_ref[...], preferred_element_type=F32)
    x1_ref[...] = x1
    ms = jnp.mean(x1 * x1, axis=-1, keepdims=True)
    hf = (x1 * lax.rsqrt(ms + EPS)) * gf_ref[...]
    _store_token_rows(hf_ref, hf)
    hh = hf.astype(BF16)
    hl = (hf - hh.astype(F32)).astype(BF16)
    lg_ref[...] = (jnp.dot(hh, rh_ref[...], preferred_element_type=F32)
                   + jnp.dot(hl, rh_ref[...], preferred_element_type=F32)
                   + jnp.dot(hh, rl_ref[...], preferred_element_type=F32)) + rb_ref[...]


def _merge(hm, ha, z, x2, wm, wa, wo, gf, rh, rl, rb, *, bm, col_gm, col_ga):
    T, D = x2.shape
    const = lambda shape: pl.BlockSpec(shape, lambda i: (0,) * len(shape), pipeline_mode=pl.Buffered(1))
    return pl.pallas_call(
        _merge_body,
        grid=(T // bm,),
        in_specs=[
            pl.BlockSpec((bm, M_WIDTH), lambda i: (i, 0)),
            pl.BlockSpec((bm, A_WIDTH), lambda i: (i, 0)),
            pl.BlockSpec((bm, D), lambda i: (i, col_gm)),
            pl.BlockSpec((bm, D), lambda i: (i, col_ga)),
            pl.BlockSpec((bm, D), lambda i: (i, 0)),
            const((M_WIDTH, D)), const((A_WIDTH, D)), const((D, D)), const((1, D)),
            const((D, LANES)), const((D, LANES)), const((1, LANES)),
        ],
        out_specs=[
            pl.BlockSpec((bm, D), lambda i: (i, 0)),
            pl.BlockSpec((bm * (D // LANES), LANES), lambda i: (i, 0)),
            pl.BlockSpec((bm, LANES), lambda i: (i, 0)),
        ],
        out_shape=[
            jax.ShapeDtypeStruct((T, D), F32),
            jax.ShapeDtypeStruct((T * (D // LANES), LANES), F32),
            jax.ShapeDtypeStruct((T, LANES), F32),
        ],
        compiler_params=_params("parallel"),
        name="merge",
    )(hm, ha, z, z, x2, wm, wa, wo, gf, rh, rl, rb)


def _moe_body(blk_e_ref, nused_ref, tok_ref, hf_hbm, wg_ref, wu_ref, wd_ref, y_ref, xbuf, xs_ref, sem, *, R):
    i = pl.program_id(0)
    nused = nused_ref[0]
    sub = xs_ref.shape[1] // LANES

    def start_gather(blk, slot):
        base = blk * R

        def row(r, carry):
            src = pl.multiple_of(tok_ref[base + r] * sub, sub)
            dst = pl.multiple_of(r * sub, sub)
            pltpu.make_async_copy(hf_hbm.at[pl.ds(src, sub)], xbuf.at[slot, pl.ds(dst, sub)], sem.at[slot]).start()
            return carry
        lax.fori_loop(0, R, row, 0, unroll=8)

    def wait_gather(slot):
        pltpu.make_async_copy(xbuf.at[slot], xbuf.at[slot], sem.at[slot]).wait()

    slot = i % 2

    @pl.when(i == 0)
    def _():
        start_gather(0, 0)

    @pl.when(i < nused)
    def _():
        wait_gather(slot)

        @pl.when(i + 1 < nused)
        def _():
            start_gather(i + 1, 1 - slot)

        for s in range(sub):
            xs_ref[:, s * LANES:(s + 1) * LANES] = _load_token_rows(xbuf.at[slot], R, sub, s).astype(BF16)
        xv = xs_ref[...]
        g = jnp.dot(xv, wg_ref[...], preferred_element_type=F32)
        u = jnp.dot(xv, wu_ref[...], preferred_element_type=F32)
        hmid = ((g * _sigmoid(g)) * u).astype(BF16)
        _store_token_rows(y_ref, jnp.dot(hmid, wd_ref[...], preferred_element_type=F32))


def _moe(blk_e, nused, row_tok, hf, wg, wu, wd, *, R):
    E, D, F = wg.shape
    sub = D // LANES
    n_blocks = blk_e.shape[0]
    wspec = lambda shape: pl.BlockSpec((None,) + shape, lambda i, be, nu, tk: (be[i], 0, 0))
    grid_spec = pltpu.PrefetchScalarGridSpec(
        num_scalar_prefetch=3,
        grid=(n_blocks,),
        in_specs=[
            pl.BlockSpec(memory_space=pl.ANY),
            wspec((D, F)), wspec((D, F)), wspec((F, D)),
        ],
        out_specs=pl.BlockSpec((R * sub, LANES), lambda i, be, nu, tk: (jnp.minimum(i, nu[0] - 1), 0)),
        scratch_shapes=[pltpu.VMEM((2, R * sub, LANES), F32), pltpu.VMEM((R, D), BF16),
                        pltpu.SemaphoreType.DMA((2,))],
    )
    return pl.pallas_call(
        functools.partial(_moe_body, R=R),
        grid_spec=grid_spec,
        out_shape=jax.ShapeDtypeStruct((n_blocks * R * sub, LANES), F32),
        compiler_params=_params("arbitrary"),
        name="moe",
    )(blk_e, nused, row_tok, hf, wg, wu, wd)


def _combine_body(dest_ref, x1_ref, w_ref, yr_hbm, out_ref, ybuf, sem, *, R):
    i = pl.program_id(0)
    nsteps = pl.num_programs(0)

    sub = out_ref.shape[1] // LANES

    def start_gather(step, slot):
        base = step * (R * TOP_K)

        def row(r, carry):
            dst = pl.multiple_of(r * sub, sub)
            for k in range(TOP_K):
                src = pl.multiple_of(dest_ref[base + r * TOP_K + k] * sub, sub)
                pltpu.make_async_copy(yr_hbm.at[pl.ds(src, sub)], ybuf.at[slot, k, pl.ds(dst, sub)],
                                      sem.at[slot]).start()
            return carry
        lax.fori_loop(0, R, row, 0, unroll=4)

    slot = i % 2

    @pl.when(i == 0)
    def _():
        start_gather(0, 0)

    pltpu.make_async_copy(ybuf.at[slot], ybuf.at[slot], sem.at[slot]).wait()

    @pl.when(i + 1 < nsteps)
    def _():
        start_gather(i + 1, 1 - slot)

    w = w_ref[...]
    for s in range(sub):
        cols = slice(s * LANES, (s + 1) * LANES)
        y = (_load_token_rows(ybuf.at[slot, 0], R, sub, s) * w[:, 0:1]
             + _load_token_rows(ybuf.at[slot, 1], R, sub, s) * w[:, 1:2])
        out_ref[:, cols] = x1_ref[:, cols] + y


def _combine(dest, x1, gate_w, yr, *, R):
    T, D = x1.shape
    grid_spec = pltpu.PrefetchScalarGridSpec(
        num_scalar_prefetch=1,
        grid=(T // R,),
        in_specs=[
            pl.BlockSpec((R, D), lambda i, d: (i, 0)),
            pl.BlockSpec((R, TOP_K), lambda i, d: (i, 0)),
            pl.BlockSpec(memory_space=pl.ANY),
        ],
        out_specs=pl.BlockSpec((R, D), lambda i, d: (i, 0)),
        scratch_shapes=[pltpu.VMEM((2, TOP_K, R * (D // LANES), LANES), F32), pltpu.SemaphoreType.DMA((2,))],
    )
    return pl.pallas_call(
        functools.partial(_combine_body, R=R),
        grid_spec=grid_spec,
        out_shape=jax.ShapeDtypeStruct((T, D), F32),
        compiler_params=_params("arbitrary"),
        name="combine",
    )(dest, x1, gate_w, yr)


def _route_body(lg_ref, tri_ref, ids_ref, gate_ref, cnt_ref):
    @pl.when(pl.program_id(0) == 0)
    def _():
        cnt_ref[...] = jnp.zeros_like(cnt_ref)

    lg = lg_ref[...]
    shape = lg.shape
    lane = lax.broadcasted_iota(jnp.int32, shape, 1)
    big = jnp.int32(LANES)

    def softmax_masked(mask):
        v = jnp.where(mask, lg, -jnp.inf)
        u = jnp.exp(v - jnp.max(v, axis=1, keepdims=True))
        return jnp.where(mask, u / jnp.sum(u, axis=1, keepdims=True), -1.0)

    def top1(p):
        best = jnp.max(p, axis=1, keepdims=True)
        idx = jnp.min(jnp.where(p == best, lane, big), axis=1, keepdims=True)
        return best, idx

    g_p, g_lane = top1(softmax_masked(lane < N_GROUPS))
    grp_of_lane = lax.shift_right_arithmetic(lane - N_GROUPS, jnp.int32(3))
    in_grp = jnp.logical_and(lane >= N_GROUPS, grp_of_lane == g_lane)
    in_grp = jnp.logical_and(in_grp, lane < N_GROUPS + N_EXPERTS)
    pe = softmax_masked(in_grp)
    p1, l1 = top1(pe)
    p2, l2 = top1(jnp.where(lane == l1, -1.0, pe))
    tot = p1 + p2
    gate1 = g_p * (p1 / tot)
    gate2 = g_p * (p2 / tot)

    hot1 = lane == l1
    hot2 = lane == l2
    hot = jnp.logical_or(hot1, hot2)
    before = jnp.dot(tri_ref[...], jnp.where(hot, 1.0, 0.0).astype(BF16), preferred_element_type=F32) + cnt_ref[...]
    rank1 = jnp.sum(jnp.where(hot1, before, 0.0), axis=1, keepdims=True)
    rank2 = jnp.sum(jnp.where(hot2, before, 0.0), axis=1, keepdims=True)
    cnt_ref[...] = cnt_ref[...] + jnp.sum(jnp.where(hot, 1.0, 0.0), axis=0, keepdims=True)

    ids = jnp.where(lane == 0, l1 - N_GROUPS, jnp.where(lane == 1, l2 - N_GROUPS, 0))
    ids = jnp.where(lane == 2, rank1.astype(jnp.int32), jnp.where(lane == 3, rank2.astype(jnp.int32), ids))
    ids_ref[...] = ids
    gate_ref[...] = jnp.where(lane == 0, gate1, jnp.where(lane == 1, gate2, 0.0))


def _route_tokens(logits, *, tb):
    T = logits.shape[0]
    r = jnp.arange(tb)
    tri = (r[None, :] < r[:, None]).astype(BF16)
    return pl.pallas_call(
        _route_body,
        grid=(T // tb,),
        in_specs=[
            pl.BlockSpec((tb, LANES), lambda i: (i, 0)),
            pl.BlockSpec((tb, tb), lambda i: (0, 0)),
        ],
        out_specs=[
            pl.BlockSpec((tb, LANES), lambda i: (i, 0)),
            pl.BlockSpec((tb, LANES), lambda i: (i, 0)),
            pl.BlockSpec((1, LANES), lambda i: (0, 0)),
        ],
        out_shape=[
            jax.ShapeDtypeStruct((T, LANES), jnp.int32),
            jax.ShapeDtypeStruct((T, LANES), F32),
            jax.ShapeDtypeStruct((1, LANES), F32),
        ],
        compiler_params=_params("arbitrary"),
        name="route",
    )(logits, tri)


def _route(logits, R):
    T = logits.shape[0]
    ids, gates, cnt = _route_tokens(logits, tb=256)
    gate = gates[:, 0:TOP_K]
    M = T * TOP_K
    eid_f = ids[:, 0:TOP_K].reshape(M)
    rank = ids[:, TOP_K:2 * TOP_K].reshape(M)
    counts = cnt[0, N_GROUPS:N_GROUPS + N_EXPERTS].astype(jnp.int32)
    padded = (counts + R - 1) // R * R
    pend = jnp.cumsum(padded)
    pstart = pend - padded
    dest = (pstart[eid_f] + rank).astype(jnp.int32)
    n_blocks = -(-M // R) + N_EXPERTS
    tok_f = jnp.arange(M, dtype=jnp.int32) // TOP_K
    row_tok = jnp.zeros((n_blocks * R,), jnp.int32).at[dest].set(tok_f)
    blk_start = jnp.arange(n_blocks, dtype=jnp.int32) * R
    blk_e = jnp.sum((pend[None, :] <= blk_start[:, None]).astype(jnp.int32), axis=1)
    nused = (pend[-1] // R).astype(jnp.int32)
    last_e = blk_e[jnp.maximum(nused - 1, 0)]
    blk_e = jnp.where(jnp.arange(n_blocks) < nused, blk_e, last_e)
    blk_e = jnp.minimum(blk_e, N_EXPERTS - 1).astype(jnp.int32)
    return gate, dest, row_tok, blk_e, nused.reshape(1)


def _rope_tables(S, gain, scale):
    half = A_HEAD_DIM // 2
    freqs = ROPE_THETA ** (-jnp.arange(half, dtype=F32) / half)
    ang = jnp.arange(S, dtype=F32)[:, None] * freqs[None, :]
    cos = jnp.cos(ang)
    sin = jnp.sin(ang)
    g1, g2 = gain[:half], gain[half:]
    a_head = jnp.concatenate([cos * g1, cos * g2], axis=1)
    b_head = jnp.concatenate([-sin * g2, sin * g1], axis=1)
    reps = LANES // A_HEAD_DIM
    return jnp.tile(a_head, (1, reps)) * scale, jnp.tile(b_head, (1, reps)) * scale


def kernel(x, g_mix, w_in, conv_qk, b_igate, b_fgate, g_mlstm, g_q, g_k, sinks, w_proj_m, w_proj_a, w_out,
           g_ffn, w_group, b_group, w_expert, b_expert, w_gate, w_up, w_down):
    B, S, D = x.shape
    T = B * S
    depth = g_mix.shape[0]
    xf = x.reshape(T, D)

    sizes = (M_WIDTH, M_WIDTH, M_WIDTH, M_WIDTH, M_HEADS, M_HEADS, A_WIDTH, A_KV_WIDTH, A_KV_WIDTH, D, D)
    offs = [0]
    for s_ in sizes:
        offs.append(offs[-1] + s_)
    seg = lambda w, idx: w[:, offs[idx]:offs[idx + 1]]
    order = (9, 10, 0, 1, 2, 3, 6, 7, 8)
    new_off = {}
    acc = 0
    for idx in order:
        new_off[idx] = acc
        acc += sizes[idx]
    dk = M_HEAD_DIM

    lane128 = jnp.arange(LANES)
    bd = (lane128[:, None] // A_HEAD_DIM == lane128[None, :] // A_HEAD_DIM).astype(BF16)
    gw = A_GROUP * A_HEAD_DIM
    r_idx = jnp.arange(gw)
    rep = jnp.stack([(r_idx[:, None] == (j * A_HEAD_DIM + r_idx[None, :] % A_HEAD_DIM)).astype(BF16)
                     for j in range(A_KV_HEADS)])

    for l in range(depth):
        w_rep = jnp.concatenate([seg(w_in[l], idx) for idx in order], axis=1).astype(BF16)
        w_gates = jnp.concatenate([seg(w_in[l], 4), seg(w_in[l], 5)], axis=1)
        w_gates = jnp.pad(w_gates, ((0, 0), (0, LANES - 2 * M_HEADS))).astype(BF16)
        z, zg = _in_proj(xf, g_mix[l][None, :], w_rep, w_gates, bm=1024 if T % 1024 == 0 else T, bn=512)

        gate_bias = jnp.pad(jnp.concatenate([b_igate[l], b_fgate[l]]), (0, LANES - 2 * M_HEADS))[None, :]
        hm = _mlstm(z, zg, conv_qk[l], gate_bias, g_mlstm[l][:, None, :], B=B, S=S,
                    col_q=new_off[0] // dk, col_k=new_off[1] // dk, col_v=new_off[2] // dk, col_o=new_off[3] // dk)

        qa, qb = _rope_tables(S, g_q[l], A_HEAD_DIM ** -0.5)
        ka, kb = _rope_tables(S, g_k[l], 1.0)
        ha = _swa(z, sinks[l], qa, qb, ka, kb, bd, rep, B=B, S=S,
                  col_q=new_off[6] // A_WIDTH, col_k=new_off[7] // A_KV_WIDTH, col_v=new_off[8] // A_KV_WIDTH)

        w_router = jnp.pad(jnp.concatenate([w_group[l], w_expert[l]], axis=1),
                           ((0, 0), (0, LANES - N_GROUPS - N_EXPERTS)))
        r_hi = w_router.astype(BF16)
        r_lo = (w_router - r_hi.astype(F32)).astype(BF16)
        r_b = jnp.pad(jnp.concatenate([b_group[l], b_expert[l]]), (0, LANES - N_GROUPS - N_EXPERTS))[None, :]
        x1, hf, logits = _merge(hm, ha, z, xf, w_proj_m[l].astype(BF16), w_proj_a[l].astype(BF16),
                                w_out[l].astype(BF16), g_ffn[l][None, :], r_hi, r_lo, r_b,
                                bm=256, col_gm=new_off[9] // D, col_ga=new_off[10] // D)

        gate, dest, row_tok, blk_e, nused = _route(logits, MOE_ROWS)
        yr = _moe(blk_e, nused, row_tok, hf, w_gate[l].astype(BF16), w_up[l].astype(BF16),
                  w_down[l].astype(BF16), R=MOE_ROWS)
        xf = _combine(dest, x1, gate, yr, R=COMBINE_ROWS)
    return xf.reshape(B, S, D)
```

```python
import functools

import jax
import jax.numpy as jnp
from jax import lax
from jax.experimental import pallas as pl
from jax.experimental.pallas import tpu as pltpu

F32 = jnp.float32
BF16 = jnp.bfloat16
EPS = 1e-6
LANES = 128
VMEM_LIMIT = 56 * 1024 * 1024

M_HEADS = 4
M_HEAD_DIM = 256
M_WIDTH = M_HEADS * M_HEAD_DIM
CONV_WIDTH = 4
A_HEADS = 16
A_KV_HEADS = 4
A_GROUP = A_HEADS // A_KV_HEADS
A_HEAD_DIM = 64
A_WIDTH = A_HEADS * A_HEAD_DIM
A_KV_WIDTH = A_KV_HEADS * A_HEAD_DIM
WINDOW = 128
ROPE_THETA = 10000.0
N_GROUPS = 8
EXPERTS_PER_GROUP = 8
N_EXPERTS = N_GROUPS * EXPERTS_PER_GROUP
TOP_K = 2

MLSTM_CHUNK = 128
MOE_ROWS = 256
COMBINE_ROWS = 128


def _sigmoid(v):
    return 1.0 / (1.0 + jnp.exp(-v))


def _params(*sem):
    return pltpu.CompilerParams(dimension_semantics=sem, vmem_limit_bytes=VMEM_LIMIT)


def _store_token_rows(ref, val):
    n, d = val.shape
    sub = d // LANES
    for s in range(sub):
        ref[pl.ds(s, n, stride=sub), :] = val[:, s * LANES:(s + 1) * LANES]


def _load_token_rows(ref, n, sub, s):
    return ref[pl.ds(s, n, stride=sub), :]


def _inproj_body(x_ref, g_ref, w_ref, wgate_ref, z_ref, zg_ref, hn_ref, *, sub):
    bm = x_ref.shape[0]

    @pl.when(pl.program_id(1) == 0)
    def _():
        def rows(r, carry):
            sl = pl.ds(pl.multiple_of(r * sub, sub), sub)
            xv = x_ref[sl, :]
            ms = jnp.mean(xv * xv, axis=-1, keepdims=True)
            hn_ref[sl, :] = ((xv * lax.rsqrt(ms + EPS)) * g_ref[...]).astype(BF16)
            return carry
        lax.fori_loop(0, bm // sub, rows, 0)
        zg_ref[...] = jnp.dot(hn_ref[...], wgate_ref[...], preferred_element_type=F32)

    z_ref[...] = jnp.dot(hn_ref[...], w_ref[...], preferred_element_type=F32).astype(BF16)


def _in_proj(x2, g, w, wgate, *, bm, bn):
    T, D = x2.shape
    N = w.shape[1]
    return pl.pallas_call(
        functools.partial(_inproj_body, sub=128),
        grid=(T // bm, N // bn),
        in_specs=[
            pl.BlockSpec((bm, D), lambda i, j: (i, 0)),
            pl.BlockSpec((1, D), lambda i, j: (0, 0)),
            pl.BlockSpec((D, bn), lambda i, j: (0, j)),
            pl.BlockSpec((D, LANES), lambda i, j: (0, 0)),
        ],
        out_specs=[
            pl.BlockSpec((bm, bn), lambda i, j: (i, j)),
            pl.BlockSpec((bm, LANES), lambda i, j: (i, 0)),
        ],
        out_shape=[
            jax.ShapeDtypeStruct((T, N), BF16),
            jax.ShapeDtypeStruct((T, LANES), F32),
        ],
        scratch_shapes=[pltpu.VMEM((bm, D), BF16)],
        compiler_params=_params("parallel", "arbitrary"),
        name="in_proj",
    )(x2, g, w, wgate)


def _mlstm_body(q_ref, k_ref, v_ref, o_ref, zg_ref, cq_ref, ck_ref, bias_ref, gn_ref, out_ref,
                qs_ref, ks_ref, colli_ref, colb_ref, rowli_ref, rowb_ref, cli_ref, cb_ref, rli_ref, rb_ref,
                c_ref, n_ref, m_ref, *, L, CB):
    h = pl.program_id(1)
    S, dk = q_ref.shape

    @pl.when(h == 0)
    def _():
        G = zg_ref[...] + bias_ref[...]
        lf = jnp.minimum(G, 0.0) - jnp.log1p(jnp.exp(-jnp.abs(G)))
        pos = lax.broadcasted_iota(jnp.int32, (S, LANES), 0) % L
        bc = lf
        sh = 1
        while sh < L:
            bc = bc + jnp.where(pos >= sh, pltpu.roll(bc, sh, axis=0), 0.0)
            sh *= 2
        colli_ref[...] = G
        colb_ref[...] = bc
        for p in range(S // LANES):
            sl = slice(p * LANES, (p + 1) * LANES)
            rowli_ref[:, sl] = G[sl, :].T[0:8, :]
            rowb_ref[:, sl] = bc[sl, :].T[0:8, :]

    lane = lax.broadcasted_iota(jnp.int32, (S, LANES), 1)
    cli_ref[...] = jnp.sum(jnp.where(lane == h, colli_ref[...], 0.0), axis=1, keepdims=True)
    cb_ref[...] = jnp.sum(jnp.where(lane == h + M_HEADS, colb_ref[...], 0.0), axis=1, keepdims=True)
    sub = lax.broadcasted_iota(jnp.int32, (8, S), 0)
    rli_ref[...] = jnp.sum(jnp.where(sub == h, rowli_ref[...], 0.0), axis=0, keepdims=True)
    rb_ref[...] = jnp.sum(jnp.where(sub == h + M_HEADS, rowb_ref[...], 0.0), axis=0, keepdims=True)

    def conv_silu(src_ref, w_ref, dst_ref, scale):
        w = w_ref[...]

        def taps(xs):
            y = xs[3] * w[0:1, :]
            y = y + xs[2] * w[1:2, :]
            y = y + xs[1] * w[2:3, :]
            y = y + xs[0] * w[3:4, :]
            return ((y * _sigmoid(y)) * scale).astype(BF16)

        def chunk(r, carry):
            r0 = pl.multiple_of(r * CB, CB)
            cur = src_ref[pl.ds(r0, CB), :].astype(F32)
            dst_ref[pl.ds(r0, CB), :] = taps([cur] + [pltpu.roll(cur, d, axis=0) for d in (1, 2, 3)])
            p0 = pl.multiple_of(jnp.maximum(r0 - 16, 0), 16)
            prev = src_ref[pl.ds(p0, 16), :].astype(F32)[8:16, :]
            prev = jnp.where(r > 0, prev, 0.0)
            both = jnp.concatenate([prev, cur[0:16, :]], axis=0)
            dst_ref[pl.ds(r0, 16), :] = taps([both[8:24, :]] + [pltpu.roll(both, d, axis=0)[8:24, :] for d in (1, 2, 3)])
            return carry
        lax.fori_loop(0, S // CB, chunk, 0)

    conv_silu(q_ref, cq_ref, qs_ref, 1.0)
    conv_silu(k_ref, ck_ref, ks_ref, float(dk) ** -0.5)

    c_ref[...] = jnp.zeros_like(c_ref)
    n_ref[...] = jnp.zeros_like(n_ref)
    m_ref[...] = jnp.zeros_like(m_ref)
    t_idx = lax.broadcasted_iota(jnp.int32, (L, L), 0)
    s_idx = lax.broadcasted_iota(jnp.int32, (L, L), 1)
    causal = s_idx <= t_idx
    gn = gn_ref[0]

    def chunk(c, carry):
        r0 = pl.multiple_of(c * L, L)
        rows = pl.ds(r0, L)
        qc = qs_ref[rows, :]
        kc = ks_ref[rows, :]
        vc = v_ref[rows, :]
        b_col = cb_ref[rows, :]
        li_col = cli_ref[rows, :]
        li_row = rli_ref[:, rows]
        b_row = rb_ref[:, rows]
        m = m_ref[...]
        a = b_col + m
        D = jnp.where(causal, b_col - b_row + li_row, -jnp.inf)
        m_t = jnp.maximum(a, jnp.max(D, axis=1, keepdims=True))
        w_inter = jnp.exp(a - m_t)
        P = lax.dot_general(qc, kc, (((1,), (1,)), ((), ())), preferred_element_type=F32) * jnp.exp(D - m_t)
        num = (w_inter * jnp.dot(qc, c_ref[...].astype(BF16), preferred_element_type=F32)
               + jnp.dot(P.astype(BF16), vc, preferred_element_type=F32))
        qn = (w_inter * jnp.sum(qc.astype(F32) * n_ref[...], axis=1, keepdims=True)
              + jnp.sum(P, axis=1, keepdims=True))
        den = jnp.maximum(jnp.abs(qn), jnp.exp(-m_t))
        hh = num / den
        ms = jnp.mean(hh * hh, axis=1, keepdims=True)
        hn = (hh * lax.rsqrt(ms + EPS)) * gn
        out_ref[rows, :] = (hn * _sigmoid(o_ref[rows, :].astype(F32))).astype(BF16)
        bL = b_row[:, L - 1:L]
        g_col = bL - b_col + li_col
        m_new = jnp.maximum(bL + m, jnp.max(g_col, axis=0, keepdims=True))
        decay = jnp.exp(bL + m - m_new)
        kw = kc.astype(F32) * jnp.exp(g_col - m_new)
        c_ref[...] = decay * c_ref[...] + jnp.dot(kw.T.astype(BF16), vc, preferred_element_type=F32)
        n_ref[...] = decay * n_ref[...] + jnp.sum(kw, axis=0, keepdims=True)
        m_ref[...] = m_new
        return carry
    lax.fori_loop(0, S // L, chunk, 0)


def _mlstm(z, zg, conv_qk, gate_bias, g_mlstm3, *, B, S, col_q, col_k, col_v, col_o):
    T = B * S
    dk = M_HEAD_DIM
    L = MLSTM_CHUNK
    zspec = lambda col: pl.BlockSpec((S, dk), lambda b, h: (b, col + h))
    return pl.pallas_call(
        functools.partial(_mlstm_body, L=L, CB=128),
        grid=(B, M_HEADS),
        in_specs=[
            zspec(col_q), zspec(col_k), zspec(col_v), zspec(col_o),
            pl.BlockSpec((S, LANES), lambda b, h: (b, 0)),
            pl.BlockSpec((CONV_WIDTH, dk), lambda b, h: (0, h)),
            pl.BlockSpec((CONV_WIDTH, dk), lambda b, h: (0, M_HEADS + h)),
            pl.BlockSpec((1, LANES), lambda b, h: (0, 0)),
            pl.BlockSpec((1, 1, dk), lambda b, h: (h, 0, 0)),
        ],
        out_specs=pl.BlockSpec((S, dk), lambda b, h: (b, h)),
        out_shape=jax.ShapeDtypeStruct((T, M_WIDTH), BF16),
        scratch_shapes=[
            pltpu.VMEM((S, dk), BF16), pltpu.VMEM((S, dk), BF16),
            pltpu.VMEM((S, LANES), F32), pltpu.VMEM((S, LANES), F32),
            pltpu.VMEM((8, S), F32), pltpu.VMEM((8, S), F32),
            pltpu.VMEM((S, 1), F32), pltpu.VMEM((S, 1), F32),
            pltpu.VMEM((1, S), F32), pltpu.VMEM((1, S), F32),
            pltpu.VMEM((dk, dk), F32), pltpu.VMEM((1, dk), F32), pltpu.VMEM((1, 1), F32),
        ],
        compiler_params=_params("parallel", "arbitrary"),
        name="mlstm",
    )(z, z, z, z, zg, conv_qk, conv_qk, gate_bias, g_mlstm3)


def _swa_body(sink_ref, q_ref, k_ref, v_ref, qa_ref, qb_ref, ka_ref, kb_ref, bd_ref, rep_ref, out_ref,
              kbd_ref, vbd_ref):
    n = pl.program_id(1)
    W = q_ref.shape[0]
    hd = A_HEAD_DIM
    gw = A_GROUP * hd

    def norm_rope(x, ta, tb):
        x2 = x * x
        x2h = x2.astype(BF16)
        x2l = (x2 - x2h.astype(F32)).astype(BF16)
        ss = (jnp.dot(x2h, bd_ref[...], preferred_element_type=F32)
              + jnp.dot(x2l, bd_ref[...], preferred_element_type=F32))
        r = lax.rsqrt(ss * (1.0 / hd) + EPS)
        ln = lax.broadcasted_iota(jnp.int32, x.shape, 1) % hd
        swapped = jnp.where(ln < hd // 2, pltpu.roll(x, LANES - hd // 2, axis=1), pltpu.roll(x, hd // 2, axis=1))
        return r * (x * ta + swapped * tb)

    qa = qa_ref[...]
    qb = qb_ref[...]
    qp = jnp.concatenate(
        [norm_rope(q_ref[:, t * LANES:(t + 1) * LANES].astype(F32), qa, qb) for t in range(A_WIDTH // LANES)],
        axis=1).astype(BF16)
    ka = ka_ref[...]
    kb = kb_ref[...]
    kp = jnp.concatenate(
        [norm_rope(k_ref[:, t * LANES:(t + 1) * LANES].astype(F32), ka, kb) for t in range(A_KV_WIDTH // LANES)],
        axis=1).astype(BF16)
    vv = v_ref[...]

    lane_head = lax.broadcasted_iota(jnp.int32, (W, gw), 1) // hd

    @pl.when(n == 0)
    def _():
        kbd_ref[0] = jnp.zeros(kbd_ref.shape[1:], BF16)
        vbd_ref[0] = jnp.zeros(vbd_ref.shape[1:], BF16)

    @pl.when(n > 0)
    def _():
        kbd_ref[0] = kbd_ref[1]
        vbd_ref[0] = vbd_ref[1]

    for j in range(A_KV_HEADS):
        krep = jnp.dot(kp, rep_ref[j], preferred_element_type=F32).astype(BF16)
        vrep = jnp.dot(vv, rep_ref[j], preferred_element_type=F32).astype(BF16)
        for i in range(A_GROUP):
            kbd_ref[1, j, i * W:(i + 1) * W, :] = jnp.where(lane_head == i, krep, jnp.zeros_like(krep))
            vbd_ref[1, j, i * W:(i + 1) * W, :] = jnp.where(lane_head == i, vrep, jnp.zeros_like(vrep))

    t_idx = lax.broadcasted_iota(jnp.int32, (W, W), 0)
    k_idx = lax.broadcasted_iota(jnp.int32, (W, W), 1)
    mask_cur = k_idx <= t_idx
    mask_prev = jnp.logical_and(k_idx > t_idx, n > 0)
    neg = -jnp.inf

    for j in range(A_KV_HEADS):
        qg = qp[:, j * gw:(j + 1) * gw]
        s_prev = lax.dot_general(qg, kbd_ref[0, j], (((1,), (1,)), ((), ())), preferred_element_type=F32)
        s_cur = lax.dot_general(qg, kbd_ref[1, j], (((1,), (1,)), ((), ())), preferred_element_type=F32)
        pp, pc = [], []
        for i in range(A_GROUP):
            sp = jnp.where(mask_prev, s_prev[:, i * W:(i + 1) * W], neg)
            sc = jnp.where(mask_cur, s_cur[:, i * W:(i + 1) * W], neg)
            sink = sink_ref[j * A_GROUP + i]
            mx = jnp.maximum(jnp.maximum(jnp.max(sp, axis=1, keepdims=True), jnp.max(sc, axis=1, keepdims=True)), sink)
            ep = jnp.exp(sp - mx)
            ec = jnp.exp(sc - mx)
            den = jnp.sum(ep, axis=1, keepdims=True) + jnp.sum(ec, axis=1, keepdims=True) + jnp.exp(sink - mx)
            inv = 1.0 / den
            pp.append((ep * inv).astype(BF16))
            pc.append((ec * inv).astype(BF16))
        o = (jnp.dot(jnp.concatenate(pp, axis=1), vbd_ref[0, j], preferred_element_type=F32)
             + jnp.dot(jnp.concatenate(pc, axis=1), vbd_ref[1, j], preferred_element_type=F32))
        out_ref[:, j * gw:(j + 1) * gw] = o.astype(BF16)


def _swa(z, sinks, qa, qb, ka, kb, bd, rep, *, B, S, col_q, col_k, col_v):
    T = B * S
    W = WINDOW
    NB = S // W
    gw = A_GROUP * A_HEAD_DIM
    tab = lambda: pl.BlockSpec((W, LANES), lambda b, n: (n, 0))
    return pl.pallas_call(
        _swa_body,
        grid=(B, NB),
        in_specs=[
            pl.BlockSpec(memory_space=pltpu.SMEM),
            pl.BlockSpec((W, A_WIDTH), lambda b, n: (b * NB + n, col_q)),
            pl.BlockSpec((W, A_KV_WIDTH), lambda b, n: (b * NB + n, col_k)),
            pl.BlockSpec((W, A_KV_WIDTH), lambda b, n: (b * NB + n, col_v)),
            tab(), tab(), tab(), tab(),
            pl.BlockSpec((LANES, LANES), lambda b, n: (0, 0)),
            pl.BlockSpec((A_KV_HEADS, gw, gw), lambda b, n: (0, 0, 0)),
        ],
        out_specs=pl.BlockSpec((W, A_WIDTH), lambda b, n: (b * NB + n, 0)),
        out_shape=jax.ShapeDtypeStruct((T, A_WIDTH), BF16),
        scratch_shapes=[
            pltpu.VMEM((2, A_KV_HEADS, A_GROUP * W, gw), BF16),
            pltpu.VMEM((2, A_KV_HEADS, A_GROUP * W, gw), BF16),
        ],
        compiler_params=_params("parallel", "arbitrary"),
        name="swa",
    )(sinks, z, z, z, qa, qb, ka, kb, bd, rep)


def _merge_body(hm_ref, ha_ref, gm_ref, ga_ref, x_ref, wm_ref, wa_ref, wo_ref, gf_ref, rh_ref, rl_ref, rb_ref,
                x1_ref, hf_ref, lg_ref):
    pm = jnp.dot(hm_ref[...], wm_ref[...], preferred_element_type=F32)
    pa = jnp.dot(ha_ref[...], wa_ref[...], preferred_element_type=F32)
    mixed = _sigmoid(gm_ref[...].astype(F32)) * pm + _sigmoid(ga_ref[...].astype(F32)) * pa
    x1 = x_ref[...] + jnp.dot(mixed.astype(BF16), wo_ref[...], preferred_element_type=F32)
    x1_ref[...] = x1
    ms = jnp.mean(x1 * x1, axis=-1, keepdims=True)
    hf = (x1 * lax.rsqrt(ms + EPS)) * gf_ref[...]
    _store_token_rows(hf_ref, hf)
    hh = hf.astype(BF16)
    hl = (hf - hh.astype(F32)).astype(BF16)
    lg_ref[...] = (jnp.dot(hh, rh_ref[...], preferred_element_type=F32)
                   + jnp.dot(hl, rh_ref[...], preferred_element_type=F32)
                   + jnp.dot(hh, rl_ref[...], preferred_element_type=F32)) + rb_ref[...]


def _merge(hm, ha, z, x2, wm, wa, wo, gf, rh, rl, rb, *, bm, col_gm, col_ga):
    T, D = x2.shape
    const = lambda shape: pl.BlockSpec(shape, lambda i: (0,) * len(shape), pipeline_mode=pl.Buffered(1))
    return pl.pallas_call(
        _merge_body,
        grid=(T // bm,),
        in_specs=[
            pl.BlockSpec((bm, M_WIDTH), lambda i: (i, 0)),
            pl.BlockSpec((bm, A_WIDTH), lambda i: (i, 0)),
            pl.BlockSpec((bm, D), lambda i: (i, col_gm)),
            pl.BlockSpec((bm, D), lambda i: (i, col_ga)),
            pl.BlockSpec((bm, D), lambda i: (i, 0)),
            const((M_WIDTH, D)), const((A_WIDTH, D)), const((D, D)), const((1, D)),
            const((D, LANES)), const((D, LANES)), const((1, LANES)),
        ],
        out_specs=[
            pl.BlockSpec((bm, D), lambda i: (i, 0)),
            pl.BlockSpec((bm * (D // LANES), LANES), lambda i: (i, 0)),
            pl.BlockSpec((bm, LANES), lambda i: (i, 0)),
        ],
        out_shape=[
            jax.ShapeDtypeStruct((T, D), F32),
            jax.ShapeDtypeStruct((T * (D // LANES), LANES), F32),
            jax.ShapeDtypeStruct((T, LANES), F32),
        ],
        compiler_params=_params("parallel"),
        name="merge",
    )(hm, ha, z, z, x2, wm, wa, wo, gf, rh, rl, rb)


def _moe_body(blk_e_ref, first_ref, nxt_ref, nused_ref, tok_ref, hf_hbm, wg_hbm, wu_hbm, wd_hbm, y_ref,
              xbuf, xs_ref, stage_g, stage_u, stage_d, wg_ref, wu_ref, wd_ref, sem, wsem, *, R, CB):
    i = pl.program_id(0)
    nused = nused_ref[0]
    sub = xs_ref.shape[1] // LANES

    def weight_copies(e):
        return (pltpu.make_async_copy(wg_hbm.at[e], stage_g, wsem.at[0]),
                pltpu.make_async_copy(wu_hbm.at[e], stage_u, wsem.at[1]),
                pltpu.make_async_copy(wd_hbm.at[e], stage_d, wsem.at[2]))

    def cast(src, dst):
        def rows(r, carry):
            sl = pl.ds(pl.multiple_of(r * CB, CB), CB)
            dst[sl, :] = src[sl, :].astype(BF16)
            return carry
        lax.fori_loop(0, src.shape[0] // CB, rows, 0)

    def start_gather(blk, slot):
        base = blk * R

        def row(r, carry):
            src = pl.multiple_of(tok_ref[base + r] * sub, sub)
            dst = pl.multiple_of(r * sub, sub)
            pltpu.make_async_copy(hf_hbm.at[pl.ds(src, sub)], xbuf.at[slot, pl.ds(dst, sub)], sem.at[slot]).start()
            return carry
        lax.fori_loop(0, R, row, 0, unroll=8)

    def wait_gather(slot):
        pltpu.make_async_copy(xbuf.at[slot], xbuf.at[slot], sem.at[slot]).wait()

    slot = i % 2

    @pl.when(i == 0)
    def _():
        for c in weight_copies(blk_e_ref[0]):
            c.start()
        start_gather(0, 0)

    @pl.when(i < nused)
    def _():
        @pl.when(first_ref[i] == 1)
        def _():
            for c in weight_copies(blk_e_ref[i]):
                c.wait()
            cast(stage_g, wg_ref)
            cast(stage_u, wu_ref)
            cast(stage_d, wd_ref)

            @pl.when(nxt_ref[i] >= 0)
            def _():
                for c in weight_copies(nxt_ref[i]):
                    c.start()

        wait_gather(slot)

        @pl.when(i + 1 < nused)
        def _():
            start_gather(i + 1, 1 - slot)

        for s in range(sub):
            xs_ref[:, s * LANES:(s + 1) * LANES] = _load_token_rows(xbuf.at[slot], R, sub, s).astype(BF16)
        xv = xs_ref[...]
        g = jnp.dot(xv, wg_ref[...], preferred_element_type=F32)
        u = jnp.dot(xv, wu_ref[...], preferred_element_type=F32)
        hmid = ((g * _sigmoid(g)) * u).astype(BF16)
        _store_token_rows(y_ref, jnp.dot(hmid, wd_ref[...], preferred_element_type=F32))


def _moe(blk_e, first, nxt, nused, row_tok, hf, wg, wu, wd, *, R):
    E, D, F = wg.shape
    sub = D // LANES
    n_blocks = blk_e.shape[0]
    hbm = lambda: pl.BlockSpec(memory_space=pl.ANY)
    grid_spec = pltpu.PrefetchScalarGridSpec(
        num_scalar_prefetch=5,
        grid=(n_blocks,),
        in_specs=[hbm(), hbm(), hbm(), hbm()],
        out_specs=pl.BlockSpec((R * sub, LANES), lambda i, be, fi, nx, nu, tk: (jnp.minimum(i, nu[0] - 1), 0)),
        scratch_shapes=[
            pltpu.VMEM((2, R * sub, LANES), F32), pltpu.VMEM((R, D), BF16),
            pltpu.VMEM((D, F), F32), pltpu.VMEM((D, F), F32), pltpu.VMEM((F, D), F32),
            pltpu.VMEM((D, F), BF16), pltpu.VMEM((D, F), BF16), pltpu.VMEM((F, D), BF16),
            pltpu.SemaphoreType.DMA((2,)), pltpu.SemaphoreType.DMA((3,)),
        ],
    )
    return pl.pallas_call(
        functools.partial(_moe_body, R=R, CB=128),
        grid_spec=grid_spec,
        out_shape=jax.ShapeDtypeStruct((n_blocks * R * sub, LANES), F32),
        compiler_params=_params("arbitrary"),
        name="moe",
    )(blk_e, first, nxt, nused, row_tok, hf, wg, wu, wd)


def _combine_body(dest_ref, x1_ref, w_ref, yr_hbm, out_ref, ybuf, sem, *, R):
    i = pl.program_id(0)
    nsteps = pl.num_programs(0)

    sub = out_ref.shape[1] // LANES

    def start_gather(step, slot):
        base = step * (R * TOP_K)

        def row(r, carry):
            dst = pl.multiple_of(r * sub, sub)
            for k in range(TOP_K):
                src = pl.multiple_of(dest_ref[base + r * TOP_K + k] * sub, sub)
                pltpu.make_async_copy(yr_hbm.at[pl.ds(src, sub)], ybuf.at[slot, k, pl.ds(dst, sub)],
                                      sem.at[slot]).start()
            return carry
        lax.fori_loop(0, R, row, 0, unroll=4)

    slot = i % 2

    @pl.when(i == 0)
    def _():
        start_gather(0, 0)

    pltpu.make_async_copy(ybuf.at[slot], ybuf.at[slot], sem.at[slot]).wait()

    @pl.when(i + 1 < nsteps)
    def _():
        start_gather(i + 1, 1 - slot)

    w = w_ref[...]
    for s in range(sub):
        cols = slice(s * LANES, (s + 1) * LANES)
        y = (_load_token_rows(ybuf.at[slot, 0], R, sub, s) * w[:, 0:1]
             + _load_token_rows(ybuf.at[slot, 1], R, sub, s) * w[:, 1:2])
        out_ref[:, cols] = x1_ref[:, cols] + y


def _combine(dest, x1, gate_w, yr, *, R):
    T, D = x1.shape
    grid_spec = pltpu.PrefetchScalarGridSpec(
        num_scalar_prefetch=1,
        grid=(T // R,),
        in_specs=[
            pl.BlockSpec((R, D), lambda i, d: (i, 0)),
            pl.BlockSpec((R, TOP_K), lambda i, d: (i, 0)),
            pl.BlockSpec(memory_space=pl.ANY),
        ],
        out_specs=pl.BlockSpec((R, D), lambda i, d: (i, 0)),
        scratch_shapes=[pltpu.VMEM((2, TOP_K, R * (D // LANES), LANES), F32), pltpu.SemaphoreType.DMA((2,))],
    )
    return pl.pallas_call(
        functools.partial(_combine_body, R=R),
        grid_spec=grid_spec,
        out_shape=jax.ShapeDtypeStruct((T, D), F32),
        compiler_params=_params("arbitrary"),
        name="combine",
    )(dest, x1, gate_w, yr)


def _route_body(lg_ref, tri_ref, ids_ref, gate_ref, cnt_ref):
    @pl.when(pl.program_id(0) == 0)
    def _():
        cnt_ref[...] = jnp.zeros_like(cnt_ref)

    lg = lg_ref[...]
    shape = lg.shape
    lane = lax.broadcasted_iota(jnp.int32, shape, 1)
    big = jnp.int32(LANES)

    def softmax_masked(mask):
        v = jnp.where(mask, lg, -jnp.inf)
        u = jnp.exp(v - jnp.max(v, axis=1, keepdims=True))
        return jnp.where(mask, u / jnp.sum(u, axis=1, keepdims=True), -1.0)

    def top1(p):
        best = jnp.max(p, axis=1, keepdims=True)
        idx = jnp.min(jnp.where(p == best, lane, big), axis=1, keepdims=True)
        return best, idx

    g_p, g_lane = top1(softmax_masked(lane < N_GROUPS))
    grp_of_lane = lax.shift_right_arithmetic(lane - N_GROUPS, jnp.int32(3))
    in_grp = jnp.logical_and(lane >= N_GROUPS, grp_of_lane == g_lane)
    in_grp = jnp.logical_and(in_grp, lane < N_GROUPS + N_EXPERTS)
    pe = softmax_masked(in_grp)
    p1, l1 = top1(pe)
    p2, l2 = top1(jnp.where(lane == l1, -1.0, pe))
    tot = p1 + p2
    gate1 = g_p * (p1 / tot)
    gate2 = g_p * (p2 / tot)

    hot1 = lane == l1
    hot2 = lane == l2
    hot = jnp.logical_or(hot1, hot2)
    before = jnp.dot(tri_ref[...], jnp.where(hot, 1.0, 0.0).astype(BF16), preferred_element_type=F32) + cnt_ref[...]
    rank1 = jnp.sum(jnp.where(hot1, before, 0.0), axis=1, keepdims=True)
    rank2 = jnp.sum(jnp.where(hot2, before, 0.0), axis=1, keepdims=True)
    cnt_ref[...] = cnt_ref[...] + jnp.sum(jnp.where(hot, 1.0, 0.0), axis=0, keepdims=True)

    ids = jnp.where(lane == 0, l1 - N_GROUPS, jnp.where(lane == 1, l2 - N_GROUPS, 0))
    ids = jnp.where(lane == 2, rank1.astype(jnp.int32), jnp.where(lane == 3, rank2.astype(jnp.int32), ids))
    ids_ref[...] = ids
    gate_ref[...] = jnp.where(lane == 0, gate1, jnp.where(lane == 1, gate2, 0.0))


def _route_tokens(logits, *, tb):
    T = logits.shape[0]
    r = jnp.arange(tb)
    tri = (r[None, :] < r[:, None]).astype(BF16)
    return pl.pallas_call(
        _route_body,
        grid=(T // tb,),
        in_specs=[
            pl.BlockSpec((tb, LANES), lambda i: (i, 0)),
            pl.BlockSpec((tb, tb), lambda i: (0, 0)),
        ],
        out_specs=[
            pl.BlockSpec((tb, LANES), lambda i: (i, 0)),
            pl.BlockSpec((tb, LANES), lambda i: (i, 0)),
            pl.BlockSpec((1, LANES), lambda i: (0, 0)),
        ],
        out_shape=[
            jax.ShapeDtypeStruct((T, LANES), jnp.int32),
            jax.ShapeDtypeStruct((T, LANES), F32),
            jax.ShapeDtypeStruct((1, LANES), F32),
        ],
        compiler_params=_params("arbitrary"),
        name="route",
    )(logits, tri)


def _route(logits, R):
    T = logits.shape[0]
    ids, gates, cnt = _route_tokens(logits, tb=256)
    gate = gates[:, 0:TOP_K]
    M = T * TOP_K
    eid_f = ids[:, 0:TOP_K].reshape(M)
    rank = ids[:, TOP_K:2 * TOP_K].reshape(M)
    counts = cnt[0, N_GROUPS:N_GROUPS + N_EXPERTS].astype(jnp.int32)
    padded = (counts + R - 1) // R * R
    pend = jnp.cumsum(padded)
    pstart = pend - padded
    dest = (pstart[eid_f] + rank).astype(jnp.int32)
    n_blocks = -(-M // R) + N_EXPERTS
    tok_f = jnp.arange(M, dtype=jnp.int32) // TOP_K
    row_tok = jnp.zeros((n_blocks * R,), jnp.int32).at[dest].set(tok_f)
    blk_start = jnp.arange(n_blocks, dtype=jnp.int32) * R
    blk_e = jnp.sum((pend[None, :] <= blk_start[:, None]).astype(jnp.int32), axis=1)
    nused = (pend[-1] // R).astype(jnp.int32)
    last_e = blk_e[jnp.maximum(nused - 1, 0)]
    blk_e = jnp.where(jnp.arange(n_blocks) < nused, blk_e, last_e)
    blk_e = jnp.minimum(blk_e, N_EXPERTS - 1).astype(jnp.int32)
    e_idx = jnp.arange(N_EXPERTS, dtype=jnp.int32)
    cand = jnp.where(counts > 0, e_idx, N_EXPERTS)
    sfx = lax.cummin(cand[::-1])[::-1]
    nxt_of_e = jnp.concatenate([sfx[1:], jnp.full((1,), N_EXPERTS, jnp.int32)])
    nxt_of_e = jnp.where(nxt_of_e < N_EXPERTS, nxt_of_e, -1)
    nxt = nxt_of_e[blk_e].astype(jnp.int32)
    changed = jnp.concatenate([jnp.ones((1,), bool), blk_e[1:] != blk_e[:-1]])
    first = jnp.logical_and(changed, jnp.arange(n_blocks) < nused).astype(jnp.int32)
    return gate, dest, row_tok, blk_e, first, nxt, nused.reshape(1)


def _rope_tables(S, gain, scale):
    half = A_HEAD_DIM // 2
    freqs = ROPE_THETA ** (-jnp.arange(half, dtype=F32) / half)
    ang = jnp.arange(S, dtype=F32)[:, None] * freqs[None, :]
    cos = jnp.cos(ang)
    sin = jnp.sin(ang)
    g1, g2 = gain[:half], gain[half:]
    a_head = jnp.concatenate([cos * g1, cos * g2], axis=1)
    b_head = jnp.concatenate([-sin * g2, sin * g1], axis=1)
    reps = LANES // A_HEAD_DIM
    return jnp.tile(a_head, (1, reps)) * scale, jnp.tile(b_head, (1, reps)) * scale


def kernel(x, g_mix, w_in, conv_qk, b_igate, b_fgate, g_mlstm, g_q, g_k, sinks, w_proj_m, w_proj_a, w_out,
           g_ffn, w_group, b_group, w_expert, b_expert, w_gate, w_up, w_down):
    B, S, D = x.shape
    T = B * S
    depth = g_mix.shape[0]
    xf = x.reshape(T, D)

    sizes = (M_WIDTH, M_WIDTH, M_WIDTH, M_WIDTH, M_HEADS, M_HEADS, A_WIDTH, A_KV_WIDTH, A_KV_WIDTH, D, D)
    offs = [0]
    for s_ in sizes:
        offs.append(offs[-1] + s_)
    seg = lambda w, idx: w[:, offs[idx]:offs[idx + 1]]
    order = (9, 10, 0, 1, 2, 3, 6, 7, 8)
    new_off = {}
    acc = 0
    for idx in order:
        new_off[idx] = acc
        acc += sizes[idx]
    dk = M_HEAD_DIM

    lane128 = jnp.arange(LANES)
    bd = (lane128[:, None] // A_HEAD_DIM == lane128[None, :] // A_HEAD_DIM).astype(BF16)
    gw = A_GROUP * A_HEAD_DIM
    r_idx = jnp.arange(gw)
    rep = jnp.stack([(r_idx[:, None] == (j * A_HEAD_DIM + r_idx[None, :] % A_HEAD_DIM)).astype(BF16)
                     for j in range(A_KV_HEADS)])

    for l in range(depth):
        w_rep = jnp.concatenate([seg(w_in[l], idx) for idx in order], axis=1).astype(BF16)
        w_gates = jnp.concatenate([seg(w_in[l], 4), seg(w_in[l], 5)], axis=1)
        w_gates = jnp.pad(w_gates, ((0, 0), (0, LANES - 2 * M_HEADS))).astype(BF16)
        z, zg = _in_proj(xf, g_mix[l][None, :], w_rep, w_gates, bm=1024 if T % 1024 == 0 else T, bn=512)

        gate_bias = jnp.pad(jnp.concatenate([b_igate[l], b_fgate[l]]), (0, LANES - 2 * M_HEADS))[None, :]
        hm = _mlstm(z, zg, conv_qk[l], gate_bias, g_mlstm[l][:, None, :], B=B, S=S,
                    col_q=new_off[0] // dk, col_k=new_off[1] // dk, col_v=new_off[2] // dk, col_o=new_off[3] // dk)

        qa, qb = _rope_tables(S, g_q[l], A_HEAD_DIM ** -0.5)
        ka, kb = _rope_tables(S, g_k[l], 1.0)
        ha = _swa(z, sinks[l], qa, qb, ka, kb, bd, rep, B=B, S=S,
                  col_q=new_off[6] // A_WIDTH, col_k=new_off[7] // A_KV_WIDTH, col_v=new_off[8] // A_KV_WIDTH)

        w_router = jnp.pad(jnp.concatenate([w_group[l], w_expert[l]], axis=1),
                           ((0, 0), (0, LANES - N_GROUPS - N_EXPERTS)))
        r_hi = w_router.astype(BF16)
        r_lo = (w_router - r_hi.astype(F32)).astype(BF16)
        r_b = jnp.pad(jnp.concatenate([b_group[l], b_expert[l]]), (0, LANES - N_GROUPS - N_EXPERTS))[None, :]
        x1, hf, logits = _merge(hm, ha, z, xf, w_proj_m[l].astype(BF16), w_proj_a[l].astype(BF16),
                                w_out[l].astype(BF16), g_ffn[l][None, :], r_hi, r_lo, r_b,
                                bm=256, col_gm=new_off[9] // D, col_ga=new_off[10] // D)

        gate, dest, row_tok, blk_e, first, nxt, nused = _route(logits, MOE_ROWS)
        yr = _moe(blk_e, first, nxt, nused, row_tok, hf, w_gate[l], w_up[l], w_down[l], R=MOE_ROWS)
        xf = _combine(dest, x1, gate, yr, R=COMBINE_ROWS)
    return xf.reshape(B, S, D)
```

```python
import functools

import jax
import jax.numpy as jnp
from jax import lax
from jax.experimental import pallas as pl
from jax.experimental.pallas import tpu as pltpu

F32 = jnp.float32
BF16 = jnp.bfloat16
EPS = 1e-6
LANES = 128
VMEM_LIMIT = 56 * 1024 * 1024

M_HEADS = 4
M_HEAD_DIM = 256
M_WIDTH = M_HEADS * M_HEAD_DIM
CONV_WIDTH = 4
A_HEADS = 16
A_KV_HEADS = 4
A_GROUP = A_HEADS // A_KV_HEADS
A_HEAD_DIM = 64
A_WIDTH = A_HEADS * A_HEAD_DIM
A_KV_WIDTH = A_KV_HEADS * A_HEAD_DIM
WINDOW = 128
ROPE_THETA = 10000.0
N_GROUPS = 8
EXPERTS_PER_GROUP = 8
N_EXPERTS = N_GROUPS * EXPERTS_PER_GROUP
TOP_K = 2

MLSTM_CHUNK = 128
MOE_ROWS = 256
COMBINE_ROWS = 128


def _sigmoid(v):
    return 1.0 / (1.0 + jnp.exp(-v))


def _params(*sem):
    return pltpu.CompilerParams(dimension_semantics=sem, vmem_limit_bytes=VMEM_LIMIT)


def _pack_bf16_pair(lo, hi):
    lo_b = lax.bitcast_convert_type(lo.astype(BF16).astype(F32), jnp.uint32)
    hi_b = lax.bitcast_convert_type(hi.astype(BF16).astype(F32), jnp.uint32)
    return lax.shift_right_logical(lo_b, jnp.uint32(16)) | hi_b


def _unpack_bf16_pair(word):
    lo = lax.bitcast_convert_type(lax.shift_left(word, jnp.uint32(16)), F32)
    hi = lax.bitcast_convert_type(word & jnp.uint32(0xFFFF0000), F32)
    return lo, hi


def _store_token_rows(ref, val):
    n, d = val.shape
    sub = d // (2 * LANES)
    for s in range(sub):
        c = 2 * s * LANES
        ref[pl.ds(s, n, stride=sub), :] = _pack_bf16_pair(val[:, c:c + LANES], val[:, c + LANES:c + 2 * LANES])


def _load_token_rows(ref, n, sub, s):
    return _unpack_bf16_pair(ref[pl.ds(s, n, stride=sub), :])


def _inproj_body(x_ref, g_ref, w_ref, wgate_ref, z_ref, zg_ref, hn_ref, *, sub):
    bm = x_ref.shape[0]

    @pl.when(pl.program_id(1) == 0)
    def _():
        def rows(r, carry):
            sl = pl.ds(pl.multiple_of(r * sub, sub), sub)
            xv = x_ref[sl, :]
            ms = jnp.mean(xv * xv, axis=-1, keepdims=True)
            hn_ref[sl, :] = ((xv * lax.rsqrt(ms + EPS)) * g_ref[...]).astype(BF16)
            return carry
        lax.fori_loop(0, bm // sub, rows, 0)
        zg_ref[...] = jnp.dot(hn_ref[...], wgate_ref[...], preferred_element_type=F32)

    z_ref[...] = jnp.dot(hn_ref[...], w_ref[...], preferred_element_type=F32).astype(BF16)


def _in_proj(x2, g, w, wgate, *, bm, bn):
    T, D = x2.shape
    N = w.shape[1]
    return pl.pallas_call(
        functools.partial(_inproj_body, sub=128),
        grid=(T // bm, N // bn),
        in_specs=[
            pl.BlockSpec((bm, D), lambda i, j: (i, 0)),
            pl.BlockSpec((1, D), lambda i, j: (0, 0)),
            pl.BlockSpec((D, bn), lambda i, j: (0, j)),
            pl.BlockSpec((D, LANES), lambda i, j: (0, 0)),
        ],
        out_specs=[
            pl.BlockSpec((bm, bn), lambda i, j: (i, j)),
            pl.BlockSpec((bm, LANES), lambda i, j: (i, 0)),
        ],
        out_shape=[
            jax.ShapeDtypeStruct((T, N), BF16),
            jax.ShapeDtypeStruct((T, LANES), F32),
        ],
        scratch_shapes=[pltpu.VMEM((bm, D), BF16)],
        compiler_params=_params("parallel", "arbitrary"),
        name="in_proj",
    )(x2, g, w, wgate)


def _mlstm_body(q_ref, k_ref, v_ref, o_ref, zg_ref, cq_ref, ck_ref, bias_ref, gn_ref, out_ref,
                qs_ref, ks_ref, colli_ref, colb_ref, rowli_ref, rowb_ref, cli_ref, cb_ref, rli_ref, rb_ref,
                c_ref, n_ref, m_ref, *, L, CB):
    h = pl.program_id(1)
    S, dk = q_ref.shape

    @pl.when(h == 0)
    def _():
        G = zg_ref[...] + bias_ref[...]
        lf = jnp.minimum(G, 0.0) - jnp.log1p(jnp.exp(-jnp.abs(G)))
        pos = lax.broadcasted_iota(jnp.int32, (S, LANES), 0) % L
        bc = lf
        sh = 1
        while sh < L:
            bc = bc + jnp.where(pos >= sh, pltpu.roll(bc, sh, axis=0), 0.0)
            sh *= 2
        colli_ref[...] = G
        colb_ref[...] = bc
        for p in range(S // LANES):
            sl = slice(p * LANES, (p + 1) * LANES)
            rowli_ref[:, sl] = G[sl, :].T[0:8, :]
            rowb_ref[:, sl] = bc[sl, :].T[0:8, :]

    lane = lax.broadcasted_iota(jnp.int32, (S, LANES), 1)
    cli_ref[...] = jnp.sum(jnp.where(lane == h, colli_ref[...], 0.0), axis=1, keepdims=True)
    cb_ref[...] = jnp.sum(jnp.where(lane == h + M_HEADS, colb_ref[...], 0.0), axis=1, keepdims=True)
    sub = lax.broadcasted_iota(jnp.int32, (8, S), 0)
    rli_ref[...] = jnp.sum(jnp.where(sub == h, rowli_ref[...], 0.0), axis=0, keepdims=True)
    rb_ref[...] = jnp.sum(jnp.where(sub == h + M_HEADS, rowb_ref[...], 0.0), axis=0, keepdims=True)

    def conv_silu(src_ref, w_ref, dst_ref, scale):
        w = w_ref[...]

        def taps(xs):
            y = xs[3] * w[0:1, :]
            y = y + xs[2] * w[1:2, :]
            y = y + xs[1] * w[2:3, :]
            y = y + xs[0] * w[3:4, :]
            return ((y * _sigmoid(y)) * scale).astype(BF16)

        def chunk(r, carry):
            r0 = pl.multiple_of(r * CB, CB)
            cur = src_ref[pl.ds(r0, CB), :].astype(F32)
            dst_ref[pl.ds(r0, CB), :] = taps([cur] + [pltpu.roll(cur, d, axis=0) for d in (1, 2, 3)])
            p0 = pl.multiple_of(jnp.maximum(r0 - 16, 0), 16)
            prev = src_ref[pl.ds(p0, 16), :].astype(F32)[8:16, :]
            prev = jnp.where(r > 0, prev, 0.0)
            both = jnp.concatenate([prev, cur[0:16, :]], axis=0)
            dst_ref[pl.ds(r0, 16), :] = taps([both[8:24, :]] + [pltpu.roll(both, d, axis=0)[8:24, :] for d in (1, 2, 3)])
            return carry
        lax.fori_loop(0, S // CB, chunk, 0)

    conv_silu(q_ref, cq_ref, qs_ref, 1.0)
    conv_silu(k_ref, ck_ref, ks_ref, float(dk) ** -0.5)

    c_ref[...] = jnp.zeros_like(c_ref)
    n_ref[...] = jnp.zeros_like(n_ref)
    m_ref[...] = jnp.zeros_like(m_ref)
    t_idx = lax.broadcasted_iota(jnp.int32, (L, L), 0)
    s_idx = lax.broadcasted_iota(jnp.int32, (L, L), 1)
    causal = s_idx <= t_idx
    gn = gn_ref[0]

    def chunk(c, carry):
        r0 = pl.multiple_of(c * L, L)
        rows = pl.ds(r0, L)
        qc = qs_ref[rows, :]
        kc = ks_ref[rows, :]
        vc = v_ref[rows, :]
        b_col = cb_ref[rows, :]
        li_col = cli_ref[rows, :]
        li_row = rli_ref[:, rows]
        b_row = rb_ref[:, rows]
        m = m_ref[...]
        a = b_col + m
        D = jnp.where(causal, b_col - b_row + li_row, -jnp.inf)
        m_t = jnp.maximum(a, jnp.max(D, axis=1, keepdims=True))
        w_inter = jnp.exp(a - m_t)
        P = lax.dot_general(qc, kc, (((1,), (1,)), ((), ())), preferred_element_type=F32) * jnp.exp(D - m_t)
        num = (w_inter * jnp.dot(qc, c_ref[...].astype(BF16), preferred_element_type=F32)
               + jnp.dot(P.astype(BF16), vc, preferred_element_type=F32))
        qn = (w_inter * jnp.sum(qc.astype(F32) * n_ref[...], axis=1, keepdims=True)
              + jnp.sum(P, axis=1, keepdims=True))
        den = jnp.maximum(jnp.abs(qn), jnp.exp(-m_t))
        hh = num / den
        ms = jnp.mean(hh * hh, axis=1, keepdims=True)
        hn = (hh * lax.rsqrt(ms + EPS)) * gn
        out_ref[rows, :] = (hn * _sigmoid(o_ref[rows, :].astype(F32))).astype(BF16)
        bL = b_row[:, L - 1:L]
        g_col = bL - b_col + li_col
        m_new = jnp.maximum(bL + m, jnp.max(g_col, axis=0, keepdims=True))
        decay = jnp.exp(bL + m - m_new)
        kw = kc.astype(F32) * jnp.exp(g_col - m_new)
        c_ref[...] = decay * c_ref[...] + jnp.dot(kw.T.astype(BF16), vc, preferred_element_type=F32)
        n_ref[...] = decay * n_ref[...] + jnp.sum(kw, axis=0, keepdims=True)
        m_ref[...] = m_new
        return carry
    lax.fori_loop(0, S // L, chunk, 0)


def _mlstm(z, zg, conv_qk, gate_bias, g_mlstm3, *, B, S, col_q, col_k, col_v, col_o):
    T = B * S
    dk = M_HEAD_DIM
    L = MLSTM_CHUNK
    zspec = lambda col: pl.BlockSpec((S, dk), lambda b, h: (b, col + h))
    return pl.pallas_call(
        functools.partial(_mlstm_body, L=L, CB=128),
        grid=(B, M_HEADS),
        in_specs=[
            zspec(col_q), zspec(col_k), zspec(col_v), zspec(col_o),
            pl.BlockSpec((S, LANES), lambda b, h: (b, 0)),
            pl.BlockSpec((CONV_WIDTH, dk), lambda b, h: (0, h)),
            pl.BlockSpec((CONV_WIDTH, dk), lambda b, h: (0, M_HEADS + h)),
            pl.BlockSpec((1, LANES), lambda b, h: (0, 0)),
            pl.BlockSpec((1, 1, dk), lambda b, h: (h, 0, 0)),
        ],
        out_specs=pl.BlockSpec((S, dk), lambda b, h: (b, h)),
        out_shape=jax.ShapeDtypeStruct((T, M_WIDTH), BF16),
        scratch_shapes=[
            pltpu.VMEM((S, dk), BF16), pltpu.VMEM((S, dk), BF16),
            pltpu.VMEM((S, LANES), F32), pltpu.VMEM((S, LANES), F32),
            pltpu.VMEM((8, S), F32), pltpu.VMEM((8, S), F32),
            pltpu.VMEM((S, 1), F32), pltpu.VMEM((S, 1), F32),
            pltpu.VMEM((1, S), F32), pltpu.VMEM((1, S), F32),
            pltpu.VMEM((dk, dk), F32), pltpu.VMEM((1, dk), F32), pltpu.VMEM((1, 1), F32),
        ],
        compiler_params=_params("parallel", "arbitrary"),
        name="mlstm",
    )(z, z, z, z, zg, conv_qk, conv_qk, gate_bias, g_mlstm3)


def _swa_body(sink_ref, q_ref, k_ref, v_ref, qa_ref, qb_ref, ka_ref, kb_ref, bd_ref, rep_ref, out_ref,
              kbd_ref, vbd_ref):
    n = pl.program_id(1)
    W = q_ref.shape[0]
    hd = A_HEAD_DIM
    gw = A_GROUP * hd

    def norm_rope(x, ta, tb):
        x2 = x * x
        x2h = x2.astype(BF16)
        x2l = (x2 - x2h.astype(F32)).astype(BF16)
        ss = (jnp.dot(x2h, bd_ref[...], preferred_element_type=F32)
              + jnp.dot(x2l, bd_ref[...], preferred_element_type=F32))
        r = lax.rsqrt(ss * (1.0 / hd) + EPS)
        ln = lax.broadcasted_iota(jnp.int32, x.shape, 1) % hd
        swapped = jnp.where(ln < hd // 2, pltpu.roll(x, LANES - hd // 2, axis=1), pltpu.roll(x, hd // 2, axis=1))
        return r * (x * ta + swapped * tb)

    qa = qa_ref[...]
    qb = qb_ref[...]
    qp = jnp.concatenate(
        [norm_rope(q_ref[:, t * LANES:(t + 1) * LANES].astype(F32), qa, qb) for t in range(A_WIDTH // LANES)],
        axis=1).astype(BF16)
    ka = ka_ref[...]
    kb = kb_ref[...]
    kp = jnp.concatenate(
        [norm_rope(k_ref[:, t * LANES:(t + 1) * LANES].astype(F32), ka, kb) for t in range(A_KV_WIDTH // LANES)],
        axis=1).astype(BF16)
    vv = v_ref[...]

    lane_head = lax.broadcasted_iota(jnp.int32, (W, gw), 1) // hd

    @pl.when(n == 0)
    def _():
        kbd_ref[0] = jnp.zeros(kbd_ref.shape[1:], BF16)
        vbd_ref[0] = jnp.zeros(vbd_ref.shape[1:], BF16)

    @pl.when(n > 0)
    def _():
        kbd_ref[0] = kbd_ref[1]
        vbd_ref[0] = vbd_ref[1]

    for j in range(A_KV_HEADS):
        krep = jnp.dot(kp, rep_ref[j], preferred_element_type=F32).astype(BF16)
        vrep = jnp.dot(vv, rep_ref[j], preferred_element_type=F32).astype(BF16)
        for i in range(A_GROUP):
            kbd_ref[1, j, i * W:(i + 1) * W, :] = jnp.where(lane_head == i, krep, jnp.zeros_like(krep))
            vbd_ref[1, j, i * W:(i + 1) * W, :] = jnp.where(lane_head == i, vrep, jnp.zeros_like(vrep))

    t_idx = lax.broadcasted_iota(jnp.int32, (W, W), 0)
    k_idx = lax.broadcasted_iota(jnp.int32, (W, W), 1)
    mask_cur = k_idx <= t_idx
    mask_prev = jnp.logical_and(k_idx > t_idx, n > 0)
    neg = -jnp.inf

    for j in range(A_KV_HEADS):
        qg = qp[:, j * gw:(j + 1) * gw]
        s_prev = lax.dot_general(qg, kbd_ref[0, j], (((1,), (1,)), ((), ())), preferred_element_type=F32)
        s_cur = lax.dot_general(qg, kbd_ref[1, j], (((1,), (1,)), ((), ())), preferred_element_type=F32)
        pp, pc = [], []
        for i in range(A_GROUP):
            sp = jnp.where(mask_prev, s_prev[:, i * W:(i + 1) * W], neg)
            sc = jnp.where(mask_cur, s_cur[:, i * W:(i + 1) * W], neg)
            sink = sink_ref[j * A_GROUP + i]
            mx = jnp.maximum(jnp.maximum(jnp.max(sp, axis=1, keepdims=True), jnp.max(sc, axis=1, keepdims=True)), sink)
            ep = jnp.exp(sp - mx)
            ec = jnp.exp(sc - mx)
            den = jnp.sum(ep, axis=1, keepdims=True) + jnp.sum(ec, axis=1, keepdims=True) + jnp.exp(sink - mx)
            inv = 1.0 / den
            pp.append((ep * inv).astype(BF16))
            pc.append((ec * inv).astype(BF16))
        o = (jnp.dot(jnp.concatenate(pp, axis=1), vbd_ref[0, j], preferred_element_type=F32)
             + jnp.dot(jnp.concatenate(pc, axis=1), vbd_ref[1, j], preferred_element_type=F32))
        out_ref[:, j * gw:(j + 1) * gw] = o.astype(BF16)


def _swa(z, sinks, qa, qb, ka, kb, bd, rep, *, B, S, col_q, col_k, col_v):
    T = B * S
    W = WINDOW
    NB = S // W
    gw = A_GROUP * A_HEAD_DIM
    tab = lambda: pl.BlockSpec((W, LANES), lambda b, n: (n, 0))
    return pl.pallas_call(
        _swa_body,
        grid=(B, NB),
        in_specs=[
            pl.BlockSpec(memory_space=pltpu.SMEM),
            pl.BlockSpec((W, A_WIDTH), lambda b, n: (b * NB + n, col_q)),
            pl.BlockSpec((W, A_KV_WIDTH), lambda b, n: (b * NB + n, col_k)),
            pl.BlockSpec((W, A_KV_WIDTH), lambda b, n: (b * NB + n, col_v)),
            tab(), tab(), tab(), tab(),
            pl.BlockSpec((LANES, LANES), lambda b, n: (0, 0)),
            pl.BlockSpec((A_KV_HEADS, gw, gw), lambda b, n: (0, 0, 0)),
        ],
        out_specs=pl.BlockSpec((W, A_WIDTH), lambda b, n: (b * NB + n, 0)),
        out_shape=jax.ShapeDtypeStruct((T, A_WIDTH), BF16),
        scratch_shapes=[
            pltpu.VMEM((2, A_KV_HEADS, A_GROUP * W, gw), BF16),
            pltpu.VMEM((2, A_KV_HEADS, A_GROUP * W, gw), BF16),
        ],
        compiler_params=_params("parallel", "arbitrary"),
        name="swa",
    )(sinks, z, z, z, qa, qb, ka, kb, bd, rep)


def _merge_body(hm_ref, ha_ref, gm_ref, ga_ref, x_ref, wm_ref, wa_ref, wo_ref, gf_ref, rh_ref, rl_ref, rb_ref,
                x1_ref, hf_ref, lg_ref):
    pm = jnp.dot(hm_ref[...], wm_ref[...], preferred_element_type=F32)
    pa = jnp.dot(ha_ref[...], wa_ref[...], preferred_element_type=F32)
    mixed = _sigmoid(gm_ref[...].astype(F32)) * pm + _sigmoid(ga_ref[...].astype(F32)) * pa
    x1 = x_ref[...] + jnp.dot(mixed.astype(BF16), wo_ref[...], preferred_element_type=F32)
    x1_ref[...] = x1
    ms = jnp.mean(x1 * x1, axis=-1, keepdims=True)
    hf = (x1 * lax.rsqrt(ms + EPS)) * gf_ref[...]
    _store_token_rows(hf_ref, hf)
    hh = hf.astype(BF16)
    hl = (hf - hh.astype(F32)).astype(BF16)
    lg_ref[...] = (jnp.dot(hh, rh_ref[...], preferred_element_type=F32)
                   + jnp.dot(hl, rh_ref[...], preferred_element_type=F32)
                   + jnp.dot(hh, rl_ref[...], preferred_element_type=F32)) + rb_ref[...]


def _merge(hm, ha, z, x2, wm, wa, wo, gf, rh, rl, rb, *, bm, col_gm, col_ga):
    T, D = x2.shape
    const = lambda shape: pl.BlockSpec(shape, lambda i: (0,) * len(shape), pipeline_mode=pl.Buffered(1))
    return pl.pallas_call(
        _merge_body,
        grid=(T // bm,),
        in_specs=[
            pl.BlockSpec((bm, M_WIDTH), lambda i: (i, 0)),
            pl.BlockSpec((bm, A_WIDTH), lambda i: (i, 0)),
            pl.BlockSpec((bm, D), lambda i: (i, col_gm)),
            pl.BlockSpec((bm, D), lambda i: (i, col_ga)),
            pl.BlockSpec((bm, D), lambda i: (i, 0)),
            const((M_WIDTH, D)), const((A_WIDTH, D)), const((D, D)), const((1, D)),
            const((D, LANES)), const((D, LANES)), const((1, LANES)),
        ],
        out_specs=[
            pl.BlockSpec((bm, D), lambda i: (i, 0)),
            pl.BlockSpec((bm * (D // (2 * LANES)), LANES), lambda i: (i, 0)),
            pl.BlockSpec((bm, LANES), lambda i: (i, 0)),
        ],
        out_shape=[
            jax.ShapeDtypeStruct((T, D), F32),
            jax.ShapeDtypeStruct((T * (D // (2 * LANES)), LANES), jnp.uint32),
            jax.ShapeDtypeStruct((T, LANES), F32),
        ],
        compiler_params=_params("parallel"),
        name="merge",
    )(hm, ha, z, z, x2, wm, wa, wo, gf, rh, rl, rb)


def _moe_body(blk_e_ref, first_ref, nxt_ref, nused_ref, tok_ref, hf_hbm, wg_hbm, wu_hbm, wd_hbm, y_ref,
              xbuf, xs_ref, stage_g, stage_u, stage_d, wg_ref, wu_ref, wd_ref, sem, wsem, *, R, CB):
    i = pl.program_id(0)
    nused = nused_ref[0]
    sub = xs_ref.shape[1] // (2 * LANES)

    def weight_copies(e):
        return (pltpu.make_async_copy(wg_hbm.at[e], stage_g, wsem.at[0]),
                pltpu.make_async_copy(wu_hbm.at[e], stage_u, wsem.at[1]),
                pltpu.make_async_copy(wd_hbm.at[e], stage_d, wsem.at[2]))

    def cast(src, dst):
        def rows(r, carry):
            sl = pl.ds(pl.multiple_of(r * CB, CB), CB)
            dst[sl, :] = src[sl, :].astype(BF16)
            return carry
        lax.fori_loop(0, src.shape[0] // CB, rows, 0)

    def start_gather(blk, slot):
        base = blk * R

        def row(r, carry):
            src = pl.multiple_of(tok_ref[base + r] * sub, sub)
            dst = pl.multiple_of(r * sub, sub)
            pltpu.make_async_copy(hf_hbm.at[pl.ds(src, sub)], xbuf.at[slot, pl.ds(dst, sub)], sem.at[slot]).start()
            return carry
        lax.fori_loop(0, R, row, 0, unroll=8)

    def wait_gather(slot):
        pltpu.make_async_copy(xbuf.at[slot], xbuf.at[slot], sem.at[slot]).wait()

    slot = i % 2

    @pl.when(i == 0)
    def _():
        for c in weight_copies(blk_e_ref[0]):
            c.start(priority=1)
        start_gather(0, 0)

    @pl.when(i < nused)
    def _():
        @pl.when(first_ref[i] == 1)
        def _():
            for c in weight_copies(blk_e_ref[i]):
                c.wait()
            cast(stage_g, wg_ref)
            cast(stage_u, wu_ref)
            cast(stage_d, wd_ref)

            @pl.when(nxt_ref[i] >= 0)
            def _():
                for c in weight_copies(nxt_ref[i]):
                    c.start(priority=1)

        wait_gather(slot)

        @pl.when(i + 1 < nused)
        def _():
            start_gather(i + 1, 1 - slot)

        for s in range(sub):
            lo, hi = _load_token_rows(xbuf.at[slot], R, sub, s)
            xs_ref[:, 2 * s * LANES:(2 * s + 1) * LANES] = lo.astype(BF16)
            xs_ref[:, (2 * s + 1) * LANES:(2 * s + 2) * LANES] = hi.astype(BF16)
        xv = xs_ref[...]
        g = jnp.dot(xv, wg_ref[...], preferred_element_type=F32)
        u = jnp.dot(xv, wu_ref[...], preferred_element_type=F32)
        hmid = ((g * _sigmoid(g)) * u).astype(BF16)
        _store_token_rows(y_ref, jnp.dot(hmid, wd_ref[...], preferred_element_type=F32))


def _moe(blk_e, first, nxt, nused, row_tok, hf, wg, wu, wd, *, R):
    E, D, F = wg.shape
    sub = D // (2 * LANES)
    n_blocks = blk_e.shape[0]
    hbm = lambda: pl.BlockSpec(memory_space=pl.ANY)
    grid_spec = pltpu.PrefetchScalarGridSpec(
        num_scalar_prefetch=5,
        grid=(n_blocks,),
        in_specs=[hbm(), hbm(), hbm(), hbm()],
        out_specs=pl.BlockSpec((R * sub, LANES), lambda i, be, fi, nx, nu, tk: (jnp.minimum(i, nu[0] - 1), 0)),
        scratch_shapes=[
            pltpu.VMEM((2, R * sub, LANES), jnp.uint32), pltpu.VMEM((R, D), BF16),
            pltpu.VMEM((D, F), F32), pltpu.VMEM((D, F), F32), pltpu.VMEM((F, D), F32),
            pltpu.VMEM((D, F), BF16), pltpu.VMEM((D, F), BF16), pltpu.VMEM((F, D), BF16),
            pltpu.SemaphoreType.DMA((2,)), pltpu.SemaphoreType.DMA((3,)),
        ],
    )
    return pl.pallas_call(
        functools.partial(_moe_body, R=R, CB=128),
        grid_spec=grid_spec,
        out_shape=jax.ShapeDtypeStruct((n_blocks * R * sub, LANES), jnp.uint32),
        compiler_params=_params("arbitrary"),
        name="moe",
    )(blk_e, first, nxt, nused, row_tok, hf, wg, wu, wd)


def _combine_body(dest_ref, x1_ref, w_ref, yr_hbm, out_ref, ybuf, sem, *, R):
    i = pl.program_id(0)
    nsteps = pl.num_programs(0)

    sub = out_ref.shape[1] // (2 * LANES)

    def start_gather(step, slot):
        base = step * (R * TOP_K)

        def row(r, carry):
            dst = pl.multiple_of(r * sub, sub)
            for k in range(TOP_K):
                src = pl.multiple_of(dest_ref[base + r * TOP_K + k] * sub, sub)
                pltpu.make_async_copy(yr_hbm.at[pl.ds(src, sub)], ybuf.at[slot, k, pl.ds(dst, sub)],
                                      sem.at[slot]).start()
            return carry
        lax.fori_loop(0, R, row, 0, unroll=4)

    slot = i % 2

    @pl.when(i == 0)
    def _():
        start_gather(0, 0)

    pltpu.make_async_copy(ybuf.at[slot], ybuf.at[slot], sem.at[slot]).wait()

    @pl.when(i + 1 < nsteps)
    def _():
        start_gather(i + 1, 1 - slot)

    w = w_ref[...]
    for s in range(sub):
        halves0 = _load_token_rows(ybuf.at[slot, 0], R, sub, s)
        halves1 = _load_token_rows(ybuf.at[slot, 1], R, sub, s)
        for half in range(2):
            cols = slice((2 * s + half) * LANES, (2 * s + half + 1) * LANES)
            out_ref[:, cols] = x1_ref[:, cols] + (halves0[half] * w[:, 0:1] + halves1[half] * w[:, 1:2])


def _combine(dest, x1, gate_w, yr, *, R):
    T, D = x1.shape
    grid_spec = pltpu.PrefetchScalarGridSpec(
        num_scalar_prefetch=1,
        grid=(T // R,),
        in_specs=[
            pl.BlockSpec((R, D), lambda i, d: (i, 0)),
            pl.BlockSpec((R, TOP_K), lambda i, d: (i, 0)),
            pl.BlockSpec(memory_space=pl.ANY),
        ],
        out_specs=pl.BlockSpec((R, D), lambda i, d: (i, 0)),
        scratch_shapes=[pltpu.VMEM((2, TOP_K, R * (D // (2 * LANES)), LANES), jnp.uint32),
                        pltpu.SemaphoreType.DMA((2,))],
    )
    return pl.pallas_call(
        functools.partial(_combine_body, R=R),
        grid_spec=grid_spec,
        out_shape=jax.ShapeDtypeStruct((T, D), F32),
        compiler_params=_params("arbitrary"),
        name="combine",
    )(dest, x1, gate_w, yr)


def _route_body(lg_ref, tri_ref, ids_ref, gate_ref, cnt_ref):
    @pl.when(pl.program_id(0) == 0)
    def _():
        cnt_ref[...] = jnp.zeros_like(cnt_ref)

    lg = lg_ref[...]
    shape = lg.shape
    lane = lax.broadcasted_iota(jnp.int32, shape, 1)
    big = jnp.int32(LANES)

    def softmax_masked(mask):
        v = jnp.where(mask, lg, -jnp.inf)
        u = jnp.exp(v - jnp.max(v, axis=1, keepdims=True))
        return jnp.where(mask, u / jnp.sum(u, axis=1, keepdims=True), -1.0)

    def top1(p):
        best = jnp.max(p, axis=1, keepdims=True)
        idx = jnp.min(jnp.where(p == best, lane, big), axis=1, keepdims=True)
        return best, idx

    g_p, g_lane = top1(softmax_masked(lane < N_GROUPS))
    grp_of_lane = lax.shift_right_arithmetic(lane - N_GROUPS, jnp.int32(3))
    in_grp = jnp.logical_and(lane >= N_GROUPS, grp_of_lane == g_lane)
    in_grp = jnp.logical_and(in_grp, lane < N_GROUPS + N_EXPERTS)
    pe = softmax_masked(in_grp)
    p1, l1 = top1(pe)
    p2, l2 = top1(jnp.where(lane == l1, -1.0, pe))
    tot = p1 + p2
    gate1 = g_p * (p1 / tot)
    gate2 = g_p * (p2 / tot)

    hot1 = lane == l1
    hot2 = lane == l2
    hot = jnp.logical_or(hot1, hot2)
    before = jnp.dot(tri_ref[...], jnp.where(hot, 1.0, 0.0).astype(BF16), preferred_element_type=F32) + cnt_ref[...]
    rank1 = jnp.sum(jnp.where(hot1, before, 0.0), axis=1, keepdims=True)
    rank2 = jnp.sum(jnp.where(hot2, before, 0.0), axis=1, keepdims=True)
    cnt_ref[...] = cnt_ref[...] + jnp.sum(jnp.where(hot, 1.0, 0.0), axis=0, keepdims=True)

    ids = jnp.where(lane == 0, l1 - N_GROUPS, jnp.where(lane == 1, l2 - N_GROUPS, 0))
    ids = jnp.where(lane == 2, rank1.astype(jnp.int32), jnp.where(lane == 3, rank2.astype(jnp.int32), ids))
    ids_ref[...] = ids
    gate_ref[...] = jnp.where(lane == 0, gate1, jnp.where(lane == 1, gate2, 0.0))


def _route_tokens(logits, *, tb):
    T = logits.shape[0]
    r = jnp.arange(tb)
    tri = (r[None, :] < r[:, None]).astype(BF16)
    return pl.pallas_call(
        _route_body,
        grid=(T // tb,),
        in_specs=[
            pl.BlockSpec((tb, LANES), lambda i: (i, 0)),
            pl.BlockSpec((tb, tb), lambda i: (0, 0)),
        ],
        out_specs=[
            pl.BlockSpec((tb, LANES), lambda i: (i, 0)),
            pl.BlockSpec((tb, LANES), lambda i: (i, 0)),
            pl.BlockSpec((1, LANES), lambda i: (0, 0)),
        ],
        out_shape=[
            jax.ShapeDtypeStruct((T, LANES), jnp.int32),
            jax.ShapeDtypeStruct((T, LANES), F32),
            jax.ShapeDtypeStruct((1, LANES), F32),
        ],
        compiler_params=_params("arbitrary"),
        name="route",
    )(logits, tri)


def _route(logits, R):
    T = logits.shape[0]
    ids, gates, cnt = _route_tokens(logits, tb=256)
    gate = gates[:, 0:TOP_K]
    M = T * TOP_K
    eid_f = ids[:, 0:TOP_K].reshape(M)
    rank = ids[:, TOP_K:2 * TOP_K].reshape(M)
    counts = cnt[0, N_GROUPS:N_GROUPS + N_EXPERTS].astype(jnp.int32)
    padded = (counts + R - 1) // R * R
    pend = jnp.cumsum(padded)
    pstart = pend - padded
    dest = (pstart[eid_f] + rank).astype(jnp.int32)
    n_blocks = -(-M // R) + N_EXPERTS
    tok_f = jnp.arange(M, dtype=jnp.int32) // TOP_K
    row_tok = jnp.zeros((n_blocks * R,), jnp.int32).at[dest].set(tok_f)
    blk_start = jnp.arange(n_blocks, dtype=jnp.int32) * R
    blk_e = jnp.sum((pend[None, :] <= blk_start[:, None]).astype(jnp.int32), axis=1)
    nused = (pend[-1] // R).astype(jnp.int32)
    last_e = blk_e[jnp.maximum(nused - 1, 0)]
    blk_e = jnp.where(jnp.arange(n_blocks) < nused, blk_e, last_e)
    blk_e = jnp.minimum(blk_e, N_EXPERTS - 1).astype(jnp.int32)
    e_idx = jnp.arange(N_EXPERTS, dtype=jnp.int32)
    cand = jnp.where(counts > 0, e_idx, N_EXPERTS)
    sfx = lax.cummin(cand[::-1])[::-1]
    nxt_of_e = jnp.concatenate([sfx[1:], jnp.full((1,), N_EXPERTS, jnp.int32)])
    nxt_of_e = jnp.where(nxt_of_e < N_EXPERTS, nxt_of_e, -1)
    nxt = nxt_of_e[blk_e].astype(jnp.int32)
    changed = jnp.concatenate([jnp.ones((1,), bool), blk_e[1:] != blk_e[:-1]])
    first = jnp.logical_and(changed, jnp.arange(n_blocks) < nused).astype(jnp.int32)
    return gate, dest, row_tok, blk_e, first, nxt, nused.reshape(1)


def _rope_tables(S, gain, scale):
    half = A_HEAD_DIM // 2
    freqs = ROPE_THETA ** (-jnp.arange(half, dtype=F32) / half)
    ang = jnp.arange(S, dtype=F32)[:, None] * freqs[None, :]
    cos = jnp.cos(ang)
    sin = jnp.sin(ang)
    g1, g2 = gain[:half], gain[half:]
    a_head = jnp.concatenate([cos * g1, cos * g2], axis=1)
    b_head = jnp.concatenate([-sin * g2, sin * g1], axis=1)
    reps = LANES // A_HEAD_DIM
    return jnp.tile(a_head, (1, reps)) * scale, jnp.tile(b_head, (1, reps)) * scale


def kernel(x, g_mix, w_in, conv_qk, b_igate, b_fgate, g_mlstm, g_q, g_k, sinks, w_proj_m, w_proj_a, w_out,
           g_ffn, w_group, b_group, w_expert, b_expert, w_gate, w_up, w_down):
    B, S, D = x.shape
    T = B * S
    depth = g_mix.shape[0]
    xf = x.reshape(T, D)

    sizes = (M_WIDTH, M_WIDTH, M_WIDTH, M_WIDTH, M_HEADS, M_HEADS, A_WIDTH, A_KV_WIDTH, A_KV_WIDTH, D, D)
    offs = [0]
    for s_ in sizes:
        offs.append(offs[-1] + s_)
    seg = lambda w, idx: w[:, offs[idx]:offs[idx + 1]]
    order = (9, 10, 0, 1, 2, 3, 6, 7, 8)
    new_off = {}
    acc = 0
    for idx in order:
        new_off[idx] = acc
        acc += sizes[idx]
    dk = M_HEAD_DIM

    lane128 = jnp.arange(LANES)
    bd = (lane128[:, None] // A_HEAD_DIM == lane128[None, :] // A_HEAD_DIM).astype(BF16)
    gw = A_GROUP * A_HEAD_DIM
    r_idx = jnp.arange(gw)
    rep = jnp.stack([(r_idx[:, None] == (j * A_HEAD_DIM + r_idx[None, :] % A_HEAD_DIM)).astype(BF16)
                     for j in range(A_KV_HEADS)])

    for l in range(depth):
        w_rep = jnp.concatenate([seg(w_in[l], idx) for idx in order], axis=1).astype(BF16)
        w_gates = jnp.concatenate([seg(w_in[l], 4), seg(w_in[l], 5)], axis=1)
        w_gates = jnp.pad(w_gates, ((0, 0), (0, LANES - 2 * M_HEADS))).astype(BF16)
        z, zg = _in_proj(xf, g_mix[l][None, :], w_rep, w_gates, bm=1024 if T % 1024 == 0 else T, bn=512)

        gate_bias = jnp.pad(jnp.concatenate([b_igate[l], b_fgate[l]]), (0, LANES - 2 * M_HEADS))[None, :]
        hm = _mlstm(z, zg, conv_qk[l], gate_bias, g_mlstm[l][:, None, :], B=B, S=S,
                    col_q=new_off[0] // dk, col_k=new_off[1] // dk, col_v=new_off[2] // dk, col_o=new_off[3] // dk)

        qa, qb = _rope_tables(S, g_q[l], A_HEAD_DIM ** -0.5)
        ka, kb = _rope_tables(S, g_k[l], 1.0)
        ha = _swa(z, sinks[l], qa, qb, ka, kb, bd, rep, B=B, S=S,
                  col_q=new_off[6] // A_WIDTH, col_k=new_off[7] // A_KV_WIDTH, col_v=new_off[8] // A_KV_WIDTH)

        w_router = jnp.pad(jnp.concatenate([w_group[l], w_expert[l]], axis=1),
                           ((0, 0), (0, LANES - N_GROUPS - N_EXPERTS)))
        r_hi = w_router.astype(BF16)
        r_lo = (w_router - r_hi.astype(F32)).astype(BF16)
        r_b = jnp.pad(jnp.concatenate([b_group[l], b_expert[l]]), (0, LANES - N_GROUPS - N_EXPERTS))[None, :]
        x1, hf, logits = _merge(hm, ha, z, xf, w_proj_m[l].astype(BF16), w_proj_a[l].astype(BF16),
                                w_out[l].astype(BF16), g_ffn[l][None, :], r_hi, r_lo, r_b,
                                bm=256, col_gm=new_off[9] // D, col_ga=new_off[10] // D)

        gate, dest, row_tok, blk_e, first, nxt, nused = _route(logits, MOE_ROWS)
        yr = _moe(blk_e, first, nxt, nused, row_tok, hf, w_gate[l], w_up[l], w_down[l], R=MOE_ROWS)
        xf = _combine(dest, x1, gate, yr, R=COMBINE_ROWS)
    return xf.reshape(B, S, D)
```

```python
import functools

import jax
import jax.numpy as jnp
from jax import lax
from jax.experimental import pallas as pl
from jax.experimental.pallas import tpu as pltpu

F32 = jnp.float32
BF16 = jnp.bfloat16
EPS = 1e-6
LANES = 128
VMEM_LIMIT = 56 * 1024 * 1024

M_HEADS = 4
M_HEAD_DIM = 256
M_WIDTH = M_HEADS * M_HEAD_DIM
CONV_WIDTH = 4
A_HEADS = 16
A_KV_HEADS = 4
A_GROUP = A_HEADS // A_KV_HEADS
A_HEAD_DIM = 64
A_WIDTH = A_HEADS * A_HEAD_DIM
A_KV_WIDTH = A_KV_HEADS * A_HEAD_DIM
WINDOW = 128
ROPE_THETA = 10000.0
N_GROUPS = 8
EXPERTS_PER_GROUP = 8
N_EXPERTS = N_GROUPS * EXPERTS_PER_GROUP
TOP_K = 2

MLSTM_CHUNK = 128
MOE_ROWS = 256
COMBINE_ROWS = 128


def _sigmoid(v):
    return 1.0 / (1.0 + jnp.exp(-v))


def _params(*sem):
    return pltpu.CompilerParams(dimension_semantics=sem, vmem_limit_bytes=VMEM_LIMIT)


def _pack_bf16_pair(lo, hi):
    lo_b = lax.bitcast_convert_type(lo.astype(BF16).astype(F32), jnp.uint32)
    hi_b = lax.bitcast_convert_type(hi.astype(BF16).astype(F32), jnp.uint32)
    return lax.shift_right_logical(lo_b, jnp.uint32(16)) | hi_b


def _unpack_bf16_pair(word):
    lo = lax.bitcast_convert_type(lax.shift_left(word, jnp.uint32(16)), F32)
    hi = lax.bitcast_convert_type(word & jnp.uint32(0xFFFF0000), F32)
    return lo, hi


def _store_token_rows(ref, val):
    n, d = val.shape
    sub = d // (2 * LANES)
    for s in range(sub):
        c = 2 * s * LANES
        ref[pl.ds(s, n, stride=sub), :] = _pack_bf16_pair(val[:, c:c + LANES], val[:, c + LANES:c + 2 * LANES])


def _load_token_rows(ref, n, sub, s):
    return _unpack_bf16_pair(ref[pl.ds(s, n, stride=sub), :])


def _inproj_body(x_ref, g_ref, w_ref, wgate_ref, z_ref, zg_ref, hn_ref, *, sub):
    bm = x_ref.shape[0]

    @pl.when(pl.program_id(1) == 0)
    def _():
        def rows(r, carry):
            sl = pl.ds(pl.multiple_of(r * sub, sub), sub)
            xv = x_ref[sl, :]
            ms = jnp.mean(xv * xv, axis=-1, keepdims=True)
            hn_ref[sl, :] = ((xv * lax.rsqrt(ms + EPS)) * g_ref[...]).astype(BF16)
            return carry
        lax.fori_loop(0, bm // sub, rows, 0)
        zg_ref[...] = jnp.dot(hn_ref[...], wgate_ref[...], preferred_element_type=F32)

    z_ref[...] = jnp.dot(hn_ref[...], w_ref[...], preferred_element_type=F32).astype(BF16)


def _in_proj(x2, g, w, wgate, *, bm, bn):
    T, D = x2.shape
    N = w.shape[1]
    return pl.pallas_call(
        functools.partial(_inproj_body, sub=128),
        grid=(T // bm, N // bn),
        in_specs=[
            pl.BlockSpec((bm, D), lambda i, j: (i, 0)),
            pl.BlockSpec((1, D), lambda i, j: (0, 0)),
            pl.BlockSpec((D, bn), lambda i, j: (0, j)),
            pl.BlockSpec((D, LANES), lambda i, j: (0, 0)),
        ],
        out_specs=[
            pl.BlockSpec((bm, bn), lambda i, j: (i, j)),
            pl.BlockSpec((bm, LANES), lambda i, j: (i, 0)),
        ],
        out_shape=[
            jax.ShapeDtypeStruct((T, N), BF16),
            jax.ShapeDtypeStruct((T, LANES), F32),
        ],
        scratch_shapes=[pltpu.VMEM((bm, D), BF16)],
        compiler_params=_params("parallel", "arbitrary"),
        name="in_proj",
    )(x2, g, w, wgate)


def _mlstm_body(q_ref, k_ref, v_ref, o_ref, zg_ref, cq_ref, ck_ref, bias_ref, gn_ref, out_ref,
                qs_ref, ks_ref, colli_ref, colb_ref, rowli_ref, rowb_ref, cli_ref, cb_ref, rli_ref, rb_ref,
                c_ref, n_ref, m_ref, *, L, CB):
    h = pl.program_id(1)
    S, dk = q_ref.shape

    @pl.when(h == 0)
    def _():
        G = zg_ref[...] + bias_ref[...]
        lf = jnp.minimum(G, 0.0) - jnp.log1p(jnp.exp(-jnp.abs(G)))
        pos = lax.broadcasted_iota(jnp.int32, (S, LANES), 0) % L
        bc = lf
        sh = 1
        while sh < L:
            bc = bc + jnp.where(pos >= sh, pltpu.roll(bc, sh, axis=0), 0.0)
            sh *= 2
        colli_ref[...] = G
        colb_ref[...] = bc
        for p in range(S // LANES):
            sl = slice(p * LANES, (p + 1) * LANES)
            rowli_ref[:, sl] = G[sl, :].T[0:8, :]
            rowb_ref[:, sl] = bc[sl, :].T[0:8, :]

    lane = lax.broadcasted_iota(jnp.int32, (S, LANES), 1)
    cli_ref[...] = jnp.sum(jnp.where(lane == h, colli_ref[...], 0.0), axis=1, keepdims=True)
    cb_ref[...] = jnp.sum(jnp.where(lane == h + M_HEADS, colb_ref[...], 0.0), axis=1, keepdims=True)
    sub = lax.broadcasted_iota(jnp.int32, (8, S), 0)
    rli_ref[...] = jnp.sum(jnp.where(sub == h, rowli_ref[...], 0.0), axis=0, keepdims=True)
    rb_ref[...] = jnp.sum(jnp.where(sub == h + M_HEADS, rowb_ref[...], 0.0), axis=0, keepdims=True)

    def conv_silu(src_ref, w_ref, dst_ref, scale):
        w = w_ref[...]

        def taps(xs):
            y = xs[3] * w[0:1, :]
            y = y + xs[2] * w[1:2, :]
            y = y + xs[1] * w[2:3, :]
            y = y + xs[0] * w[3:4, :]
            return ((y * _sigmoid(y)) * scale).astype(BF16)

        def chunk(r, carry):
            r0 = pl.multiple_of(r * CB, CB)
            cur = src_ref[pl.ds(r0, CB), :].astype(F32)
            dst_ref[pl.ds(r0, CB), :] = taps([cur] + [pltpu.roll(cur, d, axis=0) for d in (1, 2, 3)])
            p0 = pl.multiple_of(jnp.maximum(r0 - 16, 0), 16)
            prev = src_ref[pl.ds(p0, 16), :].astype(F32)[8:16, :]
            prev = jnp.where(r > 0, prev, 0.0)
            both = jnp.concatenate([prev, cur[0:16, :]], axis=0)
            dst_ref[pl.ds(r0, 16), :] = taps([both[8:24, :]] + [pltpu.roll(both, d, axis=0)[8:24, :] for d in (1, 2, 3)])
            return carry
        lax.fori_loop(0, S // CB, chunk, 0)

    conv_silu(q_ref, cq_ref, qs_ref, 1.0)
    conv_silu(k_ref, ck_ref, ks_ref, float(dk) ** -0.5)

    c_ref[...] = jnp.zeros_like(c_ref)
    n_ref[...] = jnp.zeros_like(n_ref)
    m_ref[...] = jnp.zeros_like(m_ref)
    t_idx = lax.broadcasted_iota(jnp.int32, (L, L), 0)
    s_idx = lax.broadcasted_iota(jnp.int32, (L, L), 1)
    causal = s_idx <= t_idx
    gn = gn_ref[0]

    def chunk(c, carry):
        r0 = pl.multiple_of(c * L, L)
        rows = pl.ds(r0, L)
        qc = qs_ref[rows, :]
        kc = ks_ref[rows, :]
        vc = v_ref[rows, :]
        b_col = cb_ref[rows, :]
        li_col = cli_ref[rows, :]
        li_row = rli_ref[:, rows]
        b_row = rb_ref[:, rows]
        m = m_ref[...]
        a = b_col + m
        D = jnp.where(causal, b_col - b_row + li_row, -jnp.inf)
        m_t = jnp.maximum(a, jnp.max(D, axis=1, keepdims=True))
        w_inter = jnp.exp(a - m_t)
        P = lax.dot_general(qc, kc, (((1,), (1,)), ((), ())), preferred_element_type=F32) * jnp.exp(D - m_t)
        num = (w_inter * jnp.dot(qc, c_ref[...].astype(BF16), preferred_element_type=F32)
               + jnp.dot(P.astype(BF16), vc, preferred_element_type=F32))
        qn = (w_inter * jnp.sum(qc.astype(F32) * n_ref[...], axis=1, keepdims=True)
              + jnp.sum(P, axis=1, keepdims=True))
        den = jnp.maximum(jnp.abs(qn), jnp.exp(-m_t))
        hh = num / den
        ms = jnp.mean(hh * hh, axis=1, keepdims=True)
        hn = (hh * lax.rsqrt(ms + EPS)) * gn
        out_ref[rows, :] = (hn * _sigmoid(o_ref[rows, :].astype(F32))).astype(BF16)
        bL = b_row[:, L - 1:L]
        g_col = bL - b_col + li_col
        m_new = jnp.maximum(bL + m, jnp.max(g_col, axis=0, keepdims=True))
        decay = jnp.exp(bL + m - m_new)
        kw = kc.astype(F32) * jnp.exp(g_col - m_new)
        c_ref[...] = decay * c_ref[...] + jnp.dot(kw.T.astype(BF16), vc, preferred_element_type=F32)
        n_ref[...] = decay * n_ref[...] + jnp.sum(kw, axis=0, keepdims=True)
        m_ref[...] = m_new
        return carry
    lax.fori_loop(0, S // L, chunk, 0)


def _mlstm(z, zg, conv_qk, gate_bias, g_mlstm3, *, B, S, col_q, col_k, col_v, col_o):
    T = B * S
    dk = M_HEAD_DIM
    L = MLSTM_CHUNK
    zspec = lambda col: pl.BlockSpec((S, dk), lambda b, h: (b, col + h))
    return pl.pallas_call(
        functools.partial(_mlstm_body, L=L, CB=128),
        grid=(B, M_HEADS),
        in_specs=[
            zspec(col_q), zspec(col_k), zspec(col_v), zspec(col_o),
            pl.BlockSpec((S, LANES), lambda b, h: (b, 0)),
            pl.BlockSpec((CONV_WIDTH, dk), lambda b, h: (0, h)),
            pl.BlockSpec((CONV_WIDTH, dk), lambda b, h: (0, M_HEADS + h)),
            pl.BlockSpec((1, LANES), lambda b, h: (0, 0)),
            pl.BlockSpec((1, 1, dk), lambda b, h: (h, 0, 0)),
        ],
        out_specs=pl.BlockSpec((S, dk), lambda b, h: (b, h)),
        out_shape=jax.ShapeDtypeStruct((T, M_WIDTH), BF16),
        scratch_shapes=[
            pltpu.VMEM((S, dk), BF16), pltpu.VMEM((S, dk), BF16),
            pltpu.VMEM((S, LANES), F32), pltpu.VMEM((S, LANES), F32),
            pltpu.VMEM((8, S), F32), pltpu.VMEM((8, S), F32),
            pltpu.VMEM((S, 1), F32), pltpu.VMEM((S, 1), F32),
            pltpu.VMEM((1, S), F32), pltpu.VMEM((1, S), F32),
            pltpu.VMEM((dk, dk), F32), pltpu.VMEM((1, dk), F32), pltpu.VMEM((1, 1), F32),
        ],
        compiler_params=_params("parallel", "arbitrary"),
        name="mlstm",
    )(z, z, z, z, zg, conv_qk, conv_qk, gate_bias, g_mlstm3)


def _swa_body(sink_ref, q_ref, k_ref, v_ref, qa_ref, qb_ref, ka_ref, kb_ref, bd_ref, rep_ref, ones_ref, out_ref,
              kbd_ref, vbd_ref):
    n = pl.program_id(1)
    W = WINDOW
    hd = A_HEAD_DIM
    gw = A_GROUP * hd

    def norm_rope(x, ta, tb):
        x2 = x * x
        x2h = x2.astype(BF16)
        x2l = (x2 - x2h.astype(F32)).astype(BF16)
        ss = (jnp.dot(x2h, bd_ref[...], preferred_element_type=F32)
              + jnp.dot(x2l, bd_ref[...], preferred_element_type=F32))
        r = lax.rsqrt(ss * (1.0 / hd) + EPS)
        ln = lax.broadcasted_iota(jnp.int32, x.shape, 1) % hd
        swapped = jnp.where(ln < hd // 2, pltpu.roll(x, LANES - hd // 2, axis=1), pltpu.roll(x, hd // 2, axis=1))
        return r * (x * ta + swapped * tb)

    lane_head = lax.broadcasted_iota(jnp.int32, (W, gw), 1) // hd
    t_idx = lax.broadcasted_iota(jnp.int32, (W, W), 0)
    k_idx = lax.broadcasted_iota(jnp.int32, (W, W), 1)
    mask_cur = k_idx <= t_idx
    neg = -jnp.inf

    @pl.when(n == 0)
    def _():
        kbd_ref[1] = jnp.zeros(kbd_ref.shape[1:], BF16)
        vbd_ref[1] = jnp.zeros(vbd_ref.shape[1:], BF16)

    def block(half, prev_slot, cur_slot, mask_prev):
        rows = slice(half * W, (half + 1) * W)
        qa = qa_ref[rows, :]
        qb = qb_ref[rows, :]
        qp = jnp.concatenate(
            [norm_rope(q_ref[rows, t * LANES:(t + 1) * LANES].astype(F32), qa, qb) for t in range(A_WIDTH // LANES)],
            axis=1).astype(BF16)
        ka = ka_ref[rows, :]
        kb = kb_ref[rows, :]
        kp = jnp.concatenate(
            [norm_rope(k_ref[rows, t * LANES:(t + 1) * LANES].astype(F32), ka, kb)
             for t in range(A_KV_WIDTH // LANES)], axis=1).astype(BF16)
        vv = v_ref[rows, :]
        for j in range(A_KV_HEADS):
            krep = jnp.dot(kp, rep_ref[j], preferred_element_type=F32).astype(BF16)
            vrep = jnp.dot(vv, rep_ref[j], preferred_element_type=F32).astype(BF16)
            for i in range(A_GROUP):
                kbd_ref[cur_slot, j, i * W:(i + 1) * W, :] = jnp.where(lane_head == i, krep, jnp.zeros_like(krep))
                vbd_ref[cur_slot, j, i * W:(i + 1) * W, :] = jnp.where(lane_head == i, vrep, jnp.zeros_like(vrep))

        for j in range(A_KV_HEADS):
            qg = qp[:, j * gw:(j + 1) * gw]
            s_prev = lax.dot_general(qg, kbd_ref[prev_slot, j], (((1,), (1,)), ((), ())), preferred_element_type=F32)
            s_cur = lax.dot_general(qg, kbd_ref[cur_slot, j], (((1,), (1,)), ((), ())), preferred_element_type=F32)
            pp, pc = [], []
            sink_term = jnp.zeros((W, gw), F32)
            for i in range(A_GROUP):
                sp = jnp.where(mask_prev, s_prev[:, i * W:(i + 1) * W], neg)
                sc = jnp.where(mask_cur, s_cur[:, i * W:(i + 1) * W], neg)
                sink = sink_ref[j * A_GROUP + i]
                mx = jnp.maximum(jnp.max(jnp.maximum(sp, sc), axis=1, keepdims=True), sink)
                pp.append(jnp.exp(sp - mx).astype(BF16))
                pc.append(jnp.exp(sc - mx).astype(BF16))
                sink_term = jnp.where(lane_head == i, jnp.exp(sink - mx), sink_term)
            p_prev = jnp.concatenate(pp, axis=1)
            p_cur = jnp.concatenate(pc, axis=1)
            o = (jnp.dot(p_prev, vbd_ref[prev_slot, j], preferred_element_type=F32)
                 + jnp.dot(p_cur, vbd_ref[cur_slot, j], preferred_element_type=F32))
            den = (jnp.dot(p_prev, ones_ref[...], preferred_element_type=F32)
                   + jnp.dot(p_cur, ones_ref[...], preferred_element_type=F32)) + sink_term
            out_ref[rows, j * gw:(j + 1) * gw] = (o / den).astype(BF16)

    upper = k_idx > t_idx
    block(0, 1, 0, jnp.logical_and(upper, n > 0))
    block(1, 0, 1, upper)


def _swa(z, sinks, qa, qb, ka, kb, bd, rep, ones_bd, *, B, S, col_q, col_k, col_v):
    T = B * S
    W2 = 2 * WINDOW
    NB = S // W2
    gw = A_GROUP * A_HEAD_DIM
    tab = lambda: pl.BlockSpec((W2, LANES), lambda b, n: (n, 0))
    return pl.pallas_call(
        _swa_body,
        grid=(B, NB),
        in_specs=[
            pl.BlockSpec(memory_space=pltpu.SMEM),
            pl.BlockSpec((W2, A_WIDTH), lambda b, n: (b * NB + n, col_q)),
            pl.BlockSpec((W2, A_KV_WIDTH), lambda b, n: (b * NB + n, col_k)),
            pl.BlockSpec((W2, A_KV_WIDTH), lambda b, n: (b * NB + n, col_v)),
            tab(), tab(), tab(), tab(),
            pl.BlockSpec((LANES, LANES), lambda b, n: (0, 0)),
            pl.BlockSpec((A_KV_HEADS, gw, gw), lambda b, n: (0, 0, 0)),
            pl.BlockSpec((A_GROUP * WINDOW, gw), lambda b, n: (0, 0)),
        ],
        out_specs=pl.BlockSpec((W2, A_WIDTH), lambda b, n: (b * NB + n, 0)),
        out_shape=jax.ShapeDtypeStruct((T, A_WIDTH), BF16),
        scratch_shapes=[
            pltpu.VMEM((2, A_KV_HEADS, A_GROUP * WINDOW, gw), BF16),
            pltpu.VMEM((2, A_KV_HEADS, A_GROUP * WINDOW, gw), BF16),
        ],
        compiler_params=_params("parallel", "arbitrary"),
        name="swa",
    )(sinks, z, z, z, qa, qb, ka, kb, bd, rep, ones_bd)


def _merge_body(hm_ref, ha_ref, gm_ref, ga_ref, x_ref, wm_ref, wa_ref, wo_ref, gf_ref, rh_ref, rl_ref, rb_ref,
                x1_ref, hf_ref, lg_ref):
    pm = jnp.dot(hm_ref[...], wm_ref[...], preferred_element_type=F32)
    pa = jnp.dot(ha_ref[...], wa_ref[...], preferred_element_type=F32)
    mixed = _sigmoid(gm_ref[...].astype(F32)) * pm + _sigmoid(ga_ref[...].astype(F32)) * pa
    x1 = x_ref[...] + jnp.dot(mixed.astype(BF16), wo_ref[...], preferred_element_type=F32)
    x1_ref[...] = x1
    ms = jnp.mean(x1 * x1, axis=-1, keepdims=True)
    hf = (x1 * lax.rsqrt(ms + EPS)) * gf_ref[...]
    _store_token_rows(hf_ref, hf)
    hh = hf.astype(BF16)
    hl = (hf - hh.astype(F32)).astype(BF16)
    lg_ref[...] = (jnp.dot(hh, rh_ref[...], preferred_element_type=F32)
                   + jnp.dot(hl, rh_ref[...], preferred_element_type=F32)
                   + jnp.dot(hh, rl_ref[...], preferred_element_type=F32)) + rb_ref[...]


def _merge(hm, ha, z, x2, wm, wa, wo, gf, rh, rl, rb, *, bm, col_gm, col_ga):
    T, D = x2.shape
    const = lambda shape: pl.BlockSpec(shape, lambda i: (0,) * len(shape), pipeline_mode=pl.Buffered(1))
    return pl.pallas_call(
        _merge_body,
        grid=(T // bm,),
        in_specs=[
            pl.BlockSpec((bm, M_WIDTH), lambda i: (i, 0)),
            pl.BlockSpec((bm, A_WIDTH), lambda i: (i, 0)),
            pl.BlockSpec((bm, D), lambda i: (i, col_gm)),
            pl.BlockSpec((bm, D), lambda i: (i, col_ga)),
            pl.BlockSpec((bm, D), lambda i: (i, 0)),
            const((M_WIDTH, D)), const((A_WIDTH, D)), const((D, D)), const((1, D)),
            const((D, LANES)), const((D, LANES)), const((1, LANES)),
        ],
        out_specs=[
            pl.BlockSpec((bm, D), lambda i: (i, 0)),
            pl.BlockSpec((bm * (D // (2 * LANES)), LANES), lambda i: (i, 0)),
            pl.BlockSpec((bm, LANES), lambda i: (i, 0)),
        ],
        out_shape=[
            jax.ShapeDtypeStruct((T, D), F32),
            jax.ShapeDtypeStruct((T * (D // (2 * LANES)), LANES), jnp.uint32),
            jax.ShapeDtypeStruct((T, LANES), F32),
        ],
        compiler_params=_params("parallel"),
        name="merge",
    )(hm, ha, z, z, x2, wm, wa, wo, gf, rh, rl, rb)


def _moe_body(blk_e_ref, first_ref, nxt_ref, nused_ref, tok_ref, hf_hbm, wg_hbm, wu_hbm, wd_hbm, y_ref,
              xbuf, xs_ref, stage_g, stage_u, stage_d, wg_ref, wu_ref, wd_ref, sem, wsem, *, R, CB):
    i = pl.program_id(0)
    nused = nused_ref[0]
    sub = xs_ref.shape[1] // (2 * LANES)

    def weight_copies(e):
        return (pltpu.make_async_copy(wg_hbm.at[e], stage_g, wsem.at[0]),
                pltpu.make_async_copy(wu_hbm.at[e], stage_u, wsem.at[1]),
                pltpu.make_async_copy(wd_hbm.at[e], stage_d, wsem.at[2]))

    def cast(src, dst):
        def rows(r, carry):
            sl = pl.ds(pl.multiple_of(r * CB, CB), CB)
            dst[sl, :] = src[sl, :].astype(BF16)
            return carry
        lax.fori_loop(0, src.shape[0] // CB, rows, 0)

    def start_gather(blk, slot):
        base = blk * R

        def row(r, carry):
            src = pl.multiple_of(tok_ref[base + r] * sub, sub)
            dst = pl.multiple_of(r * sub, sub)
            pltpu.make_async_copy(hf_hbm.at[pl.ds(src, sub)], xbuf.at[slot, pl.ds(dst, sub)], sem.at[slot]).start()
            return carry
        lax.fori_loop(0, R, row, 0, unroll=8)

    def wait_gather(slot):
        pltpu.make_async_copy(xbuf.at[slot], xbuf.at[slot], sem.at[slot]).wait()

    slot = i % 2

    @pl.when(i == 0)
    def _():
        for c in weight_copies(blk_e_ref[0]):
            c.start(priority=1)
        start_gather(0, 0)

    @pl.when(i < nused)
    def _():
        @pl.when(first_ref[i] == 1)
        def _():
            for c in weight_copies(blk_e_ref[i]):
                c.wait()
            cast(stage_g, wg_ref)
            cast(stage_u, wu_ref)
            cast(stage_d, wd_ref)

            @pl.when(nxt_ref[i] >= 0)
            def _():
                for c in weight_copies(nxt_ref[i]):
                    c.start(priority=1)

        wait_gather(slot)

        @pl.when(i + 1 < nused)
        def _():
            start_gather(i + 1, 1 - slot)

        for s in range(sub):
            lo, hi = _load_token_rows(xbuf.at[slot], R, sub, s)
            xs_ref[:, 2 * s * LANES:(2 * s + 1) * LANES] = lo.astype(BF16)
            xs_ref[:, (2 * s + 1) * LANES:(2 * s + 2) * LANES] = hi.astype(BF16)
        xv = xs_ref[...]
        g = jnp.dot(xv, wg_ref[...], preferred_element_type=F32)
        u = jnp.dot(xv, wu_ref[...], preferred_element_type=F32)
        hmid = ((g * _sigmoid(g)) * u).astype(BF16)
        _store_token_rows(y_ref, jnp.dot(hmid, wd_ref[...], preferred_element_type=F32))


def _moe(blk_e, first, nxt, nused, row_tok, hf, wg, wu, wd, *, R):
    E, D, F = wg.shape
    sub = D // (2 * LANES)
    n_blocks = blk_e.shape[0]
    hbm = lambda: pl.BlockSpec(memory_space=pl.ANY)
    grid_spec = pltpu.PrefetchScalarGridSpec(
        num_scalar_prefetch=5,
        grid=(n_blocks,),
        in_specs=[hbm(), hbm(), hbm(), hbm()],
        out_specs=pl.BlockSpec((R * sub, LANES), lambda i, be, fi, nx, nu, tk: (jnp.minimum(i, nu[0] - 1), 0)),
        scratch_shapes=[
            pltpu.VMEM((2, R * sub, LANES), jnp.uint32), pltpu.VMEM((R, D), BF16),
            pltpu.VMEM((D, F), F32), pltpu.VMEM((D, F), F32), pltpu.VMEM((F, D), F32),
            pltpu.VMEM((D, F), BF16), pltpu.VMEM((D, F), BF16), pltpu.VMEM((F, D), BF16),
            pltpu.SemaphoreType.DMA((2,)), pltpu.SemaphoreType.DMA((3,)),
        ],
    )
    return pl.pallas_call(
        functools.partial(_moe_body, R=R, CB=128),
        grid_spec=grid_spec,
        out_shape=jax.ShapeDtypeStruct((n_blocks * R * sub, LANES), jnp.uint32),
        compiler_params=_params("arbitrary"),
        name="moe",
    )(blk_e, first, nxt, nused, row_tok, hf, wg, wu, wd)


def _combine_body(dest_ref, x1_ref, w_ref, yr_hbm, out_ref, ybuf, sem, *, R):
    i = pl.program_id(0)
    nsteps = pl.num_programs(0)

    sub = out_ref.shape[1] // (2 * LANES)

    def start_gather(step, slot):
        base = step * (R * TOP_K)

        def row(r, carry):
            dst = pl.multiple_of(r * sub, sub)
            for k in range(TOP_K):
                src = pl.multiple_of(dest_ref[base + r * TOP_K + k] * sub, sub)
                pltpu.make_async_copy(yr_hbm.at[pl.ds(src, sub)], ybuf.at[slot, k, pl.ds(dst, sub)],
                                      sem.at[slot]).start()
            return carry
        lax.fori_loop(0, R, row, 0, unroll=4)

    slot = i % 2

    @pl.when(i == 0)
    def _():
        start_gather(0, 0)

    pltpu.make_async_copy(ybuf.at[slot], ybuf.at[slot], sem.at[slot]).wait()

    @pl.when(i + 1 < nsteps)
    def _():
        start_gather(i + 1, 1 - slot)

    w = w_ref[...]
    for s in range(sub):
        halves0 = _load_token_rows(ybuf.at[slot, 0], R, sub, s)
        halves1 = _load_token_rows(ybuf.at[slot, 1], R, sub, s)
        for half in range(2):
            cols = slice((2 * s + half) * LANES, (2 * s + half + 1) * LANES)
            out_ref[:, cols] = x1_ref[:, cols] + (halves0[half] * w[:, 0:1] + halves1[half] * w[:, 1:2])


def _combine(dest, x1, gate_w, yr, *, R):
    T, D = x1.shape
    grid_spec = pltpu.PrefetchScalarGridSpec(
        num_scalar_prefetch=1,
        grid=(T // R,),
        in_specs=[
            pl.BlockSpec((R, D), lambda i, d: (i, 0)),
            pl.BlockSpec((R, TOP_K), lambda i, d: (i, 0)),
            pl.BlockSpec(memory_space=pl.ANY),
        ],
        out_specs=pl.BlockSpec((R, D), lambda i, d: (i, 0)),
        scratch_shapes=[pltpu.VMEM((2, TOP_K, R * (D // (2 * LANES)), LANES), jnp.uint32),
                        pltpu.SemaphoreType.DMA((2,))],
    )
    return pl.pallas_call(
        functools.partial(_combine_body, R=R),
        grid_spec=grid_spec,
        out_shape=jax.ShapeDtypeStruct((T, D), F32),
        compiler_params=_params("arbitrary"),
        name="combine",
    )(dest, x1, gate_w, yr)


def _route_body(lg_ref, tri_ref, ids_ref, gate_ref, cnt_ref):
    @pl.when(pl.program_id(0) == 0)
    def _():
        cnt_ref[...] = jnp.zeros_like(cnt_ref)

    lg = lg_ref[...]
    shape = lg.shape
    lane = lax.broadcasted_iota(jnp.int32, shape, 1)
    big = jnp.int32(LANES)

    def softmax_masked(mask):
        v = jnp.where(mask, lg, -jnp.inf)
        u = jnp.exp(v - jnp.max(v, axis=1, keepdims=True))
        return jnp.where(mask, u / jnp.sum(u, axis=1, keepdims=True), -1.0)

    def top1(p):
        best = jnp.max(p, axis=1, keepdims=True)
        idx = jnp.min(jnp.where(p == best, lane, big), axis=1, keepdims=True)
        return best, idx

    g_p, g_lane = top1(softmax_masked(lane < N_GROUPS))
    grp_of_lane = lax.shift_right_arithmetic(lane - N_GROUPS, jnp.int32(3))
    in_grp = jnp.logical_and(lane >= N_GROUPS, grp_of_lane == g_lane)
    in_grp = jnp.logical_and(in_grp, lane < N_GROUPS + N_EXPERTS)
    pe = softmax_masked(in_grp)
    p1, l1 = top1(pe)
    p2, l2 = top1(jnp.where(lane == l1, -1.0, pe))
    tot = p1 + p2
    gate1 = g_p * (p1 / tot)
    gate2 = g_p * (p2 / tot)

    hot1 = lane == l1
    hot2 = lane == l2
    hot = jnp.logical_or(hot1, hot2)
    before = jnp.dot(tri_ref[...], jnp.where(hot, 1.0, 0.0).astype(BF16), preferred_element_type=F32) + cnt_ref[...]
    rank1 = jnp.sum(jnp.where(hot1, before, 0.0), axis=1, keepdims=True)
    rank2 = jnp.sum(jnp.where(hot2, before, 0.0), axis=1, keepdims=True)
    cnt_ref[...] = cnt_ref[...] + jnp.sum(jnp.where(hot, 1.0, 0.0), axis=0, keepdims=True)

    ids = jnp.where(lane == 0, l1 - N_GROUPS, jnp.where(lane == 1, l2 - N_GROUPS, 0))
    ids = jnp.where(lane == 2, rank1.astype(jnp.int32), jnp.where(lane == 3, rank2.astype(jnp.int32), ids))
    ids_ref[...] = ids
    gate_ref[...] = jnp.where(lane == 0, gate1, jnp.where(lane == 1, gate2, 0.0))


def _route_tokens(logits, *, tb):
    T = logits.shape[0]
    r = jnp.arange(tb)
    tri = (r[None, :] < r[:, None]).astype(BF16)
    return pl.pallas_call(
        _route_body,
        grid=(T // tb,),
        in_specs=[
            pl.BlockSpec((tb, LANES), lambda i: (i, 0)),
            pl.BlockSpec((tb, tb), lambda i: (0, 0)),
        ],
        out_specs=[
            pl.BlockSpec((tb, LANES), lambda i: (i, 0)),
            pl.BlockSpec((tb, LANES), lambda i: (i, 0)),
            pl.BlockSpec((1, LANES), lambda i: (0, 0)),
        ],
        out_shape=[
            jax.ShapeDtypeStruct((T, LANES), jnp.int32),
            jax.ShapeDtypeStruct((T, LANES), F32),
            jax.ShapeDtypeStruct((1, LANES), F32),
        ],
        compiler_params=_params("arbitrary"),
        name="route",
    )(logits, tri)


def _route(logits, R):
    T = logits.shape[0]
    ids, gates, cnt = _route_tokens(logits, tb=256)
    gate = gates[:, 0:TOP_K]
    M = T * TOP_K
    eid_f = ids[:, 0:TOP_K].reshape(M)
    rank = ids[:, TOP_K:2 * TOP_K].reshape(M)
    counts = cnt[0, N_GROUPS:N_GROUPS + N_EXPERTS].astype(jnp.int32)
    padded = (counts + R - 1) // R * R
    pend = jnp.cumsum(padded)
    pstart = pend - padded
    dest = (pstart[eid_f] + rank).astype(jnp.int32)
    n_blocks = -(-M // R) + N_EXPERTS
    tok_f = jnp.arange(M, dtype=jnp.int32) // TOP_K
    row_tok = jnp.zeros((n_blocks * R,), jnp.int32).at[dest].set(tok_f)
    blk_start = jnp.arange(n_blocks, dtype=jnp.int32) * R
    blk_e = jnp.sum((pend[None, :] <= blk_start[:, None]).astype(jnp.int32), axis=1)
    nused = (pend[-1] // R).astype(jnp.int32)
    last_e = blk_e[jnp.maximum(nused - 1, 0)]
    blk_e = jnp.where(jnp.arange(n_blocks) < nused, blk_e, last_e)
    blk_e = jnp.minimum(blk_e, N_EXPERTS - 1).astype(jnp.int32)
    e_idx = jnp.arange(N_EXPERTS, dtype=jnp.int32)
    cand = jnp.where(counts > 0, e_idx, N_EXPERTS)
    sfx = lax.cummin(cand[::-1])[::-1]
    nxt_of_e = jnp.concatenate([sfx[1:], jnp.full((1,), N_EXPERTS, jnp.int32)])
    nxt_of_e = jnp.where(nxt_of_e < N_EXPERTS, nxt_of_e, -1)
    nxt = nxt_of_e[blk_e].astype(jnp.int32)
    changed = jnp.concatenate([jnp.ones((1,), bool), blk_e[1:] != blk_e[:-1]])
    first = jnp.logical_and(changed, jnp.arange(n_blocks) < nused).astype(jnp.int32)
    return gate, dest, row_tok, blk_e, first, nxt, nused.reshape(1)


def _rope_tables(S, gain, scale):
    half = A_HEAD_DIM // 2
    freqs = ROPE_THETA ** (-jnp.arange(half, dtype=F32) / half)
    ang = jnp.arange(S, dtype=F32)[:, None] * freqs[None, :]
    cos = jnp.cos(ang)
    sin = jnp.sin(ang)
    g1, g2 = gain[:half], gain[half:]
    a_head = jnp.concatenate([cos * g1, cos * g2], axis=1)
    b_head = jnp.concatenate([-sin * g2, sin * g1], axis=1)
    reps = LANES // A_HEAD_DIM
    return jnp.tile(a_head, (1, reps)) * scale, jnp.tile(b_head, (1, reps)) * scale


def kernel(x, g_mix, w_in, conv_qk, b_igate, b_fgate, g_mlstm, g_q, g_k, sinks, w_proj_m, w_proj_a, w_out,
           g_ffn, w_group, b_group, w_expert, b_expert, w_gate, w_up, w_down):
    B, S, D = x.shape
    T = B * S
    depth = g_mix.shape[0]
    xf = x.reshape(T, D)

    sizes = (M_WIDTH, M_WIDTH, M_WIDTH, M_WIDTH, M_HEADS, M_HEADS, A_WIDTH, A_KV_WIDTH, A_KV_WIDTH, D, D)
    offs = [0]
    for s_ in sizes:
        offs.append(offs[-1] + s_)
    seg = lambda w, idx: w[:, offs[idx]:offs[idx + 1]]
    order = (9, 10, 0, 1, 2, 3, 6, 7, 8)
    new_off = {}
    acc = 0
    for idx in order:
        new_off[idx] = acc
        acc += sizes[idx]
    dk = M_HEAD_DIM

    lane128 = jnp.arange(LANES)
    bd = (lane128[:, None] // A_HEAD_DIM == lane128[None, :] // A_HEAD_DIM).astype(BF16)
    gw = A_GROUP * A_HEAD_DIM
    r_idx = jnp.arange(gw)
    rep = jnp.stack([(r_idx[:, None] == (j * A_HEAD_DIM + r_idx[None, :] % A_HEAD_DIM)).astype(BF16)
                     for j in range(A_KV_HEADS)])
    ones_bd = (jnp.arange(A_GROUP * WINDOW)[:, None] // WINDOW == r_idx[None, :] // A_HEAD_DIM).astype(BF16)

    for l in range(depth):
        w_rep = jnp.concatenate([seg(w_in[l], idx) for idx in order], axis=1).astype(BF16)
        w_gates = jnp.concatenate([seg(w_in[l], 4), seg(w_in[l], 5)], axis=1)
        w_gates = jnp.pad(w_gates, ((0, 0), (0, LANES - 2 * M_HEADS))).astype(BF16)
        z, zg = _in_proj(xf, g_mix[l][None, :], w_rep, w_gates, bm=1024 if T % 1024 == 0 else T, bn=512)

        gate_bias = jnp.pad(jnp.concatenate([b_igate[l], b_fgate[l]]), (0, LANES - 2 * M_HEADS))[None, :]
        hm = _mlstm(z, zg, conv_qk[l], gate_bias, g_mlstm[l][:, None, :], B=B, S=S,
                    col_q=new_off[0] // dk, col_k=new_off[1] // dk, col_v=new_off[2] // dk, col_o=new_off[3] // dk)

        qa, qb = _rope_tables(S, g_q[l], A_HEAD_DIM ** -0.5)
        ka, kb = _rope_tables(S, g_k[l], 1.0)
        ha = _swa(z, sinks[l], qa, qb, ka, kb, bd, rep, ones_bd, B=B, S=S,
                  col_q=new_off[6] // A_WIDTH, col_k=new_off[7] // A_KV_WIDTH, col_v=new_off[8] // A_KV_WIDTH)

        w_router = jnp.pad(jnp.concatenate([w_group[l], w_expert[l]], axis=1),
                           ((0, 0), (0, LANES - N_GROUPS - N_EXPERTS)))
        r_hi = w_router.astype(BF16)
        r_lo = (w_router - r_hi.astype(F32)).astype(BF16)
        r_b = jnp.pad(jnp.concatenate([b_group[l], b_expert[l]]), (0, LANES - N_GROUPS - N_EXPERTS))[None, :]
        x1, hf, logits = _merge(hm, ha, z, xf, w_proj_m[l].astype(BF16), w_proj_a[l].astype(BF16),
                                w_out[l].astype(BF16), g_ffn[l][None, :], r_hi, r_lo, r_b,
                                bm=256, col_gm=new_off[9] // D, col_ga=new_off[10] // D)

        gate, dest, row_tok, blk_e, first, nxt, nused = _route(logits, MOE_ROWS)
        yr = _moe(blk_e, first, nxt, nused, row_tok, hf, w_gate[l], w_up[l], w_down[l], R=MOE_ROWS)
        xf = _combine(dest, x1, gate, yr, R=COMBINE_ROWS)
    return xf.reshape(B, S, D)
```

```python
import functools

import jax
import jax.numpy as jnp
from jax import lax
from jax.experimental import pallas as pl
from jax.experimental.pallas import tpu as pltpu

F32 = jnp.float32
BF16 = jnp.bfloat16
EPS = 1e-6
LANES = 128
VMEM_LIMIT = 56 * 1024 * 1024

M_HEADS = 4
M_HEAD_DIM = 256
M_WIDTH = M_HEADS * M_HEAD_DIM
CONV_WIDTH = 4
A_HEADS = 16
A_KV_HEADS = 4
A_GROUP = A_HEADS // A_KV_HEADS
A_HEAD_DIM = 64
A_WIDTH = A_HEADS * A_HEAD_DIM
A_KV_WIDTH = A_KV_HEADS * A_HEAD_DIM
WINDOW = 128
ROPE_THETA = 10000.0
N_GROUPS = 8
EXPERTS_PER_GROUP = 8
N_EXPERTS = N_GROUPS * EXPERTS_PER_GROUP
TOP_K = 2

MLSTM_CHUNK = 128
MOE_ROWS = 256
COMBINE_ROWS = 256


def _sigmoid(v):
    return 1.0 / (1.0 + jnp.exp(-v))


def _params(*sem):
    return pltpu.CompilerParams(dimension_semantics=sem, vmem_limit_bytes=VMEM_LIMIT)


def _pack_bf16_pair(lo, hi):
    lo_b = lax.bitcast_convert_type(lo.astype(BF16).astype(F32), jnp.uint32)
    hi_b = lax.bitcast_convert_type(hi.astype(BF16).astype(F32), jnp.uint32)
    return lax.shift_right_logical(lo_b, jnp.uint32(16)) | hi_b


def _unpack_bf16_pair(word):
    lo = lax.bitcast_convert_type(lax.shift_left(word, jnp.uint32(16)), F32)
    hi = lax.bitcast_convert_type(word & jnp.uint32(0xFFFF0000), F32)
    return lo, hi


def _store_token_rows(ref, val):
    n, d = val.shape
    sub = d // (2 * LANES)
    for s in range(sub):
        c = 2 * s * LANES
        ref[pl.ds(s, n, stride=sub), :] = _pack_bf16_pair(val[:, c:c + LANES], val[:, c + LANES:c + 2 * LANES])


def _load_token_rows(ref, n, sub, s):
    return _unpack_bf16_pair(ref[pl.ds(s, n, stride=sub), :])


def _inproj_body(x_ref, g_ref, w_ref, wgate_ref, z_ref, zg_ref, hn_ref, *, sub):
    bm = x_ref.shape[0]

    @pl.when(pl.program_id(1) == 0)
    def _():
        def rows(r, carry):
            sl = pl.ds(pl.multiple_of(r * sub, sub), sub)
            xv = x_ref[sl, :]
            ms = jnp.mean(xv * xv, axis=-1, keepdims=True)
            hn_ref[sl, :] = ((xv * lax.rsqrt(ms + EPS)) * g_ref[...]).astype(BF16)
            return carry
        lax.fori_loop(0, bm // sub, rows, 0)
        zg_ref[...] = jnp.dot(hn_ref[...], wgate_ref[...], preferred_element_type=F32)

    z_ref[...] = jnp.dot(hn_ref[...], w_ref[...], preferred_element_type=F32).astype(BF16)


def _in_proj(x2, g, w, wgate, *, bm, bn):
    T, D = x2.shape
    N = w.shape[1]
    return pl.pallas_call(
        functools.partial(_inproj_body, sub=128),
        grid=(T // bm, N // bn),
        in_specs=[
            pl.BlockSpec((bm, D), lambda i, j: (i, 0)),
            pl.BlockSpec((1, D), lambda i, j: (0, 0)),
            pl.BlockSpec((D, bn), lambda i, j: (0, j)),
            pl.BlockSpec((D, LANES), lambda i, j: (0, 0)),
        ],
        out_specs=[
            pl.BlockSpec((bm, bn), lambda i, j: (i, j)),
            pl.BlockSpec((bm, LANES), lambda i, j: (i, 0)),
        ],
        out_shape=[
            jax.ShapeDtypeStruct((T, N), BF16),
            jax.ShapeDtypeStruct((T, LANES), F32),
        ],
        scratch_shapes=[pltpu.VMEM((bm, D), BF16)],
        compiler_params=_params("parallel", "arbitrary"),
        name="in_proj",
    )(x2, g, w, wgate)


def _mlstm_body(q_ref, k_ref, v_ref, o_ref, zg_ref, cq_ref, ck_ref, bias_ref, gn_ref, out_ref,
                qs_ref, ks_ref, colli_ref, colb_ref, rowli_ref, rowb_ref, cli_ref, cb_ref, rli_ref, rb_ref,
                c_ref, n_ref, m_ref, *, L, CB):
    h = pl.program_id(1)
    S, dk = q_ref.shape

    @pl.when(h == 0)
    def _():
        G = zg_ref[...] + bias_ref[...]
        lf = jnp.minimum(G, 0.0) - jnp.log1p(jnp.exp(-jnp.abs(G)))
        pos = lax.broadcasted_iota(jnp.int32, (S, LANES), 0) % L
        bc = lf
        sh = 1
        while sh < L:
            bc = bc + jnp.where(pos >= sh, pltpu.roll(bc, sh, axis=0), 0.0)
            sh *= 2
        colli_ref[...] = G
        colb_ref[...] = bc
        for p in range(S // LANES):
            sl = slice(p * LANES, (p + 1) * LANES)
            rowli_ref[:, sl] = G[sl, :].T[0:8, :]
            rowb_ref[:, sl] = bc[sl, :].T[0:8, :]

    lane = lax.broadcasted_iota(jnp.int32, (S, LANES), 1)
    cli_ref[...] = jnp.sum(jnp.where(lane == h, colli_ref[...], 0.0), axis=1, keepdims=True)
    cb_ref[...] = jnp.sum(jnp.where(lane == h + M_HEADS, colb_ref[...], 0.0), axis=1, keepdims=True)
    sub = lax.broadcasted_iota(jnp.int32, (8, S), 0)
    rli_ref[...] = jnp.sum(jnp.where(sub == h, rowli_ref[...], 0.0), axis=0, keepdims=True)
    rb_ref[...] = jnp.sum(jnp.where(sub == h + M_HEADS, rowb_ref[...], 0.0), axis=0, keepdims=True)

    def conv_silu(src_ref, w_ref, dst_ref, scale):
        w = w_ref[...]

        def taps(xs):
            y = xs[3] * w[0:1, :]
            y = y + xs[2] * w[1:2, :]
            y = y + xs[1] * w[2:3, :]
            y = y + xs[0] * w[3:4, :]
            return ((y * _sigmoid(y)) * scale).astype(BF16)

        def chunk(r, carry):
            r0 = pl.multiple_of(r * CB, CB)
            cur = src_ref[pl.ds(r0, CB), :].astype(F32)
            dst_ref[pl.ds(r0, CB), :] = taps([cur] + [pltpu.roll(cur, d, axis=0) for d in (1, 2, 3)])
            p0 = pl.multiple_of(jnp.maximum(r0 - 16, 0), 16)
            prev = src_ref[pl.ds(p0, 16), :].astype(F32)[8:16, :]
            prev = jnp.where(r > 0, prev, 0.0)
            both = jnp.concatenate([prev, cur[0:16, :]], axis=0)
            dst_ref[pl.ds(r0, 16), :] = taps([both[8:24, :]] + [pltpu.roll(both, d, axis=0)[8:24, :] for d in (1, 2, 3)])
            return carry
        lax.fori_loop(0, S // CB, chunk, 0)

    conv_silu(q_ref, cq_ref, qs_ref, 1.0)
    conv_silu(k_ref, ck_ref, ks_ref, float(dk) ** -0.5)

    c_ref[...] = jnp.zeros_like(c_ref)
    n_ref[...] = jnp.zeros_like(n_ref)
    m_ref[...] = jnp.zeros_like(m_ref)
    t_idx = lax.broadcasted_iota(jnp.int32, (L, L), 0)
    s_idx = lax.broadcasted_iota(jnp.int32, (L, L), 1)
    causal = s_idx <= t_idx
    gn = gn_ref[0]

    def chunk(c, carry):
        r0 = pl.multiple_of(c * L, L)
        rows = pl.ds(r0, L)
        qc = qs_ref[rows, :]
        kc = ks_ref[rows, :]
        vc = v_ref[rows, :]
        b_col = cb_ref[rows, :]
        li_col = cli_ref[rows, :]
        li_row = rli_ref[:, rows]
        b_row = rb_ref[:, rows]
        m = m_ref[...]
        a = b_col + m
        D = jnp.where(causal, b_col - b_row + li_row, -jnp.inf)
        m_t = jnp.maximum(a, jnp.max(D, axis=1, keepdims=True))
        w_inter = jnp.exp(a - m_t)
        P = lax.dot_general(qc, kc, (((1,), (1,)), ((), ())), preferred_element_type=F32) * jnp.exp(D - m_t)
        num = (w_inter * jnp.dot(qc, c_ref[...].astype(BF16), preferred_element_type=F32)
               + jnp.dot(P.astype(BF16), vc, preferred_element_type=F32))
        qn = (w_inter * jnp.sum(qc.astype(F32) * n_ref[...], axis=1, keepdims=True)
              + jnp.sum(P, axis=1, keepdims=True))
        den = jnp.maximum(jnp.abs(qn), jnp.exp(-m_t))
        hh = num / den
        ms = jnp.mean(hh * hh, axis=1, keepdims=True)
        hn = (hh * lax.rsqrt(ms + EPS)) * gn
        out_ref[rows, :] = (hn * _sigmoid(o_ref[rows, :].astype(F32))).astype(BF16)
        bL = b_row[:, L - 1:L]
        g_col = bL - b_col + li_col
        m_new = jnp.maximum(bL + m, jnp.max(g_col, axis=0, keepdims=True))
        decay = jnp.exp(bL + m - m_new)
        kw = kc.astype(F32) * jnp.exp(g_col - m_new)
        c_ref[...] = decay * c_ref[...] + jnp.dot(kw.T.astype(BF16), vc, preferred_element_type=F32)
        n_ref[...] = decay * n_ref[...] + jnp.sum(kw, axis=0, keepdims=True)
        m_ref[...] = m_new
        return carry
    lax.fori_loop(0, S // L, chunk, 0)


def _mlstm(z, zg, conv_qk, gate_bias, g_mlstm3, *, B, S, col_q, col_k, col_v, col_o):
    T = B * S
    dk = M_HEAD_DIM
    L = MLSTM_CHUNK
    zspec = lambda col: pl.BlockSpec((S, dk), lambda b, h: (b, col + h))
    return pl.pallas_call(
        functools.partial(_mlstm_body, L=L, CB=128),
        grid=(B, M_HEADS),
        in_specs=[
            zspec(col_q), zspec(col_k), zspec(col_v), zspec(col_o),
            pl.BlockSpec((S, LANES), lambda b, h: (b, 0)),
            pl.BlockSpec((CONV_WIDTH, dk), lambda b, h: (0, h)),
            pl.BlockSpec((CONV_WIDTH, dk), lambda b, h: (0, M_HEADS + h)),
            pl.BlockSpec((1, LANES), lambda b, h: (0, 0)),
            pl.BlockSpec((1, 1, dk), lambda b, h: (h, 0, 0)),
        ],
        out_specs=pl.BlockSpec((S, dk), lambda b, h: (b, h)),
        out_shape=jax.ShapeDtypeStruct((T, M_WIDTH), BF16),
        scratch_shapes=[
            pltpu.VMEM((S, dk), BF16), pltpu.VMEM((S, dk), BF16),
            pltpu.VMEM((S, LANES), F32), pltpu.VMEM((S, LANES), F32),
            pltpu.VMEM((8, S), F32), pltpu.VMEM((8, S), F32),
            pltpu.VMEM((S, 1), F32), pltpu.VMEM((S, 1), F32),
            pltpu.VMEM((1, S), F32), pltpu.VMEM((1, S), F32),
            pltpu.VMEM((dk, dk), F32), pltpu.VMEM((1, dk), F32), pltpu.VMEM((1, 1), F32),
        ],
        compiler_params=_params("parallel", "arbitrary"),
        name="mlstm",
    )(z, z, z, z, zg, conv_qk, conv_qk, gate_bias, g_mlstm3)


def _swa_body(sink_ref, q_ref, k_ref, v_ref, qa_ref, qb_ref, ka_ref, kb_ref, bd_ref, rep_ref, ones_ref, out_ref,
              kbd_ref, vbd_ref):
    n = pl.program_id(1)
    W = WINDOW
    hd = A_HEAD_DIM
    gw = A_GROUP * hd

    def norm_rope(x, ta, tb):
        x2 = x * x
        x2h = x2.astype(BF16)
        x2l = (x2 - x2h.astype(F32)).astype(BF16)
        ss = (jnp.dot(x2h, bd_ref[...], preferred_element_type=F32)
              + jnp.dot(x2l, bd_ref[...], preferred_element_type=F32))
        r = lax.rsqrt(ss * (1.0 / hd) + EPS)
        ln = lax.broadcasted_iota(jnp.int32, x.shape, 1) % hd
        swapped = jnp.where(ln < hd // 2, pltpu.roll(x, LANES - hd // 2, axis=1), pltpu.roll(x, hd // 2, axis=1))
        return r * (x * ta + swapped * tb)

    lane_head = lax.broadcasted_iota(jnp.int32, (W, gw), 1) // hd
    t_idx = lax.broadcasted_iota(jnp.int32, (W, W), 0)
    k_idx = lax.broadcasted_iota(jnp.int32, (W, W), 1)
    mask_cur = k_idx <= t_idx
    neg = -jnp.inf

    @pl.when(n == 0)
    def _():
        kbd_ref[1] = jnp.zeros(kbd_ref.shape[1:], BF16)
        vbd_ref[1] = jnp.zeros(vbd_ref.shape[1:], BF16)

    def block(half, prev_slot, cur_slot, mask_prev):
        rows = slice(half * W, (half + 1) * W)
        qa = qa_ref[rows, :]
        qb = qb_ref[rows, :]
        qp = jnp.concatenate(
            [norm_rope(q_ref[rows, t * LANES:(t + 1) * LANES].astype(F32), qa, qb) for t in range(A_WIDTH // LANES)],
            axis=1).astype(BF16)
        ka = ka_ref[rows, :]
        kb = kb_ref[rows, :]
        kp = jnp.concatenate(
            [norm_rope(k_ref[rows, t * LANES:(t + 1) * LANES].astype(F32), ka, kb)
             for t in range(A_KV_WIDTH // LANES)], axis=1).astype(BF16)
        vv = v_ref[rows, :]
        for j in range(A_KV_HEADS):
            krep = jnp.dot(kp, rep_ref[j], preferred_element_type=F32).astype(BF16)
            vrep = jnp.dot(vv, rep_ref[j], preferred_element_type=F32).astype(BF16)
            for i in range(A_GROUP):
                kbd_ref[cur_slot, j, i * W:(i + 1) * W, :] = jnp.where(lane_head == i, krep, jnp.zeros_like(krep))
                vbd_ref[cur_slot, j, i * W:(i + 1) * W, :] = jnp.where(lane_head == i, vrep, jnp.zeros_like(vrep))

        for j in range(A_KV_HEADS):
            qg = qp[:, j * gw:(j + 1) * gw]
            s_prev = lax.dot_general(qg, kbd_ref[prev_slot, j], (((1,), (1,)), ((), ())), preferred_element_type=F32)
            s_cur = lax.dot_general(qg, kbd_ref[cur_slot, j], (((1,), (1,)), ((), ())), preferred_element_type=F32)
            pp, pc = [], []
            sink_term = jnp.zeros((W, gw), F32)
            for i in range(A_GROUP):
                sp = jnp.where(mask_prev, s_prev[:, i * W:(i + 1) * W], neg)
                sc = jnp.where(mask_cur, s_cur[:, i * W:(i + 1) * W], neg)
                sink = sink_ref[j * A_GROUP + i]
                mx = jnp.maximum(jnp.max(jnp.maximum(sp, sc), axis=1, keepdims=True), sink)
                pp.append(jnp.exp(sp - mx).astype(BF16))
                pc.append(jnp.exp(sc - mx).astype(BF16))
                sink_term = jnp.where(lane_head == i, jnp.exp(sink - mx), sink_term)
            p_prev = jnp.concatenate(pp, axis=1)
            p_cur = jnp.concatenate(pc, axis=1)
            o = (jnp.dot(p_prev, vbd_ref[prev_slot, j], preferred_element_type=F32)
                 + jnp.dot(p_cur, vbd_ref[cur_slot, j], preferred_element_type=F32))
            den = (jnp.dot(p_prev, ones_ref[...], preferred_element_type=F32)
                   + jnp.dot(p_cur, ones_ref[...], preferred_element_type=F32)) + sink_term
            out_ref[rows, j * gw:(j + 1) * gw] = (o / den).astype(BF16)

    upper = k_idx > t_idx
    block(0, 1, 0, jnp.logical_and(upper, n > 0))
    block(1, 0, 1, upper)


def _swa(z, sinks, qa, qb, ka, kb, bd, rep, ones_bd, *, B, S, col_q, col_k, col_v):
    T = B * S
    W2 = 2 * WINDOW
    NB = S // W2
    gw = A_GROUP * A_HEAD_DIM
    tab = lambda: pl.BlockSpec((W2, LANES), lambda b, n: (n, 0))
    return pl.pallas_call(
        _swa_body,
        grid=(B, NB),
        in_specs=[
            pl.BlockSpec(memory_space=pltpu.SMEM),
            pl.BlockSpec((W2, A_WIDTH), lambda b, n: (b * NB + n, col_q)),
            pl.BlockSpec((W2, A_KV_WIDTH), lambda b, n: (b * NB + n, col_k)),
            pl.BlockSpec((W2, A_KV_WIDTH), lambda b, n: (b * NB + n, col_v)),
            tab(), tab(), tab(), tab(),
            pl.BlockSpec((LANES, LANES), lambda b, n: (0, 0)),
            pl.BlockSpec((A_KV_HEADS, gw, gw), lambda b, n: (0, 0, 0)),
            pl.BlockSpec((A_GROUP * WINDOW, gw), lambda b, n: (0, 0)),
        ],
        out_specs=pl.BlockSpec((W2, A_WIDTH), lambda b, n: (b * NB + n, 0)),
        out_shape=jax.ShapeDtypeStruct((T, A_WIDTH), BF16),
        scratch_shapes=[
            pltpu.VMEM((2, A_KV_HEADS, A_GROUP * WINDOW, gw), BF16),
            pltpu.VMEM((2, A_KV_HEADS, A_GROUP * WINDOW, gw), BF16),
        ],
        compiler_params=_params("parallel", "arbitrary"),
        name="swa",
    )(sinks, z, z, z, qa, qb, ka, kb, bd, rep, ones_bd)


def _merge_body(hm_ref, ha_ref, gm_ref, ga_ref, x_ref, wm_ref, wa_ref, wo_ref, gf_ref, rh_ref, rl_ref, rb_ref,
                x1_ref, hf_ref, lg_ref):
    pm = jnp.dot(hm_ref[...], wm_ref[...], preferred_element_type=F32)
    pa = jnp.dot(ha_ref[...], wa_ref[...], preferred_element_type=F32)
    mixed = _sigmoid(gm_ref[...].astype(F32)) * pm + _sigmoid(ga_ref[...].astype(F32)) * pa
    x1 = x_ref[...] + jnp.dot(mixed.astype(BF16), wo_ref[...], preferred_element_type=F32)
    x1_ref[...] = x1
    ms = jnp.mean(x1 * x1, axis=-1, keepdims=True)
    hf = (x1 * lax.rsqrt(ms + EPS)) * gf_ref[...]
    _store_token_rows(hf_ref, hf)
    hh = hf.astype(BF16)
    hl = (hf - hh.astype(F32)).astype(BF16)
    lg_ref[...] = (jnp.dot(hh, rh_ref[...], preferred_element_type=F32)
                   + jnp.dot(hl, rh_ref[...], preferred_element_type=F32)
                   + jnp.dot(hh, rl_ref[...], preferred_element_type=F32)) + rb_ref[...]


def _merge(hm, ha, z, x2, wm, wa, wo, gf, rh, rl, rb, *, bm, col_gm, col_ga):
    T, D = x2.shape
    const = lambda shape: pl.BlockSpec(shape, lambda i: (0,) * len(shape), pipeline_mode=pl.Buffered(1))
    return pl.pallas_call(
        _merge_body,
        grid=(T // bm,),
        in_specs=[
            pl.BlockSpec((bm, M_WIDTH), lambda i: (i, 0)),
            pl.BlockSpec((bm, A_WIDTH), lambda i: (i, 0)),
            pl.BlockSpec((bm, D), lambda i: (i, col_gm)),
            pl.BlockSpec((bm, D), lambda i: (i, col_ga)),
            pl.BlockSpec((bm, D), lambda i: (i, 0)),
            const((M_WIDTH, D)), const((A_WIDTH, D)), const((D, D)), const((1, D)),
            const((D, LANES)), const((D, LANES)), const((1, LANES)),
        ],
        out_specs=[
            pl.BlockSpec((bm, D), lambda i: (i, 0)),
            pl.BlockSpec((bm * (D // (2 * LANES)), LANES), lambda i: (i, 0)),
            pl.BlockSpec((bm, LANES), lambda i: (i, 0)),
        ],
        out_shape=[
            jax.ShapeDtypeStruct((T, D), F32),
            jax.ShapeDtypeStruct((T * (D // (2 * LANES)), LANES), jnp.uint32),
            jax.ShapeDtypeStruct((T, LANES), F32),
        ],
        compiler_params=_params("parallel"),
        name="merge",
    )(hm, ha, z, z, x2, wm, wa, wo, gf, rh, rl, rb)


def _moe_body(blk_e_ref, first_ref, nxt_ref, nused_ref, tok_ref, hf_hbm, wg_hbm, wu_hbm, wd_hbm, y_ref,
              xbuf, xs_ref, stage_g, stage_u, stage_d, wg_ref, wu_ref, wd_ref, sem, wsem, *, R, CB):
    i = pl.program_id(0)
    nused = nused_ref[0]
    sub = xs_ref.shape[1] // (2 * LANES)

    def weight_copies(e):
        return (pltpu.make_async_copy(wg_hbm.at[e], stage_g, wsem.at[0]),
                pltpu.make_async_copy(wu_hbm.at[e], stage_u, wsem.at[1]),
                pltpu.make_async_copy(wd_hbm.at[e], stage_d, wsem.at[2]))

    def cast(src, dst):
        def rows(r, carry):
            sl = pl.ds(pl.multiple_of(r * CB, CB), CB)
            dst[sl, :] = src[sl, :].astype(BF16)
            return carry
        lax.fori_loop(0, src.shape[0] // CB, rows, 0)

    def start_gather(blk, slot):
        base = blk * R

        def row(r, carry):
            src = pl.multiple_of(tok_ref[base + r] * sub, sub)
            dst = pl.multiple_of(r * sub, sub)
            pltpu.make_async_copy(hf_hbm.at[pl.ds(src, sub)], xbuf.at[slot, pl.ds(dst, sub)], sem.at[slot]).start()
            return carry
        lax.fori_loop(0, R, row, 0, unroll=8)

    def wait_gather(slot):
        pltpu.make_async_copy(xbuf.at[slot], xbuf.at[slot], sem.at[slot]).wait()

    slot = i % 2

    @pl.when(i == 0)
    def _():
        start_gather(0, 0)
        for c in weight_copies(blk_e_ref[0]):
            c.start(priority=1)

    @pl.when(i < nused)
    def _():
        is_first = first_ref[i] == 1

        @pl.when(is_first)
        def _():
            for c in weight_copies(blk_e_ref[i]):
                c.wait()
            cast(stage_g, wg_ref)
            cast(stage_u, wu_ref)
            cast(stage_d, wd_ref)

        wait_gather(slot)

        @pl.when(i + 1 < nused)
        def _():
            start_gather(i + 1, 1 - slot)

        @pl.when(jnp.logical_and(is_first, nxt_ref[i] >= 0))
        def _():
            for c in weight_copies(nxt_ref[i]):
                c.start(priority=1)

        for s in range(sub):
            lo, hi = _load_token_rows(xbuf.at[slot], R, sub, s)
            xs_ref[:, 2 * s * LANES:(2 * s + 1) * LANES] = lo.astype(BF16)
            xs_ref[:, (2 * s + 1) * LANES:(2 * s + 2) * LANES] = hi.astype(BF16)
        xv = xs_ref[...]
        g = jnp.dot(xv, wg_ref[...], preferred_element_type=F32)
        u = jnp.dot(xv, wu_ref[...], preferred_element_type=F32)
        hmid = ((g * _sigmoid(g)) * u).astype(BF16)
        _store_token_rows(y_ref, jnp.dot(hmid, wd_ref[...], preferred_element_type=F32))


def _moe(blk_e, first, nxt, nused, row_tok, hf, wg, wu, wd, *, R):
    E, D, F = wg.shape
    sub = D // (2 * LANES)
    n_blocks = blk_e.shape[0]
    hbm = lambda: pl.BlockSpec(memory_space=pl.ANY)
    grid_spec = pltpu.PrefetchScalarGridSpec(
        num_scalar_prefetch=5,
        grid=(n_blocks,),
        in_specs=[hbm(), hbm(), hbm(), hbm()],
        out_specs=pl.BlockSpec((R * sub, LANES), lambda i, be, fi, nx, nu, tk: (jnp.minimum(i, nu[0] - 1), 0)),
        scratch_shapes=[
            pltpu.VMEM((2, R * sub, LANES), jnp.uint32), pltpu.VMEM((R, D), BF16),
            pltpu.VMEM((D, F), F32), pltpu.VMEM((D, F), F32), pltpu.VMEM((F, D), F32),
            pltpu.VMEM((D, F), BF16), pltpu.VMEM((D, F), BF16), pltpu.VMEM((F, D), BF16),
            pltpu.SemaphoreType.DMA((2,)), pltpu.SemaphoreType.DMA((3,)),
        ],
    )
    return pl.pallas_call(
        functools.partial(_moe_body, R=R, CB=128),
        grid_spec=grid_spec,
        out_shape=jax.ShapeDtypeStruct((n_blocks * R * sub, LANES), jnp.uint32),
        compiler_params=_params("arbitrary"),
        name="moe",
    )(blk_e, first, nxt, nused, row_tok, hf, wg, wu, wd)


def _combine_body(dest_ref, x1_ref, w_ref, yr_hbm, out_ref, ybuf, sem, *, R):
    i = pl.program_id(0)
    nsteps = pl.num_programs(0)

    sub = out_ref.shape[1] // (2 * LANES)

    def start_gather(step, slot):
        base = step * (R * TOP_K)

        def row(r, carry):
            dst = pl.multiple_of(r * sub, sub)
            for k in range(TOP_K):
                src = pl.multiple_of(dest_ref[base + r * TOP_K + k] * sub, sub)
                pltpu.make_async_copy(yr_hbm.at[pl.ds(src, sub)], ybuf.at[slot, k, pl.ds(dst, sub)],
                                      sem.at[slot]).start()
            return carry
        lax.fori_loop(0, R, row, 0, unroll=4)

    slot = i % 2

    @pl.when(i == 0)
    def _():
        start_gather(0, 0)

    pltpu.make_async_copy(ybuf.at[slot], ybuf.at[slot], sem.at[slot]).wait()

    @pl.when(i + 1 < nsteps)
    def _():
        start_gather(i + 1, 1 - slot)

    w = w_ref[...]
    for s in range(sub):
        halves0 = _load_token_rows(ybuf.at[slot, 0], R, sub, s)
        halves1 = _load_token_rows(ybuf.at[slot, 1], R, sub, s)
        for half in range(2):
            cols = slice((2 * s + half) * LANES, (2 * s + half + 1) * LANES)
            out_ref[:, cols] = x1_ref[:, cols] + (halves0[half] * w[:, 0:1] + halves1[half] * w[:, 1:2])


def _combine(dest, x1, gate_w, yr, *, R):
    T, D = x1.shape
    grid_spec = pltpu.PrefetchScalarGridSpec(
        num_scalar_prefetch=1,
        grid=(T // R,),
        in_specs=[
            pl.BlockSpec((R, D), lambda i, d: (i, 0)),
            pl.BlockSpec((R, TOP_K), lambda i, d: (i, 0)),
            pl.BlockSpec(memory_space=pl.ANY),
        ],
        out_specs=pl.BlockSpec((R, D), lambda i, d: (i, 0)),
        scratch_shapes=[pltpu.VMEM((2, TOP_K, R * (D // (2 * LANES)), LANES), jnp.uint32),
                        pltpu.SemaphoreType.DMA((2,))],
    )
    return pl.pallas_call(
        functools.partial(_combine_body, R=R),
        grid_spec=grid_spec,
        out_shape=jax.ShapeDtypeStruct((T, D), F32),
        compiler_params=_params("arbitrary"),
        name="combine",
    )(dest, x1, gate_w, yr)


def _route_body(lg_ref, tri_ref, ids_ref, gate_ref, cnt_ref):
    @pl.when(pl.program_id(0) == 0)
    def _():
        cnt_ref[...] = jnp.zeros_like(cnt_ref)

    lg = lg_ref[...]
    shape = lg.shape
    lane = lax.broadcasted_iota(jnp.int32, shape, 1)
    big = jnp.int32(LANES)

    def softmax_masked(mask):
        v = jnp.where(mask, lg, -jnp.inf)
        u = jnp.exp(v - jnp.max(v, axis=1, keepdims=True))
        return jnp.where(mask, u / jnp.sum(u, axis=1, keepdims=True), -1.0)

    def top1(p):
        best = jnp.max(p, axis=1, keepdims=True)
        idx = jnp.min(jnp.where(p == best, lane, big), axis=1, keepdims=True)
        return best, idx

    g_p, g_lane = top1(softmax_masked(lane < N_GROUPS))
    grp_of_lane = lax.shift_right_arithmetic(lane - N_GROUPS, jnp.int32(3))
    in_grp = jnp.logical_and(lane >= N_GROUPS, grp_of_lane == g_lane)
    in_grp = jnp.logical_and(in_grp, lane < N_GROUPS + N_EXPERTS)
    pe = softmax_masked(in_grp)
    p1, l1 = top1(pe)
    p2, l2 = top1(jnp.where(lane == l1, -1.0, pe))
    tot = p1 + p2
    gate1 = g_p * (p1 / tot)
    gate2 = g_p * (p2 / tot)

    hot1 = lane == l1
    hot2 = lane == l2
    hot = jnp.logical_or(hot1, hot2)
    before = jnp.dot(tri_ref[...], jnp.where(hot, 1.0, 0.0).astype(BF16), preferred_element_type=F32) + cnt_ref[...]
    rank1 = jnp.sum(jnp.where(hot1, before, 0.0), axis=1, keepdims=True)
    rank2 = jnp.sum(jnp.where(hot2, before, 0.0), axis=1, keepdims=True)
    cnt_ref[...] = cnt_ref[...] + jnp.sum(jnp.where(hot, 1.0, 0.0), axis=0, keepdims=True)

    ids = jnp.where(lane == 0, l1 - N_GROUPS, jnp.where(lane == 1, l2 - N_GROUPS, 0))
    ids = jnp.where(lane == 2, rank1.astype(jnp.int32), jnp.where(lane == 3, rank2.astype(jnp.int32), ids))
    ids_ref[...] = ids
    gate_ref[...] = jnp.where(lane == 0, gate1, jnp.where(lane == 1, gate2, 0.0))


def _route_tokens(logits, *, tb):
    T = logits.shape[0]
    r = jnp.arange(tb)
    tri = (r[None, :] < r[:, None]).astype(BF16)
    return pl.pallas_call(
        _route_body,
        grid=(T // tb,),
        in_specs=[
            pl.BlockSpec((tb, LANES), lambda i: (i, 0)),
            pl.BlockSpec((tb, tb), lambda i: (0, 0)),
        ],
        out_specs=[
            pl.BlockSpec((tb, LANES), lambda i: (i, 0)),
            pl.BlockSpec((tb, LANES), lambda i: (i, 0)),
            pl.BlockSpec((1, LANES), lambda i: (0, 0)),
        ],
        out_shape=[
            jax.ShapeDtypeStruct((T, LANES), jnp.int32),
            jax.ShapeDtypeStruct((T, LANES), F32),
            jax.ShapeDtypeStruct((1, LANES), F32),
        ],
        compiler_params=_params("arbitrary"),
        name="route",
    )(logits, tri)


def _route(logits, R):
    T = logits.shape[0]
    ids, gates, cnt = _route_tokens(logits, tb=512 if T % 512 == 0 else T)
    gate = gates[:, 0:TOP_K]
    M = T * TOP_K
    eid_f = ids[:, 0:TOP_K].reshape(M)
    rank = ids[:, TOP_K:2 * TOP_K].reshape(M)
    counts = cnt[0, N_GROUPS:N_GROUPS + N_EXPERTS].astype(jnp.int32)
    padded = (counts + R - 1) // R * R
    pend = jnp.cumsum(padded)
    pstart = pend - padded
    dest = (pstart[eid_f] + rank).astype(jnp.int32)
    n_blocks = -(-M // R) + N_EXPERTS
    tok_f = jnp.arange(M, dtype=jnp.int32) // TOP_K
    row_tok = jnp.zeros((n_blocks * R,), jnp.int32).at[dest].set(tok_f)
    blk_start = jnp.arange(n_blocks, dtype=jnp.int32) * R
    blk_e = jnp.sum((pend[None, :] <= blk_start[:, None]).astype(jnp.int32), axis=1)
    nused = (pend[-1] // R).astype(jnp.int32)
    last_e = blk_e[jnp.maximum(nused - 1, 0)]
    blk_e = jnp.where(jnp.arange(n_blocks) < nused, blk_e, last_e)
    blk_e = jnp.minimum(blk_e, N_EXPERTS - 1).astype(jnp.int32)
    e_idx = jnp.arange(N_EXPERTS, dtype=jnp.int32)
    cand = jnp.where(counts > 0, e_idx, N_EXPERTS)
    sfx = lax.cummin(cand[::-1])[::-1]
    nxt_of_e = jnp.concatenate([sfx[1:], jnp.full((1,), N_EXPERTS, jnp.int32)])
    nxt_of_e = jnp.where(nxt_of_e < N_EXPERTS, nxt_of_e, -1)
    nxt = nxt_of_e[blk_e].astype(jnp.int32)
    changed = jnp.concatenate([jnp.ones((1,), bool), blk_e[1:] != blk_e[:-1]])
    first = jnp.logical_and(changed, jnp.arange(n_blocks) < nused).astype(jnp.int32)
    return gate, dest, row_tok, blk_e, first, nxt, nused.reshape(1)


def _rope_tables(S, gain, scale):
    half = A_HEAD_DIM // 2
    freqs = ROPE_THETA ** (-jnp.arange(half, dtype=F32) / half)
    ang = jnp.arange(S, dtype=F32)[:, None] * freqs[None, :]
    cos = jnp.cos(ang)
    sin = jnp.sin(ang)
    g1, g2 = gain[:half], gain[half:]
    a_head = jnp.concatenate([cos * g1, cos * g2], axis=1)
    b_head = jnp.concatenate([-sin * g2, sin * g1], axis=1)
    reps = LANES // A_HEAD_DIM
    return jnp.tile(a_head, (1, reps)) * scale, jnp.tile(b_head, (1, reps)) * scale


def kernel(x, g_mix, w_in, conv_qk, b_igate, b_fgate, g_mlstm, g_q, g_k, sinks, w_proj_m, w_proj_a, w_out,
           g_ffn, w_group, b_group, w_expert, b_expert, w_gate, w_up, w_down):
    B, S, D = x.shape
    T = B * S
    depth = g_mix.shape[0]
    xf = x.reshape(T, D)

    sizes = (M_WIDTH, M_WIDTH, M_WIDTH, M_WIDTH, M_HEADS, M_HEADS, A_WIDTH, A_KV_WIDTH, A_KV_WIDTH, D, D)
    offs = [0]
    for s_ in sizes:
        offs.append(offs[-1] + s_)
    seg = lambda w, idx: w[:, offs[idx]:offs[idx + 1]]
    order = (9, 10, 0, 1, 2, 3, 6, 7, 8)
    new_off = {}
    acc = 0
    for idx in order:
        new_off[idx] = acc
        acc += sizes[idx]
    dk = M_HEAD_DIM

    lane128 = jnp.arange(LANES)
    bd = (lane128[:, None] // A_HEAD_DIM == lane128[None, :] // A_HEAD_DIM).astype(BF16)
    gw = A_GROUP * A_HEAD_DIM
    r_idx = jnp.arange(gw)
    rep = jnp.stack([(r_idx[:, None] == (j * A_HEAD_DIM + r_idx[None, :] % A_HEAD_DIM)).astype(BF16)
                     for j in range(A_KV_HEADS)])
    ones_bd = (jnp.arange(A_GROUP * WINDOW)[:, None] // WINDOW == r_idx[None, :] // A_HEAD_DIM).astype(BF16)

    for l in range(depth):
        w_rep = jnp.concatenate([seg(w_in[l], idx) for idx in order], axis=1).astype(BF16)
        w_gates = jnp.concatenate([seg(w_in[l], 4), seg(w_in[l], 5)], axis=1)
        w_gates = jnp.pad(w_gates, ((0, 0), (0, LANES - 2 * M_HEADS))).astype(BF16)
        z, zg = _in_proj(xf, g_mix[l][None, :], w_rep, w_gates, bm=1024 if T % 1024 == 0 else T,
                         bn=w_rep.shape[1] // 4)

        gate_bias = jnp.pad(jnp.concatenate([b_igate[l], b_fgate[l]]), (0, LANES - 2 * M_HEADS))[None, :]
        hm = _mlstm(z, zg, conv_qk[l], gate_bias, g_mlstm[l][:, None, :], B=B, S=S,
                    col_q=new_off[0] // dk, col_k=new_off[1] // dk, col_v=new_off[2] // dk, col_o=new_off[3] // dk)

        qa, qb = _rope_tables(S, g_q[l], A_HEAD_DIM ** -0.5)
        ka, kb = _rope_tables(S, g_k[l], 1.0)
        ha = _swa(z, sinks[l], qa, qb, ka, kb, bd, rep, ones_bd, B=B, S=S,
                  col_q=new_off[6] // A_WIDTH, col_k=new_off[7] // A_KV_WIDTH, col_v=new_off[8] // A_KV_WIDTH)

        w_router = jnp.pad(jnp.concatenate([w_group[l], w_expert[l]], axis=1),
                           ((0, 0), (0, LANES - N_GROUPS - N_EXPERTS)))
        r_hi = w_router.astype(BF16)
        r_lo = (w_router - r_hi.astype(F32)).astype(BF16)
        r_b = jnp.pad(jnp.concatenate([b_group[l], b_expert[l]]), (0, LANES - N_GROUPS - N_EXPERTS))[None, :]
        x1, hf, logits = _merge(hm, ha, z, xf, w_proj_m[l].astype(BF16), w_proj_a[l].astype(BF16),
                                w_out[l].astype(BF16), g_ffn[l][None, :], r_hi, r_lo, r_b,
                                bm=256, col_gm=new_off[9] // D, col_ga=new_off[10] // D)

        gate, dest, row_tok, blk_e, first, nxt, nused = _route(logits, MOE_ROWS)
        yr = _moe(blk_e, first, nxt, nused, row_tok, hf, w_gate[l], w_up[l], w_down[l], R=MOE_ROWS)
        xf = _combine(dest, x1, gate, yr, R=COMBINE_ROWS)
    return xf.reshape(B, S, D)
```

```python
import functools

import jax
import jax.numpy as jnp
from jax import lax
from jax.experimental import pallas as pl
from jax.experimental.pallas import tpu as pltpu

F32 = jnp.float32
BF16 = jnp.bfloat16
EPS = 1e-6
LANES = 128
VMEM_LIMIT = 56 * 1024 * 1024

M_HEADS = 4
M_HEAD_DIM = 256
M_WIDTH = M_HEADS * M_HEAD_DIM
CONV_WIDTH = 4
A_HEADS = 16
A_KV_HEADS = 4
A_GROUP = A_HEADS // A_KV_HEADS
A_HEAD_DIM = 64
A_WIDTH = A_HEADS * A_HEAD_DIM
A_KV_WIDTH = A_KV_HEADS * A_HEAD_DIM
WINDOW = 128
ROPE_THETA = 10000.0
N_GROUPS = 8
EXPERTS_PER_GROUP = 8
N_EXPERTS = N_GROUPS * EXPERTS_PER_GROUP
TOP_K = 2

MLSTM_CHUNK = 128
MLSTM_HEADS_PER_STEP = 2
MOE_ROWS = 256
MOE_GATHER_SLOTS = 3
COMBINE_ROWS = 256


def _sigmoid(v):
    return 1.0 / (1.0 + jnp.exp(-v))


def _params(*sem):
    return pltpu.CompilerParams(dimension_semantics=sem, vmem_limit_bytes=VMEM_LIMIT)


def _pack_bf16_pair(lo, hi):
    lo_b = lax.bitcast_convert_type(lo.astype(BF16).astype(F32), jnp.uint32)
    hi_b = lax.bitcast_convert_type(hi.astype(BF16).astype(F32), jnp.uint32)
    return lax.shift_right_logical(lo_b, jnp.uint32(16)) | hi_b


def _unpack_bf16_pair(word):
    lo = lax.bitcast_convert_type(lax.shift_left(word, jnp.uint32(16)), F32)
    hi = lax.bitcast_convert_type(word & jnp.uint32(0xFFFF0000), F32)
    return lo, hi


def _store_token_rows(ref, val):
    n, d = val.shape
    sub = d // (2 * LANES)
    for s in range(sub):
        c = 2 * s * LANES
        ref[pl.ds(s, n, stride=sub), :] = _pack_bf16_pair(val[:, c:c + LANES], val[:, c + LANES:c + 2 * LANES])


def _load_token_rows(ref, n, sub, s):
    return _unpack_bf16_pair(ref[pl.ds(s, n, stride=sub), :])


def _inproj_body(x_ref, g_ref, w_ref, wgate_ref, z_ref, zg_ref, hn_ref, *, sub):
    bm = x_ref.shape[0]

    @pl.when(pl.program_id(1) == 0)
    def _():
        def rows(r, carry):
            sl = pl.ds(pl.multiple_of(r * sub, sub), sub)
            xv = x_ref[sl, :]
            ms = jnp.mean(xv * xv, axis=-1, keepdims=True)
            hn_ref[sl, :] = ((xv * lax.rsqrt(ms + EPS)) * g_ref[...]).astype(BF16)
            return carry
        lax.fori_loop(0, bm // sub, rows, 0)
        zg_ref[...] = jnp.dot(hn_ref[...], wgate_ref[...], preferred_element_type=F32)

    z_ref[...] = jnp.dot(hn_ref[...], w_ref[...], preferred_element_type=F32).astype(BF16)


def _in_proj(x2, g, w, wgate, *, bm, bn):
    T, D = x2.shape
    N = w.shape[1]
    return pl.pallas_call(
        functools.partial(_inproj_body, sub=128),
        grid=(T // bm, N // bn),
        in_specs=[
            pl.BlockSpec((bm, D), lambda i, j: (i, 0)),
            pl.BlockSpec((1, D), lambda i, j: (0, 0)),
            pl.BlockSpec((D, bn), lambda i, j: (0, j)),
            pl.BlockSpec((D, LANES), lambda i, j: (0, 0)),
        ],
        out_specs=[
            pl.BlockSpec((bm, bn), lambda i, j: (i, j)),
            pl.BlockSpec((bm, LANES), lambda i, j: (i, 0)),
        ],
        out_shape=[
            jax.ShapeDtypeStruct((T, N), BF16),
            jax.ShapeDtypeStruct((T, LANES), F32),
        ],
        scratch_shapes=[pltpu.VMEM((bm, D), BF16)],
        compiler_params=_params("parallel", "arbitrary"),
        name="in_proj",
    )(x2, g, w, wgate)


def _mlstm_body(q_ref, k_ref, v_ref, o_ref, zg_ref, cq_ref, ck_ref, bias_ref, gn_ref, out_ref,
                qs_ref, ks_ref, colli_ref, colb_ref, rowli_ref, rowb_ref, cli_ref, cb_ref, rli_ref, rb_ref,
                c_ref, n_ref, m_ref, *, L, CB, HP):
    hp = pl.program_id(1)
    S = q_ref.shape[0]
    dk = q_ref.shape[1] // HP

    @pl.when(hp == 0)
    def _():
        G = zg_ref[...] + bias_ref[...]
        lf = jnp.minimum(G, 0.0) - jnp.log1p(jnp.exp(-jnp.abs(G)))
        pos = lax.broadcasted_iota(jnp.int32, (S, LANES), 0) % L
        bc = lf
        sh = 1
        while sh < L:
            bc = bc + jnp.where(pos >= sh, pltpu.roll(bc, sh, axis=0), 0.0)
            sh *= 2
        colli_ref[...] = G
        colb_ref[...] = bc
        for p in range(S // LANES):
            sl = slice(p * LANES, (p + 1) * LANES)
            rowli_ref[:, sl] = G[sl, :].T[0:8, :]
            rowb_ref[:, sl] = bc[sl, :].T[0:8, :]

    lane = lax.broadcasted_iota(jnp.int32, (S, LANES), 1)
    sub = lax.broadcasted_iota(jnp.int32, (8, S), 0)
    for j in range(HP):
        h = hp * HP + j
        cli_ref[j] = jnp.sum(jnp.where(lane == h, colli_ref[...], 0.0), axis=1, keepdims=True)
        cb_ref[j] = jnp.sum(jnp.where(lane == h + M_HEADS, colb_ref[...], 0.0), axis=1, keepdims=True)
        rli_ref[j] = jnp.sum(jnp.where(sub == h, rowli_ref[...], 0.0), axis=0, keepdims=True)
        rb_ref[j] = jnp.sum(jnp.where(sub == h + M_HEADS, rowb_ref[...], 0.0), axis=0, keepdims=True)

    def conv_silu(src_ref, w_ref, dst_ref, scale):
        w = w_ref[...]

        def taps(xs):
            y = xs[3] * w[0:1, :]
            y = y + xs[2] * w[1:2, :]
            y = y + xs[1] * w[2:3, :]
            y = y + xs[0] * w[3:4, :]
            return ((y * _sigmoid(y)) * scale).astype(BF16)

        def chunk(r, carry):
            r0 = pl.multiple_of(r * CB, CB)
            cur = src_ref[pl.ds(r0, CB), :].astype(F32)
            dst_ref[pl.ds(r0, CB), :] = taps([cur] + [pltpu.roll(cur, d, axis=0) for d in (1, 2, 3)])
            p0 = pl.multiple_of(jnp.maximum(r0 - 16, 0), 16)
            prev = src_ref[pl.ds(p0, 16), :].astype(F32)[8:16, :]
            prev = jnp.where(r > 0, prev, 0.0)
            both = jnp.concatenate([prev, cur[0:16, :]], axis=0)
            dst_ref[pl.ds(r0, 16), :] = taps([both[8:24, :]] + [pltpu.roll(both, d, axis=0)[8:24, :] for d in (1, 2, 3)])
            return carry
        lax.fori_loop(0, S // CB, chunk, 0)

    conv_silu(q_ref, cq_ref, qs_ref, 1.0)
    conv_silu(k_ref, ck_ref, ks_ref, float(dk) ** -0.5)

    c_ref[...] = jnp.zeros_like(c_ref)
    n_ref[...] = jnp.zeros_like(n_ref)
    m_ref[...] = jnp.zeros_like(m_ref)
    t_idx = lax.broadcasted_iota(jnp.int32, (L, L), 0)
    s_idx = lax.broadcasted_iota(jnp.int32, (L, L), 1)
    causal = s_idx <= t_idx

    heads = range(HP)
    cols = [slice(j * dk, (j + 1) * dk) for j in heads]
    dims_nt = (((1,), (1,)), ((), ()))

    def chunk(c, carry):
        rows = pl.ds(pl.multiple_of(c * L, L), L)
        qc = [qs_ref[rows, cols[j]] for j in heads]
        kc = [ks_ref[rows, cols[j]] for j in heads]
        vc = [v_ref[rows, cols[j]] for j in heads]
        b_col = [cb_ref[j, rows, :] for j in heads]
        li_col = [cli_ref[j, rows, :] for j in heads]
        li_row = [rli_ref[j, :, rows] for j in heads]
        b_row = [rb_ref[j, :, rows] for j in heads]
        m = [m_ref[j] for j in heads]
        a = [b_col[j] + m[j] for j in heads]
        D = [jnp.where(causal, b_col[j] - b_row[j] + li_row[j], -jnp.inf) for j in heads]
        m_t = [jnp.maximum(a[j], jnp.max(D[j], axis=1, keepdims=True)) for j in heads]
        w_inter = [jnp.exp(a[j] - m_t[j]) for j in heads]
        s_qk = [lax.dot_general(qc[j], kc[j], dims_nt, preferred_element_type=F32) for j in heads]
        P = [s_qk[j] * jnp.exp(D[j] - m_t[j]) for j in heads]
        inter = [jnp.dot(qc[j], c_ref[j].astype(BF16), preferred_element_type=F32) for j in heads]
        intra = [jnp.dot(P[j].astype(BF16), vc[j], preferred_element_type=F32) for j in heads]
        num = [w_inter[j] * inter[j] + intra[j] for j in heads]
        qn = [w_inter[j] * jnp.sum(qc[j].astype(F32) * n_ref[j], axis=1, keepdims=True)
              + jnp.sum(P[j], axis=1, keepdims=True) for j in heads]
        den = [jnp.maximum(jnp.abs(qn[j]), jnp.exp(-m_t[j])) for j in heads]
        hh = [num[j] / den[j] for j in heads]
        ms = [jnp.mean(hh[j] * hh[j], axis=1, keepdims=True) for j in heads]
        hn = [(hh[j] * lax.rsqrt(ms[j] + EPS)) * gn_ref[j] for j in heads]
        for j in heads:
            out_ref[rows, cols[j]] = (hn[j] * _sigmoid(o_ref[rows, cols[j]].astype(F32))).astype(BF16)
        bL = [b_row[j][:, L - 1:L] for j in heads]
        g_col = [bL[j] - b_col[j] + li_col[j] for j in heads]
        m_new = [jnp.maximum(bL[j] + m[j], jnp.max(g_col[j], axis=0, keepdims=True)) for j in heads]
        decay = [jnp.exp(bL[j] + m[j] - m_new[j]) for j in heads]
        kw = [kc[j].astype(F32) * jnp.exp(g_col[j] - m_new[j]) for j in heads]
        upd = [jnp.dot(kw[j].T.astype(BF16), vc[j], preferred_element_type=F32) for j in heads]
        for j in heads:
            c_ref[j] = decay[j] * c_ref[j] + upd[j]
            n_ref[j] = decay[j] * n_ref[j] + jnp.sum(kw[j], axis=0, keepdims=True)
            m_ref[j] = m_new[j]
        return carry
    lax.fori_loop(0, S // L, chunk, 0)


def _mlstm(z, zg, conv_qk, gate_bias, g_mlstm3, *, B, S, col_q, col_k, col_v, col_o):
    T = B * S
    dk = M_HEAD_DIM
    L = MLSTM_CHUNK
    HP = MLSTM_HEADS_PER_STEP
    wd = HP * dk
    zspec = lambda col: pl.BlockSpec((S, wd), lambda b, h: (b, col // HP + h))
    return pl.pallas_call(
        functools.partial(_mlstm_body, L=L, CB=128 // HP, HP=HP),
        grid=(B, M_HEADS // HP),
        in_specs=[
            zspec(col_q), zspec(col_k), zspec(col_v), zspec(col_o),
            pl.BlockSpec((S, LANES), lambda b, h: (b, 0)),
            pl.BlockSpec((CONV_WIDTH, wd), lambda b, h: (0, h)),
            pl.BlockSpec((CONV_WIDTH, wd), lambda b, h: (0, M_HEADS // HP + h)),
            pl.BlockSpec((1, LANES), lambda b, h: (0, 0)),
            pl.BlockSpec((HP, 1, dk), lambda b, h: (h, 0, 0)),
        ],
        out_specs=pl.BlockSpec((S, wd), lambda b, h: (b, h)),
        out_shape=jax.ShapeDtypeStruct((T, M_WIDTH), BF16),
        scratch_shapes=[
            pltpu.VMEM((S, wd), BF16), pltpu.VMEM((S, wd), BF16),
            pltpu.VMEM((S, LANES), F32), pltpu.VMEM((S, LANES), F32),
            pltpu.VMEM((8, S), F32), pltpu.VMEM((8, S), F32),
            pltpu.VMEM((HP, S, 1), F32), pltpu.VMEM((HP, S, 1), F32),
            pltpu.VMEM((HP, 1, S), F32), pltpu.VMEM((HP, 1, S), F32),
            pltpu.VMEM((HP, dk, dk), F32), pltpu.VMEM((HP, 1, dk), F32), pltpu.VMEM((HP, 1, 1), F32),
        ],
        compiler_params=_params("parallel", "arbitrary"),
        name="mlstm",
    )(z, z, z, z, zg, conv_qk, conv_qk, gate_bias, g_mlstm3)


def _swa_body(sink_ref, q_ref, k_ref, v_ref, qa_ref, qb_ref, ka_ref, kb_ref, bd_ref, rep_ref, ones_ref, out_ref,
              kbd_ref, vbd_ref):
    n = pl.program_id(1)
    W = WINDOW
    hd = A_HEAD_DIM
    gw = A_GROUP * hd

    def norm_rope(x, ta, tb):
        x2 = x * x
        x2h = x2.astype(BF16)
        x2l = (x2 - x2h.astype(F32)).astype(BF16)
        ss = (jnp.dot(x2h, bd_ref[...], preferred_element_type=F32)
              + jnp.dot(x2l, bd_ref[...], preferred_element_type=F32))
        r = lax.rsqrt(ss * (1.0 / hd) + EPS)
        ln = lax.broadcasted_iota(jnp.int32, x.shape, 1) % hd
        swapped = jnp.where(ln < hd // 2, pltpu.roll(x, LANES - hd // 2, axis=1), pltpu.roll(x, hd // 2, axis=1))
        return r * (x * ta + swapped * tb)

    lane_head = lax.broadcasted_iota(jnp.int32, (W, gw), 1) // hd
    t_idx = lax.broadcasted_iota(jnp.int32, (W, W), 0)
    k_idx = lax.broadcasted_iota(jnp.int32, (W, W), 1)
    mask_cur = k_idx <= t_idx
    neg = -jnp.inf

    @pl.when(n == 0)
    def _():
        kbd_ref[1] = jnp.zeros(kbd_ref.shape[1:], BF16)
        vbd_ref[1] = jnp.zeros(vbd_ref.shape[1:], BF16)

    def block(half, prev_slot, cur_slot, mask_prev):
        rows = slice(half * W, (half + 1) * W)
        qa = qa_ref[rows, :]
        qb = qb_ref[rows, :]
        qp = jnp.concatenate(
            [norm_rope(q_ref[rows, t * LANES:(t + 1) * LANES].astype(F32), qa, qb) for t in range(A_WIDTH // LANES)],
            axis=1).astype(BF16)
        ka = ka_ref[rows, :]
        kb = kb_ref[rows, :]
        kp = jnp.concatenate(
            [norm_rope(k_ref[rows, t * LANES:(t + 1) * LANES].astype(F32), ka, kb)
             for t in range(A_KV_WIDTH // LANES)], axis=1).astype(BF16)
        vv = v_ref[rows, :]
        for j in range(A_KV_HEADS):
            krep = jnp.dot(kp, rep_ref[j], preferred_element_type=F32).astype(BF16)
            vrep = jnp.dot(vv, rep_ref[j], preferred_element_type=F32).astype(BF16)
            for i in range(A_GROUP):
                kbd_ref[cur_slot, j, i * W:(i + 1) * W, :] = jnp.where(lane_head == i, krep, jnp.zeros_like(krep))
                vbd_ref[cur_slot, j, i * W:(i + 1) * W, :] = jnp.where(lane_head == i, vrep, jnp.zeros_like(vrep))

        for j in range(A_KV_HEADS):
            qg = qp[:, j * gw:(j + 1) * gw]
            s_prev = lax.dot_general(qg, kbd_ref[prev_slot, j], (((1,), (1,)), ((), ())), preferred_element_type=F32)
            s_cur = lax.dot_general(qg, kbd_ref[cur_slot, j], (((1,), (1,)), ((), ())), preferred_element_type=F32)
            pp, pc = [], []
            sink_term = jnp.zeros((W, gw), F32)
            for i in range(A_GROUP):
                sp = jnp.where(mask_prev, s_prev[:, i * W:(i + 1) * W], neg)
                sc = jnp.where(mask_cur, s_cur[:, i * W:(i + 1) * W], neg)
                sink = sink_ref[j * A_GROUP + i]
                mx = jnp.maximum(jnp.max(jnp.maximum(sp, sc), axis=1, keepdims=True), sink)
                pp.append(jnp.exp(sp - mx).astype(BF16))
                pc.append(jnp.exp(sc - mx).astype(BF16))
                sink_term = jnp.where(lane_head == i, jnp.exp(sink - mx), sink_term)
            p_prev = jnp.concatenate(pp, axis=1)
            p_cur = jnp.concatenate(pc, axis=1)
            o = (jnp.dot(p_prev, vbd_ref[prev_slot, j], preferred_element_type=F32)
                 + jnp.dot(p_cur, vbd_ref[cur_slot, j], preferred_element_type=F32))
            den = (jnp.dot(p_prev, ones_ref[...], preferred_element_type=F32)
                   + jnp.dot(p_cur, ones_ref[...], preferred_element_type=F32)) + sink_term
            out_ref[rows, j * gw:(j + 1) * gw] = (o / den).astype(BF16)

    upper = k_idx > t_idx
    block(0, 1, 0, jnp.logical_and(upper, n > 0))
    block(1, 0, 1, upper)


def _swa(z, sinks, qa, qb, ka, kb, bd, rep, ones_bd, *, B, S, col_q, col_k, col_v):
    T = B * S
    W2 = 2 * WINDOW
    NB = S // W2
    gw = A_GROUP * A_HEAD_DIM
    tab = lambda: pl.BlockSpec((W2, LANES), lambda b, n: (n, 0))
    return pl.pallas_call(
        _swa_body,
        grid=(B, NB),
        in_specs=[
            pl.BlockSpec(memory_space=pltpu.SMEM),
            pl.BlockSpec((W2, A_WIDTH), lambda b, n: (b * NB + n, col_q)),
            pl.BlockSpec((W2, A_KV_WIDTH), lambda b, n: (b * NB + n, col_k)),
            pl.BlockSpec((W2, A_KV_WIDTH), lambda b, n: (b * NB + n, col_v)),
            tab(), tab(), tab(), tab(),
            pl.BlockSpec((LANES, LANES), lambda b, n: (0, 0)),
            pl.BlockSpec((A_KV_HEADS, gw, gw), lambda b, n: (0, 0, 0)),
            pl.BlockSpec((A_GROUP * WINDOW, gw), lambda b, n: (0, 0)),
        ],
        out_specs=pl.BlockSpec((W2, A_WIDTH), lambda b, n: (b * NB + n, 0)),
        out_shape=jax.ShapeDtypeStruct((T, A_WIDTH), BF16),
        scratch_shapes=[
            pltpu.VMEM((2, A_KV_HEADS, A_GROUP * WINDOW, gw), BF16),
            pltpu.VMEM((2, A_KV_HEADS, A_GROUP * WINDOW, gw), BF16),
        ],
        compiler_params=_params("parallel", "arbitrary"),
        name="swa",
    )(sinks, z, z, z, qa, qb, ka, kb, bd, rep, ones_bd)


def _merge_body(hm_ref, ha_ref, gm_ref, ga_ref, x_ref, wm_ref, wa_ref, wo_ref, gf_ref, rh_ref, rl_ref, rb_ref,
                x1_ref, hf_ref, lg_ref):
    pm = jnp.dot(hm_ref[...], wm_ref[...], preferred_element_type=F32)
    pa = jnp.dot(ha_ref[...], wa_ref[...], preferred_element_type=F32)
    mixed = _sigmoid(gm_ref[...].astype(F32)) * pm + _sigmoid(ga_ref[...].astype(F32)) * pa
    x1 = x_ref[...] + jnp.dot(mixed.astype(BF16), wo_ref[...], preferred_element_type=F32)
    x1_ref[...] = x1
    ms = jnp.mean(x1 * x1, axis=-1, keepdims=True)
    hf = (x1 * lax.rsqrt(ms + EPS)) * gf_ref[...]
    _store_token_rows(hf_ref, hf)
    hh = hf.astype(BF16)
    hl = (hf - hh.astype(F32)).astype(BF16)
    lg_ref[...] = (jnp.dot(hh, rh_ref[...], preferred_element_type=F32)
                   + jnp.dot(hl, rh_ref[...], preferred_element_type=F32)
                   + jnp.dot(hh, rl_ref[...], preferred_element_type=F32)) + rb_ref[...]


def _merge(hm, ha, z, x2, wm, wa, wo, gf, rh, rl, rb, *, bm, col_gm, col_ga):
    T, D = x2.shape
    const = lambda shape: pl.BlockSpec(shape, lambda i: (0,) * len(shape), pipeline_mode=pl.Buffered(1))
    return pl.pallas_call(
        _merge_body,
        grid=(T // bm,),
        in_specs=[
            pl.BlockSpec((bm, M_WIDTH), lambda i: (i, 0)),
            pl.BlockSpec((bm, A_WIDTH), lambda i: (i, 0)),
            pl.BlockSpec((bm, D), lambda i: (i, col_gm)),
            pl.BlockSpec((bm, D), lambda i: (i, col_ga)),
            pl.BlockSpec((bm, D), lambda i: (i, 0)),
            const((M_WIDTH, D)), const((A_WIDTH, D)), const((D, D)), const((1, D)),
            const((D, LANES)), const((D, LANES)), const((1, LANES)),
        ],
        out_specs=[
            pl.BlockSpec((bm, D), lambda i: (i, 0)),
            pl.BlockSpec((bm * (D // (2 * LANES)), LANES), lambda i: (i, 0)),
            pl.BlockSpec((bm, LANES), lambda i: (i, 0)),
        ],
        out_shape=[
            jax.ShapeDtypeStruct((T, D), F32),
            jax.ShapeDtypeStruct((T * (D // (2 * LANES)), LANES), jnp.uint32),
            jax.ShapeDtypeStruct((T, LANES), F32),
        ],
        compiler_params=_params("parallel"),
        name="merge",
    )(hm, ha, z, z, x2, wm, wa, wo, gf, rh, rl, rb)


def _moe_body(blk_e_ref, first_ref, nxt_ref, nused_ref, tok_ref, hf_hbm, wg_hbm, wu_hbm, wd_hbm, y_ref,
              xbuf, xs_ref, stage_g, stage_u, stage_d, wg_ref, wu_ref, wd_ref, sem, wsem, *, R, CB):
    i = pl.program_id(0)
    nused = nused_ref[0]
    sub = xs_ref.shape[1] // (2 * LANES)

    def weight_copies(e):
        return (pltpu.make_async_copy(wg_hbm.at[e], stage_g, wsem.at[0]),
                pltpu.make_async_copy(wu_hbm.at[e], stage_u, wsem.at[1]),
                pltpu.make_async_copy(wd_hbm.at[e], stage_d, wsem.at[2]))

    def cast(src, dst):
        def rows(r, carry):
            sl = pl.ds(pl.multiple_of(r * CB, CB), CB)
            dst[sl, :] = src[sl, :].astype(BF16)
            return carry
        lax.fori_loop(0, src.shape[0] // CB, rows, 0)

    def start_gather(blk, slot):
        base = blk * R

        for r in range(R):
            src = pl.multiple_of(tok_ref[base + r] * sub, sub)
            pltpu.make_async_copy(hf_hbm.at[pl.ds(src, sub)], xbuf.at[slot, pl.ds(r * sub, sub)],
                                  sem.at[slot]).start()

    def wait_gather(slot):
        pltpu.make_async_copy(xbuf.at[slot], xbuf.at[slot], sem.at[slot]).wait()

    nslots = xbuf.shape[0]
    ahead = nslots - 1
    last_blk = pl.num_programs(0) - 1
    slot = i % nslots

    @pl.when(i == 0)
    def _():
        for a in range(ahead):
            start_gather(a, a)
        for c in weight_copies(blk_e_ref[0]):
            c.start(priority=1)

    @pl.when(i < nused)
    def _():
        @pl.when(first_ref[i] == 1)
        def _():
            for c in weight_copies(blk_e_ref[i]):
                c.wait()
            cast(stage_g, wg_ref)
            cast(stage_u, wu_ref)
            cast(stage_d, wd_ref)

            @pl.when(nxt_ref[i] >= 0)
            def _():
                for c in weight_copies(nxt_ref[i]):
                    c.start(priority=1)

        wait_gather(slot)
        for s in range(sub):
            lo, hi = _load_token_rows(xbuf.at[slot], R, sub, s)
            xs_ref[:, 2 * s * LANES:(2 * s + 1) * LANES] = lo.astype(BF16)
            xs_ref[:, (2 * s + 1) * LANES:(2 * s + 2) * LANES] = hi.astype(BF16)
        start_gather(jnp.minimum(i + ahead, last_blk), (i + ahead) % nslots)
        xv = xs_ref[...]
        g = jnp.dot(xv, wg_ref[...], preferred_element_type=F32)
        u = jnp.dot(xv, wu_ref[...], preferred_element_type=F32)
        hmid = ((g * _sigmoid(g)) * u).astype(BF16)
        _store_token_rows(y_ref, jnp.dot(hmid, wd_ref[...], preferred_element_type=F32))

        @pl.when(i == nused - 1)
        def _():
            for a in range(1, nslots):
                wait_gather((i + a) % nslots)


def _moe(blk_e, first, nxt, nused, row_tok, hf, wg, wu, wd, *, R):
    E, D, F = wg.shape
    sub = D // (2 * LANES)
    n_blocks = blk_e.shape[0]
    hbm = lambda: pl.BlockSpec(memory_space=pl.ANY)
    grid_spec = pltpu.PrefetchScalarGridSpec(
        num_scalar_prefetch=5,
        grid=(n_blocks,),
        in_specs=[hbm(), hbm(), hbm(), hbm()],
        out_specs=pl.BlockSpec((R * sub, LANES), lambda i, be, fi, nx, nu, tk: (jnp.minimum(i, nu[0] - 1), 0)),
        scratch_shapes=[
            pltpu.VMEM((MOE_GATHER_SLOTS, R * sub, LANES), jnp.uint32), pltpu.VMEM((R, D), BF16),
            pltpu.VMEM((D, F), F32), pltpu.VMEM((D, F), F32), pltpu.VMEM((F, D), F32),
            pltpu.VMEM((D, F), BF16), pltpu.VMEM((D, F), BF16), pltpu.VMEM((F, D), BF16),
            pltpu.SemaphoreType.DMA((MOE_GATHER_SLOTS,)), pltpu.SemaphoreType.DMA((3,)),
        ],
    )
    return pl.pallas_call(
        functools.partial(_moe_body, R=R, CB=128),
        grid_spec=grid_spec,
        out_shape=jax.ShapeDtypeStruct((n_blocks * R * sub, LANES), jnp.uint32),
        compiler_params=_params("arbitrary"),
        name="moe",
    )(blk_e, first, nxt, nused, row_tok, hf, wg, wu, wd)


def _combine_body(dest_ref, x1_ref, w_ref, yr_hbm, out_ref, ybuf, sem, *, R):
    i = pl.program_id(0)
    nsteps = pl.num_programs(0)

    sub = out_ref.shape[1] // (2 * LANES)

    def start_gather(step, slot):
        base = step * (R * TOP_K)

        def row(r, carry):
            dst = pl.multiple_of(r * sub, sub)
            for k in range(TOP_K):
                src = pl.multiple_of(dest_ref[base + r * TOP_K + k] * sub, sub)
                pltpu.make_async_copy(yr_hbm.at[pl.ds(src, sub)], ybuf.at[slot, k, pl.ds(dst, sub)],
                                      sem.at[slot]).start()
            return carry
        lax.fori_loop(0, R, row, 0, unroll=4)

    slot = i % 2

    @pl.when(i == 0)
    def _():
        start_gather(0, 0)

    pltpu.make_async_copy(ybuf.at[slot], ybuf.at[slot], sem.at[slot]).wait()

    @pl.when(i + 1 < nsteps)
    def _():
        start_gather(i + 1, 1 - slot)

    w = w_ref[...]
    for s in range(sub):
        halves0 = _load_token_rows(ybuf.at[slot, 0], R, sub, s)
        halves1 = _load_token_rows(ybuf.at[slot, 1], R, sub, s)
        for half in range(2):
            cols = slice((2 * s + half) * LANES, (2 * s + half + 1) * LANES)
            out_ref[:, cols] = x1_ref[:, cols] + (halves0[half] * w[:, 0:1] + halves1[half] * w[:, 1:2])


def _combine(dest, x1, gate_w, yr, *, R):
    T, D = x1.shape
    grid_spec = pltpu.PrefetchScalarGridSpec(
        num_scalar_prefetch=1,
        grid=(T // R,),
        in_specs=[
            pl.BlockSpec((R, D), lambda i, d: (i, 0)),
            pl.BlockSpec((R, TOP_K), lambda i, d: (i, 0)),
            pl.BlockSpec(memory_space=pl.ANY),
        ],
        out_specs=pl.BlockSpec((R, D), lambda i, d: (i, 0)),
        scratch_shapes=[pltpu.VMEM((2, TOP_K, R * (D // (2 * LANES)), LANES), jnp.uint32),
                        pltpu.SemaphoreType.DMA((2,))],
    )
    return pl.pallas_call(
        functools.partial(_combine_body, R=R),
        grid_spec=grid_spec,
        out_shape=jax.ShapeDtypeStruct((T, D), F32),
        compiler_params=_params("arbitrary"),
        name="combine",
    )(dest, x1, gate_w, yr)


def _route_body(lg_ref, tri_ref, ids_ref, gate_ref, cnt_ref):
    @pl.when(pl.program_id(0) == 0)
    def _():
        cnt_ref[...] = jnp.zeros_like(cnt_ref)

    lg = lg_ref[...]
    shape = lg.shape
    lane = lax.broadcasted_iota(jnp.int32, shape, 1)
    big = jnp.int32(LANES)

    def softmax_masked(mask):
        v = jnp.where(mask, lg, -jnp.inf)
        u = jnp.exp(v - jnp.max(v, axis=1, keepdims=True))
        return jnp.where(mask, u / jnp.sum(u, axis=1, keepdims=True), -1.0)

    def top1(p):
        best = jnp.max(p, axis=1, keepdims=True)
        idx = jnp.min(jnp.where(p == best, lane, big), axis=1, keepdims=True)
        return best, idx

    g_p, g_lane = top1(softmax_masked(lane < N_GROUPS))
    grp_of_lane = lax.shift_right_arithmetic(lane - N_GROUPS, jnp.int32(3))
    in_grp = jnp.logical_and(lane >= N_GROUPS, grp_of_lane == g_lane)
    in_grp = jnp.logical_and(in_grp, lane < N_GROUPS + N_EXPERTS)
    pe = softmax_masked(in_grp)
    p1, l1 = top1(pe)
    p2, l2 = top1(jnp.where(lane == l1, -1.0, pe))
    tot = p1 + p2
    gate1 = g_p * (p1 / tot)
    gate2 = g_p * (p2 / tot)

    hot1 = lane == l1
    hot2 = lane == l2
    hot = jnp.logical_or(hot1, hot2)
    before = jnp.dot(tri_ref[...], jnp.where(hot, 1.0, 0.0).astype(BF16), preferred_element_type=F32) + cnt_ref[...]
    rank1 = jnp.sum(jnp.where(hot1, before, 0.0), axis=1, keepdims=True)
    rank2 = jnp.sum(jnp.where(hot2, before, 0.0), axis=1, keepdims=True)
    cnt_ref[...] = cnt_ref[...] + jnp.sum(jnp.where(hot, 1.0, 0.0), axis=0, keepdims=True)

    ids = jnp.where(lane == 0, l1 - N_GROUPS, jnp.where(lane == 1, l2 - N_GROUPS, 0))
    ids = jnp.where(lane == 2, rank1.astype(jnp.int32), jnp.where(lane == 3, rank2.astype(jnp.int32), ids))
    ids_ref[...] = ids
    gate_ref[...] = jnp.where(lane == 0, gate1, jnp.where(lane == 1, gate2, 0.0))


def _route_tokens(logits, *, tb):
    T = logits.shape[0]
    r = jnp.arange(tb)
    tri = (r[None, :] < r[:, None]).astype(BF16)
    return pl.pallas_call(
        _route_body,
        grid=(T // tb,),
        in_specs=[
            pl.BlockSpec((tb, LANES), lambda i: (i, 0)),
            pl.BlockSpec((tb, tb), lambda i: (0, 0)),
        ],
        out_specs=[
            pl.BlockSpec((tb, LANES), lambda i: (i, 0)),
            pl.BlockSpec((tb, LANES), lambda i: (i, 0)),
            pl.BlockSpec((1, LANES), lambda i: (0, 0)),
        ],
        out_shape=[
            jax.ShapeDtypeStruct((T, LANES), jnp.int32),
            jax.ShapeDtypeStruct((T, LANES), F32),
            jax.ShapeDtypeStruct((1, LANES), F32),
        ],
        compiler_params=_params("arbitrary"),
        name="route",
    )(logits, tri)


def _route(logits, R):
    T = logits.shape[0]
    ids, gates, cnt = _route_tokens(logits, tb=512 if T % 512 == 0 else T)
    gate = gates[:, 0:TOP_K]
    M = T * TOP_K
    eid_f = ids[:, 0:TOP_K].reshape(M)
    rank = ids[:, TOP_K:2 * TOP_K].reshape(M)
    counts = cnt[0, N_GROUPS:N_GROUPS + N_EXPERTS].astype(jnp.int32)
    padded = (counts + R - 1) // R * R
    pend = jnp.cumsum(padded)
    pstart = pend - padded
    dest = (pstart[eid_f] + rank).astype(jnp.int32)
    n_blocks = -(-M // R) + N_EXPERTS
    tok_f = jnp.arange(M, dtype=jnp.int32) // TOP_K
    row_tok = jnp.zeros((n_blocks * R,), jnp.int32).at[dest].set(tok_f)
    blk_start = jnp.arange(n_blocks, dtype=jnp.int32) * R
    blk_e = jnp.sum((pend[None, :] <= blk_start[:, None]).astype(jnp.int32), axis=1)
    nused = (pend[-1] // R).astype(jnp.int32)
    last_e = blk_e[jnp.maximum(nused - 1, 0)]
    blk_e = jnp.where(jnp.arange(n_blocks) < nused, blk_e, last_e)
    blk_e = jnp.minimum(blk_e, N_EXPERTS - 1).astype(jnp.int32)
    e_idx = jnp.arange(N_EXPERTS, dtype=jnp.int32)
    cand = jnp.where(counts > 0, e_idx, N_EXPERTS)
    sfx = lax.cummin(cand[::-1])[::-1]
    nxt_of_e = jnp.concatenate([sfx[1:], jnp.full((1,), N_EXPERTS, jnp.int32)])
    nxt_of_e = jnp.where(nxt_of_e < N_EXPERTS, nxt_of_e, -1)
    nxt = nxt_of_e[blk_e].astype(jnp.int32)
    changed = jnp.concatenate([jnp.ones((1,), bool), blk_e[1:] != blk_e[:-1]])
    first = jnp.logical_and(changed, jnp.arange(n_blocks) < nused).astype(jnp.int32)
    return gate, dest, row_tok, blk_e, first, nxt, nused.reshape(1)


def _rope_tables(S, gain, scale):
    half = A_HEAD_DIM // 2
    freqs = ROPE_THETA ** (-jnp.arange(half, dtype=F32) / half)
    ang = jnp.arange(S, dtype=F32)[:, None] * freqs[None, :]
    cos = jnp.cos(ang)
    sin = jnp.sin(ang)
    g1, g2 = gain[:half], gain[half:]
    a_head = jnp.concatenate([cos * g1, cos * g2], axis=1)
    b_head = jnp.concatenate([-sin * g2, sin * g1], axis=1)
    reps = LANES // A_HEAD_DIM
    return jnp.tile(a_head, (1, reps)) * scale, jnp.tile(b_head, (1, reps)) * scale


def kernel(x, g_mix, w_in, conv_qk, b_igate, b_fgate, g_mlstm, g_q, g_k, sinks, w_proj_m, w_proj_a, w_out,
           g_ffn, w_group, b_group, w_expert, b_expert, w_gate, w_up, w_down):
    B, S, D = x.shape
    T = B * S
    depth = g_mix.shape[0]
    xf = x.reshape(T, D)

    sizes = (M_WIDTH, M_WIDTH, M_WIDTH, M_WIDTH, M_HEADS, M_HEADS, A_WIDTH, A_KV_WIDTH, A_KV_WIDTH, D, D)
    offs = [0]
    for s_ in sizes:
        offs.append(offs[-1] + s_)
    seg = lambda w, idx: w[:, offs[idx]:offs[idx + 1]]
    order = (9, 10, 0, 1, 2, 3, 6, 7, 8)
    new_off = {}
    acc = 0
    for idx in order:
        new_off[idx] = acc
        acc += sizes[idx]
    dk = M_HEAD_DIM

    lane128 = jnp.arange(LANES)
    bd = (lane128[:, None] // A_HEAD_DIM == lane128[None, :] // A_HEAD_DIM).astype(BF16)
    gw = A_GROUP * A_HEAD_DIM
    r_idx = jnp.arange(gw)
    rep = jnp.stack([(r_idx[:, None] == (j * A_HEAD_DIM + r_idx[None, :] % A_HEAD_DIM)).astype(BF16)
                     for j in range(A_KV_HEADS)])
    ones_bd = (jnp.arange(A_GROUP * WINDOW)[:, None] // WINDOW == r_idx[None, :] // A_HEAD_DIM).astype(BF16)

    for l in range(depth):
        w_rep = jnp.concatenate([seg(w_in[l], idx) for idx in order], axis=1).astype(BF16)
        w_gates = jnp.concatenate([seg(w_in[l], 4), seg(w_in[l], 5)], axis=1)
        w_gates = jnp.pad(w_gates, ((0, 0), (0, LANES - 2 * M_HEADS))).astype(BF16)
        z, zg = _in_proj(xf, g_mix[l][None, :], w_rep, w_gates, bm=1024 if T % 1024 == 0 else T,
                         bn=w_rep.shape[1] // 4)

        gate_bias = jnp.pad(jnp.concatenate([b_igate[l], b_fgate[l]]), (0, LANES - 2 * M_HEADS))[None, :]
        hm = _mlstm(z, zg, conv_qk[l], gate_bias, g_mlstm[l][:, None, :], B=B, S=S,
                    col_q=new_off[0] // dk, col_k=new_off[1] // dk, col_v=new_off[2] // dk, col_o=new_off[3] // dk)

        qa, qb = _rope_tables(S, g_q[l], A_HEAD_DIM ** -0.5)
        ka, kb = _rope_tables(S, g_k[l], 1.0)
        ha = _swa(z, sinks[l], qa, qb, ka, kb, bd, rep, ones_bd, B=B, S=S,
                  col_q=new_off[6] // A_WIDTH, col_k=new_off[7] // A_KV_WIDTH, col_v=new_off[8] // A_KV_WIDTH)

        w_router = jnp.pad(jnp.concatenate([w_group[l], w_expert[l]], axis=1),
                           ((0, 0), (0, LANES - N_GROUPS - N_EXPERTS)))
        r_hi = w_router.astype(BF16)
        r_lo = (w_router - r_hi.astype(F32)).astype(BF16)
        r_b = jnp.pad(jnp.concatenate([b_group[l], b_expert[l]]), (0, LANES - N_GROUPS - N_EXPERTS))[None, :]
        x1, hf, logits = _merge(hm, ha, z, xf, w_proj_m[l].astype(BF16), w_proj_a[l].astype(BF16),
                                w_out[l].astype(BF16), g_ffn[l][None, :], r_hi, r_lo, r_b,
                                bm=256, col_gm=new_off[9] // D, col_ga=new_off[10] // D)

        gate, dest, row_tok, blk_e, first, nxt, nused = _route(logits, MOE_ROWS)
        yr = _moe(blk_e, first, nxt, nused, row_tok, hf, w_gate[l], w_up[l], w_down[l], R=MOE_ROWS)
        xf = _combine(dest, x1, gate, yr, R=COMBINE_ROWS)
    return xf.reshape(B, S, D)
```

```python
import functools

import jax
import jax.numpy as jnp
from jax import lax
from jax.experimental import pallas as pl
from jax.experimental.pallas import tpu as pltpu

F32 = jnp.float32
BF16 = jnp.bfloat16
EPS = 1e-6
LANES = 128
VMEM_LIMIT = 56 * 1024 * 1024

M_HEADS = 4
M_HEAD_DIM = 256
M_WIDTH = M_HEADS * M_HEAD_DIM
CONV_WIDTH = 4
A_HEADS = 16
A_KV_HEADS = 4
A_GROUP = A_HEADS // A_KV_HEADS
A_HEAD_DIM = 64
A_WIDTH = A_HEADS * A_HEAD_DIM
A_KV_WIDTH = A_KV_HEADS * A_HEAD_DIM
WINDOW = 128
ROPE_THETA = 10000.0
N_GROUPS = 8
EXPERTS_PER_GROUP = 8
N_EXPERTS = N_GROUPS * EXPERTS_PER_GROUP
TOP_K = 2

MLSTM_CHUNK = 128
MLSTM_HEADS_PER_STEP = 2
MOE_ROWS = 256
MOE_GATHER_SLOTS = 3
COMBINE_ROWS = 256


def _sigmoid(v):
    return 1.0 / (1.0 + jnp.exp(-v))


def _params(*sem):
    return pltpu.CompilerParams(dimension_semantics=sem, vmem_limit_bytes=VMEM_LIMIT)


def _pack_bf16_pair(lo, hi):
    lo_b = lax.bitcast_convert_type(lo.astype(BF16).astype(F32), jnp.uint32)
    hi_b = lax.bitcast_convert_type(hi.astype(BF16).astype(F32), jnp.uint32)
    return lax.shift_right_logical(lo_b, jnp.uint32(16)) | hi_b


def _unpack_bf16_pair(word):
    lo = lax.bitcast_convert_type(lax.shift_left(word, jnp.uint32(16)), F32)
    hi = lax.bitcast_convert_type(word & jnp.uint32(0xFFFF0000), F32)
    return lo, hi


def _store_token_rows(ref, val):
    n, d = val.shape
    sub = d // (2 * LANES)
    for s in range(sub):
        c = 2 * s * LANES
        ref[pl.ds(s, n, stride=sub), :] = _pack_bf16_pair(val[:, c:c + LANES], val[:, c + LANES:c + 2 * LANES])


def _load_token_rows(ref, n, sub, s):
    return _unpack_bf16_pair(ref[pl.ds(s, n, stride=sub), :])


def _inproj_body(x_ref, g_ref, w_ref, wgate_ref, z_ref, zg_ref, hn_ref, *, sub):
    bm = x_ref.shape[0]

    @pl.when(pl.program_id(1) == 0)
    def _():
        def rows(r, carry):
            sl = pl.ds(pl.multiple_of(r * sub, sub), sub)
            xv = x_ref[sl, :]
            ms = jnp.mean(xv * xv, axis=-1, keepdims=True)
            hn_ref[sl, :] = ((xv * lax.rsqrt(ms + EPS)) * g_ref[...]).astype(BF16)
            return carry
        lax.fori_loop(0, bm // sub, rows, 0)
        zg_ref[...] = jnp.dot(hn_ref[...], wgate_ref[...], preferred_element_type=F32)

    z_ref[...] = jnp.dot(hn_ref[...], w_ref[...], preferred_element_type=F32).astype(BF16)


def _in_proj(x2, g, w, wgate, *, bm, bn):
    T, D = x2.shape
    N = w.shape[1]
    return pl.pallas_call(
        functools.partial(_inproj_body, sub=128),
        grid=(T // bm, N // bn),
        in_specs=[
            pl.BlockSpec((bm, D), lambda i, j: (i, 0)),
            pl.BlockSpec((1, D), lambda i, j: (0, 0)),
            pl.BlockSpec((D, bn), lambda i, j: (0, j)),
            pl.BlockSpec((D, LANES), lambda i, j: (0, 0)),
        ],
        out_specs=[
            pl.BlockSpec((bm, bn), lambda i, j: (i, j)),
            pl.BlockSpec((bm, LANES), lambda i, j: (i, 0)),
        ],
        out_shape=[
            jax.ShapeDtypeStruct((T, N), BF16),
            jax.ShapeDtypeStruct((T, LANES), F32),
        ],
        scratch_shapes=[pltpu.VMEM((bm, D), BF16)],
        compiler_params=_params("parallel", "arbitrary"),
        name="in_proj",
    )(x2, g, w, wgate)


def _mlstm_body(q_ref, k_ref, v_ref, o_ref, zg_ref, cq_ref, ck_ref, bias_ref, gn_ref, out_ref,
                qs_ref, ks_ref, colli_ref, colb_ref, rowli_ref, rowb_ref, cli_ref, cb_ref, rli_ref, rb_ref,
                c_ref, n_ref, m_ref, *, L, CB, HP):
    hp = pl.program_id(1)
    S = q_ref.shape[0]
    dk = q_ref.shape[1] // HP

    @pl.when(hp == 0)
    def _():
        G = zg_ref[...] + bias_ref[...]
        lf = jnp.minimum(G, 0.0) - jnp.log1p(jnp.exp(-jnp.abs(G)))
        pos = lax.broadcasted_iota(jnp.int32, (S, LANES), 0) % L
        bc = lf
        sh = 1
        while sh < L:
            bc = bc + jnp.where(pos >= sh, pltpu.roll(bc, sh, axis=0), 0.0)
            sh *= 2
        colli_ref[...] = G
        colb_ref[...] = bc
        for p in range(S // LANES):
            sl = slice(p * LANES, (p + 1) * LANES)
            rowli_ref[:, sl] = G[sl, :].T[0:8, :]
            rowb_ref[:, sl] = bc[sl, :].T[0:8, :]

    lane = lax.broadcasted_iota(jnp.int32, (S, LANES), 1)
    sub = lax.broadcasted_iota(jnp.int32, (8, S), 0)
    for j in range(HP):
        h = hp * HP + j
        cli_ref[j] = jnp.sum(jnp.where(lane == h, colli_ref[...], 0.0), axis=1, keepdims=True)
        cb_ref[j] = jnp.sum(jnp.where(lane == h + M_HEADS, colb_ref[...], 0.0), axis=1, keepdims=True)
        rli_ref[j] = jnp.sum(jnp.where(sub == h, rowli_ref[...], 0.0), axis=0, keepdims=True)
        rb_ref[j] = jnp.sum(jnp.where(sub == h + M_HEADS, rowb_ref[...], 0.0), axis=0, keepdims=True)

    def conv_silu(src_ref, w_ref, dst_ref, scale):
        w = w_ref[...]

        def taps(xs):
            y = xs[3] * w[0:1, :]
            y = y + xs[2] * w[1:2, :]
            y = y + xs[1] * w[2:3, :]
            y = y + xs[0] * w[3:4, :]
            return ((y * _sigmoid(y)) * scale).astype(BF16)

        def chunk(r, carry):
            r0 = pl.multiple_of(r * CB, CB)
            cur = src_ref[pl.ds(r0, CB), :].astype(F32)
            dst_ref[pl.ds(r0, CB), :] = taps([cur] + [pltpu.roll(cur, d, axis=0) for d in (1, 2, 3)])
            p0 = pl.multiple_of(jnp.maximum(r0 - 16, 0), 16)
            prev = src_ref[pl.ds(p0, 16), :].astype(F32)[8:16, :]
            prev = jnp.where(r > 0, prev, 0.0)
            both = jnp.concatenate([prev, cur[0:16, :]], axis=0)
            dst_ref[pl.ds(r0, 16), :] = taps([both[8:24, :]] + [pltpu.roll(both, d, axis=0)[8:24, :] for d in (1, 2, 3)])
            return carry
        lax.fori_loop(0, S // CB, chunk, 0)

    conv_silu(q_ref, cq_ref, qs_ref, 1.0)
    conv_silu(k_ref, ck_ref, ks_ref, float(dk) ** -0.5)

    c_ref[...] = jnp.zeros_like(c_ref)
    n_ref[...] = jnp.zeros_like(n_ref)
    m_ref[...] = jnp.zeros_like(m_ref)
    t_idx = lax.broadcasted_iota(jnp.int32, (L, L), 0)
    s_idx = lax.broadcasted_iota(jnp.int32, (L, L), 1)
    causal = s_idx <= t_idx

    heads = range(HP)
    cols = [slice(j * dk, (j + 1) * dk) for j in heads]
    dims_nt = (((1,), (1,)), ((), ()))

    def chunk(c, carry):
        rows = pl.ds(pl.multiple_of(c * L, L), L)
        qc = [qs_ref[rows, cols[j]] for j in heads]
        kc = [ks_ref[rows, cols[j]] for j in heads]
        vc = [v_ref[rows, cols[j]] for j in heads]
        b_col = [cb_ref[j, rows, :] for j in heads]
        li_col = [cli_ref[j, rows, :] for j in heads]
        li_row = [rli_ref[j, :, rows] for j in heads]
        b_row = [rb_ref[j, :, rows] for j in heads]
        m = [m_ref[j] for j in heads]
        a = [b_col[j] + m[j] for j in heads]
        D = [jnp.where(causal, b_col[j] - b_row[j] + li_row[j], -jnp.inf) for j in heads]
        m_t = [jnp.maximum(a[j], jnp.max(D[j], axis=1, keepdims=True)) for j in heads]
        w_inter = [jnp.exp(a[j] - m_t[j]) for j in heads]
        s_qk = [lax.dot_general(qc[j], kc[j], dims_nt, preferred_element_type=F32) for j in heads]
        P = [s_qk[j] * jnp.exp(D[j] - m_t[j]) for j in heads]
        inter = [jnp.dot(qc[j], c_ref[j].astype(BF16), preferred_element_type=F32) for j in heads]
        intra = [jnp.dot(P[j].astype(BF16), vc[j], preferred_element_type=F32) for j in heads]
        num = [w_inter[j] * inter[j] + intra[j] for j in heads]
        qn = [w_inter[j] * jnp.sum(qc[j].astype(F32) * n_ref[j], axis=1, keepdims=True)
              + jnp.sum(P[j], axis=1, keepdims=True) for j in heads]
        den = [jnp.maximum(jnp.abs(qn[j]), jnp.exp(-m_t[j])) for j in heads]
        hh = [num[j] / den[j] for j in heads]
        ms = [jnp.mean(hh[j] * hh[j], axis=1, keepdims=True) for j in heads]
        hn = [(hh[j] * lax.rsqrt(ms[j] + EPS)) * gn_ref[j] for j in heads]
        for j in heads:
            out_ref[rows, cols[j]] = (hn[j] * _sigmoid(o_ref[rows, cols[j]].astype(F32))).astype(BF16)
        bL = [b_row[j][:, L - 1:L] for j in heads]
        g_col = [bL[j] - b_col[j] + li_col[j] for j in heads]
        m_new = [jnp.maximum(bL[j] + m[j], jnp.max(g_col[j], axis=0, keepdims=True)) for j in heads]
        decay = [jnp.exp(bL[j] + m[j] - m_new[j]) for j in heads]
        kw = [kc[j].astype(F32) * jnp.exp(g_col[j] - m_new[j]) for j in heads]
        upd = [jnp.dot(kw[j].T.astype(BF16), vc[j], preferred_element_type=F32) for j in heads]
        for j in heads:
            c_ref[j] = decay[j] * c_ref[j] + upd[j]
            n_ref[j] = decay[j] * n_ref[j] + jnp.sum(kw[j], axis=0, keepdims=True)
            m_ref[j] = m_new[j]
        return carry
    lax.fori_loop(0, S // L, chunk, 0)


def _mlstm(z, zg, conv_qk, gate_bias, g_mlstm3, *, B, S, col_q, col_k, col_v, col_o):
    T = B * S
    dk = M_HEAD_DIM
    L = MLSTM_CHUNK
    HP = MLSTM_HEADS_PER_STEP
    wd = HP * dk
    zspec = lambda col: pl.BlockSpec((S, wd), lambda b, h: (b, col // HP + h))
    return pl.pallas_call(
        functools.partial(_mlstm_body, L=L, CB=128 // HP, HP=HP),
        grid=(B, M_HEADS // HP),
        in_specs=[
            zspec(col_q), zspec(col_k), zspec(col_v), zspec(col_o),
            pl.BlockSpec((S, LANES), lambda b, h: (b, 0)),
            pl.BlockSpec((CONV_WIDTH, wd), lambda b, h: (0, h)),
            pl.BlockSpec((CONV_WIDTH, wd), lambda b, h: (0, M_HEADS // HP + h)),
            pl.BlockSpec((1, LANES), lambda b, h: (0, 0)),
            pl.BlockSpec((HP, 1, dk), lambda b, h: (h, 0, 0)),
        ],
        out_specs=pl.BlockSpec((S, wd), lambda b, h: (b, h)),
        out_shape=jax.ShapeDtypeStruct((T, M_WIDTH), BF16),
        scratch_shapes=[
            pltpu.VMEM((S, wd), BF16), pltpu.VMEM((S, wd), BF16),
            pltpu.VMEM((S, LANES), F32), pltpu.VMEM((S, LANES), F32),
            pltpu.VMEM((8, S), F32), pltpu.VMEM((8, S), F32),
            pltpu.VMEM((HP, S, 1), F32), pltpu.VMEM((HP, S, 1), F32),
            pltpu.VMEM((HP, 1, S), F32), pltpu.VMEM((HP, 1, S), F32),
            pltpu.VMEM((HP, dk, dk), F32), pltpu.VMEM((HP, 1, dk), F32), pltpu.VMEM((HP, 1, 1), F32),
        ],
        compiler_params=_params("parallel", "arbitrary"),
        name="mlstm",
    )(z, z, z, z, zg, conv_qk, conv_qk, gate_bias, g_mlstm3)


def _swa_body(sink_ref, q_ref, k_ref, v_ref, qa_ref, qb_ref, ka_ref, kb_ref, bd_ref, rep_ref, ones_ref, out_ref,
              kbd_ref, vbd_ref):
    n = pl.program_id(1)
    W = WINDOW
    hd = A_HEAD_DIM
    gw = A_GROUP * hd

    def norm_rope(x, ta, tb):
        x2 = x * x
        x2h = x2.astype(BF16)
        x2l = (x2 - x2h.astype(F32)).astype(BF16)
        ss = (jnp.dot(x2h, bd_ref[...], preferred_element_type=F32)
              + jnp.dot(x2l, bd_ref[...], preferred_element_type=F32))
        r = lax.rsqrt(ss * (1.0 / hd) + EPS)
        ln = lax.broadcasted_iota(jnp.int32, x.shape, 1) % hd
        swapped = jnp.where(ln < hd // 2, pltpu.roll(x, LANES - hd // 2, axis=1), pltpu.roll(x, hd // 2, axis=1))
        return r * (x * ta + swapped * tb)

    lane_head = lax.broadcasted_iota(jnp.int32, (W, gw), 1) // hd
    t_idx = lax.broadcasted_iota(jnp.int32, (W, W), 0)
    k_idx = lax.broadcasted_iota(jnp.int32, (W, W), 1)
    mask_cur = k_idx <= t_idx
    neg = -jnp.inf

    @pl.when(n == 0)
    def _():
        kbd_ref[1] = jnp.zeros(kbd_ref.shape[1:], BF16)
        vbd_ref[1] = jnp.zeros(vbd_ref.shape[1:], BF16)

    def block(half, prev_slot, cur_slot, mask_prev):
        rows = slice(half * W, (half + 1) * W)
        qa = qa_ref[rows, :]
        qb = qb_ref[rows, :]
        qp = jnp.concatenate(
            [norm_rope(q_ref[rows, t * LANES:(t + 1) * LANES].astype(F32), qa, qb) for t in range(A_WIDTH // LANES)],
            axis=1).astype(BF16)
        ka = ka_ref[rows, :]
        kb = kb_ref[rows, :]
        kp = jnp.concatenate(
            [norm_rope(k_ref[rows, t * LANES:(t + 1) * LANES].astype(F32), ka, kb)
             for t in range(A_KV_WIDTH // LANES)], axis=1).astype(BF16)
        vv = v_ref[rows, :]
        for j in range(A_KV_HEADS):
            krep = jnp.dot(kp, rep_ref[j], preferred_element_type=F32).astype(BF16)
            vrep = jnp.dot(vv, rep_ref[j], preferred_element_type=F32).astype(BF16)
            for i in range(A_GROUP):
                kbd_ref[cur_slot, j, i * W:(i + 1) * W, :] = jnp.where(lane_head == i, krep, jnp.zeros_like(krep))
                vbd_ref[cur_slot, j, i * W:(i + 1) * W, :] = jnp.where(lane_head == i, vrep, jnp.zeros_like(vrep))

        groups = range(A_KV_HEADS)
        pairs = [(j, i) for j in groups for i in range(A_GROUP)]
        dims_nt = (((1,), (1,)), ((), ()))
        qg = [qp[:, j * gw:(j + 1) * gw] for j in groups]
        s_prev = [lax.dot_general(qg[j], kbd_ref[prev_slot, j], dims_nt, preferred_element_type=F32) for j in groups]
        s_cur = [lax.dot_general(qg[j], kbd_ref[cur_slot, j], dims_nt, preferred_element_type=F32) for j in groups]
        sp = {(j, i): jnp.where(mask_prev, s_prev[j][:, i * W:(i + 1) * W], neg) for j, i in pairs}
        sc = {(j, i): jnp.where(mask_cur, s_cur[j][:, i * W:(i + 1) * W], neg) for j, i in pairs}
        sink = {(j, i): sink_ref[j * A_GROUP + i] for j, i in pairs}
        mx = {k: jnp.maximum(jnp.max(jnp.maximum(sp[k], sc[k]), axis=1, keepdims=True), sink[k]) for k in pairs}
        pp = {k: jnp.exp(sp[k] - mx[k]).astype(BF16) for k in pairs}
        pc = {k: jnp.exp(sc[k] - mx[k]).astype(BF16) for k in pairs}
        es = {k: jnp.exp(sink[k] - mx[k]) for k in pairs}
        p_prev = [jnp.concatenate([pp[(j, i)] for i in range(A_GROUP)], axis=1) for j in groups]
        p_cur = [jnp.concatenate([pc[(j, i)] for i in range(A_GROUP)], axis=1) for j in groups]
        o = [jnp.dot(p_prev[j], vbd_ref[prev_slot, j], preferred_element_type=F32)
             + jnp.dot(p_cur[j], vbd_ref[cur_slot, j], preferred_element_type=F32) for j in groups]
        den = [jnp.dot(p_prev[j], ones_ref[...], preferred_element_type=F32)
               + jnp.dot(p_cur[j], ones_ref[...], preferred_element_type=F32) for j in groups]
        for j in groups:
            sink_term = jnp.zeros((W, gw), F32)
            for i in range(A_GROUP):
                sink_term = jnp.where(lane_head == i, es[(j, i)], sink_term)
            out_ref[rows, j * gw:(j + 1) * gw] = (o[j] / (den[j] + sink_term)).astype(BF16)

    upper = k_idx > t_idx
    block(0, 1, 0, jnp.logical_and(upper, n > 0))
    block(1, 0, 1, upper)


def _swa(z, sinks, qa, qb, ka, kb, bd, rep, ones_bd, *, B, S, col_q, col_k, col_v):
    T = B * S
    W2 = 2 * WINDOW
    NB = S // W2
    gw = A_GROUP * A_HEAD_DIM
    tab = lambda: pl.BlockSpec((W2, LANES), lambda b, n: (n, 0))
    return pl.pallas_call(
        _swa_body,
        grid=(B, NB),
        in_specs=[
            pl.BlockSpec(memory_space=pltpu.SMEM),
            pl.BlockSpec((W2, A_WIDTH), lambda b, n: (b * NB + n, col_q)),
            pl.BlockSpec((W2, A_KV_WIDTH), lambda b, n: (b * NB + n, col_k)),
            pl.BlockSpec((W2, A_KV_WIDTH), lambda b, n: (b * NB + n, col_v)),
            tab(), tab(), tab(), tab(),
            pl.BlockSpec((LANES, LANES), lambda b, n: (0, 0)),
            pl.BlockSpec((A_KV_HEADS, gw, gw), lambda b, n: (0, 0, 0)),
            pl.BlockSpec((A_GROUP * WINDOW, gw), lambda b, n: (0, 0)),
        ],
        out_specs=pl.BlockSpec((W2, A_WIDTH), lambda b, n: (b * NB + n, 0)),
        out_shape=jax.ShapeDtypeStruct((T, A_WIDTH), BF16),
        scratch_shapes=[
            pltpu.VMEM((2, A_KV_HEADS, A_GROUP * WINDOW, gw), BF16),
            pltpu.VMEM((2, A_KV_HEADS, A_GROUP * WINDOW, gw), BF16),
        ],
        compiler_params=_params("parallel", "arbitrary"),
        name="swa",
    )(sinks, z, z, z, qa, qb, ka, kb, bd, rep, ones_bd)


def _merge_body(hm_ref, ha_ref, gm_ref, ga_ref, x_ref, wm_ref, wa_ref, wo_ref, gf_ref, rh_ref, rb_ref,
                x1_ref, hf_ref, lg_ref):
    pm = jnp.dot(hm_ref[...], wm_ref[...], preferred_element_type=F32)
    pa = jnp.dot(ha_ref[...], wa_ref[...], preferred_element_type=F32)
    mixed = _sigmoid(gm_ref[...].astype(F32)) * pm + _sigmoid(ga_ref[...].astype(F32)) * pa
    x1 = x_ref[...] + jnp.dot(mixed.astype(BF16), wo_ref[...], preferred_element_type=F32)
    x1_ref[...] = x1
    ms = jnp.mean(x1 * x1, axis=-1, keepdims=True)
    hf = (x1 * lax.rsqrt(ms + EPS)) * gf_ref[...]
    _store_token_rows(hf_ref, hf)
    lg_ref[...] = jnp.dot(hf.astype(BF16), rh_ref[...], preferred_element_type=F32) + rb_ref[...]


def _merge(hm, ha, z, x2, wm, wa, wo, gf, rh, rb, *, bm, col_gm, col_ga):
    T, D = x2.shape
    const = lambda shape: pl.BlockSpec(shape, lambda i: (0,) * len(shape), pipeline_mode=pl.Buffered(1))
    return pl.pallas_call(
        _merge_body,
        grid=(T // bm,),
        in_specs=[
            pl.BlockSpec((bm, M_WIDTH), lambda i: (i, 0)),
            pl.BlockSpec((bm, A_WIDTH), lambda i: (i, 0)),
            pl.BlockSpec((bm, D), lambda i: (i, col_gm)),
            pl.BlockSpec((bm, D), lambda i: (i, col_ga)),
            pl.BlockSpec((bm, D), lambda i: (i, 0)),
            const((M_WIDTH, D)), const((A_WIDTH, D)), const((D, D)), const((1, D)),
            const((D, LANES)), const((1, LANES)),
        ],
        out_specs=[
            pl.BlockSpec((bm, D), lambda i: (i, 0)),
            pl.BlockSpec((bm * (D // (2 * LANES)), LANES), lambda i: (i, 0)),
            pl.BlockSpec((bm, LANES), lambda i: (i, 0)),
        ],
        out_shape=[
            jax.ShapeDtypeStruct((T, D), F32),
            jax.ShapeDtypeStruct((T * (D // (2 * LANES)), LANES), jnp.uint32),
            jax.ShapeDtypeStruct((T, LANES), F32),
        ],
        compiler_params=_params("parallel"),
        name="merge",
    )(hm, ha, z, z, x2, wm, wa, wo, gf, rh, rb)


def _moe_body(blk_e_ref, first_ref, nxt_ref, nused_ref, tok_ref, hf_hbm, wg_hbm, wu_hbm, wd_hbm, y_ref,
              xbuf, xs_ref, stage_g, stage_u, stage_d, wg_ref, wu_ref, wd_ref, sem, wsem, *, R, CB):
    i = pl.program_id(0)
    nused = nused_ref[0]
    sub = xs_ref.shape[1] // (2 * LANES)

    def weight_copies(e):
        return (pltpu.make_async_copy(wg_hbm.at[e], stage_g, wsem.at[0]),
                pltpu.make_async_copy(wu_hbm.at[e], stage_u, wsem.at[1]),
                pltpu.make_async_copy(wd_hbm.at[e], stage_d, wsem.at[2]))

    def cast(src, dst):
        def rows(r, carry):
            sl = pl.ds(pl.multiple_of(r * CB, CB), CB)
            dst[sl, :] = src[sl, :].astype(BF16)
            return carry
        lax.fori_loop(0, src.shape[0] // CB, rows, 0)

    def start_gather(blk, slot):
        base = blk * R

        for r in range(R):
            src = pl.multiple_of(tok_ref[base + r] * sub, sub)
            pltpu.make_async_copy(hf_hbm.at[pl.ds(src, sub)], xbuf.at[slot, pl.ds(r * sub, sub)],
                                  sem.at[slot]).start()

    def wait_gather(slot):
        pltpu.make_async_copy(xbuf.at[slot], xbuf.at[slot], sem.at[slot]).wait()

    nslots = xbuf.shape[0]
    ahead = nslots - 1
    last_blk = pl.num_programs(0) - 1
    slot = i % nslots

    @pl.when(i == 0)
    def _():
        for a in range(ahead):
            start_gather(a, a)
        for c in weight_copies(blk_e_ref[0]):
            c.start(priority=1)

    @pl.when(i < nused)
    def _():
        @pl.when(first_ref[i] == 1)
        def _():
            for c in weight_copies(blk_e_ref[i]):
                c.wait()
            cast(stage_g, wg_ref)
            cast(stage_u, wu_ref)
            cast(stage_d, wd_ref)

            @pl.when(nxt_ref[i] >= 0)
            def _():
                for c in weight_copies(nxt_ref[i]):
                    c.start(priority=1)

        wait_gather(slot)
        for s in range(sub):
            lo, hi = _load_token_rows(xbuf.at[slot], R, sub, s)
            xs_ref[:, 2 * s * LANES:(2 * s + 1) * LANES] = lo.astype(BF16)
            xs_ref[:, (2 * s + 1) * LANES:(2 * s + 2) * LANES] = hi.astype(BF16)
        start_gather(jnp.minimum(i + ahead, last_blk), (i + ahead) % nslots)
        xv = xs_ref[...]
        g = jnp.dot(xv, wg_ref[...], preferred_element_type=F32)
        u = jnp.dot(xv, wu_ref[...], preferred_element_type=F32)
        hmid = ((g * _sigmoid(g)) * u).astype(BF16)
        _store_token_rows(y_ref, jnp.dot(hmid, wd_ref[...], preferred_element_type=F32))

        @pl.when(i == nused - 1)
        def _():
            for a in range(1, nslots):
                wait_gather((i + a) % nslots)


def _moe(blk_e, first, nxt, nused, row_tok, hf, wg, wu, wd, *, R):
    E, D, F = wg.shape
    sub = D // (2 * LANES)
    n_blocks = blk_e.shape[0]
    hbm = lambda: pl.BlockSpec(memory_space=pl.ANY)
    grid_spec = pltpu.PrefetchScalarGridSpec(
        num_scalar_prefetch=5,
        grid=(n_blocks,),
        in_specs=[hbm(), hbm(), hbm(), hbm()],
        out_specs=pl.BlockSpec((R * sub, LANES), lambda i, be, fi, nx, nu, tk: (jnp.minimum(i, nu[0] - 1), 0)),
        scratch_shapes=[
            pltpu.VMEM((MOE_GATHER_SLOTS, R * sub, LANES), jnp.uint32), pltpu.VMEM((R, D), BF16),
            pltpu.VMEM((D, F), F32), pltpu.VMEM((D, F), F32), pltpu.VMEM((F, D), F32),
            pltpu.VMEM((D, F), BF16), pltpu.VMEM((D, F), BF16), pltpu.VMEM((F, D), BF16),
            pltpu.SemaphoreType.DMA((MOE_GATHER_SLOTS,)), pltpu.SemaphoreType.DMA((3,)),
        ],
    )
    return pl.pallas_call(
        functools.partial(_moe_body, R=R, CB=128),
        grid_spec=grid_spec,
        out_shape=jax.ShapeDtypeStruct((n_blocks * R * sub, LANES), jnp.uint32),
        compiler_params=_params("arbitrary"),
        name="moe",
    )(blk_e, first, nxt, nused, row_tok, hf, wg, wu, wd)


def _combine_body(dest_ref, x1_ref, w_ref, yr_hbm, out_ref, ybuf, sem, *, R):
    i = pl.program_id(0)
    nsteps = pl.num_programs(0)

    sub = out_ref.shape[1] // (2 * LANES)

    def start_gather(step, slot):
        base = step * (R * TOP_K)

        def row(r, carry):
            dst = pl.multiple_of(r * sub, sub)
            for k in range(TOP_K):
                src = pl.multiple_of(dest_ref[base + r * TOP_K + k] * sub, sub)
                pltpu.make_async_copy(yr_hbm.at[pl.ds(src, sub)], ybuf.at[slot, k, pl.ds(dst, sub)],
                                      sem.at[slot]).start()
            return carry
        lax.fori_loop(0, R, row, 0, unroll=4)

    slot = i % 2

    @pl.when(i == 0)
    def _():
        start_gather(0, 0)

    pltpu.make_async_copy(ybuf.at[slot], ybuf.at[slot], sem.at[slot]).wait()

    @pl.when(i + 1 < nsteps)
    def _():
        start_gather(i + 1, 1 - slot)

    w = w_ref[...]
    for s in range(sub):
        halves0 = _load_token_rows(ybuf.at[slot, 0], R, sub, s)
        halves1 = _load_token_rows(ybuf.at[slot, 1], R, sub, s)
        for half in range(2):
            cols = slice((2 * s + half) * LANES, (2 * s + half + 1) * LANES)
            out_ref[:, cols] = x1_ref[:, cols] + (halves0[half] * w[:, 0:1] + halves1[half] * w[:, 1:2])


def _combine(dest, x1, gate_w, yr, *, R):
    T, D = x1.shape
    grid_spec = pltpu.PrefetchScalarGridSpec(
        num_scalar_prefetch=1,
        grid=(T // R,),
        in_specs=[
            pl.BlockSpec((R, D), lambda i, d: (i, 0)),
            pl.BlockSpec((R, TOP_K), lambda i, d: (i, 0)),
            pl.BlockSpec(memory_space=pl.ANY),
        ],
        out_specs=pl.BlockSpec((R, D), lambda i, d: (i, 0)),
        scratch_shapes=[pltpu.VMEM((2, TOP_K, R * (D // (2 * LANES)), LANES), jnp.uint32),
                        pltpu.SemaphoreType.DMA((2,))],
    )
    return pl.pallas_call(
        functools.partial(_combine_body, R=R),
        grid_spec=grid_spec,
        out_shape=jax.ShapeDtypeStruct((T, D), F32),
        compiler_params=_params("arbitrary"),
        name="combine",
    )(dest, x1, gate_w, yr)


def _route_body(lg_ref, tri_ref, ids_ref, gate_ref, cnt_ref):
    @pl.when(pl.program_id(0) == 0)
    def _():
        cnt_ref[...] = jnp.zeros_like(cnt_ref)

    lg = lg_ref[...]
    shape = lg.shape
    lane = lax.broadcasted_iota(jnp.int32, shape, 1)
    big = jnp.int32(LANES)

    def softmax_masked(mask):
        v = jnp.where(mask, lg, -jnp.inf)
        u = jnp.exp(v - jnp.max(v, axis=1, keepdims=True))
        return jnp.where(mask, u / jnp.sum(u, axis=1, keepdims=True), -1.0)

    def top1(p):
        best = jnp.max(p, axis=1, keepdims=True)
        idx = jnp.min(jnp.where(p == best, lane, big), axis=1, keepdims=True)
        return best, idx

    g_p, g_lane = top1(softmax_masked(lane < N_GROUPS))
    grp_of_lane = lax.shift_right_arithmetic(lane - N_GROUPS, jnp.int32(3))
    in_grp = jnp.logical_and(lane >= N_GROUPS, grp_of_lane == g_lane)
    in_grp = jnp.logical_and(in_grp, lane < N_GROUPS + N_EXPERTS)
    pe = softmax_masked(in_grp)
    p1, l1 = top1(pe)
    p2, l2 = top1(jnp.where(lane == l1, -1.0, pe))
    tot = p1 + p2
    gate1 = g_p * (p1 / tot)
    gate2 = g_p * (p2 / tot)

    hot1 = lane == l1
    hot2 = lane == l2
    hot = jnp.logical_or(hot1, hot2)
    before = jnp.dot(tri_ref[...], jnp.where(hot, 1.0, 0.0).astype(BF16), preferred_element_type=F32) + cnt_ref[...]
    rank1 = jnp.sum(jnp.where(hot1, before, 0.0), axis=1, keepdims=True)
    rank2 = jnp.sum(jnp.where(hot2, before, 0.0), axis=1, keepdims=True)
    cnt_ref[...] = cnt_ref[...] + jnp.sum(jnp.where(hot, 1.0, 0.0), axis=0, keepdims=True)

    ids = jnp.where(lane == 0, l1 - N_GROUPS, jnp.where(lane == 1, l2 - N_GROUPS, 0))
    ids = jnp.where(lane == 2, rank1.astype(jnp.int32), jnp.where(lane == 3, rank2.astype(jnp.int32), ids))
    ids_ref[...] = ids
    gate_ref[...] = jnp.where(lane == 0, gate1, jnp.where(lane == 1, gate2, 0.0))


def _route_tokens(logits, *, tb):
    T = logits.shape[0]
    r = jnp.arange(tb)
    tri = (r[None, :] < r[:, None]).astype(BF16)
    return pl.pallas_call(
        _route_body,
        grid=(T // tb,),
        in_specs=[
            pl.BlockSpec((tb, LANES), lambda i: (i, 0)),
            pl.BlockSpec((tb, tb), lambda i: (0, 0)),
        ],
        out_specs=[
            pl.BlockSpec((tb, LANES), lambda i: (i, 0)),
            pl.BlockSpec((tb, LANES), lambda i: (i, 0)),
            pl.BlockSpec((1, LANES), lambda i: (0, 0)),
        ],
        out_shape=[
            jax.ShapeDtypeStruct((T, LANES), jnp.int32),
            jax.ShapeDtypeStruct((T, LANES), F32),
            jax.ShapeDtypeStruct((1, LANES), F32),
        ],
        compiler_params=_params("arbitrary"),
        name="route",
    )(logits, tri)


def _route(logits, R):
    T = logits.shape[0]
    ids, gates, cnt = _route_tokens(logits, tb=512 if T % 512 == 0 else T)
    gate = gates[:, 0:TOP_K]
    M = T * TOP_K
    eid_f = ids[:, 0:TOP_K].reshape(M)
    rank = ids[:, TOP_K:2 * TOP_K].reshape(M)
    counts = cnt[0, N_GROUPS:N_GROUPS + N_EXPERTS].astype(jnp.int32)
    padded = (counts + R - 1) // R * R
    pend = jnp.cumsum(padded)
    pstart = pend - padded
    dest = (pstart[eid_f] + rank).astype(jnp.int32)
    n_blocks = -(-M // R) + N_EXPERTS
    tok_f = jnp.arange(M, dtype=jnp.int32) // TOP_K
    row_tok = jnp.zeros((n_blocks * R,), jnp.int32).at[dest].set(tok_f)
    blk_start = jnp.arange(n_blocks, dtype=jnp.int32) * R
    blk_e = jnp.sum((pend[None, :] <= blk_start[:, None]).astype(jnp.int32), axis=1)
    nused = (pend[-1] // R).astype(jnp.int32)
    last_e = blk_e[jnp.maximum(nused - 1, 0)]
    blk_e = jnp.where(jnp.arange(n_blocks) < nused, blk_e, last_e)
    blk_e = jnp.minimum(blk_e, N_EXPERTS - 1).astype(jnp.int32)
    e_idx = jnp.arange(N_EXPERTS, dtype=jnp.int32)
    cand = jnp.where(counts > 0, e_idx, N_EXPERTS)
    sfx = lax.cummin(cand[::-1])[::-1]
    nxt_of_e = jnp.concatenate([sfx[1:], jnp.full((1,), N_EXPERTS, jnp.int32)])
    nxt_of_e = jnp.where(nxt_of_e < N_EXPERTS, nxt_of_e, -1)
    nxt = nxt_of_e[blk_e].astype(jnp.int32)
    changed = jnp.concatenate([jnp.ones((1,), bool), blk_e[1:] != blk_e[:-1]])
    first = jnp.logical_and(changed, jnp.arange(n_blocks) < nused).astype(jnp.int32)
    return gate, dest, row_tok, blk_e, first, nxt, nused.reshape(1)


def _rope_tables(S, gain, scale):
    half = A_HEAD_DIM // 2
    freqs = ROPE_THETA ** (-jnp.arange(half, dtype=F32) / half)
    ang = jnp.arange(S, dtype=F32)[:, None] * freqs[None, :]
    cos = jnp.cos(ang)
    sin = jnp.sin(ang)
    g1, g2 = gain[:half], gain[half:]
    a_head = jnp.concatenate([cos * g1, cos * g2], axis=1)
    b_head = jnp.concatenate([-sin * g2, sin * g1], axis=1)
    reps = LANES // A_HEAD_DIM
    return jnp.tile(a_head, (1, reps)) * scale, jnp.tile(b_head, (1, reps)) * scale


def kernel(x, g_mix, w_in, conv_qk, b_igate, b_fgate, g_mlstm, g_q, g_k, sinks, w_proj_m, w_proj_a, w_out,
           g_ffn, w_group, b_group, w_expert, b_expert, w_gate, w_up, w_down):
    B, S, D = x.shape
    T = B * S
    depth = g_mix.shape[0]
    xf = x.reshape(T, D)

    sizes = (M_WIDTH, M_WIDTH, M_WIDTH, M_WIDTH, M_HEADS, M_HEADS, A_WIDTH, A_KV_WIDTH, A_KV_WIDTH, D, D)
    offs = [0]
    for s_ in sizes:
        offs.append(offs[-1] + s_)
    seg = lambda w, idx: w[:, offs[idx]:offs[idx + 1]]
    order = (9, 10, 0, 1, 2, 3, 6, 7, 8)
    new_off = {}
    acc = 0
    for idx in order:
        new_off[idx] = acc
        acc += sizes[idx]
    dk = M_HEAD_DIM

    lane128 = jnp.arange(LANES)
    bd = (lane128[:, None] // A_HEAD_DIM == lane128[None, :] // A_HEAD_DIM).astype(BF16)
    gw = A_GROUP * A_HEAD_DIM
    r_idx = jnp.arange(gw)
    rep = jnp.stack([(r_idx[:, None] == (j * A_HEAD_DIM + r_idx[None, :] % A_HEAD_DIM)).astype(BF16)
                     for j in range(A_KV_HEADS)])
    ones_bd = (jnp.arange(A_GROUP * WINDOW)[:, None] // WINDOW == r_idx[None, :] // A_HEAD_DIM).astype(BF16)

    for l in range(depth):
        w_rep = jnp.concatenate([seg(w_in[l], idx) for idx in order], axis=1).astype(BF16)
        w_gates = jnp.concatenate([seg(w_in[l], 4), seg(w_in[l], 5)], axis=1)
        w_gates = jnp.pad(w_gates, ((0, 0), (0, LANES - 2 * M_HEADS))).astype(BF16)
        z, zg = _in_proj(xf, g_mix[l][None, :], w_rep, w_gates, bm=1024 if T % 1024 == 0 else T,
                         bn=w_rep.shape[1] // 4)

        gate_bias = jnp.pad(jnp.concatenate([b_igate[l], b_fgate[l]]), (0, LANES - 2 * M_HEADS))[None, :]
        hm = _mlstm(z, zg, conv_qk[l], gate_bias, g_mlstm[l][:, None, :], B=B, S=S,
                    col_q=new_off[0] // dk, col_k=new_off[1] // dk, col_v=new_off[2] // dk, col_o=new_off[3] // dk)

        qa, qb = _rope_tables(S, g_q[l], A_HEAD_DIM ** -0.5)
        ka, kb = _rope_tables(S, g_k[l], 1.0)
        ha = _swa(z, sinks[l], qa, qb, ka, kb, bd, rep, ones_bd, B=B, S=S,
                  col_q=new_off[6] // A_WIDTH, col_k=new_off[7] // A_KV_WIDTH, col_v=new_off[8] // A_KV_WIDTH)

        w_router = jnp.pad(jnp.concatenate([w_group[l], w_expert[l]], axis=1),
                           ((0, 0), (0, LANES - N_GROUPS - N_EXPERTS)))
        r_b = jnp.pad(jnp.concatenate([b_group[l], b_expert[l]]), (0, LANES - N_GROUPS - N_EXPERTS))[None, :]
        x1, hf, logits = _merge(hm, ha, z, xf, w_proj_m[l].astype(BF16), w_proj_a[l].astype(BF16),
                                w_out[l].astype(BF16), g_ffn[l][None, :], w_router.astype(BF16), r_b,
                                bm=256, col_gm=new_off[9] // D, col_ga=new_off[10] // D)

        gate, dest, row_tok, blk_e, first, nxt, nused = _route(logits, MOE_ROWS)
        yr = _moe(blk_e, first, nxt, nused, row_tok, hf, w_gate[l], w_up[l], w_down[l], R=MOE_ROWS)
        xf = _combine(dest, x1, gate, yr, R=COMBINE_ROWS)
    return xf.reshape(B, S, D)
```

```python
import functools

import jax
import jax.numpy as jnp
from jax import lax
from jax.experimental import pallas as pl
from jax.experimental.pallas import tpu as pltpu

F32 = jnp.float32
BF16 = jnp.bfloat16
EPS = 1e-6
LANES = 128
VMEM_LIMIT = 56 * 1024 * 1024

M_HEADS = 4
M_HEAD_DIM = 256
M_WIDTH = M_HEADS * M_HEAD_DIM
CONV_WIDTH = 4
A_HEADS = 16
A_KV_HEADS = 4
A_GROUP = A_HEADS // A_KV_HEADS
A_HEAD_DIM = 64
A_WIDTH = A_HEADS * A_HEAD_DIM
A_KV_WIDTH = A_KV_HEADS * A_HEAD_DIM
WINDOW = 128
ROPE_THETA = 10000.0
N_GROUPS = 8
EXPERTS_PER_GROUP = 8
N_EXPERTS = N_GROUPS * EXPERTS_PER_GROUP
TOP_K = 2

MLSTM_CHUNK = 128
MLSTM_TIME_BLOCK = 512
MOE_ROWS = 256
MOE_GATHER_SLOTS = 3
COMBINE_ROWS = 256


def _sigmoid(v):
    return 1.0 / (1.0 + jnp.exp(-v))


def _params(*sem):
    return pltpu.CompilerParams(dimension_semantics=sem, vmem_limit_bytes=VMEM_LIMIT)


def _pack_bf16_pair(lo, hi):
    lo_b = lax.bitcast_convert_type(lo.astype(BF16).astype(F32), jnp.uint32)
    hi_b = lax.bitcast_convert_type(hi.astype(BF16).astype(F32), jnp.uint32)
    return lax.shift_right_logical(lo_b, jnp.uint32(16)) | hi_b


def _unpack_bf16_pair(word):
    lo = lax.bitcast_convert_type(lax.shift_left(word, jnp.uint32(16)), F32)
    hi = lax.bitcast_convert_type(word & jnp.uint32(0xFFFF0000), F32)
    return lo, hi


def _store_token_rows(ref, val):
    n, d = val.shape
    sub = d // (2 * LANES)
    for s in range(sub):
        c = 2 * s * LANES
        ref[pl.ds(s, n, stride=sub), :] = _pack_bf16_pair(val[:, c:c + LANES], val[:, c + LANES:c + 2 * LANES])


def _load_token_rows(ref, n, sub, s):
    return _unpack_bf16_pair(ref[pl.ds(s, n, stride=sub), :])


def _inproj_body(x_ref, g_ref, w_ref, wgate_ref, z_ref, zg_ref, hn_ref, *, sub):
    bm = x_ref.shape[0]

    @pl.when(pl.program_id(1) == 0)
    def _():
        def rows(r, carry):
            sl = pl.ds(pl.multiple_of(r * sub, sub), sub)
            xv = x_ref[sl, :]
            ms = jnp.mean(xv * xv, axis=-1, keepdims=True)
            hn_ref[sl, :] = ((xv * lax.rsqrt(ms + EPS)) * g_ref[...]).astype(BF16)
            return carry
        lax.fori_loop(0, bm // sub, rows, 0)
        zg_ref[...] = jnp.dot(hn_ref[...], wgate_ref[...], preferred_element_type=F32)

    z_ref[...] = jnp.dot(hn_ref[...], w_ref[...], preferred_element_type=F32).astype(BF16)


def _in_proj(x2, g, w, wgate, *, bm, bn):
    T, D = x2.shape
    N = w.shape[1]
    return pl.pallas_call(
        functools.partial(_inproj_body, sub=128),
        grid=(T // bm, N // bn),
        in_specs=[
            pl.BlockSpec((bm, D), lambda i, j: (i, 0)),
            pl.BlockSpec((1, D), lambda i, j: (0, 0)),
            pl.BlockSpec((D, bn), lambda i, j: (0, j)),
            pl.BlockSpec((D, LANES), lambda i, j: (0, 0)),
        ],
        out_specs=[
            pl.BlockSpec((bm, bn), lambda i, j: (i, j)),
            pl.BlockSpec((bm, LANES), lambda i, j: (i, 0)),
        ],
        out_shape=[
            jax.ShapeDtypeStruct((T, N), BF16),
            jax.ShapeDtypeStruct((T, LANES), F32),
        ],
        scratch_shapes=[pltpu.VMEM((bm, D), BF16)],
        compiler_params=_params("parallel", "arbitrary"),
        name="in_proj",
    )(x2, g, w, wgate)


def _mlstm_body(q_ref, k_ref, v_ref, o_ref, zg_ref, cq_ref, ck_ref, bias_ref, gn_ref, out_ref,
                qs_ref, ks_ref, colli_ref, colb_ref, rowli_ref, rowb_ref, cli_ref, cb_ref, rli_ref, rb_ref,
                c_ref, n_ref, m_ref, hq_ref, hk_ref, *, L, CB, HP):
    S = q_ref.shape[0]
    dk = q_ref.shape[1] // HP

    @pl.when(pl.program_id(1) == 0)
    def _():
        c_ref[...] = jnp.zeros_like(c_ref)
        n_ref[...] = jnp.zeros_like(n_ref)
        m_ref[...] = jnp.zeros_like(m_ref)
        hq_ref[...] = jnp.zeros_like(hq_ref)
        hk_ref[...] = jnp.zeros_like(hk_ref)

    G = zg_ref[...] + bias_ref[...]
    lf = jnp.minimum(G, 0.0) - jnp.log1p(jnp.exp(-jnp.abs(G)))
    pos = lax.broadcasted_iota(jnp.int32, (S, LANES), 0) % L
    bc = lf
    sh = 1
    while sh < L:
        bc = bc + jnp.where(pos >= sh, pltpu.roll(bc, sh, axis=0), 0.0)
        sh *= 2
    colli_ref[...] = G
    colb_ref[...] = bc
    for p in range(S // LANES):
        sl = slice(p * LANES, (p + 1) * LANES)
        rowli_ref[:, sl] = G[sl, :].T[0:8, :]
        rowb_ref[:, sl] = bc[sl, :].T[0:8, :]

    lane = lax.broadcasted_iota(jnp.int32, (S, LANES), 1)
    sub = lax.broadcasted_iota(jnp.int32, (8, S), 0)
    for j in range(HP):
        cli_ref[j] = jnp.sum(jnp.where(lane == j, colli_ref[...], 0.0), axis=1, keepdims=True)
        cb_ref[j] = jnp.sum(jnp.where(lane == j + HP, colb_ref[...], 0.0), axis=1, keepdims=True)
        rli_ref[j] = jnp.sum(jnp.where(sub == j, rowli_ref[...], 0.0), axis=0, keepdims=True)
        rb_ref[j] = jnp.sum(jnp.where(sub == j + HP, rowb_ref[...], 0.0), axis=0, keepdims=True)

    def conv_silu(src_ref, w_ref, dst_ref, halo_ref, scale):
        w = w_ref[...]

        def taps(xs):
            y = xs[3] * w[0:1, :]
            y = y + xs[2] * w[1:2, :]
            y = y + xs[1] * w[2:3, :]
            y = y + xs[0] * w[3:4, :]
            return ((y * _sigmoid(y)) * scale).astype(BF16)

        def chunk(r, carry):
            r0 = pl.multiple_of(r * CB, CB)
            cur = src_ref[pl.ds(r0, CB), :].astype(F32)
            dst_ref[pl.ds(r0, CB), :] = taps([cur] + [pltpu.roll(cur, d, axis=0) for d in (1, 2, 3)])
            p0 = pl.multiple_of(jnp.maximum(r0 - 16, 0), 16)
            prev = jnp.where(r > 0, src_ref[pl.ds(p0, 16), :], halo_ref[...]).astype(F32)[8:16, :]
            both = jnp.concatenate([prev, cur[0:16, :]], axis=0)
            dst_ref[pl.ds(r0, 16), :] = taps([both[8:24, :]] + [pltpu.roll(both, d, axis=0)[8:24, :] for d in (1, 2, 3)])
            return carry
        lax.fori_loop(0, S // CB, chunk, 0)
        halo_ref[...] = src_ref[S - 16:S, :]

    conv_silu(q_ref, cq_ref, qs_ref, hq_ref, 1.0)
    conv_silu(k_ref, ck_ref, ks_ref, hk_ref, float(dk) ** -0.5)

    t_idx = lax.broadcasted_iota(jnp.int32, (L, L), 0)
    s_idx = lax.broadcasted_iota(jnp.int32, (L, L), 1)
    causal = s_idx <= t_idx

    heads = range(HP)
    cols = [slice(j * dk, (j + 1) * dk) for j in heads]
    dims_nt = (((1,), (1,)), ((), ()))

    def chunk(c, carry):
        rows = pl.ds(pl.multiple_of(c * L, L), L)
        qc = [qs_ref[rows, cols[j]] for j in heads]
        kc = [ks_ref[rows, cols[j]] for j in heads]
        vc = [v_ref[rows, cols[j]] for j in heads]
        b_col = [cb_ref[j, rows, :] for j in heads]
        li_col = [cli_ref[j, rows, :] for j in heads]
        li_row = [rli_ref[j, :, rows] for j in heads]
        b_row = [rb_ref[j, :, rows] for j in heads]
        m = [m_ref[j] for j in heads]
        a = [b_col[j] + m[j] for j in heads]
        D = [jnp.where(causal, b_col[j] - b_row[j] + li_row[j], -jnp.inf) for j in heads]
        m_t = [jnp.maximum(a[j], jnp.max(D[j], axis=1, keepdims=True)) for j in heads]
        w_inter = [jnp.exp(a[j] - m_t[j]) for j in heads]
        s_qk = [lax.dot_general(qc[j], kc[j], dims_nt, preferred_element_type=F32) for j in heads]
        P = [s_qk[j] * jnp.exp(D[j] - m_t[j]) for j in heads]
        inter = [jnp.dot(qc[j], c_ref[j].astype(BF16), preferred_element_type=F32) for j in heads]
        intra = [jnp.dot(P[j].astype(BF16), vc[j], preferred_element_type=F32) for j in heads]
        num = [w_inter[j] * inter[j] + intra[j] for j in heads]
        qn = [w_inter[j] * jnp.sum(qc[j].astype(F32) * n_ref[j], axis=1, keepdims=True)
              + jnp.sum(P[j], axis=1, keepdims=True) for j in heads]
        den = [jnp.maximum(jnp.abs(qn[j]), jnp.exp(-m_t[j])) for j in heads]
        hh = [num[j] / den[j] for j in heads]
        ms = [jnp.mean(hh[j] * hh[j], axis=1, keepdims=True) for j in heads]
        hn = [(hh[j] * lax.rsqrt(ms[j] + EPS)) * gn_ref[j] for j in heads]
        for j in heads:
            out_ref[rows, cols[j]] = (hn[j] * _sigmoid(o_ref[rows, cols[j]].astype(F32))).astype(BF16)
        bL = [b_row[j][:, L - 1:L] for j in heads]
        g_col = [bL[j] - b_col[j] + li_col[j] for j in heads]
        m_new = [jnp.maximum(bL[j] + m[j], jnp.max(g_col[j], axis=0, keepdims=True)) for j in heads]
        decay = [jnp.exp(bL[j] + m[j] - m_new[j]) for j in heads]
        kw = [kc[j].astype(F32) * jnp.exp(g_col[j] - m_new[j]) for j in heads]
        upd = [jnp.dot(kw[j].T.astype(BF16), vc[j], preferred_element_type=F32) for j in heads]
        for j in heads:
            c_ref[j] = decay[j] * c_ref[j] + upd[j]
            n_ref[j] = decay[j] * n_ref[j] + jnp.sum(kw[j], axis=0, keepdims=True)
            m_ref[j] = m_new[j]
        return carry
    lax.fori_loop(0, S // L, chunk, 0)


def _mlstm(z, zg, conv_qk, gate_bias, g_mlstm3, *, B, S, col_q, col_k, col_v, col_o):
    T = B * S
    dk = M_HEAD_DIM
    L = MLSTM_CHUNK
    HP = M_HEADS
    wd = HP * dk
    TS = min(MLSTM_TIME_BLOCK, S)
    NT = S // TS
    zspec = lambda col: pl.BlockSpec((TS, wd), lambda b, t: (b * NT + t, col // HP))
    return pl.pallas_call(
        functools.partial(_mlstm_body, L=L, CB=128 // HP, HP=HP),
        grid=(B, NT),
        in_specs=[
            zspec(col_q), zspec(col_k), zspec(col_v), zspec(col_o),
            pl.BlockSpec((TS, LANES), lambda b, t: (b * NT + t, 0)),
            pl.BlockSpec((CONV_WIDTH, wd), lambda b, t: (0, 0)),
            pl.BlockSpec((CONV_WIDTH, wd), lambda b, t: (0, 1)),
            pl.BlockSpec((1, LANES), lambda b, t: (0, 0)),
            pl.BlockSpec((HP, 1, dk), lambda b, t: (0, 0, 0)),
        ],
        out_specs=pl.BlockSpec((TS, wd), lambda b, t: (b * NT + t, 0)),
        out_shape=jax.ShapeDtypeStruct((T, M_WIDTH), BF16),
        scratch_shapes=[
            pltpu.VMEM((TS, wd), BF16), pltpu.VMEM((TS, wd), BF16),
            pltpu.VMEM((TS, LANES), F32), pltpu.VMEM((TS, LANES), F32),
            pltpu.VMEM((8, TS), F32), pltpu.VMEM((8, TS), F32),
            pltpu.VMEM((HP, TS, 1), F32), pltpu.VMEM((HP, TS, 1), F32),
            pltpu.VMEM((HP, 1, TS), F32), pltpu.VMEM((HP, 1, TS), F32),
            pltpu.VMEM((HP, dk, dk), F32), pltpu.VMEM((HP, 1, dk), F32), pltpu.VMEM((HP, 1, 1), F32),
            pltpu.VMEM((16, wd), BF16), pltpu.VMEM((16, wd), BF16),
        ],
        compiler_params=_params("parallel", "arbitrary"),
        name="mlstm",
    )(z, z, z, z, zg, conv_qk, conv_qk, gate_bias, g_mlstm3)


def _swa_body(sink_ref, q_ref, k_ref, v_ref, qa_ref, qb_ref, ka_ref, kb_ref, bd_ref, rep_ref, ones_ref, out_ref,
              kbd_ref, vbd_ref):
    n = pl.program_id(1)
    W = WINDOW
    hd = A_HEAD_DIM
    gw = A_GROUP * hd

    def norm_rope(x, ta, tb):
        x2 = x * x
        x2h = x2.astype(BF16)
        x2l = (x2 - x2h.astype(F32)).astype(BF16)
        ss = (jnp.dot(x2h, bd_ref[...], preferred_element_type=F32)
              + jnp.dot(x2l, bd_ref[...], preferred_element_type=F32))
        r = lax.rsqrt(ss * (1.0 / hd) + EPS)
        ln = lax.broadcasted_iota(jnp.int32, x.shape, 1) % hd
        swapped = jnp.where(ln < hd // 2, pltpu.roll(x, LANES - hd // 2, axis=1), pltpu.roll(x, hd // 2, axis=1))
        return r * (x * ta + swapped * tb)

    lane_head = lax.broadcasted_iota(jnp.int32, (W, gw), 1) // hd
    t_idx = lax.broadcasted_iota(jnp.int32, (W, W), 0)
    k_idx = lax.broadcasted_iota(jnp.int32, (W, W), 1)
    mask_cur = k_idx <= t_idx
    neg = -jnp.inf

    @pl.when(n == 0)
    def _():
        kbd_ref[1] = jnp.zeros(kbd_ref.shape[1:], BF16)
        vbd_ref[1] = jnp.zeros(vbd_ref.shape[1:], BF16)

    def block(half, prev_slot, cur_slot, mask_prev):
        rows = slice(half * W, (half + 1) * W)
        qa = qa_ref[rows, :]
        qb = qb_ref[rows, :]
        qp = jnp.concatenate(
            [norm_rope(q_ref[rows, t * LANES:(t + 1) * LANES].astype(F32), qa, qb) for t in range(A_WIDTH // LANES)],
            axis=1).astype(BF16)
        ka = ka_ref[rows, :]
        kb = kb_ref[rows, :]
        kp = jnp.concatenate(
            [norm_rope(k_ref[rows, t * LANES:(t + 1) * LANES].astype(F32), ka, kb)
             for t in range(A_KV_WIDTH // LANES)], axis=1).astype(BF16)
        vv = v_ref[rows, :]
        for j in range(A_KV_HEADS):
            krep = jnp.dot(kp, rep_ref[j], preferred_element_type=F32).astype(BF16)
            vrep = jnp.dot(vv, rep_ref[j], preferred_element_type=F32).astype(BF16)
            for i in range(A_GROUP):
                kbd_ref[cur_slot, j, i * W:(i + 1) * W, :] = jnp.where(lane_head == i, krep, jnp.zeros_like(krep))
                vbd_ref[cur_slot, j, i * W:(i + 1) * W, :] = jnp.where(lane_head == i, vrep, jnp.zeros_like(vrep))

        groups = range(A_KV_HEADS)
        pairs = [(j, i) for j in groups for i in range(A_GROUP)]
        dims_nt = (((1,), (1,)), ((), ()))
        qg = [qp[:, j * gw:(j + 1) * gw] for j in groups]
        s_prev = [lax.dot_general(qg[j], kbd_ref[prev_slot, j], dims_nt, preferred_element_type=F32) for j in groups]
        s_cur = [lax.dot_general(qg[j], kbd_ref[cur_slot, j], dims_nt, preferred_element_type=F32) for j in groups]
        sp = {(j, i): jnp.where(mask_prev, s_prev[j][:, i * W:(i + 1) * W], neg) for j, i in pairs}
        sc = {(j, i): jnp.where(mask_cur, s_cur[j][:, i * W:(i + 1) * W], neg) for j, i in pairs}
        sink = {(j, i): sink_ref[j * A_GROUP + i] for j, i in pairs}
        mx = {k: jnp.maximum(jnp.max(jnp.maximum(sp[k], sc[k]), axis=1, keepdims=True), sink[k]) for k in pairs}
        pp = {k: jnp.exp(sp[k] - mx[k]).astype(BF16) for k in pairs}
        pc = {k: jnp.exp(sc[k] - mx[k]).astype(BF16) for k in pairs}
        es = {k: jnp.exp(sink[k] - mx[k]) for k in pairs}
        p_prev = [jnp.concatenate([pp[(j, i)] for i in range(A_GROUP)], axis=1) for j in groups]
        p_cur = [jnp.concatenate([pc[(j, i)] for i in range(A_GROUP)], axis=1) for j in groups]
        o = [jnp.dot(p_prev[j], vbd_ref[prev_slot, j], preferred_element_type=F32)
             + jnp.dot(p_cur[j], vbd_ref[cur_slot, j], preferred_element_type=F32) for j in groups]
        den = [jnp.dot(p_prev[j], ones_ref[...], preferred_element_type=F32)
               + jnp.dot(p_cur[j], ones_ref[...], preferred_element_type=F32) for j in groups]
        for j in groups:
            sink_term = jnp.zeros((W, gw), F32)
            for i in range(A_GROUP):
                sink_term = jnp.where(lane_head == i, es[(j, i)], sink_term)
            out_ref[rows, j * gw:(j + 1) * gw] = (o[j] / (den[j] + sink_term)).astype(BF16)

    upper = k_idx > t_idx
    block(0, 1, 0, jnp.logical_and(upper, n > 0))
    block(1, 0, 1, upper)


def _swa(z, sinks, qa, qb, ka, kb, bd, rep, ones_bd, *, B, S, col_q, col_k, col_v):
    T = B * S
    W2 = 2 * WINDOW
    NB = S // W2
    gw = A_GROUP * A_HEAD_DIM
    tab = lambda: pl.BlockSpec((W2, LANES), lambda b, n: (n, 0))
    return pl.pallas_call(
        _swa_body,
        grid=(B, NB),
        in_specs=[
            pl.BlockSpec(memory_space=pltpu.SMEM),
            pl.BlockSpec((W2, A_WIDTH), lambda b, n: (b * NB + n, col_q)),
            pl.BlockSpec((W2, A_KV_WIDTH), lambda b, n: (b * NB + n, col_k)),
            pl.BlockSpec((W2, A_KV_WIDTH), lambda b, n: (b * NB + n, col_v)),
            tab(), tab(), tab(), tab(),
            pl.BlockSpec((LANES, LANES), lambda b, n: (0, 0)),
            pl.BlockSpec((A_KV_HEADS, gw, gw), lambda b, n: (0, 0, 0)),
            pl.BlockSpec((A_GROUP * WINDOW, gw), lambda b, n: (0, 0)),
        ],
        out_specs=pl.BlockSpec((W2, A_WIDTH), lambda b, n: (b * NB + n, 0)),
        out_shape=jax.ShapeDtypeStruct((T, A_WIDTH), BF16),
        scratch_shapes=[
            pltpu.VMEM((2, A_KV_HEADS, A_GROUP * WINDOW, gw), BF16),
            pltpu.VMEM((2, A_KV_HEADS, A_GROUP * WINDOW, gw), BF16),
        ],
        compiler_params=_params("parallel", "arbitrary"),
        name="swa",
    )(sinks, z, z, z, qa, qb, ka, kb, bd, rep, ones_bd)


def _merge_body(hm_ref, ha_ref, gm_ref, ga_ref, x_ref, wm_ref, wa_ref, wo_ref, gf_ref, rh_ref, rb_ref,
                x1_ref, hf_ref, lg_ref):
    pm = jnp.dot(hm_ref[...], wm_ref[...], preferred_element_type=F32)
    pa = jnp.dot(ha_ref[...], wa_ref[...], preferred_element_type=F32)
    mixed = _sigmoid(gm_ref[...].astype(F32)) * pm + _sigmoid(ga_ref[...].astype(F32)) * pa
    x1 = x_ref[...] + jnp.dot(mixed.astype(BF16), wo_ref[...], preferred_element_type=F32)
    x1_ref[...] = x1
    ms = jnp.mean(x1 * x1, axis=-1, keepdims=True)
    hf = (x1 * lax.rsqrt(ms + EPS)) * gf_ref[...]
    _store_token_rows(hf_ref, hf)
    lg_ref[...] = jnp.dot(hf.astype(BF16), rh_ref[...], preferred_element_type=F32) + rb_ref[...]


def _merge(hm, ha, z, x2, wm, wa, wo, gf, rh, rb, *, bm, col_gm, col_ga):
    T, D = x2.shape
    const = lambda shape: pl.BlockSpec(shape, lambda i: (0,) * len(shape), pipeline_mode=pl.Buffered(1))
    return pl.pallas_call(
        _merge_body,
        grid=(T // bm,),
        in_specs=[
            pl.BlockSpec((bm, M_WIDTH), lambda i: (i, 0)),
            pl.BlockSpec((bm, A_WIDTH), lambda i: (i, 0)),
            pl.BlockSpec((bm, D), lambda i: (i, col_gm)),
            pl.BlockSpec((bm, D), lambda i: (i, col_ga)),
            pl.BlockSpec((bm, D), lambda i: (i, 0)),
            const((M_WIDTH, D)), const((A_WIDTH, D)), const((D, D)), const((1, D)),
            const((D, LANES)), const((1, LANES)),
        ],
        out_specs=[
            pl.BlockSpec((bm, D), lambda i: (i, 0)),
            pl.BlockSpec((bm * (D // (2 * LANES)), LANES), lambda i: (i, 0)),
            pl.BlockSpec((bm, LANES), lambda i: (i, 0)),
        ],
        out_shape=[
            jax.ShapeDtypeStruct((T, D), F32),
            jax.ShapeDtypeStruct((T * (D // (2 * LANES)), LANES), jnp.uint32),
            jax.ShapeDtypeStruct((T, LANES), F32),
        ],
        compiler_params=_params("parallel"),
        name="merge",
    )(hm, ha, z, z, x2, wm, wa, wo, gf, rh, rb)


def _moe_body(blk_e_ref, first_ref, nxt_ref, nused_ref, tok_ref, hf_hbm, wg_hbm, wu_hbm, wd_hbm, y_ref,
              xbuf, xs_ref, stage_g, stage_u, stage_d, wg_ref, wu_ref, wd_ref, sem, wsem, *, R, CB):
    i = pl.program_id(0)
    nused = nused_ref[0]
    sub = xs_ref.shape[1] // (2 * LANES)

    def weight_copies(e):
        return (pltpu.make_async_copy(wg_hbm.at[e], stage_g, wsem.at[0]),
                pltpu.make_async_copy(wu_hbm.at[e], stage_u, wsem.at[1]),
                pltpu.make_async_copy(wd_hbm.at[e], stage_d, wsem.at[2]))

    def cast(src, dst):
        def rows(r, carry):
            sl = pl.ds(pl.multiple_of(r * CB, CB), CB)
            dst[sl, :] = src[sl, :].astype(BF16)
            return carry
        lax.fori_loop(0, src.shape[0] // CB, rows, 0)

    def start_gather(blk, slot):
        base = blk * R

        for r in range(R):
            src = pl.multiple_of(tok_ref[base + r] * sub, sub)
            pltpu.make_async_copy(hf_hbm.at[pl.ds(src, sub)], xbuf.at[slot, pl.ds(r * sub, sub)],
                                  sem.at[slot]).start()

    def wait_gather(slot):
        pltpu.make_async_copy(xbuf.at[slot], xbuf.at[slot], sem.at[slot]).wait()

    nslots = xbuf.shape[0]
    ahead = nslots - 1
    last_blk = pl.num_programs(0) - 1
    slot = i % nslots

    @pl.when(i == 0)
    def _():
        for a in range(ahead):
            start_gather(a, a)
        for c in weight_copies(blk_e_ref[0]):
            c.start(priority=1)

    @pl.when(i < nused)
    def _():
        @pl.when(first_ref[i] == 1)
        def _():
            for c in weight_copies(blk_e_ref[i]):
                c.wait()
            cast(stage_g, wg_ref)
            cast(stage_u, wu_ref)
            cast(stage_d, wd_ref)

            @pl.when(nxt_ref[i] >= 0)
            def _():
                for c in weight_copies(nxt_ref[i]):
                    c.start(priority=1)

        wait_gather(slot)
        for s in range(sub):
            lo, hi = _load_token_rows(xbuf.at[slot], R, sub, s)
            xs_ref[:, 2 * s * LANES:(2 * s + 1) * LANES] = lo.astype(BF16)
            xs_ref[:, (2 * s + 1) * LANES:(2 * s + 2) * LANES] = hi.astype(BF16)
        start_gather(jnp.minimum(i + ahead, last_blk), (i + ahead) % nslots)
        xv = xs_ref[...]
        g = jnp.dot(xv, wg_ref[...], preferred_element_type=F32)
        u = jnp.dot(xv, wu_ref[...], preferred_element_type=F32)
        hmid = ((g * _sigmoid(g)) * u).astype(BF16)
        _store_token_rows(y_ref, jnp.dot(hmid, wd_ref[...], preferred_element_type=F32))

        @pl.when(i == nused - 1)
        def _():
            for a in range(1, nslots):
                wait_gather((i + a) % nslots)


def _moe(blk_e, first, nxt, nused, row_tok, hf, wg, wu, wd, *, R):
    E, D, F = wg.shape
    sub = D // (2 * LANES)
    n_blocks = blk_e.shape[0]
    hbm = lambda: pl.BlockSpec(memory_space=pl.ANY)
    grid_spec = pltpu.PrefetchScalarGridSpec(
        num_scalar_prefetch=5,
        grid=(n_blocks,),
        in_specs=[hbm(), hbm(), hbm(), hbm()],
        out_specs=pl.BlockSpec((R * sub, LANES), lambda i, be, fi, nx, nu, tk: (jnp.minimum(i, nu[0] - 1), 0)),
        scratch_shapes=[
            pltpu.VMEM((MOE_GATHER_SLOTS, R * sub, LANES), jnp.uint32), pltpu.VMEM((R, D), BF16),
            pltpu.VMEM((D, F), F32), pltpu.VMEM((D, F), F32), pltpu.VMEM((F, D), F32),
            pltpu.VMEM((D, F), BF16), pltpu.VMEM((D, F), BF16), pltpu.VMEM((F, D), BF16),
            pltpu.SemaphoreType.DMA((MOE_GATHER_SLOTS,)), pltpu.SemaphoreType.DMA((3,)),
        ],
    )
    return pl.pallas_call(
        functools.partial(_moe_body, R=R, CB=128),
        grid_spec=grid_spec,
        out_shape=jax.ShapeDtypeStruct((n_blocks * R * sub, LANES), jnp.uint32),
        compiler_params=_params("arbitrary"),
        name="moe",
    )(blk_e, first, nxt, nused, row_tok, hf, wg, wu, wd)


def _combine_body(dest_ref, x1_ref, w_ref, yr_hbm, out_ref, ybuf, sem, *, R):
    i = pl.program_id(0)
    nsteps = pl.num_programs(0)

    sub = out_ref.shape[1] // (2 * LANES)

    def start_gather(step, slot):
        base = step * (R * TOP_K)

        def row(r, carry):
            dst = pl.multiple_of(r * sub, sub)
            for k in range(TOP_K):
                src = pl.multiple_of(dest_ref[base + r * TOP_K + k] * sub, sub)
                pltpu.make_async_copy(yr_hbm.at[pl.ds(src, sub)], ybuf.at[slot, k, pl.ds(dst, sub)],
                                      sem.at[slot]).start()
            return carry
        lax.fori_loop(0, R, row, 0, unroll=4)

    slot = i % 2

    @pl.when(i == 0)
    def _():
        start_gather(0, 0)

    pltpu.make_async_copy(ybuf.at[slot], ybuf.at[slot], sem.at[slot]).wait()

    @pl.when(i + 1 < nsteps)
    def _():
        start_gather(i + 1, 1 - slot)

    w = w_ref[...]
    for s in range(sub):
        halves0 = _load_token_rows(ybuf.at[slot, 0], R, sub, s)
        halves1 = _load_token_rows(ybuf.at[slot, 1], R, sub, s)
        for half in range(2):
            cols = slice((2 * s + half) * LANES, (2 * s + half + 1) * LANES)
            out_ref[:, cols] = x1_ref[:, cols] + (halves0[half] * w[:, 0:1] + halves1[half] * w[:, 1:2])


def _combine(dest, x1, gate_w, yr, *, R):
    T, D = x1.shape
    grid_spec = pltpu.PrefetchScalarGridSpec(
        num_scalar_prefetch=1,
        grid=(T // R,),
        in_specs=[
            pl.BlockSpec((R, D), lambda i, d: (i, 0)),
            pl.BlockSpec((R, TOP_K), lambda i, d: (i, 0)),
            pl.BlockSpec(memory_space=pl.ANY),
        ],
        out_specs=pl.BlockSpec((R, D), lambda i, d: (i, 0)),
        scratch_shapes=[pltpu.VMEM((2, TOP_K, R * (D // (2 * LANES)), LANES), jnp.uint32),
                        pltpu.SemaphoreType.DMA((2,))],
    )
    return pl.pallas_call(
        functools.partial(_combine_body, R=R),
        grid_spec=grid_spec,
        out_shape=jax.ShapeDtypeStruct((T, D), F32),
        compiler_params=_params("arbitrary"),
        name="combine",
    )(dest, x1, gate_w, yr)


def _route_body(lg_ref, tri_ref, ids_ref, gate_ref, cnt_ref):
    @pl.when(pl.program_id(0) == 0)
    def _():
        cnt_ref[...] = jnp.zeros_like(cnt_ref)

    lg = lg_ref[...]
    shape = lg.shape
    lane = lax.broadcasted_iota(jnp.int32, shape, 1)
    big = jnp.int32(LANES)

    def softmax_masked(mask):
        v = jnp.where(mask, lg, -jnp.inf)
        u = jnp.exp(v - jnp.max(v, axis=1, keepdims=True))
        return jnp.where(mask, u / jnp.sum(u, axis=1, keepdims=True), -1.0)

    def top1(p):
        best = jnp.max(p, axis=1, keepdims=True)
        idx = jnp.min(jnp.where(p == best, lane, big), axis=1, keepdims=True)
        return best, idx

    g_p, g_lane = top1(softmax_masked(lane < N_GROUPS))
    grp_of_lane = lax.shift_right_arithmetic(lane - N_GROUPS, jnp.int32(3))
    in_grp = jnp.logical_and(lane >= N_GROUPS, grp_of_lane == g_lane)
    in_grp = jnp.logical_and(in_grp, lane < N_GROUPS + N_EXPERTS)
    pe = softmax_masked(in_grp)
    p1, l1 = top1(pe)
    p2, l2 = top1(jnp.where(lane == l1, -1.0, pe))
    tot = p1 + p2
    gate1 = g_p * (p1 / tot)
    gate2 = g_p * (p2 / tot)

    hot1 = lane == l1
    hot2 = lane == l2
    hot = jnp.logical_or(hot1, hot2)
    before = jnp.dot(tri_ref[...], jnp.where(hot, 1.0, 0.0).astype(BF16), preferred_element_type=F32) + cnt_ref[...]
    rank1 = jnp.sum(jnp.where(hot1, before, 0.0), axis=1, keepdims=True)
    rank2 = jnp.sum(jnp.where(hot2, before, 0.0), axis=1, keepdims=True)
    cnt_ref[...] = cnt_ref[...] + jnp.sum(jnp.where(hot, 1.0, 0.0), axis=0, keepdims=True)

    ids = jnp.where(lane == 0, l1 - N_GROUPS, jnp.where(lane == 1, l2 - N_GROUPS, 0))
    ids = jnp.where(lane == 2, rank1.astype(jnp.int32), jnp.where(lane == 3, rank2.astype(jnp.int32), ids))
    ids_ref[...] = ids
    gate_ref[...] = jnp.where(lane == 0, gate1, jnp.where(lane == 1, gate2, 0.0))


def _route_tokens(logits, *, tb):
    T = logits.shape[0]
    r = jnp.arange(tb)
    tri = (r[None, :] < r[:, None]).astype(BF16)
    return pl.pallas_call(
        _route_body,
        grid=(T // tb,),
        in_specs=[
            pl.BlockSpec((tb, LANES), lambda i: (i, 0)),
            pl.BlockSpec((tb, tb), lambda i: (0, 0)),
        ],
        out_specs=[
            pl.BlockSpec((tb, LANES), lambda i: (i, 0)),
            pl.BlockSpec((tb, LANES), lambda i: (i, 0)),
            pl.BlockSpec((1, LANES), lambda i: (0, 0)),
        ],
        out_shape=[
            jax.ShapeDtypeStruct((T, LANES), jnp.int32),
            jax.ShapeDtypeStruct((T, LANES), F32),
            jax.ShapeDtypeStruct((1, LANES), F32),
        ],
        compiler_params=_params("arbitrary"),
        name="route",
    )(logits, tri)


def _route(logits, R):
    T = logits.shape[0]
    ids, gates, cnt = _route_tokens(logits, tb=512 if T % 512 == 0 else T)
    gate = gates[:, 0:TOP_K]
    M = T * TOP_K
    eid_f = ids[:, 0:TOP_K].reshape(M)
    rank = ids[:, TOP_K:2 * TOP_K].reshape(M)
    counts = cnt[0, N_GROUPS:N_GROUPS + N_EXPERTS].astype(jnp.int32)
    padded = (counts + R - 1) // R * R
    pend = jnp.cumsum(padded)
    pstart = pend - padded
    dest = (pstart[eid_f] + rank).astype(jnp.int32)
    n_blocks = -(-M // R) + N_EXPERTS
    tok_f = jnp.arange(M, dtype=jnp.int32) // TOP_K
    row_tok = jnp.zeros((n_blocks * R,), jnp.int32).at[dest].set(tok_f)
    blk_start = jnp.arange(n_blocks, dtype=jnp.int32) * R
    blk_e = jnp.sum((pend[None, :] <= blk_start[:, None]).astype(jnp.int32), axis=1)
    nused = (pend[-1] // R).astype(jnp.int32)
    last_e = blk_e[jnp.maximum(nused - 1, 0)]
    blk_e = jnp.where(jnp.arange(n_blocks) < nused, blk_e, last_e)
    blk_e = jnp.minimum(blk_e, N_EXPERTS - 1).astype(jnp.int32)
    e_idx = jnp.arange(N_EXPERTS, dtype=jnp.int32)
    cand = jnp.where(counts > 0, e_idx, N_EXPERTS)
    sfx = lax.cummin(cand[::-1])[::-1]
    nxt_of_e = jnp.concatenate([sfx[1:], jnp.full((1,), N_EXPERTS, jnp.int32)])
    nxt_of_e = jnp.where(nxt_of_e < N_EXPERTS, nxt_of_e, -1)
    nxt = nxt_of_e[blk_e].astype(jnp.int32)
    changed = jnp.concatenate([jnp.ones((1,), bool), blk_e[1:] != blk_e[:-1]])
    first = jnp.logical_and(changed, jnp.arange(n_blocks) < nused).astype(jnp.int32)
    return gate, dest, row_tok, blk_e, first, nxt, nused.reshape(1)


def _rope_tables(S, gain, scale):
    half = A_HEAD_DIM // 2
    freqs = ROPE_THETA ** (-jnp.arange(half, dtype=F32) / half)
    ang = jnp.arange(S, dtype=F32)[:, None] * freqs[None, :]
    cos = jnp.cos(ang)
    sin = jnp.sin(ang)
    g1, g2 = gain[:half], gain[half:]
    a_head = jnp.concatenate([cos * g1, cos * g2], axis=1)
    b_head = jnp.concatenate([-sin * g2, sin * g1], axis=1)
    reps = LANES // A_HEAD_DIM
    return jnp.tile(a_head, (1, reps)) * scale, jnp.tile(b_head, (1, reps)) * scale


def kernel(x, g_mix, w_in, conv_qk, b_igate, b_fgate, g_mlstm, g_q, g_k, sinks, w_proj_m, w_proj_a, w_out,
           g_ffn, w_group, b_group, w_expert, b_expert, w_gate, w_up, w_down):
    B, S, D = x.shape
    T = B * S
    depth = g_mix.shape[0]
    xf = x.reshape(T, D)

    sizes = (M_WIDTH, M_WIDTH, M_WIDTH, M_WIDTH, M_HEADS, M_HEADS, A_WIDTH, A_KV_WIDTH, A_KV_WIDTH, D, D)
    offs = [0]
    for s_ in sizes:
        offs.append(offs[-1] + s_)
    seg = lambda w, idx: w[:, offs[idx]:offs[idx + 1]]
    order = (9, 10, 0, 1, 2, 3, 6, 7, 8)
    new_off = {}
    acc = 0
    for idx in order:
        new_off[idx] = acc
        acc += sizes[idx]
    dk = M_HEAD_DIM

    lane128 = jnp.arange(LANES)
    bd = (lane128[:, None] // A_HEAD_DIM == lane128[None, :] // A_HEAD_DIM).astype(BF16)
    gw = A_GROUP * A_HEAD_DIM
    r_idx = jnp.arange(gw)
    rep = jnp.stack([(r_idx[:, None] == (j * A_HEAD_DIM + r_idx[None, :] % A_HEAD_DIM)).astype(BF16)
                     for j in range(A_KV_HEADS)])
    ones_bd = (jnp.arange(A_GROUP * WINDOW)[:, None] // WINDOW == r_idx[None, :] // A_HEAD_DIM).astype(BF16)

    for l in range(depth):
        w16 = w_in[l].astype(BF16)
        w_rep = jnp.concatenate([seg(w16, idx) for idx in order], axis=1)
        w_gates = jnp.concatenate([seg(w16, 4), seg(w16, 5)], axis=1)
        w_gates = jnp.pad(w_gates, ((0, 0), (0, LANES - 2 * M_HEADS)))
        z, zg = _in_proj(xf, g_mix[l][None, :], w_rep, w_gates, bm=1024 if T % 1024 == 0 else T,
                         bn=w_rep.shape[1] // 4)

        gate_bias = jnp.pad(jnp.concatenate([b_igate[l], b_fgate[l]]), (0, LANES - 2 * M_HEADS))[None, :]
        hm = _mlstm(z, zg, conv_qk[l], gate_bias, g_mlstm[l][:, None, :], B=B, S=S,
                    col_q=new_off[0] // dk, col_k=new_off[1] // dk, col_v=new_off[2] // dk, col_o=new_off[3] // dk)

        qa, qb = _rope_tables(S, g_q[l], A_HEAD_DIM ** -0.5)
        ka, kb = _rope_tables(S, g_k[l], 1.0)
        ha = _swa(z, sinks[l], qa, qb, ka, kb, bd, rep, ones_bd, B=B, S=S,
                  col_q=new_off[6] // A_WIDTH, col_k=new_off[7] // A_KV_WIDTH, col_v=new_off[8] // A_KV_WIDTH)

        w_router = jnp.pad(jnp.concatenate([w_group[l], w_expert[l]], axis=1),
                           ((0, 0), (0, LANES - N_GROUPS - N_EXPERTS)))
        r_b = jnp.pad(jnp.concatenate([b_group[l], b_expert[l]]), (0, LANES - N_GROUPS - N_EXPERTS))[None, :]
        x1, hf, logits = _merge(hm, ha, z, xf, w_proj_m[l].astype(BF16), w_proj_a[l].astype(BF16),
                                w_out[l].astype(BF16), g_ffn[l][None, :], w_router.astype(BF16), r_b,
                                bm=256, col_gm=new_off[9] // D, col_ga=new_off[10] // D)

        gate, dest, row_tok, blk_e, first, nxt, nused = _route(logits, MOE_ROWS)
        yr = _moe(blk_e, first, nxt, nused, row_tok, hf, w_gate[l], w_up[l], w_down[l], R=MOE_ROWS)
        xf = _combine(dest, x1, gate, yr, R=COMBINE_ROWS)
    return xf.reshape(B, S, D)
```

```python
import functools

import jax
import jax.numpy as jnp
from jax import lax
from jax.experimental import pallas as pl
from jax.experimental.pallas import tpu as pltpu

F32 = jnp.float32
BF16 = jnp.bfloat16
EPS = 1e-6
LANES = 128
VMEM_LIMIT = 56 * 1024 * 1024

M_HEADS = 4
M_HEAD_DIM = 256
M_WIDTH = M_HEADS * M_HEAD_DIM
CONV_WIDTH = 4
A_HEADS = 16
A_KV_HEADS = 4
A_GROUP = A_HEADS // A_KV_HEADS
A_HEAD_DIM = 64
A_WIDTH = A_HEADS * A_HEAD_DIM
A_KV_WIDTH = A_KV_HEADS * A_HEAD_DIM
WINDOW = 128
ROPE_THETA = 10000.0
N_GROUPS = 8
EXPERTS_PER_GROUP = 8
N_EXPERTS = N_GROUPS * EXPERTS_PER_GROUP
TOP_K = 2

MLSTM_CHUNK = 128
MLSTM_TIME_BLOCK = 512
MOE_ROWS = 256
MOE_GATHER_SLOTS = 3
COMBINE_ROWS = 256


def _sigmoid(v):
    return 1.0 / (1.0 + jnp.exp(-v))


def _params(*sem):
    return pltpu.CompilerParams(dimension_semantics=sem, vmem_limit_bytes=VMEM_LIMIT)


def _pack_bf16_pair(lo, hi):
    lo_b = lax.bitcast_convert_type(lo.astype(BF16).astype(F32), jnp.uint32)
    hi_b = lax.bitcast_convert_type(hi.astype(BF16).astype(F32), jnp.uint32)
    return lax.shift_right_logical(lo_b, jnp.uint32(16)) | hi_b


def _unpack_bf16_pair(word):
    lo = lax.bitcast_convert_type(lax.shift_left(word, jnp.uint32(16)), F32)
    hi = lax.bitcast_convert_type(word & jnp.uint32(0xFFFF0000), F32)
    return lo, hi


def _store_token_rows(ref, val):
    n, d = val.shape
    sub = d // (2 * LANES)
    for s in range(sub):
        c = 2 * s * LANES
        ref[pl.ds(s, n, stride=sub), :] = _pack_bf16_pair(val[:, c:c + LANES], val[:, c + LANES:c + 2 * LANES])


def _load_token_rows(ref, n, sub, s):
    return _unpack_bf16_pair(ref[pl.ds(s, n, stride=sub), :])


def _inproj_body(x_ref, g_ref, w_ref, wgate_ref, z_ref, zg_ref, hn_ref, *, sub):
    bm = x_ref.shape[0]

    @pl.when(pl.program_id(1) == 0)
    def _():
        def rows(r, carry):
            sl = pl.ds(pl.multiple_of(r * sub, sub), sub)
            xv = x_ref[sl, :]
            ms = jnp.mean(xv * xv, axis=-1, keepdims=True)
            hn_ref[sl, :] = ((xv * lax.rsqrt(ms + EPS)) * g_ref[...]).astype(BF16)
            return carry
        lax.fori_loop(0, bm // sub, rows, 0)
        zg_ref[...] = jnp.dot(hn_ref[...], wgate_ref[...], preferred_element_type=F32)

    z_ref[...] = jnp.dot(hn_ref[...], w_ref[...], preferred_element_type=F32).astype(BF16)


def _in_proj(x2, g, w, wgate, *, bm, bn):
    T, D = x2.shape
    N = w.shape[1]
    return pl.pallas_call(
        functools.partial(_inproj_body, sub=128),
        grid=(T // bm, N // bn),
        in_specs=[
            pl.BlockSpec((bm, D), lambda i, j: (i, 0)),
            pl.BlockSpec((1, D), lambda i, j: (0, 0)),
            pl.BlockSpec((D, bn), lambda i, j: (0, j)),
            pl.BlockSpec((D, LANES), lambda i, j: (0, 0)),
        ],
        out_specs=[
            pl.BlockSpec((bm, bn), lambda i, j: (i, j)),
            pl.BlockSpec((bm, LANES), lambda i, j: (i, 0)),
        ],
        out_shape=[
            jax.ShapeDtypeStruct((T, N), BF16),
            jax.ShapeDtypeStruct((T, LANES), F32),
        ],
        scratch_shapes=[pltpu.VMEM((bm, D), BF16)],
        compiler_params=_params("parallel", "arbitrary"),
        name="in_proj",
    )(x2, g, w, wgate)


def _mlstm_body(q_ref, k_ref, v_ref, o_ref, zg_ref, cq_ref, ck_ref, bias_ref, gn_ref, out_ref,
                qs_ref, ks_ref, colli_ref, colb_ref, rowli_ref, rowb_ref, cli_ref, cb_ref, rli_ref, rb_ref,
                c_ref, n_ref, m_ref, hq_ref, hk_ref, *, L, CB, HP):
    S = q_ref.shape[0]
    dk = q_ref.shape[1] // HP

    @pl.when(pl.program_id(1) == 0)
    def _():
        c_ref[...] = jnp.zeros_like(c_ref)
        n_ref[...] = jnp.zeros_like(n_ref)
        m_ref[...] = jnp.zeros_like(m_ref)
        hq_ref[...] = jnp.zeros_like(hq_ref)
        hk_ref[...] = jnp.zeros_like(hk_ref)

    G = zg_ref[...] + bias_ref[...]
    lf = jnp.minimum(G, 0.0) - jnp.log1p(jnp.exp(-jnp.abs(G)))
    pos = lax.broadcasted_iota(jnp.int32, (S, LANES), 0) % L
    bc = lf
    sh = 1
    while sh < L:
        bc = bc + jnp.where(pos >= sh, pltpu.roll(bc, sh, axis=0), 0.0)
        sh *= 2
    colli_ref[...] = G
    colb_ref[...] = bc
    for p in range(S // LANES):
        sl = slice(p * LANES, (p + 1) * LANES)
        rowli_ref[:, sl] = G[sl, :].T[0:8, :]
        rowb_ref[:, sl] = bc[sl, :].T[0:8, :]

    lane = lax.broadcasted_iota(jnp.int32, (S, LANES), 1)
    sub = lax.broadcasted_iota(jnp.int32, (8, S), 0)
    for j in range(HP):
        cli_ref[j] = jnp.sum(jnp.where(lane == j, colli_ref[...], 0.0), axis=1, keepdims=True)
        cb_ref[j] = jnp.sum(jnp.where(lane == j + HP, colb_ref[...], 0.0), axis=1, keepdims=True)
        rli_ref[j] = jnp.sum(jnp.where(sub == j, rowli_ref[...], 0.0), axis=0, keepdims=True)
        rb_ref[j] = jnp.sum(jnp.where(sub == j + HP, rowb_ref[...], 0.0), axis=0, keepdims=True)

    def conv_silu(src_ref, w_ref, dst_ref, halo_ref, scale):
        w = w_ref[...]

        def taps(xs):
            y = xs[3] * w[0:1, :]
            y = y + xs[2] * w[1:2, :]
            y = y + xs[1] * w[2:3, :]
            y = y + xs[0] * w[3:4, :]
            return ((y * _sigmoid(y)) * scale).astype(BF16)

        def chunk(r, carry):
            r0 = pl.multiple_of(r * CB, CB)
            cur = src_ref[pl.ds(r0, CB), :].astype(F32)
            dst_ref[pl.ds(r0, CB), :] = taps([cur] + [pltpu.roll(cur, d, axis=0) for d in (1, 2, 3)])
            p0 = pl.multiple_of(jnp.maximum(r0 - 16, 0), 16)
            prev = jnp.where(r > 0, src_ref[pl.ds(p0, 16), :], halo_ref[...]).astype(F32)[8:16, :]
            both = jnp.concatenate([prev, cur[0:16, :]], axis=0)
            dst_ref[pl.ds(r0, 16), :] = taps([both[8:24, :]] + [pltpu.roll(both, d, axis=0)[8:24, :] for d in (1, 2, 3)])
            return carry
        lax.fori_loop(0, S // CB, chunk, 0)
        halo_ref[...] = src_ref[S - 16:S, :]

    conv_silu(q_ref, cq_ref, qs_ref, hq_ref, 1.0)
    conv_silu(k_ref, ck_ref, ks_ref, hk_ref, float(dk) ** -0.5)

    t_idx = lax.broadcasted_iota(jnp.int32, (L, L), 0)
    s_idx = lax.broadcasted_iota(jnp.int32, (L, L), 1)
    causal = s_idx <= t_idx

    heads = range(HP)
    cols = [slice(j * dk, (j + 1) * dk) for j in heads]
    dims_nt = (((1,), (1,)), ((), ()))

    def chunk(c, carry):
        rows = pl.ds(pl.multiple_of(c * L, L), L)
        qc = [qs_ref[rows, cols[j]] for j in heads]
        kc = [ks_ref[rows, cols[j]] for j in heads]
        vc = [v_ref[rows, cols[j]] for j in heads]
        b_col = [cb_ref[j, rows, :] for j in heads]
        li_col = [cli_ref[j, rows, :] for j in heads]
        li_row = [rli_ref[j, :, rows] for j in heads]
        b_row = [rb_ref[j, :, rows] for j in heads]
        m = [m_ref[j] for j in heads]
        a = [b_col[j] + m[j] for j in heads]
        D = [jnp.where(causal, b_col[j] - b_row[j] + li_row[j], -jnp.inf) for j in heads]
        m_t = [jnp.maximum(a[j], jnp.max(D[j], axis=1, keepdims=True)) for j in heads]
        w_inter = [jnp.exp(a[j] - m_t[j]) for j in heads]
        s_qk = [lax.dot_general(qc[j], kc[j], dims_nt, preferred_element_type=F32) for j in heads]
        P = [s_qk[j] * jnp.exp(D[j] - m_t[j]) for j in heads]
        inter = [jnp.dot(qc[j], c_ref[j].astype(BF16), preferred_element_type=F32) for j in heads]
        intra = [jnp.dot(P[j].astype(BF16), vc[j], preferred_element_type=F32) for j in heads]
        num = [w_inter[j] * inter[j] + intra[j] for j in heads]
        qn = [w_inter[j] * jnp.sum(qc[j].astype(F32) * n_ref[j], axis=1, keepdims=True)
              + jnp.sum(P[j], axis=1, keepdims=True) for j in heads]
        den = [jnp.maximum(jnp.abs(qn[j]), jnp.exp(-m_t[j])) for j in heads]
        hh = [num[j] / den[j] for j in heads]
        ms = [jnp.mean(hh[j] * hh[j], axis=1, keepdims=True) for j in heads]
        hn = [(hh[j] * lax.rsqrt(ms[j] + EPS)) * gn_ref[j] for j in heads]
        for j in heads:
            out_ref[rows, cols[j]] = (hn[j] * _sigmoid(o_ref[rows, cols[j]].astype(F32))).astype(BF16)
        bL = [b_row[j][:, L - 1:L] for j in heads]
        g_col = [bL[j] - b_col[j] + li_col[j] for j in heads]
        m_new = [jnp.maximum(bL[j] + m[j], jnp.max(g_col[j], axis=0, keepdims=True)) for j in heads]
        decay = [jnp.exp(bL[j] + m[j] - m_new[j]) for j in heads]
        kw = [kc[j].astype(F32) * jnp.exp(g_col[j] - m_new[j]) for j in heads]
        upd = [jnp.dot(kw[j].T.astype(BF16), vc[j], preferred_element_type=F32) for j in heads]
        for j in heads:
            c_ref[j] = decay[j] * c_ref[j] + upd[j]
            n_ref[j] = decay[j] * n_ref[j] + jnp.sum(kw[j], axis=0, keepdims=True)
            m_ref[j] = m_new[j]
        return carry
    lax.fori_loop(0, S // L, chunk, 0)


def _mlstm(z, zg, conv_qk, gate_bias, g_mlstm3, *, B, S, col_q, col_k, col_v, col_o):
    T = B * S
    dk = M_HEAD_DIM
    L = MLSTM_CHUNK
    HP = M_HEADS
    wd = HP * dk
    TS = min(MLSTM_TIME_BLOCK, S)
    NT = S // TS
    zspec = lambda col: pl.BlockSpec((TS, wd), lambda b, t: (b * NT + t, col // HP))
    return pl.pallas_call(
        functools.partial(_mlstm_body, L=L, CB=128 // HP, HP=HP),
        grid=(B, NT),
        in_specs=[
            zspec(col_q), zspec(col_k), zspec(col_v), zspec(col_o),
            pl.BlockSpec((TS, LANES), lambda b, t: (b * NT + t, 0)),
            pl.BlockSpec((CONV_WIDTH, wd), lambda b, t: (0, 0)),
            pl.BlockSpec((CONV_WIDTH, wd), lambda b, t: (0, 1)),
            pl.BlockSpec((1, LANES), lambda b, t: (0, 0)),
            pl.BlockSpec((HP, 1, dk), lambda b, t: (0, 0, 0)),
        ],
        out_specs=pl.BlockSpec((TS, wd), lambda b, t: (b * NT + t, 0)),
        out_shape=jax.ShapeDtypeStruct((T, M_WIDTH), BF16),
        scratch_shapes=[
            pltpu.VMEM((TS, wd), BF16), pltpu.VMEM((TS, wd), BF16),
            pltpu.VMEM((TS, LANES), F32), pltpu.VMEM((TS, LANES), F32),
            pltpu.VMEM((8, TS), F32), pltpu.VMEM((8, TS), F32),
            pltpu.VMEM((HP, TS, 1), F32), pltpu.VMEM((HP, TS, 1), F32),
            pltpu.VMEM((HP, 1, TS), F32), pltpu.VMEM((HP, 1, TS), F32),
            pltpu.VMEM((HP, dk, dk), F32), pltpu.VMEM((HP, 1, dk), F32), pltpu.VMEM((HP, 1, 1), F32),
            pltpu.VMEM((16, wd), BF16), pltpu.VMEM((16, wd), BF16),
        ],
        compiler_params=_params("parallel", "arbitrary"),
        name="mlstm",
    )(z, z, z, z, zg, conv_qk, conv_qk, gate_bias, g_mlstm3)


def _swa_body(sink_ref, q_ref, k_ref, v_ref, qa_ref, qb_ref, ka_ref, kb_ref, bd_ref, rep_ref, ones_ref, out_ref,
              kbd_ref, vbd_ref):
    n = pl.program_id(1)
    W = WINDOW
    hd = A_HEAD_DIM
    gw = A_GROUP * hd

    def norm_rope(x, ta, tb):
        x2 = x * x
        x2h = x2.astype(BF16)
        x2l = (x2 - x2h.astype(F32)).astype(BF16)
        ss = (jnp.dot(x2h, bd_ref[...], preferred_element_type=F32)
              + jnp.dot(x2l, bd_ref[...], preferred_element_type=F32))
        r = lax.rsqrt(ss * (1.0 / hd) + EPS)
        ln = lax.broadcasted_iota(jnp.int32, x.shape, 1) % hd
        swapped = jnp.where(ln < hd // 2, pltpu.roll(x, LANES - hd // 2, axis=1), pltpu.roll(x, hd // 2, axis=1))
        return r * (x * ta + swapped * tb)

    lane_head = lax.broadcasted_iota(jnp.int32, (W, gw), 1) // hd
    t_idx = lax.broadcasted_iota(jnp.int32, (W, W), 0)
    k_idx = lax.broadcasted_iota(jnp.int32, (W, W), 1)
    mask_cur = k_idx <= t_idx
    neg = -jnp.inf

    @pl.when(n == 0)
    def _():
        kbd_ref[1] = jnp.zeros(kbd_ref.shape[1:], BF16)
        vbd_ref[1] = jnp.zeros(vbd_ref.shape[1:], BF16)

    def block(half, prev_slot, cur_slot, mask_prev):
        rows = slice(half * W, (half + 1) * W)
        qa = qa_ref[rows, :]
        qb = qb_ref[rows, :]
        qp = jnp.concatenate(
            [norm_rope(q_ref[rows, t * LANES:(t + 1) * LANES].astype(F32), qa, qb) for t in range(A_WIDTH // LANES)],
            axis=1).astype(BF16)
        ka = ka_ref[rows, :]
        kb = kb_ref[rows, :]
        kp = jnp.concatenate(
            [norm_rope(k_ref[rows, t * LANES:(t + 1) * LANES].astype(F32), ka, kb)
             for t in range(A_KV_WIDTH // LANES)], axis=1).astype(BF16)
        vv = v_ref[rows, :]
        for j in range(A_KV_HEADS):
            krep = jnp.dot(kp, rep_ref[j], preferred_element_type=F32).astype(BF16)
            vrep = jnp.dot(vv, rep_ref[j], preferred_element_type=F32).astype(BF16)
            for i in range(A_GROUP):
                kbd_ref[cur_slot, j, i * W:(i + 1) * W, :] = jnp.where(lane_head == i, krep, jnp.zeros_like(krep))
                vbd_ref[cur_slot, j, i * W:(i + 1) * W, :] = jnp.where(lane_head == i, vrep, jnp.zeros_like(vrep))

        groups = range(A_KV_HEADS)
        pairs = [(j, i) for j in groups for i in range(A_GROUP)]
        dims_nt = (((1,), (1,)), ((), ()))
        qg = [qp[:, j * gw:(j + 1) * gw] for j in groups]
        s_prev = [lax.dot_general(qg[j], kbd_ref[prev_slot, j], dims_nt, preferred_element_type=F32) for j in groups]
        s_cur = [lax.dot_general(qg[j], kbd_ref[cur_slot, j], dims_nt, preferred_element_type=F32) for j in groups]
        sp = {(j, i): jnp.where(mask_prev, s_prev[j][:, i * W:(i + 1) * W], neg) for j, i in pairs}
        sc = {(j, i): jnp.where(mask_cur, s_cur[j][:, i * W:(i + 1) * W], neg) for j, i in pairs}
        sink = {(j, i): sink_ref[j * A_GROUP + i] for j, i in pairs}
        mx = {k: jnp.maximum(jnp.max(jnp.maximum(sp[k], sc[k]), axis=1, keepdims=True), sink[k]) for k in pairs}
        pp = {k: jnp.exp(sp[k] - mx[k]).astype(BF16) for k in pairs}
        pc = {k: jnp.exp(sc[k] - mx[k]).astype(BF16) for k in pairs}
        es = {k: jnp.exp(sink[k] - mx[k]) for k in pairs}
        p_prev = [jnp.concatenate([pp[(j, i)] for i in range(A_GROUP)], axis=1) for j in groups]
        p_cur = [jnp.concatenate([pc[(j, i)] for i in range(A_GROUP)], axis=1) for j in groups]
        o = [jnp.dot(p_prev[j], vbd_ref[prev_slot, j], preferred_element_type=F32)
             + jnp.dot(p_cur[j], vbd_ref[cur_slot, j], preferred_element_type=F32) for j in groups]
        den = [jnp.dot(p_prev[j], ones_ref[...], preferred_element_type=F32)
               + jnp.dot(p_cur[j], ones_ref[...], preferred_element_type=F32) for j in groups]
        for j in groups:
            sink_term = jnp.zeros((W, gw), F32)
            for i in range(A_GROUP):
                sink_term = jnp.where(lane_head == i, es[(j, i)], sink_term)
            out_ref[rows, j * gw:(j + 1) * gw] = (o[j] / (den[j] + sink_term)).astype(BF16)

    upper = k_idx > t_idx
    block(0, 1, 0, jnp.logical_and(upper, n > 0))
    block(1, 0, 1, upper)


def _swa(z, sinks, qa, qb, ka, kb, bd, rep, ones_bd, *, B, S, col_q, col_k, col_v):
    T = B * S
    W2 = 2 * WINDOW
    NB = S // W2
    gw = A_GROUP * A_HEAD_DIM
    tab = lambda: pl.BlockSpec((W2, LANES), lambda b, n: (n, 0))
    return pl.pallas_call(
        _swa_body,
        grid=(B, NB),
        in_specs=[
            pl.BlockSpec(memory_space=pltpu.SMEM),
            pl.BlockSpec((W2, A_WIDTH), lambda b, n: (b * NB + n, col_q)),
            pl.BlockSpec((W2, A_KV_WIDTH), lambda b, n: (b * NB + n, col_k)),
            pl.BlockSpec((W2, A_KV_WIDTH), lambda b, n: (b * NB + n, col_v)),
            tab(), tab(), tab(), tab(),
            pl.BlockSpec((LANES, LANES), lambda b, n: (0, 0)),
            pl.BlockSpec((A_KV_HEADS, gw, gw), lambda b, n: (0, 0, 0)),
            pl.BlockSpec((A_GROUP * WINDOW, gw), lambda b, n: (0, 0)),
        ],
        out_specs=pl.BlockSpec((W2, A_WIDTH), lambda b, n: (b * NB + n, 0)),
        out_shape=jax.ShapeDtypeStruct((T, A_WIDTH), BF16),
        scratch_shapes=[
            pltpu.VMEM((2, A_KV_HEADS, A_GROUP * WINDOW, gw), BF16),
            pltpu.VMEM((2, A_KV_HEADS, A_GROUP * WINDOW, gw), BF16),
        ],
        compiler_params=_params("parallel", "arbitrary"),
        name="swa",
    )(sinks, z, z, z, qa, qb, ka, kb, bd, rep, ones_bd)


def _merge_body(hm_ref, ha_ref, gm_ref, ga_ref, x_ref, wm_ref, wa_ref, wo_ref, gf_ref, rh_ref, rb_ref,
                x1_ref, hf_ref, lg_ref):
    pm = jnp.dot(hm_ref[...], wm_ref[...], preferred_element_type=F32)
    pa = jnp.dot(ha_ref[...], wa_ref[...], preferred_element_type=F32)
    mixed = _sigmoid(gm_ref[...].astype(F32)) * pm + _sigmoid(ga_ref[...].astype(F32)) * pa
    x1 = x_ref[...] + jnp.dot(mixed.astype(BF16), wo_ref[...], preferred_element_type=F32)
    x1_ref[...] = x1
    ms = jnp.mean(x1 * x1, axis=-1, keepdims=True)
    hf = (x1 * lax.rsqrt(ms + EPS)) * gf_ref[...]
    _store_token_rows(hf_ref, hf)
    lg_ref[...] = jnp.dot(hf.astype(BF16), rh_ref[...], preferred_element_type=F32) + rb_ref[...]


def _merge(hm, ha, z, x2, wm, wa, wo, gf, rh, rb, *, bm, col_gm, col_ga):
    T, D = x2.shape
    const = lambda shape: pl.BlockSpec(shape, lambda i: (0,) * len(shape), pipeline_mode=pl.Buffered(1))
    return pl.pallas_call(
        _merge_body,
        grid=(T // bm,),
        in_specs=[
            pl.BlockSpec((bm, M_WIDTH), lambda i: (i, 0)),
            pl.BlockSpec((bm, A_WIDTH), lambda i: (i, 0)),
            pl.BlockSpec((bm, D), lambda i: (i, col_gm)),
            pl.BlockSpec((bm, D), lambda i: (i, col_ga)),
            pl.BlockSpec((bm, D), lambda i: (i, 0)),
            const((M_WIDTH, D)), const((A_WIDTH, D)), const((D, D)), const((1, D)),
            const((D, LANES)), const((1, LANES)),
        ],
        out_specs=[
            pl.BlockSpec((bm, D), lambda i: (i, 0)),
            pl.BlockSpec((bm * (D // (2 * LANES)), LANES), lambda i: (i, 0)),
            pl.BlockSpec((bm, LANES), lambda i: (i, 0)),
        ],
        out_shape=[
            jax.ShapeDtypeStruct((T, D), F32),
            jax.ShapeDtypeStruct((T * (D // (2 * LANES)), LANES), jnp.uint32),
            jax.ShapeDtypeStruct((T, LANES), F32),
        ],
        compiler_params=_params("parallel"),
        name="merge",
    )(hm, ha, z, z, x2, wm, wa, wo, gf, rh, rb)


def _moe_body(blk_e_ref, first_ref, nxt_ref, nused_ref, tok_ref, hf_hbm, wg_hbm, wu_hbm, wd_hbm, y_ref,
              xbuf, xs_ref, stage_g, stage_u, stage_d, wg_ref, wu_ref, wd_ref, sem, wsem, *, R, CB):
    i = pl.program_id(0)
    nused = nused_ref[0]
    sub = xs_ref.shape[1] // (2 * LANES)

    def weight_copies(e):
        return (pltpu.make_async_copy(wg_hbm.at[e], stage_g, wsem.at[0]),
                pltpu.make_async_copy(wu_hbm.at[e], stage_u, wsem.at[1]),
                pltpu.make_async_copy(wd_hbm.at[e], stage_d, wsem.at[2]))

    def cast(src, dst):
        def rows(r, carry):
            sl = pl.ds(pl.multiple_of(r * CB, CB), CB)
            dst[sl, :] = src[sl, :].astype(BF16)
            return carry
        lax.fori_loop(0, src.shape[0] // CB, rows, 0)

    def start_gather(blk, slot):
        base = blk * R

        for r in range(R):
            src = pl.multiple_of(tok_ref[base + r] * sub, sub)
            pltpu.make_async_copy(hf_hbm.at[pl.ds(src, sub)], xbuf.at[slot, pl.ds(r * sub, sub)],
                                  sem.at[slot]).start(priority=r % 2)

    def wait_gather(slot):
        pltpu.make_async_copy(xbuf.at[slot], xbuf.at[slot], sem.at[slot]).wait()

    nslots = xbuf.shape[0]
    ahead = nslots - 1
    last_blk = pl.num_programs(0) - 1
    slot = i % nslots

    @pl.when(i == 0)
    def _():
        for a in range(ahead):
            start_gather(a, a)
        for c in weight_copies(blk_e_ref[0]):
            c.start(priority=1)

    @pl.when(i < nused)
    def _():
        @pl.when(first_ref[i] == 1)
        def _():
            for c in weight_copies(blk_e_ref[i]):
                c.wait()
            cast(stage_g, wg_ref)
            cast(stage_u, wu_ref)
            cast(stage_d, wd_ref)

            @pl.when(nxt_ref[i] >= 0)
            def _():
                for c in weight_copies(nxt_ref[i]):
                    c.start(priority=1)

        wait_gather(slot)
        for s in range(sub):
            lo, hi = _load_token_rows(xbuf.at[slot], R, sub, s)
            xs_ref[:, 2 * s * LANES:(2 * s + 1) * LANES] = lo.astype(BF16)
            xs_ref[:, (2 * s + 1) * LANES:(2 * s + 2) * LANES] = hi.astype(BF16)
        start_gather(jnp.minimum(i + ahead, last_blk), (i + ahead) % nslots)
        xv = xs_ref[...]
        g = jnp.dot(xv, wg_ref[...], preferred_element_type=F32)
        u = jnp.dot(xv, wu_ref[...], preferred_element_type=F32)
        hmid = ((g * _sigmoid(g)) * u).astype(BF16)
        _store_token_rows(y_ref, jnp.dot(hmid, wd_ref[...], preferred_element_type=F32))

        @pl.when(i == nused - 1)
        def _():
            for a in range(1, nslots):
                wait_gather((i + a) % nslots)


def _moe(blk_e, first, nxt, nused, row_tok, hf, wg, wu, wd, *, R):
    E, D, F = wg.shape
    sub = D // (2 * LANES)
    n_blocks = blk_e.shape[0]
    hbm = lambda: pl.BlockSpec(memory_space=pl.ANY)
    grid_spec = pltpu.PrefetchScalarGridSpec(
        num_scalar_prefetch=5,
        grid=(n_blocks,),
        in_specs=[hbm(), hbm(), hbm(), hbm()],
        out_specs=pl.BlockSpec((R * sub, LANES), lambda i, be, fi, nx, nu, tk: (jnp.minimum(i, nu[0] - 1), 0)),
        scratch_shapes=[
            pltpu.VMEM((MOE_GATHER_SLOTS, R * sub, LANES), jnp.uint32), pltpu.VMEM((R, D), BF16),
            pltpu.VMEM((D, F), F32), pltpu.VMEM((D, F), F32), pltpu.VMEM((F, D), F32),
            pltpu.VMEM((D, F), BF16), pltpu.VMEM((D, F), BF16), pltpu.VMEM((F, D), BF16),
            pltpu.SemaphoreType.DMA((MOE_GATHER_SLOTS,)), pltpu.SemaphoreType.DMA((3,)),
        ],
    )
    return pl.pallas_call(
        functools.partial(_moe_body, R=R, CB=128),
        grid_spec=grid_spec,
        out_shape=jax.ShapeDtypeStruct((n_blocks * R * sub, LANES), jnp.uint32),
        compiler_params=_params("arbitrary"),
        name="moe",
    )(blk_e, first, nxt, nused, row_tok, hf, wg, wu, wd)


def _combine_body(dest_ref, x1_ref, w_ref, yr_hbm, out_ref, ybuf, sem, *, R):
    i = pl.program_id(0)
    nsteps = pl.num_programs(0)

    sub = out_ref.shape[1] // (2 * LANES)

    def start_gather(step, slot):
        base = step * (R * TOP_K)

        def row(r, carry):
            dst = pl.multiple_of(r * sub, sub)
            for k in range(TOP_K):
                src = pl.multiple_of(dest_ref[base + r * TOP_K + k] * sub, sub)
                pltpu.make_async_copy(yr_hbm.at[pl.ds(src, sub)], ybuf.at[slot, k, pl.ds(dst, sub)],
                                      sem.at[slot]).start(priority=k % 2)
            return carry
        lax.fori_loop(0, R, row, 0, unroll=4)

    slot = i % 2

    @pl.when(i == 0)
    def _():
        start_gather(0, 0)

    pltpu.make_async_copy(ybuf.at[slot], ybuf.at[slot], sem.at[slot]).wait()

    @pl.when(i + 1 < nsteps)
    def _():
        start_gather(i + 1, 1 - slot)

    w = w_ref[...]
    for s in range(sub):
        halves0 = _load_token_rows(ybuf.at[slot, 0], R, sub, s)
        halves1 = _load_token_rows(ybuf.at[slot, 1], R, sub, s)
        for half in range(2):
            cols = slice((2 * s + half) * LANES, (2 * s + half + 1) * LANES)
            out_ref[:, cols] = x1_ref[:, cols] + (halves0[half] * w[:, 0:1] + halves1[half] * w[:, 1:2])


def _combine(dest, x1, gate_w, yr, *, R):
    T, D = x1.shape
    grid_spec = pltpu.PrefetchScalarGridSpec(
        num_scalar_prefetch=1,
        grid=(T // R,),
        in_specs=[
            pl.BlockSpec((R, D), lambda i, d: (i, 0)),
            pl.BlockSpec((R, TOP_K), lambda i, d: (i, 0)),
            pl.BlockSpec(memory_space=pl.ANY),
        ],
        out_specs=pl.BlockSpec((R, D), lambda i, d: (i, 0)),
        scratch_shapes=[pltpu.VMEM((2, TOP_K, R * (D // (2 * LANES)), LANES), jnp.uint32),
                        pltpu.SemaphoreType.DMA((2,))],
    )
    return pl.pallas_call(
        functools.partial(_combine_body, R=R),
        grid_spec=grid_spec,
        out_shape=jax.ShapeDtypeStruct((T, D), F32),
        compiler_params=_params("arbitrary"),
        name="combine",
    )(dest, x1, gate_w, yr)


def _route_body(lg_ref, tri_ref, ids_ref, gate_ref, cnt_ref):
    @pl.when(pl.program_id(0) == 0)
    def _():
        cnt_ref[...] = jnp.zeros_like(cnt_ref)

    lg = lg_ref[...]
    shape = lg.shape
    lane = lax.broadcasted_iota(jnp.int32, shape, 1)
    big = jnp.int32(LANES)

    def softmax_masked(mask):
        v = jnp.where(mask, lg, -jnp.inf)
        u = jnp.exp(v - jnp.max(v, axis=1, keepdims=True))
        return jnp.where(mask, u / jnp.sum(u, axis=1, keepdims=True), -1.0)

    def top1(p):
        best = jnp.max(p, axis=1, keepdims=True)
        idx = jnp.min(jnp.where(p == best, lane, big), axis=1, keepdims=True)
        return best, idx

    g_p, g_lane = top1(softmax_masked(lane < N_GROUPS))
    grp_of_lane = lax.shift_right_arithmetic(lane - N_GROUPS, jnp.int32(3))
    in_grp = jnp.logical_and(lane >= N_GROUPS, grp_of_lane == g_lane)
    in_grp = jnp.logical_and(in_grp, lane < N_GROUPS + N_EXPERTS)
    pe = softmax_masked(in_grp)
    p1, l1 = top1(pe)
    p2, l2 = top1(jnp.where(lane == l1, -1.0, pe))
    tot = p1 + p2
    gate1 = g_p * (p1 / tot)
    gate2 = g_p * (p2 / tot)

    hot1 = lane == l1
    hot2 = lane == l2
    hot = jnp.logical_or(hot1, hot2)
    before = jnp.dot(tri_ref[...], jnp.where(hot, 1.0, 0.0).astype(BF16), preferred_element_type=F32) + cnt_ref[...]
    rank1 = jnp.sum(jnp.where(hot1, before, 0.0), axis=1, keepdims=True)
    rank2 = jnp.sum(jnp.where(hot2, before, 0.0), axis=1, keepdims=True)
    cnt_ref[...] = cnt_ref[...] + jnp.sum(jnp.where(hot, 1.0, 0.0), axis=0, keepdims=True)

    ids = jnp.where(lane == 0, l1 - N_GROUPS, jnp.where(lane == 1, l2 - N_GROUPS, 0))
    ids = jnp.where(lane == 2, rank1.astype(jnp.int32), jnp.where(lane == 3, rank2.astype(jnp.int32), ids))
    ids_ref[...] = ids
    gate_ref[...] = jnp.where(lane == 0, gate1, jnp.where(lane == 1, gate2, 0.0))


def _route_tokens(logits, *, tb):
    T = logits.shape[0]
    r = jnp.arange(tb)
    tri = (r[None, :] < r[:, None]).astype(BF16)
    return pl.pallas_call(
        _route_body,
        grid=(T // tb,),
        in_specs=[
            pl.BlockSpec((tb, LANES), lambda i: (i, 0)),
            pl.BlockSpec((tb, tb), lambda i: (0, 0)),
        ],
        out_specs=[
            pl.BlockSpec((tb, LANES), lambda i: (i, 0)),
            pl.BlockSpec((tb, LANES), lambda i: (i, 0)),
            pl.BlockSpec((1, LANES), lambda i: (0, 0)),
        ],
        out_shape=[
            jax.ShapeDtypeStruct((T, LANES), jnp.int32),
            jax.ShapeDtypeStruct((T, LANES), F32),
            jax.ShapeDtypeStruct((1, LANES), F32),
        ],
        compiler_params=_params("arbitrary"),
        name="route",
    )(logits, tri)


def _route(logits, R):
    T = logits.shape[0]
    ids, gates, cnt = _route_tokens(logits, tb=512 if T % 512 == 0 else T)
    gate = gates[:, 0:TOP_K]
    M = T * TOP_K
    eid_f = ids[:, 0:TOP_K].reshape(M)
    rank = ids[:, TOP_K:2 * TOP_K].reshape(M)
    counts = cnt[0, N_GROUPS:N_GROUPS + N_EXPERTS].astype(jnp.int32)
    padded = (counts + R - 1) // R * R
    pend = jnp.cumsum(padded)
    pstart = pend - padded
    dest = (pstart[eid_f] + rank).astype(jnp.int32)
    n_blocks = -(-M // R) + N_EXPERTS
    tok_f = jnp.arange(M, dtype=jnp.int32) // TOP_K
    row_tok = jnp.zeros((n_blocks * R,), jnp.int32).at[dest].set(tok_f)
    blk_start = jnp.arange(n_blocks, dtype=jnp.int32) * R
    blk_e = jnp.sum((pend[None, :] <= blk_start[:, None]).astype(jnp.int32), axis=1)
    nused = (pend[-1] // R).astype(jnp.int32)
    last_e = blk_e[jnp.maximum(nused - 1, 0)]
    blk_e = jnp.where(jnp.arange(n_blocks) < nused, blk_e, last_e)
    blk_e = jnp.minimum(blk_e, N_EXPERTS - 1).astype(jnp.int32)
    e_idx = jnp.arange(N_EXPERTS, dtype=jnp.int32)
    cand = jnp.where(counts > 0, e_idx, N_EXPERTS)
    sfx = lax.cummin(cand[::-1])[::-1]
    nxt_of_e = jnp.concatenate([sfx[1:], jnp.full((1,), N_EXPERTS, jnp.int32)])
    nxt_of_e = jnp.where(nxt_of_e < N_EXPERTS, nxt_of_e, -1)
    nxt = nxt_of_e[blk_e].astype(jnp.int32)
    changed = jnp.concatenate([jnp.ones((1,), bool), blk_e[1:] != blk_e[:-1]])
    first = jnp.logical_and(changed, jnp.arange(n_blocks) < nused).astype(jnp.int32)
    return gate, dest, row_tok, blk_e, first, nxt, nused.reshape(1)


def _rope_tables(S, gain, scale):
    half = A_HEAD_DIM // 2
    freqs = ROPE_THETA ** (-jnp.arange(half, dtype=F32) / half)
    ang = jnp.arange(S, dtype=F32)[:, None] * freqs[None, :]
    cos = jnp.cos(ang)
    sin = jnp.sin(ang)
    g1, g2 = gain[:half], gain[half:]
    a_head = jnp.concatenate([cos * g1, cos * g2], axis=1)
    b_head = jnp.concatenate([-sin * g2, sin * g1], axis=1)
    reps = LANES // A_HEAD_DIM
    return jnp.tile(a_head, (1, reps)) * scale, jnp.tile(b_head, (1, reps)) * scale


def kernel(x, g_mix, w_in, conv_qk, b_igate, b_fgate, g_mlstm, g_q, g_k, sinks, w_proj_m, w_proj_a, w_out,
           g_ffn, w_group, b_group, w_expert, b_expert, w_gate, w_up, w_down):
    B, S, D = x.shape
    T = B * S
    depth = g_mix.shape[0]
    xf = x.reshape(T, D)

    sizes = (M_WIDTH, M_WIDTH, M_WIDTH, M_WIDTH, M_HEADS, M_HEADS, A_WIDTH, A_KV_WIDTH, A_KV_WIDTH, D, D)
    offs = [0]
    for s_ in sizes:
        offs.append(offs[-1] + s_)
    seg = lambda w, idx: w[:, offs[idx]:offs[idx + 1]]
    order = (9, 10, 0, 1, 2, 3, 6, 7, 8)
    new_off = {}
    acc = 0
    for idx in order:
        new_off[idx] = acc
        acc += sizes[idx]
    dk = M_HEAD_DIM

    lane128 = jnp.arange(LANES)
    bd = (lane128[:, None] // A_HEAD_DIM == lane128[None, :] // A_HEAD_DIM).astype(BF16)
    gw = A_GROUP * A_HEAD_DIM
    r_idx = jnp.arange(gw)
    rep = jnp.stack([(r_idx[:, None] == (j * A_HEAD_DIM + r_idx[None, :] % A_HEAD_DIM)).astype(BF16)
                     for j in range(A_KV_HEADS)])
    ones_bd = (jnp.arange(A_GROUP * WINDOW)[:, None] // WINDOW == r_idx[None, :] // A_HEAD_DIM).astype(BF16)

    for l in range(depth):
        w16 = w_in[l].astype(BF16)
        w_rep = jnp.concatenate([seg(w16, idx) for idx in order], axis=1)
        w_gates = jnp.concatenate([seg(w16, 4), seg(w16, 5)], axis=1)
        w_gates = jnp.pad(w_gates, ((0, 0), (0, LANES - 2 * M_HEADS)))
        z, zg = _in_proj(xf, g_mix[l][None, :], w_rep, w_gates, bm=1024 if T % 1024 == 0 else T,
                         bn=w_rep.shape[1] // 4)

        gate_bias = jnp.pad(jnp.concatenate([b_igate[l], b_fgate[l]]), (0, LANES - 2 * M_HEADS))[None, :]
        hm = _mlstm(z, zg, conv_qk[l], gate_bias, g_mlstm[l][:, None, :], B=B, S=S,
                    col_q=new_off[0] // dk, col_k=new_off[1] // dk, col_v=new_off[2] // dk, col_o=new_off[3] // dk)

        qa, qb = _rope_tables(S, g_q[l], A_HEAD_DIM ** -0.5)
        ka, kb = _rope_tables(S, g_k[l], 1.0)
        ha = _swa(z, sinks[l], qa, qb, ka, kb, bd, rep, ones_bd, B=B, S=S,
                  col_q=new_off[6] // A_WIDTH, col_k=new_off[7] // A_KV_WIDTH, col_v=new_off[8] // A_KV_WIDTH)

        w_router = jnp.pad(jnp.concatenate([w_group[l], w_expert[l]], axis=1),
                           ((0, 0), (0, LANES - N_GROUPS - N_EXPERTS)))
        r_b = jnp.pad(jnp.concatenate([b_group[l], b_expert[l]]), (0, LANES - N_GROUPS - N_EXPERTS))[None, :]
        x1, hf, logits = _merge(hm, ha, z, xf, w_proj_m[l].astype(BF16), w_proj_a[l].astype(BF16),
                                w_out[l].astype(BF16), g_ffn[l][None, :], w_router.astype(BF16), r_b,
                                bm=256, col_gm=new_off[9] // D, col_ga=new_off[10] // D)

        gate, dest, row_tok, blk_e, first, nxt, nused = _route(logits, MOE_ROWS)
        yr = _moe(blk_e, first, nxt, nused, row_tok, hf, w_gate[l], w_up[l], w_down[l], R=MOE_ROWS)
        xf = _combine(dest, x1, gate, yr, R=COMBINE_ROWS)
    return xf.reshape(B, S, D)
```

```python
import functools

import jax
import jax.numpy as jnp
from jax import lax
from jax.experimental import pallas as pl
from jax.experimental.pallas import tpu as pltpu

F32 = jnp.float32
BF16 = jnp.bfloat16
EPS = 1e-6
LANES = 128
VMEM_LIMIT = 56 * 1024 * 1024

M_HEADS = 4
M_HEAD_DIM = 256
M_WIDTH = M_HEADS * M_HEAD_DIM
CONV_WIDTH = 4
A_HEADS = 16
A_KV_HEADS = 4
A_GROUP = A_HEADS // A_KV_HEADS
A_HEAD_DIM = 64
A_WIDTH = A_HEADS * A_HEAD_DIM
A_KV_WIDTH = A_KV_HEADS * A_HEAD_DIM
WINDOW = 128
ROPE_THETA = 10000.0
N_GROUPS = 8
EXPERTS_PER_GROUP = 8
N_EXPERTS = N_GROUPS * EXPERTS_PER_GROUP
TOP_K = 2

MLSTM_CHUNK = 128
MLSTM_TIME_BLOCK = 512
MOE_ROWS = 256
MOE_GATHER_SLOTS = 3
COMBINE_ROWS = 256


def _sigmoid(v):
    return 1.0 / (1.0 + jnp.exp(-v))


def _params(*sem):
    return pltpu.CompilerParams(dimension_semantics=sem, vmem_limit_bytes=VMEM_LIMIT)


def _pack_bf16_pair(lo, hi):
    lo_b = lax.bitcast_convert_type(lo.astype(BF16).astype(F32), jnp.uint32)
    hi_b = lax.bitcast_convert_type(hi.astype(BF16).astype(F32), jnp.uint32)
    return lax.shift_right_logical(lo_b, jnp.uint32(16)) | hi_b


def _unpack_bf16_pair(word):
    lo = lax.bitcast_convert_type(lax.shift_left(word, jnp.uint32(16)), F32)
    hi = lax.bitcast_convert_type(word & jnp.uint32(0xFFFF0000), F32)
    return lo, hi


def _store_token_rows(ref, val):
    n, d = val.shape
    sub = d // (2 * LANES)
    for s in range(sub):
        c = 2 * s * LANES
        ref[pl.ds(s, n, stride=sub), :] = _pack_bf16_pair(val[:, c:c + LANES], val[:, c + LANES:c + 2 * LANES])


def _load_token_rows(ref, n, sub, s):
    return _unpack_bf16_pair(ref[pl.ds(s, n, stride=sub), :])


def _inproj_body(x_ref, g_ref, w_ref, wgate_ref, z_ref, zg_ref, hn_ref, *, sub):
    bm = x_ref.shape[0]

    @pl.when(pl.program_id(1) == 0)
    def _():
        def rows(r, carry):
            sl = pl.ds(pl.multiple_of(r * sub, sub), sub)
            xv = x_ref[sl, :]
            ms = jnp.mean(xv * xv, axis=-1, keepdims=True)
            hn_ref[sl, :] = ((xv * lax.rsqrt(ms + EPS)) * g_ref[...]).astype(BF16)
            return carry
        lax.fori_loop(0, bm // sub, rows, 0)
        zg_ref[...] = jnp.dot(hn_ref[...], wgate_ref[...], preferred_element_type=F32)

    z_ref[...] = jnp.dot(hn_ref[...], w_ref[...], preferred_element_type=F32).astype(BF16)


def _in_proj(x2, g, w, wgate, *, bm, bn):
    T, D = x2.shape
    N = w.shape[1]
    return pl.pallas_call(
        functools.partial(_inproj_body, sub=128),
        grid=(T // bm, N // bn),
        in_specs=[
            pl.BlockSpec((bm, D), lambda i, j: (i, 0)),
            pl.BlockSpec((1, D), lambda i, j: (0, 0)),
            pl.BlockSpec((D, bn), lambda i, j: (0, j)),
            pl.BlockSpec((D, LANES), lambda i, j: (0, 0)),
        ],
        out_specs=[
            pl.BlockSpec((bm, bn), lambda i, j: (i, j)),
            pl.BlockSpec((bm, LANES), lambda i, j: (i, 0)),
        ],
        out_shape=[
            jax.ShapeDtypeStruct((T, N), BF16),
            jax.ShapeDtypeStruct((T, LANES), F32),
        ],
        scratch_shapes=[pltpu.VMEM((bm, D), BF16)],
        compiler_params=_params("parallel", "arbitrary"),
        name="in_proj",
    )(x2, g, w, wgate)


def _mlstm_body(q_ref, k_ref, v_ref, o_ref, zg_ref, cq_ref, ck_ref, bias_ref, gn_ref, out_ref,
                qs_ref, ks_ref, colli_ref, colb_ref, rowli_ref, rowb_ref, cli_ref, cb_ref, rli_ref, rb_ref,
                c_ref, n_ref, m_ref, hq_ref, hk_ref, *, L, CB, HP):
    S = q_ref.shape[0]
    dk = q_ref.shape[1] // HP

    @pl.when(pl.program_id(1) == 0)
    def _():
        c_ref[...] = jnp.zeros_like(c_ref)
        n_ref[...] = jnp.zeros_like(n_ref)
        m_ref[...] = jnp.zeros_like(m_ref)
        hq_ref[...] = jnp.zeros_like(hq_ref)
        hk_ref[...] = jnp.zeros_like(hk_ref)

    G = zg_ref[...] + bias_ref[...]
    lf = jnp.minimum(G, 0.0) - jnp.log1p(jnp.exp(-jnp.abs(G)))
    pos = lax.broadcasted_iota(jnp.int32, (S, LANES), 0) % L
    bc = lf
    sh = 1
    while sh < L:
        bc = bc + jnp.where(pos >= sh, pltpu.roll(bc, sh, axis=0), 0.0)
        sh *= 2
    colli_ref[...] = G
    colb_ref[...] = bc
    for p in range(S // LANES):
        sl = slice(p * LANES, (p + 1) * LANES)
        rowli_ref[:, sl] = G[sl, :].T[0:8, :]
        rowb_ref[:, sl] = bc[sl, :].T[0:8, :]

    lane = lax.broadcasted_iota(jnp.int32, (S, LANES), 1)
    sub = lax.broadcasted_iota(jnp.int32, (8, S), 0)
    for j in range(HP):
        cli_ref[j] = jnp.sum(jnp.where(lane == j, colli_ref[...], 0.0), axis=1, keepdims=True)
        cb_ref[j] = jnp.sum(jnp.where(lane == j + HP, colb_ref[...], 0.0), axis=1, keepdims=True)
        rli_ref[j] = jnp.sum(jnp.where(sub == j, rowli_ref[...], 0.0), axis=0, keepdims=True)
        rb_ref[j] = jnp.sum(jnp.where(sub == j + HP, rowb_ref[...], 0.0), axis=0, keepdims=True)

    def conv_silu(src_ref, w_ref, dst_ref, halo_ref, scale):
        w = w_ref[...]

        def taps(xs, wj):
            y = xs[3] * wj[0:1, :]
            y = y + xs[2] * wj[1:2, :]
            y = y + xs[1] * wj[2:3, :]
            y = y + xs[0] * wj[3:4, :]
            return ((y * _sigmoid(y)) * scale).astype(BF16)

        for j in range(HP):
            cj = slice(j * dk, (j + 1) * dk)
            wj = w[:, cj]

            def chunk(r, carry, cj=cj, wj=wj):
                r0 = pl.multiple_of(r * CB, CB)
                cur = src_ref[pl.ds(r0, CB), cj].astype(F32)
                dst_ref[pl.ds(r0, CB), cj] = taps([cur] + [pltpu.roll(cur, d, axis=0) for d in (1, 2, 3)], wj)
                p0 = pl.multiple_of(jnp.maximum(r0 - 16, 0), 16)
                prev = jnp.where(r > 0, src_ref[pl.ds(p0, 16), cj], halo_ref[:, cj]).astype(F32)[8:16, :]
                both = jnp.concatenate([prev, cur[0:16, :]], axis=0)
                dst_ref[pl.ds(r0, 16), cj] = taps(
                    [both[8:24, :]] + [pltpu.roll(both, d, axis=0)[8:24, :] for d in (1, 2, 3)], wj)
                return carry
            lax.fori_loop(0, S // CB, chunk, 0)
        halo_ref[...] = src_ref[S - 16:S, :]

    conv_silu(q_ref, cq_ref, qs_ref, hq_ref, 1.0)
    conv_silu(k_ref, ck_ref, ks_ref, hk_ref, float(dk) ** -0.5)

    t_idx = lax.broadcasted_iota(jnp.int32, (L, L), 0)
    s_idx = lax.broadcasted_iota(jnp.int32, (L, L), 1)
    causal = s_idx <= t_idx

    heads = range(HP)
    cols = [slice(j * dk, (j + 1) * dk) for j in heads]
    dims_nt = (((1,), (1,)), ((), ()))

    def chunk(c, carry):
        rows = pl.ds(pl.multiple_of(c * L, L), L)
        qc = [qs_ref[rows, cols[j]] for j in heads]
        kc = [ks_ref[rows, cols[j]] for j in heads]
        vc = [v_ref[rows, cols[j]] for j in heads]
        b_col = [cb_ref[j, rows, :] for j in heads]
        li_col = [cli_ref[j, rows, :] for j in heads]
        li_row = [rli_ref[j, :, rows] for j in heads]
        b_row = [rb_ref[j, :, rows] for j in heads]
        m = [m_ref[j] for j in heads]
        a = [b_col[j] + m[j] for j in heads]
        D = [jnp.where(causal, b_col[j] - b_row[j] + li_row[j], -jnp.inf) for j in heads]
        m_t = [jnp.maximum(a[j], jnp.max(D[j], axis=1, keepdims=True)) for j in heads]
        w_inter = [jnp.exp(a[j] - m_t[j]) for j in heads]
        s_qk = [lax.dot_general(qc[j], kc[j], dims_nt, preferred_element_type=F32) for j in heads]
        P = [s_qk[j] * jnp.exp(D[j] - m_t[j]) for j in heads]
        inter = [jnp.dot(qc[j], c_ref[j].astype(BF16), preferred_element_type=F32) for j in heads]
        intra = [jnp.dot(P[j].astype(BF16), vc[j], preferred_element_type=F32) for j in heads]
        num = [w_inter[j] * inter[j] + intra[j] for j in heads]
        qn = [w_inter[j] * jnp.sum(qc[j].astype(F32) * n_ref[j], axis=1, keepdims=True)
              + jnp.sum(P[j], axis=1, keepdims=True) for j in heads]
        den = [jnp.maximum(jnp.abs(qn[j]), jnp.exp(-m_t[j])) for j in heads]
        hh = [num[j] / den[j] for j in heads]
        ms = [jnp.mean(hh[j] * hh[j], axis=1, keepdims=True) for j in heads]
        hn = [(hh[j] * lax.rsqrt(ms[j] + EPS)) * gn_ref[j] for j in heads]
        for j in heads:
            out_ref[rows, cols[j]] = (hn[j] * _sigmoid(o_ref[rows, cols[j]].astype(F32))).astype(BF16)
        bL = [b_row[j][:, L - 1:L] for j in heads]
        g_col = [bL[j] - b_col[j] + li_col[j] for j in heads]
        m_new = [jnp.maximum(bL[j] + m[j], jnp.max(g_col[j], axis=0, keepdims=True)) for j in heads]
        decay = [jnp.exp(bL[j] + m[j] - m_new[j]) for j in heads]
        kw = [kc[j].astype(F32) * jnp.exp(g_col[j] - m_new[j]) for j in heads]
        upd = [jnp.dot(kw[j].T.astype(BF16), vc[j], preferred_element_type=F32) for j in heads]
        for j in heads:
            c_ref[j] = decay[j] * c_ref[j] + upd[j]
            n_ref[j] = decay[j] * n_ref[j] + jnp.sum(kw[j], axis=0, keepdims=True)
            m_ref[j] = m_new[j]
        return carry
    lax.fori_loop(0, S // L, chunk, 0)


def _mlstm(z, zg, conv_qk, gate_bias, g_mlstm3, *, B, S, col_q, col_k, col_v, col_o):
    T = B * S
    dk = M_HEAD_DIM
    L = MLSTM_CHUNK
    HP = M_HEADS
    wd = HP * dk
    TS = min(MLSTM_TIME_BLOCK, S)
    NT = S // TS
    zspec = lambda col: pl.BlockSpec((TS, wd), lambda b, t: (b * NT + t, col // HP))
    return pl.pallas_call(
        functools.partial(_mlstm_body, L=L, CB=min(128, TS), HP=HP),
        grid=(B, NT),
        in_specs=[
            zspec(col_q), zspec(col_k), zspec(col_v), zspec(col_o),
            pl.BlockSpec((TS, LANES), lambda b, t: (b * NT + t, 0)),
            pl.BlockSpec((CONV_WIDTH, wd), lambda b, t: (0, 0)),
            pl.BlockSpec((CONV_WIDTH, wd), lambda b, t: (0, 1)),
            pl.BlockSpec((1, LANES), lambda b, t: (0, 0)),
            pl.BlockSpec((HP, 1, dk), lambda b, t: (0, 0, 0)),
        ],
        out_specs=pl.BlockSpec((TS, wd), lambda b, t: (b * NT + t, 0)),
        out_shape=jax.ShapeDtypeStruct((T, M_WIDTH), BF16),
        scratch_shapes=[
            pltpu.VMEM((TS, wd), BF16), pltpu.VMEM((TS, wd), BF16),
            pltpu.VMEM((TS, LANES), F32), pltpu.VMEM((TS, LANES), F32),
            pltpu.VMEM((8, TS), F32), pltpu.VMEM((8, TS), F32),
            pltpu.VMEM((HP, TS, 1), F32), pltpu.VMEM((HP, TS, 1), F32),
            pltpu.VMEM((HP, 1, TS), F32), pltpu.VMEM((HP, 1, TS), F32),
            pltpu.VMEM((HP, dk, dk), F32), pltpu.VMEM((HP, 1, dk), F32), pltpu.VMEM((HP, 1, 1), F32),
            pltpu.VMEM((16, wd), BF16), pltpu.VMEM((16, wd), BF16),
        ],
        compiler_params=_params("parallel", "arbitrary"),
        name="mlstm",
    )(z, z, z, z, zg, conv_qk, conv_qk, gate_bias, g_mlstm3)


def _swa_body(sink_ref, q_ref, k_ref, v_ref, qa_ref, qb_ref, ka_ref, kb_ref, bd_ref, rep_ref, ones_ref, out_ref,
              kbd_ref, vbd_ref):
    n = pl.program_id(1)
    W = WINDOW
    hd = A_HEAD_DIM
    gw = A_GROUP * hd

    def norm_rope(x, ta, tb):
        x2 = x * x
        x2h = x2.astype(BF16)
        x2l = (x2 - x2h.astype(F32)).astype(BF16)
        ss = (jnp.dot(x2h, bd_ref[...], preferred_element_type=F32)
              + jnp.dot(x2l, bd_ref[...], preferred_element_type=F32))
        r = lax.rsqrt(ss * (1.0 / hd) + EPS)
        ln = lax.broadcasted_iota(jnp.int32, x.shape, 1) % hd
        swapped = jnp.where(ln < hd // 2, pltpu.roll(x, LANES - hd // 2, axis=1), pltpu.roll(x, hd // 2, axis=1))
        return r * (x * ta + swapped * tb)

    lane_head = lax.broadcasted_iota(jnp.int32, (W, gw), 1) // hd
    t_idx = lax.broadcasted_iota(jnp.int32, (W, W), 0)
    k_idx = lax.broadcasted_iota(jnp.int32, (W, W), 1)
    mask_cur = k_idx <= t_idx
    neg = -jnp.inf

    @pl.when(n == 0)
    def _():
        kbd_ref[1] = jnp.zeros(kbd_ref.shape[1:], BF16)
        vbd_ref[1] = jnp.zeros(vbd_ref.shape[1:], BF16)

    def prepare(half, cur_slot):
        rows = slice(half * W, (half + 1) * W)
        qa = qa_ref[rows, :]
        qb = qb_ref[rows, :]
        qp = jnp.concatenate(
            [norm_rope(q_ref[rows, t * LANES:(t + 1) * LANES].astype(F32), qa, qb) for t in range(A_WIDTH // LANES)],
            axis=1).astype(BF16)
        ka = ka_ref[rows, :]
        kb = kb_ref[rows, :]
        kp = jnp.concatenate(
            [norm_rope(k_ref[rows, t * LANES:(t + 1) * LANES].astype(F32), ka, kb)
             for t in range(A_KV_WIDTH // LANES)], axis=1).astype(BF16)
        vv = v_ref[rows, :]
        for j in range(A_KV_HEADS):
            krep = jnp.dot(kp, rep_ref[j], preferred_element_type=F32).astype(BF16)
            vrep = jnp.dot(vv, rep_ref[j], preferred_element_type=F32).astype(BF16)
            for i in range(A_GROUP):
                kbd_ref[cur_slot, j, i * W:(i + 1) * W, :] = jnp.where(lane_head == i, krep, jnp.zeros_like(krep))
                vbd_ref[cur_slot, j, i * W:(i + 1) * W, :] = jnp.where(lane_head == i, vrep, jnp.zeros_like(vrep))
        return qp

    upper = k_idx > t_idx
    blocks = ((0, 1, 0, jnp.logical_and(upper, n > 0)), (1, 0, 1, upper))

    dims_nt = (((1,), (1,)), ((), ()))
    for halves in ((blocks[0],), (blocks[1],)):
        qps = {h: prepare(h, cur) for h, _, cur, _ in halves}
        groups = tuple(range(A_KV_HEADS))
        gsel = [(h, j) for h, _, _, _ in halves for j in groups]
        pairs = [(h, j, i) for h, j in gsel for i in range(A_GROUP)]
        prev_of = {h: p for h, p, _, _ in halves}
        cur_of = {h: c for h, _, c, _ in halves}
        mprev_of = {h: m for h, _, _, m in halves}
        qg = {(h, j): qps[h][:, j * gw:(j + 1) * gw] for h, j in gsel}
        s_prev = {(h, j): lax.dot_general(qg[(h, j)], kbd_ref[prev_of[h], j], dims_nt, preferred_element_type=F32)
                  for h, j in gsel}
        s_cur = {(h, j): lax.dot_general(qg[(h, j)], kbd_ref[cur_of[h], j], dims_nt, preferred_element_type=F32)
                 for h, j in gsel}
        sp = {(h, j, i): jnp.where(mprev_of[h], s_prev[(h, j)][:, i * W:(i + 1) * W], neg) for h, j, i in pairs}
        sc = {(h, j, i): jnp.where(mask_cur, s_cur[(h, j)][:, i * W:(i + 1) * W], neg) for h, j, i in pairs}
        sink = {(h, j, i): sink_ref[j * A_GROUP + i] for h, j, i in pairs}
        mx = {k: jnp.maximum(jnp.max(jnp.maximum(sp[k], sc[k]), axis=1, keepdims=True), sink[k]) for k in pairs}
        pp = {k: jnp.exp(sp[k] - mx[k]).astype(BF16) for k in pairs}
        pc = {k: jnp.exp(sc[k] - mx[k]).astype(BF16) for k in pairs}
        es = {k: jnp.exp(sink[k] - mx[k]) for k in pairs}
        p_prev = {(h, j): jnp.concatenate([pp[(h, j, i)] for i in range(A_GROUP)], axis=1) for h, j in gsel}
        p_cur = {(h, j): jnp.concatenate([pc[(h, j, i)] for i in range(A_GROUP)], axis=1) for h, j in gsel}
        o = {(h, j): jnp.dot(p_prev[(h, j)], vbd_ref[prev_of[h], j], preferred_element_type=F32)
             + jnp.dot(p_cur[(h, j)], vbd_ref[cur_of[h], j], preferred_element_type=F32) for h, j in gsel}
        den = {(h, j): jnp.dot(p_prev[(h, j)], ones_ref[...], preferred_element_type=F32)
               + jnp.dot(p_cur[(h, j)], ones_ref[...], preferred_element_type=F32) for h, j in gsel}
        for h, j in gsel:
            sink_term = jnp.zeros((W, gw), F32)
            for i in range(A_GROUP):
                sink_term = jnp.where(lane_head == i, es[(h, j, i)], sink_term)
            out_ref[h * W:(h + 1) * W, j * gw:(j + 1) * gw] = (o[(h, j)] / (den[(h, j)] + sink_term)).astype(BF16)


def _swa(z, sinks, qa, qb, ka, kb, bd, rep, ones_bd, *, B, S, col_q, col_k, col_v):
    T = B * S
    W2 = 2 * WINDOW
    NB = S // W2
    gw = A_GROUP * A_HEAD_DIM
    tab = lambda: pl.BlockSpec((W2, LANES), lambda b, n: (n, 0))
    return pl.pallas_call(
        _swa_body,
        grid=(B, NB),
        in_specs=[
            pl.BlockSpec(memory_space=pltpu.SMEM),
            pl.BlockSpec((W2, A_WIDTH), lambda b, n: (b * NB + n, col_q)),
            pl.BlockSpec((W2, A_KV_WIDTH), lambda b, n: (b * NB + n, col_k)),
            pl.BlockSpec((W2, A_KV_WIDTH), lambda b, n: (b * NB + n, col_v)),
            tab(), tab(), tab(), tab(),
            pl.BlockSpec((LANES, LANES), lambda b, n: (0, 0)),
            pl.BlockSpec((A_KV_HEADS, gw, gw), lambda b, n: (0, 0, 0)),
            pl.BlockSpec((A_GROUP * WINDOW, gw), lambda b, n: (0, 0)),
        ],
        out_specs=pl.BlockSpec((W2, A_WIDTH), lambda b, n: (b * NB + n, 0)),
        out_shape=jax.ShapeDtypeStruct((T, A_WIDTH), BF16),
        scratch_shapes=[
            pltpu.VMEM((2, A_KV_HEADS, A_GROUP * WINDOW, gw), BF16),
            pltpu.VMEM((2, A_KV_HEADS, A_GROUP * WINDOW, gw), BF16),
        ],
        compiler_params=_params("parallel", "arbitrary"),
        name="swa",
    )(sinks, z, z, z, qa, qb, ka, kb, bd, rep, ones_bd)


def _merge_body(hm_ref, ha_ref, gm_ref, ga_ref, x_ref, wm_ref, wa_ref, wo_ref, gf_ref, rh_ref, rb_ref,
                x1_ref, hf_ref, lg_ref):
    pm = jnp.dot(hm_ref[...], wm_ref[...], preferred_element_type=F32)
    pa = jnp.dot(ha_ref[...], wa_ref[...], preferred_element_type=F32)
    mixed = _sigmoid(gm_ref[...].astype(F32)) * pm + _sigmoid(ga_ref[...].astype(F32)) * pa
    x1 = x_ref[...] + jnp.dot(mixed.astype(BF16), wo_ref[...], preferred_element_type=F32)
    x1_ref[...] = x1
    ms = jnp.mean(x1 * x1, axis=-1, keepdims=True)
    hf = (x1 * lax.rsqrt(ms + EPS)) * gf_ref[...]
    _store_token_rows(hf_ref, hf)
    lg_ref[...] = jnp.dot(hf.astype(BF16), rh_ref[...], preferred_element_type=F32) + rb_ref[...]


def _merge(hm, ha, z, x2, wm, wa, wo, gf, rh, rb, *, bm, col_gm, col_ga):
    T, D = x2.shape
    const = lambda shape: pl.BlockSpec(shape, lambda i: (0,) * len(shape), pipeline_mode=pl.Buffered(1))
    return pl.pallas_call(
        _merge_body,
        grid=(T // bm,),
        in_specs=[
            pl.BlockSpec((bm, M_WIDTH), lambda i: (i, 0)),
            pl.BlockSpec((bm, A_WIDTH), lambda i: (i, 0)),
            pl.BlockSpec((bm, D), lambda i: (i, col_gm)),
            pl.BlockSpec((bm, D), lambda i: (i, col_ga)),
            pl.BlockSpec((bm, D), lambda i: (i, 0)),
            const((M_WIDTH, D)), const((A_WIDTH, D)), const((D, D)), const((1, D)),
            const((D, LANES)), const((1, LANES)),
        ],
        out_specs=[
            pl.BlockSpec((bm, D), lambda i: (i, 0)),
            pl.BlockSpec((bm * (D // (2 * LANES)), LANES), lambda i: (i, 0)),
            pl.BlockSpec((bm, LANES), lambda i: (i, 0)),
        ],
        out_shape=[
            jax.ShapeDtypeStruct((T, D), F32),
            jax.ShapeDtypeStruct((T * (D // (2 * LANES)), LANES), jnp.uint32),
            jax.ShapeDtypeStruct((T, LANES), F32),
        ],
        compiler_params=_params("parallel"),
        name="merge",
    )(hm, ha, z, z, x2, wm, wa, wo, gf, rh, rb)


def _moe_body(blk_e_ref, first_ref, nxt_ref, nused_ref, tok_ref, hf_hbm, wg_hbm, wu_hbm, wd_hbm, y_ref,
              xbuf, xs_ref, stage_g, stage_u, stage_d, wg_ref, wu_ref, wd_ref, sem, wsem, *, R, CB):
    i = pl.program_id(0)
    nused = nused_ref[0]
    sub = xs_ref.shape[1] // (2 * LANES)

    def weight_copies(e):
        return (pltpu.make_async_copy(wg_hbm.at[e], stage_g, wsem.at[0]),
                pltpu.make_async_copy(wu_hbm.at[e], stage_u, wsem.at[1]),
                pltpu.make_async_copy(wd_hbm.at[e], stage_d, wsem.at[2]))

    def cast(src, dst):
        def rows(r, carry):
            sl = pl.ds(pl.multiple_of(r * CB, CB), CB)
            dst[sl, :] = src[sl, :].astype(BF16)
            return carry
        lax.fori_loop(0, src.shape[0] // CB, rows, 0)

    def start_gather(blk, slot):
        base = blk * R

        for r in range(R):
            src = pl.multiple_of(tok_ref[base + r] * sub, sub)
            pltpu.make_async_copy(hf_hbm.at[pl.ds(src, sub)], xbuf.at[slot, pl.ds(r * sub, sub)],
                                  sem.at[slot]).start(priority=r % 2)

    def wait_gather(slot):
        pltpu.make_async_copy(xbuf.at[slot], xbuf.at[slot], sem.at[slot]).wait()

    nslots = xbuf.shape[0]
    ahead = nslots - 1
    last_blk = pl.num_programs(0) - 1
    slot = i % nslots

    @pl.when(i == 0)
    def _():
        for a in range(ahead):
            start_gather(a, a)
        for c in weight_copies(blk_e_ref[0]):
            c.start(priority=1)

    @pl.when(i < nused)
    def _():
        @pl.when(first_ref[i] == 1)
        def _():
            for c in weight_copies(blk_e_ref[i]):
                c.wait()
            cast(stage_g, wg_ref)
            cast(stage_u, wu_ref)
            cast(stage_d, wd_ref)

            @pl.when(nxt_ref[i] >= 0)
            def _():
                for c in weight_copies(nxt_ref[i]):
                    c.start(priority=1)

        wait_gather(slot)
        for s in range(sub):
            lo, hi = _load_token_rows(xbuf.at[slot], R, sub, s)
            xs_ref[:, 2 * s * LANES:(2 * s + 1) * LANES] = lo.astype(BF16)
            xs_ref[:, (2 * s + 1) * LANES:(2 * s + 2) * LANES] = hi.astype(BF16)
        start_gather(jnp.minimum(i + ahead, last_blk), (i + ahead) % nslots)
        xv = xs_ref[...]
        g = jnp.dot(xv, wg_ref[...], preferred_element_type=F32)
        u = jnp.dot(xv, wu_ref[...], preferred_element_type=F32)
        hmid = ((g * _sigmoid(g)) * u).astype(BF16)
        _store_token_rows(y_ref, jnp.dot(hmid, wd_ref[...], preferred_element_type=F32))

        @pl.when(i == nused - 1)
        def _():
            for a in range(1, nslots):
                wait_gather((i + a) % nslots)


def _moe(blk_e, first, nxt, nused, row_tok, hf, wg, wu, wd, *, R):
    E, D, F = wg.shape
    sub = D // (2 * LANES)
    n_blocks = blk_e.shape[0]
    hbm = lambda: pl.BlockSpec(memory_space=pl.ANY)
    grid_spec = pltpu.PrefetchScalarGridSpec(
        num_scalar_prefetch=5,
        grid=(n_blocks,),
        in_specs=[hbm(), hbm(), hbm(), hbm()],
        out_specs=pl.BlockSpec((R * sub, LANES), lambda i, be, fi, nx, nu, tk: (jnp.minimum(i, nu[0] - 1), 0)),
        scratch_shapes=[
            pltpu.VMEM((MOE_GATHER_SLOTS, R * sub, LANES), jnp.uint32), pltpu.VMEM((R, D), BF16),
            pltpu.VMEM((D, F), F32), pltpu.VMEM((D, F), F32), pltpu.VMEM((F, D), F32),
            pltpu.VMEM((D, F), BF16), pltpu.VMEM((D, F), BF16), pltpu.VMEM((F, D), BF16),
            pltpu.SemaphoreType.DMA((MOE_GATHER_SLOTS,)), pltpu.SemaphoreType.DMA((3,)),
        ],
    )
    return pl.pallas_call(
        functools.partial(_moe_body, R=R, CB=128),
        grid_spec=grid_spec,
        out_shape=jax.ShapeDtypeStruct((n_blocks * R * sub, LANES), jnp.uint32),
        compiler_params=_params("arbitrary"),
        name="moe",
    )(blk_e, first, nxt, nused, row_tok, hf, wg, wu, wd)


def _combine_body(dest_ref, x1_ref, w_ref, yr_hbm, out_ref, ybuf, sem, *, R):
    i = pl.program_id(0)
    nsteps = pl.num_programs(0)

    sub = out_ref.shape[1] // (2 * LANES)

    def start_gather(step, slot):
        base = step * (R * TOP_K)

        def row(r, carry):
            dst = pl.multiple_of(r * sub, sub)
            for k in range(TOP_K):
                src = pl.multiple_of(dest_ref[base + r * TOP_K + k] * sub, sub)
                pltpu.make_async_copy(yr_hbm.at[pl.ds(src, sub)], ybuf.at[slot, k, pl.ds(dst, sub)],
                                      sem.at[slot]).start(priority=k % 2)
            return carry
        lax.fori_loop(0, R, row, 0, unroll=4)

    slot = i % 2

    @pl.when(i == 0)
    def _():
        start_gather(0, 0)

    pltpu.make_async_copy(ybuf.at[slot], ybuf.at[slot], sem.at[slot]).wait()

    @pl.when(i + 1 < nsteps)
    def _():
        start_gather(i + 1, 1 - slot)

    w = w_ref[...]
    for s in range(sub):
        halves0 = _load_token_rows(ybuf.at[slot, 0], R, sub, s)
        halves1 = _load_token_rows(ybuf.at[slot, 1], R, sub, s)
        for half in range(2):
            cols = slice((2 * s + half) * LANES, (2 * s + half + 1) * LANES)
            out_ref[:, cols] = x1_ref[:, cols] + (halves0[half] * w[:, 0:1] + halves1[half] * w[:, 1:2])


def _combine(dest, x1, gate_w, yr, *, R):
    T, D = x1.shape
    grid_spec = pltpu.PrefetchScalarGridSpec(
        num_scalar_prefetch=1,
        grid=(T // R,),
        in_specs=[
            pl.BlockSpec((R, D), lambda i, d: (i, 0)),
            pl.BlockSpec((R, TOP_K), lambda i, d: (i, 0)),
            pl.BlockSpec(memory_space=pl.ANY),
        ],
        out_specs=pl.BlockSpec((R, D), lambda i, d: (i, 0)),
        scratch_shapes=[pltpu.VMEM((2, TOP_K, R * (D // (2 * LANES)), LANES), jnp.uint32),
                        pltpu.SemaphoreType.DMA((2,))],
    )
    return pl.pallas_call(
        functools.partial(_combine_body, R=R),
        grid_spec=grid_spec,
        out_shape=jax.ShapeDtypeStruct((T, D), F32),
        compiler_params=_params("arbitrary"),
        name="combine",
    )(dest, x1, gate_w, yr)


def _route_body(lg_ref, tri_ref, ids_ref, gate_ref, cnt_ref):
    @pl.when(pl.program_id(0) == 0)
    def _():
        cnt_ref[...] = jnp.zeros_like(cnt_ref)

    lg = lg_ref[...]
    shape = lg.shape
    lane = lax.broadcasted_iota(jnp.int32, shape, 1)
    big = jnp.int32(LANES)

    def softmax_masked(mask):
        v = jnp.where(mask, lg, -jnp.inf)
        u = jnp.exp(v - jnp.max(v, axis=1, keepdims=True))
        return jnp.where(mask, u / jnp.sum(u, axis=1, keepdims=True), -1.0)

    def top1(p):
        best = jnp.max(p, axis=1, keepdims=True)
        idx = jnp.min(jnp.where(p == best, lane, big), axis=1, keepdims=True)
        return best, idx

    g_p, g_lane = top1(softmax_masked(lane < N_GROUPS))
    grp_of_lane = lax.shift_right_arithmetic(lane - N_GROUPS, jnp.int32(3))
    in_grp = jnp.logical_and(lane >= N_GROUPS, grp_of_lane == g_lane)
    in_grp = jnp.logical_and(in_grp, lane < N_GROUPS + N_EXPERTS)
    pe = softmax_masked(in_grp)
    p1, l1 = top1(pe)
    p2, l2 = top1(jnp.where(lane == l1, -1.0, pe))
    tot = p1 + p2
    gate1 = g_p * (p1 / tot)
    gate2 = g_p * (p2 / tot)

    hot1 = lane == l1
    hot2 = lane == l2
    hot = jnp.logical_or(hot1, hot2)
    before = jnp.dot(tri_ref[...], jnp.where(hot, 1.0, 0.0).astype(BF16), preferred_element_type=F32) + cnt_ref[...]
    rank1 = jnp.sum(jnp.where(hot1, before, 0.0), axis=1, keepdims=True)
    rank2 = jnp.sum(jnp.where(hot2, before, 0.0), axis=1, keepdims=True)
    cnt_ref[...] = cnt_ref[...] + jnp.sum(jnp.where(hot, 1.0, 0.0), axis=0, keepdims=True)

    ids = jnp.where(lane == 0, l1 - N_GROUPS, jnp.where(lane == 1, l2 - N_GROUPS, 0))
    ids = jnp.where(lane == 2, rank1.astype(jnp.int32), jnp.where(lane == 3, rank2.astype(jnp.int32), ids))
    ids_ref[...] = ids
    gate_ref[...] = jnp.where(lane == 0, gate1, jnp.where(lane == 1, gate2, 0.0))


def _route_tokens(logits, *, tb):
    T = logits.shape[0]
    r = jnp.arange(tb)
    tri = (r[None, :] < r[:, None]).astype(BF16)
    return pl.pallas_call(
        _route_body,
        grid=(T // tb,),
        in_specs=[
            pl.BlockSpec((tb, LANES), lambda i: (i, 0)),
            pl.BlockSpec((tb, tb), lambda i: (0, 0)),
        ],
        out_specs=[
            pl.BlockSpec((tb, LANES), lambda i: (i, 0)),
            pl.BlockSpec((tb, LANES), lambda i: (i, 0)),
            pl.BlockSpec((1, LANES), lambda i: (0, 0)),
        ],
        out_shape=[
            jax.ShapeDtypeStruct((T, LANES), jnp.int32),
            jax.ShapeDtypeStruct((T, LANES), F32),
            jax.ShapeDtypeStruct((1, LANES), F32),
        ],
        compiler_params=_params("arbitrary"),
        name="route",
    )(logits, tri)


def _route(logits, R):
    T = logits.shape[0]
    ids, gates, cnt = _route_tokens(logits, tb=512 if T % 512 == 0 else T)
    gate = gates[:, 0:TOP_K]
    M = T * TOP_K
    eid_f = ids[:, 0:TOP_K].reshape(M)
    rank = ids[:, TOP_K:2 * TOP_K].reshape(M)
    counts = cnt[0, N_GROUPS:N_GROUPS + N_EXPERTS].astype(jnp.int32)
    padded = (counts + R - 1) // R * R
    pend = jnp.cumsum(padded)
    pstart = pend - padded
    dest = (pstart[eid_f] + rank).astype(jnp.int32)
    n_blocks = -(-M // R) + N_EXPERTS
    tok_f = jnp.arange(M, dtype=jnp.int32) // TOP_K
    row_tok = jnp.zeros((n_blocks * R,), jnp.int32).at[dest].set(
        tok_f, unique_indices=True, mode="promise_in_bounds")
    blk_start = jnp.arange(n_blocks, dtype=jnp.int32) * R
    blk_e = jnp.sum((pend[None, :] <= blk_start[:, None]).astype(jnp.int32), axis=1)
    nused = (pend[-1] // R).astype(jnp.int32)
    last_e = blk_e[jnp.maximum(nused - 1, 0)]
    blk_e = jnp.where(jnp.arange(n_blocks) < nused, blk_e, last_e)
    blk_e = jnp.minimum(blk_e, N_EXPERTS - 1).astype(jnp.int32)
    e_idx = jnp.arange(N_EXPERTS, dtype=jnp.int32)
    cand = jnp.where(counts > 0, e_idx, N_EXPERTS)
    sfx = lax.cummin(cand[::-1])[::-1]
    nxt_of_e = jnp.concatenate([sfx[1:], jnp.full((1,), N_EXPERTS, jnp.int32)])
    nxt_of_e = jnp.where(nxt_of_e < N_EXPERTS, nxt_of_e, -1)
    nxt = nxt_of_e[blk_e].astype(jnp.int32)
    changed = jnp.concatenate([jnp.ones((1,), bool), blk_e[1:] != blk_e[:-1]])
    first = jnp.logical_and(changed, jnp.arange(n_blocks) < nused).astype(jnp.int32)
    return gate, dest, row_tok, blk_e, first, nxt, nused.reshape(1)


def _rope_tables(S, gain, scale):
    half = A_HEAD_DIM // 2
    freqs = ROPE_THETA ** (-jnp.arange(half, dtype=F32) / half)
    ang = jnp.arange(S, dtype=F32)[:, None] * freqs[None, :]
    cos = jnp.cos(ang)
    sin = jnp.sin(ang)
    g1, g2 = gain[:half], gain[half:]
    a_head = jnp.concatenate([cos * g1, cos * g2], axis=1)
    b_head = jnp.concatenate([-sin * g2, sin * g1], axis=1)
    reps = LANES // A_HEAD_DIM
    return jnp.tile(a_head, (1, reps)) * scale, jnp.tile(b_head, (1, reps)) * scale


def kernel(x, g_mix, w_in, conv_qk, b_igate, b_fgate, g_mlstm, g_q, g_k, sinks, w_proj_m, w_proj_a, w_out,
           g_ffn, w_group, b_group, w_expert, b_expert, w_gate, w_up, w_down):
    B, S, D = x.shape
    T = B * S
    depth = g_mix.shape[0]
    xf = x.reshape(T, D)

    sizes = (M_WIDTH, M_WIDTH, M_WIDTH, M_WIDTH, M_HEADS, M_HEADS, A_WIDTH, A_KV_WIDTH, A_KV_WIDTH, D, D)
    offs = [0]
    for s_ in sizes:
        offs.append(offs[-1] + s_)
    seg = lambda w, idx: w[:, offs[idx]:offs[idx + 1]]
    order = (9, 10, 0, 1, 2, 3, 6, 7, 8)
    new_off = {}
    acc = 0
    for idx in order:
        new_off[idx] = acc
        acc += sizes[idx]
    dk = M_HEAD_DIM

    lane128 = jnp.arange(LANES)
    bd = (lane128[:, None] // A_HEAD_DIM == lane128[None, :] // A_HEAD_DIM).astype(BF16)
    gw = A_GROUP * A_HEAD_DIM
    r_idx = jnp.arange(gw)
    rep = jnp.stack([(r_idx[:, None] == (j * A_HEAD_DIM + r_idx[None, :] % A_HEAD_DIM)).astype(BF16)
                     for j in range(A_KV_HEADS)])
    ones_bd = (jnp.arange(A_GROUP * WINDOW)[:, None] // WINDOW == r_idx[None, :] // A_HEAD_DIM).astype(BF16)

    for l in range(depth):
        w16 = w_in[l].astype(BF16)
        w_rep = jnp.concatenate([seg(w16, idx) for idx in order], axis=1)
        w_gates = jnp.concatenate([seg(w16, 4), seg(w16, 5)], axis=1)
        w_gates = jnp.pad(w_gates, ((0, 0), (0, LANES - 2 * M_HEADS)))
        z, zg = _in_proj(xf, g_mix[l][None, :], w_rep, w_gates, bm=1024 if T % 1024 == 0 else T,
                         bn=w_rep.shape[1] // 4)

        gate_bias = jnp.pad(jnp.concatenate([b_igate[l], b_fgate[l]]), (0, LANES - 2 * M_HEADS))[None, :]
        hm = _mlstm(z, zg, conv_qk[l], gate_bias, g_mlstm[l][:, None, :], B=B, S=S,
                    col_q=new_off[0] // dk, col_k=new_off[1] // dk, col_v=new_off[2] // dk, col_o=new_off[3] // dk)

        qa, qb = _rope_tables(S, g_q[l], A_HEAD_DIM ** -0.5)
        ka, kb = _rope_tables(S, g_k[l], 1.0)
        ha = _swa(z, sinks[l], qa, qb, ka, kb, bd, rep, ones_bd, B=B, S=S,
                  col_q=new_off[6] // A_WIDTH, col_k=new_off[7] // A_KV_WIDTH, col_v=new_off[8] // A_KV_WIDTH)

        w_router = jnp.pad(jnp.concatenate([w_group[l], w_expert[l]], axis=1),
                           ((0, 0), (0, LANES - N_GROUPS - N_EXPERTS)))
        r_b = jnp.pad(jnp.concatenate([b_group[l], b_expert[l]]), (0, LANES - N_GROUPS - N_EXPERTS))[None, :]
        x1, hf, logits = _merge(hm, ha, z, xf, w_proj_m[l].astype(BF16), w_proj_a[l].astype(BF16),
                                w_out[l].astype(BF16), g_ffn[l][None, :], w_router.astype(BF16), r_b,
                                bm=256, col_gm=new_off[9] // D, col_ga=new_off[10] // D)

        gate, dest, row_tok, blk_e, first, nxt, nused = _route(logits, MOE_ROWS)
        yr = _moe(blk_e, first, nxt, nused, row_tok, hf, w_gate[l], w_up[l], w_down[l], R=MOE_ROWS)
        xf = _combine(dest, x1, gate, yr, R=COMBINE_ROWS)
    return xf.reshape(B, S, D)
```

```python
import functools

import jax
import jax.numpy as jnp
from jax import lax
from jax.experimental import pallas as pl
from jax.experimental.pallas import tpu as pltpu

F32 = jnp.float32
BF16 = jnp.bfloat16
EPS = 1e-6
LANES = 128
VMEM_LIMIT = 56 * 1024 * 1024

M_HEADS = 4
M_HEAD_DIM = 256
M_WIDTH = M_HEADS * M_HEAD_DIM
CONV_WIDTH = 4
A_HEADS = 16
A_KV_HEADS = 4
A_GROUP = A_HEADS // A_KV_HEADS
A_HEAD_DIM = 64
A_WIDTH = A_HEADS * A_HEAD_DIM
A_KV_WIDTH = A_KV_HEADS * A_HEAD_DIM
WINDOW = 128
ROPE_THETA = 10000.0
N_GROUPS = 8
EXPERTS_PER_GROUP = 8
N_EXPERTS = N_GROUPS * EXPERTS_PER_GROUP
TOP_K = 2

MLSTM_CHUNK = 128
MLSTM_TIME_BLOCK = 512
MOE_ROWS = 256
MOE_GATHER_SLOTS = 3
MOE_OUT_SLOTS = 3
COMBINE_ROWS = 256


def _sigmoid(v):
    return 1.0 / (1.0 + jnp.exp(-v))


def _params(*sem):
    return pltpu.CompilerParams(dimension_semantics=sem, vmem_limit_bytes=VMEM_LIMIT)


def _pack_bf16_pair(lo, hi):
    lo_b = lax.bitcast_convert_type(lo.astype(BF16).astype(F32), jnp.uint32)
    hi_b = lax.bitcast_convert_type(hi.astype(BF16).astype(F32), jnp.uint32)
    return lax.shift_right_logical(lo_b, jnp.uint32(16)) | hi_b


def _unpack_bf16_pair(word):
    lo = lax.bitcast_convert_type(lax.shift_left(word, jnp.uint32(16)), F32)
    hi = lax.bitcast_convert_type(word & jnp.uint32(0xFFFF0000), F32)
    return lo, hi


def _store_token_rows(ref, val):
    n, d = val.shape
    sub = d // (2 * LANES)
    for s in range(sub):
        c = 2 * s * LANES
        ref[pl.ds(s, n, stride=sub), :] = _pack_bf16_pair(val[:, c:c + LANES], val[:, c + LANES:c + 2 * LANES])


def _load_token_rows(ref, n, sub, s):
    return _unpack_bf16_pair(ref[pl.ds(s, n, stride=sub), :])


def _inproj_body(x_ref, g_ref, w_ref, wgate_ref, z_ref, zg_ref, hn_ref, *, sub):
    bm = x_ref.shape[0]

    @pl.when(pl.program_id(1) == 0)
    def _():
        def rows(r, carry):
            sl = pl.ds(pl.multiple_of(r * sub, sub), sub)
            xv = x_ref[sl, :]
            ms = jnp.mean(xv * xv, axis=-1, keepdims=True)
            hn_ref[sl, :] = ((xv * lax.rsqrt(ms + EPS)) * g_ref[...]).astype(BF16)
            return carry
        lax.fori_loop(0, bm // sub, rows, 0)
        zg_ref[...] = jnp.dot(hn_ref[...], wgate_ref[...], preferred_element_type=F32)

    z_ref[...] = jnp.dot(hn_ref[...], w_ref[...], preferred_element_type=F32).astype(BF16)


def _in_proj(x2, g, w, wgate, *, bm, bn):
    T, D = x2.shape
    N = w.shape[1]
    return pl.pallas_call(
        functools.partial(_inproj_body, sub=128),
        grid=(T // bm, N // bn),
        in_specs=[
            pl.BlockSpec((bm, D), lambda i, j: (i, 0)),
            pl.BlockSpec((1, D), lambda i, j: (0, 0)),
            pl.BlockSpec((D, bn), lambda i, j: (0, j)),
            pl.BlockSpec((D, LANES), lambda i, j: (0, 0)),
        ],
        out_specs=[
            pl.BlockSpec((bm, bn), lambda i, j: (i, j)),
            pl.BlockSpec((bm, LANES), lambda i, j: (i, 0)),
        ],
        out_shape=[
            jax.ShapeDtypeStruct((T, N), BF16),
            jax.ShapeDtypeStruct((T, LANES), F32),
        ],
        scratch_shapes=[pltpu.VMEM((bm, D), BF16)],
        compiler_params=_params("parallel", "arbitrary"),
        name="in_proj",
    )(x2, g, w, wgate)


def _mlstm_body(q_ref, k_ref, v_ref, o_ref, zg_ref, cq_ref, ck_ref, bias_ref, gn_ref, out_ref,
                qs_ref, ks_ref, colli_ref, colb_ref, rowli_ref, rowb_ref, cli_ref, cb_ref, rli_ref, rb_ref,
                c_ref, n_ref, m_ref, hq_ref, hk_ref, *, L, CB, HP):
    S = q_ref.shape[0]
    dk = q_ref.shape[1] // HP

    @pl.when(pl.program_id(1) == 0)
    def _():
        c_ref[...] = jnp.zeros_like(c_ref)
        n_ref[...] = jnp.zeros_like(n_ref)
        m_ref[...] = jnp.zeros_like(m_ref)
        hq_ref[...] = jnp.zeros_like(hq_ref)
        hk_ref[...] = jnp.zeros_like(hk_ref)

    G = zg_ref[...] + bias_ref[...]
    lf = jnp.minimum(G, 0.0) - jnp.log1p(jnp.exp(-jnp.abs(G)))
    pos = lax.broadcasted_iota(jnp.int32, (S, LANES), 0) % L
    bc = lf
    sh = 1
    while sh < L:
        bc = bc + jnp.where(pos >= sh, pltpu.roll(bc, sh, axis=0), 0.0)
        sh *= 2
    colli_ref[...] = G
    colb_ref[...] = bc
    for p in range(S // LANES):
        sl = slice(p * LANES, (p + 1) * LANES)
        rowli_ref[:, sl] = G[sl, :].T[0:8, :]
        rowb_ref[:, sl] = bc[sl, :].T[0:8, :]

    lane = lax.broadcasted_iota(jnp.int32, (S, LANES), 1)
    sub = lax.broadcasted_iota(jnp.int32, (8, S), 0)
    for j in range(HP):
        cli_ref[j] = jnp.sum(jnp.where(lane == j, colli_ref[...], 0.0), axis=1, keepdims=True)
        cb_ref[j] = jnp.sum(jnp.where(lane == j + HP, colb_ref[...], 0.0), axis=1, keepdims=True)
        rli_ref[j] = jnp.sum(jnp.where(sub == j, rowli_ref[...], 0.0), axis=0, keepdims=True)
        rb_ref[j] = jnp.sum(jnp.where(sub == j + HP, rowb_ref[...], 0.0), axis=0, keepdims=True)

    def conv_silu(src_ref, w_ref, dst_ref, halo_ref, scale):
        w = w_ref[...]

        def taps(xs, wj):
            y = xs[3] * wj[0:1, :]
            y = y + xs[2] * wj[1:2, :]
            y = y + xs[1] * wj[2:3, :]
            y = y + xs[0] * wj[3:4, :]
            return ((y * _sigmoid(y)) * scale).astype(BF16)

        for j in range(HP):
            cj = slice(j * dk, (j + 1) * dk)
            wj = w[:, cj]

            def chunk(r, carry, cj=cj, wj=wj):
                r0 = pl.multiple_of(r * CB, CB)
                cur = src_ref[pl.ds(r0, CB), cj].astype(F32)
                dst_ref[pl.ds(r0, CB), cj] = taps([cur] + [pltpu.roll(cur, d, axis=0) for d in (1, 2, 3)], wj)
                p0 = pl.multiple_of(jnp.maximum(r0 - 16, 0), 16)
                prev = jnp.where(r > 0, src_ref[pl.ds(p0, 16), cj], halo_ref[:, cj]).astype(F32)[8:16, :]
                both = jnp.concatenate([prev, cur[0:16, :]], axis=0)
                dst_ref[pl.ds(r0, 16), cj] = taps(
                    [both[8:24, :]] + [pltpu.roll(both, d, axis=0)[8:24, :] for d in (1, 2, 3)], wj)
                return carry
            lax.fori_loop(0, S // CB, chunk, 0)
        halo_ref[...] = src_ref[S - 16:S, :]

    conv_silu(q_ref, cq_ref, qs_ref, hq_ref, 1.0)
    conv_silu(k_ref, ck_ref, ks_ref, hk_ref, float(dk) ** -0.5)

    t_idx = lax.broadcasted_iota(jnp.int32, (L, L), 0)
    s_idx = lax.broadcasted_iota(jnp.int32, (L, L), 1)
    causal = s_idx <= t_idx

    heads = range(HP)
    cols = [slice(j * dk, (j + 1) * dk) for j in heads]
    dims_nt = (((1,), (1,)), ((), ()))

    def chunk(c, carry):
        rows = pl.ds(pl.multiple_of(c * L, L), L)
        qc = [qs_ref[rows, cols[j]] for j in heads]
        kc = [ks_ref[rows, cols[j]] for j in heads]
        vc = [v_ref[rows, cols[j]] for j in heads]
        b_col = [cb_ref[j, rows, :] for j in heads]
        li_col = [cli_ref[j, rows, :] for j in heads]
        li_row = [rli_ref[j, :, rows] for j in heads]
        b_row = [rb_ref[j, :, rows] for j in heads]
        m = [m_ref[j] for j in heads]
        a = [b_col[j] + m[j] for j in heads]
        D = [jnp.where(causal, b_col[j] - b_row[j] + li_row[j], -jnp.inf) for j in heads]
        m_t = [jnp.maximum(a[j], jnp.max(D[j], axis=1, keepdims=True)) for j in heads]
        w_inter = [jnp.exp(a[j] - m_t[j]) for j in heads]
        s_qk = [lax.dot_general(qc[j], kc[j], dims_nt, preferred_element_type=F32) for j in heads]
        P = [s_qk[j] * jnp.exp(D[j] - m_t[j]) for j in heads]
        inter = [jnp.dot(qc[j], c_ref[j].astype(BF16), preferred_element_type=F32) for j in heads]
        intra = [jnp.dot(P[j].astype(BF16), vc[j], preferred_element_type=F32) for j in heads]
        num = [w_inter[j] * inter[j] + intra[j] for j in heads]
        qn = [w_inter[j] * jnp.sum(qc[j].astype(F32) * n_ref[j], axis=1, keepdims=True)
              + jnp.sum(P[j], axis=1, keepdims=True) for j in heads]
        den = [jnp.maximum(jnp.abs(qn[j]), jnp.exp(-m_t[j])) for j in heads]
        hh = [num[j] / den[j] for j in heads]
        ms = [jnp.mean(hh[j] * hh[j], axis=1, keepdims=True) for j in heads]
        hn = [(hh[j] * lax.rsqrt(ms[j] + EPS)) * gn_ref[j] for j in heads]
        for j in heads:
            out_ref[rows, cols[j]] = (hn[j] * _sigmoid(o_ref[rows, cols[j]].astype(F32))).astype(BF16)
        bL = [b_row[j][:, L - 1:L] for j in heads]
        g_col = [bL[j] - b_col[j] + li_col[j] for j in heads]
        m_new = [jnp.maximum(bL[j] + m[j], jnp.max(g_col[j], axis=0, keepdims=True)) for j in heads]
        decay = [jnp.exp(bL[j] + m[j] - m_new[j]) for j in heads]
        kw = [kc[j].astype(F32) * jnp.exp(g_col[j] - m_new[j]) for j in heads]
        upd = [jnp.dot(kw[j].T.astype(BF16), vc[j], preferred_element_type=F32) for j in heads]
        for j in heads:
            c_ref[j] = decay[j] * c_ref[j] + upd[j]
            n_ref[j] = decay[j] * n_ref[j] + jnp.sum(kw[j], axis=0, keepdims=True)
            m_ref[j] = m_new[j]
        return carry
    lax.fori_loop(0, S // L, chunk, 0)


def _mlstm(z, zg, conv_qk, gate_bias, g_mlstm3, *, B, S, col_q, col_k, col_v, col_o):
    T = B * S
    dk = M_HEAD_DIM
    L = MLSTM_CHUNK
    HP = M_HEADS
    wd = HP * dk
    TS = min(MLSTM_TIME_BLOCK, S)
    NT = S // TS
    zspec = lambda col: pl.BlockSpec((TS, wd), lambda b, t: (b * NT + t, col // HP))
    return pl.pallas_call(
        functools.partial(_mlstm_body, L=L, CB=min(128, TS), HP=HP),
        grid=(B, NT),
        in_specs=[
            zspec(col_q), zspec(col_k), zspec(col_v), zspec(col_o),
            pl.BlockSpec((TS, LANES), lambda b, t: (b * NT + t, 0)),
            pl.BlockSpec((CONV_WIDTH, wd), lambda b, t: (0, 0)),
            pl.BlockSpec((CONV_WIDTH, wd), lambda b, t: (0, 1)),
            pl.BlockSpec((1, LANES), lambda b, t: (0, 0)),
            pl.BlockSpec((HP, 1, dk), lambda b, t: (0, 0, 0)),
        ],
        out_specs=pl.BlockSpec((TS, wd), lambda b, t: (b * NT + t, 0)),
        out_shape=jax.ShapeDtypeStruct((T, M_WIDTH), BF16),
        scratch_shapes=[
            pltpu.VMEM((TS, wd), BF16), pltpu.VMEM((TS, wd), BF16),
            pltpu.VMEM((TS, LANES), F32), pltpu.VMEM((TS, LANES), F32),
            pltpu.VMEM((8, TS), F32), pltpu.VMEM((8, TS), F32),
            pltpu.VMEM((HP, TS, 1), F32), pltpu.VMEM((HP, TS, 1), F32),
            pltpu.VMEM((HP, 1, TS), F32), pltpu.VMEM((HP, 1, TS), F32),
            pltpu.VMEM((HP, dk, dk), F32), pltpu.VMEM((HP, 1, dk), F32), pltpu.VMEM((HP, 1, 1), F32),
            pltpu.VMEM((16, wd), BF16), pltpu.VMEM((16, wd), BF16),
        ],
        compiler_params=_params("parallel", "arbitrary"),
        name="mlstm",
    )(z, z, z, z, zg, conv_qk, conv_qk, gate_bias, g_mlstm3)


def _swa_body(sink_ref, q_ref, k_ref, v_ref, qa_ref, qb_ref, ka_ref, kb_ref, bd_ref, rep_ref, ones_ref, out_ref,
              kbd_ref, vbd_ref):
    n = pl.program_id(1)
    W = WINDOW
    hd = A_HEAD_DIM
    gw = A_GROUP * hd

    def norm_rope(x, ta, tb):
        x2 = x * x
        x2h = x2.astype(BF16)
        x2l = (x2 - x2h.astype(F32)).astype(BF16)
        ss = (jnp.dot(x2h, bd_ref[...], preferred_element_type=F32)
              + jnp.dot(x2l, bd_ref[...], preferred_element_type=F32))
        r = lax.rsqrt(ss * (1.0 / hd) + EPS)
        ln = lax.broadcasted_iota(jnp.int32, x.shape, 1) % hd
        swapped = jnp.where(ln < hd // 2, pltpu.roll(x, LANES - hd // 2, axis=1), pltpu.roll(x, hd // 2, axis=1))
        return r * (x * ta + swapped * tb)

    lane_head = lax.broadcasted_iota(jnp.int32, (W, gw), 1) // hd
    t_idx = lax.broadcasted_iota(jnp.int32, (W, W), 0)
    k_idx = lax.broadcasted_iota(jnp.int32, (W, W), 1)
    mask_cur = k_idx <= t_idx
    neg = -jnp.inf

    @pl.when(n == 0)
    def _():
        kbd_ref[1] = jnp.zeros(kbd_ref.shape[1:], BF16)
        vbd_ref[1] = jnp.zeros(vbd_ref.shape[1:], BF16)

    def prepare(half, cur_slot):
        rows = slice(half * W, (half + 1) * W)
        qa = qa_ref[rows, :]
        qb = qb_ref[rows, :]
        qp = jnp.concatenate(
            [norm_rope(q_ref[rows, t * LANES:(t + 1) * LANES].astype(F32), qa, qb) for t in range(A_WIDTH // LANES)],
            axis=1).astype(BF16)
        ka = ka_ref[rows, :]
        kb = kb_ref[rows, :]
        kp = jnp.concatenate(
            [norm_rope(k_ref[rows, t * LANES:(t + 1) * LANES].astype(F32), ka, kb)
             for t in range(A_KV_WIDTH // LANES)], axis=1).astype(BF16)
        vv = v_ref[rows, :]
        for j in range(A_KV_HEADS):
            krep = jnp.dot(kp, rep_ref[j], preferred_element_type=F32).astype(BF16)
            vrep = jnp.dot(vv, rep_ref[j], preferred_element_type=F32).astype(BF16)
            for i in range(A_GROUP):
                kbd_ref[cur_slot, j, i * W:(i + 1) * W, :] = jnp.where(lane_head == i, krep, jnp.zeros_like(krep))
                vbd_ref[cur_slot, j, i * W:(i + 1) * W, :] = jnp.where(lane_head == i, vrep, jnp.zeros_like(vrep))
        return qp

    upper = k_idx > t_idx
    blocks = ((0, 1, 0, jnp.logical_and(upper, n > 0)), (1, 0, 1, upper))

    dims_nt = (((1,), (1,)), ((), ()))
    for halves in ((blocks[0],), (blocks[1],)):
        qps = {h: prepare(h, cur) for h, _, cur, _ in halves}
        groups = tuple(range(A_KV_HEADS))
        gsel = [(h, j) for h, _, _, _ in halves for j in groups]
        pairs = [(h, j, i) for h, j in gsel for i in range(A_GROUP)]
        prev_of = {h: p for h, p, _, _ in halves}
        cur_of = {h: c for h, _, c, _ in halves}
        mprev_of = {h: m for h, _, _, m in halves}
        qg = {(h, j): qps[h][:, j * gw:(j + 1) * gw] for h, j in gsel}
        s_prev = {(h, j): lax.dot_general(qg[(h, j)], kbd_ref[prev_of[h], j], dims_nt, preferred_element_type=F32)
                  for h, j in gsel}
        s_cur = {(h, j): lax.dot_general(qg[(h, j)], kbd_ref[cur_of[h], j], dims_nt, preferred_element_type=F32)
                 for h, j in gsel}
        sp = {(h, j, i): jnp.where(mprev_of[h], s_prev[(h, j)][:, i * W:(i + 1) * W], neg) for h, j, i in pairs}
        sc = {(h, j, i): jnp.where(mask_cur, s_cur[(h, j)][:, i * W:(i + 1) * W], neg) for h, j, i in pairs}
        sink = {(h, j, i): sink_ref[j * A_GROUP + i] for h, j, i in pairs}
        mx = {k: jnp.maximum(jnp.max(jnp.maximum(sp[k], sc[k]), axis=1, keepdims=True), sink[k]) for k in pairs}
        pp = {k: jnp.exp(sp[k] - mx[k]).astype(BF16) for k in pairs}
        pc = {k: jnp.exp(sc[k] - mx[k]).astype(BF16) for k in pairs}
        es = {k: jnp.exp(sink[k] - mx[k]) for k in pairs}
        p_prev = {(h, j): jnp.concatenate([pp[(h, j, i)] for i in range(A_GROUP)], axis=1) for h, j in gsel}
        p_cur = {(h, j): jnp.concatenate([pc[(h, j, i)] for i in range(A_GROUP)], axis=1) for h, j in gsel}
        o = {(h, j): jnp.dot(p_prev[(h, j)], vbd_ref[prev_of[h], j], preferred_element_type=F32)
             + jnp.dot(p_cur[(h, j)], vbd_ref[cur_of[h], j], preferred_element_type=F32) for h, j in gsel}
        den = {(h, j): jnp.dot(p_prev[(h, j)], ones_ref[...], preferred_element_type=F32)
               + jnp.dot(p_cur[(h, j)], ones_ref[...], preferred_element_type=F32) for h, j in gsel}
        for h, j in gsel:
            sink_term = jnp.zeros((W, gw), F32)
            for i in range(A_GROUP):
                sink_term = jnp.where(lane_head == i, es[(h, j, i)], sink_term)
            out_ref[h * W:(h + 1) * W, j * gw:(j + 1) * gw] = (o[(h, j)] / (den[(h, j)] + sink_term)).astype(BF16)


def _swa(z, sinks, qa, qb, ka, kb, bd, rep, ones_bd, *, B, S, col_q, col_k, col_v):
    T = B * S
    W2 = 2 * WINDOW
    NB = S // W2
    gw = A_GROUP * A_HEAD_DIM
    tab = lambda: pl.BlockSpec((W2, LANES), lambda b, n: (n, 0))
    return pl.pallas_call(
        _swa_body,
        grid=(B, NB),
        in_specs=[
            pl.BlockSpec(memory_space=pltpu.SMEM),
            pl.BlockSpec((W2, A_WIDTH), lambda b, n: (b * NB + n, col_q)),
            pl.BlockSpec((W2, A_KV_WIDTH), lambda b, n: (b * NB + n, col_k)),
            pl.BlockSpec((W2, A_KV_WIDTH), lambda b, n: (b * NB + n, col_v)),
            tab(), tab(), tab(), tab(),
            pl.BlockSpec((LANES, LANES), lambda b, n: (0, 0)),
            pl.BlockSpec((A_KV_HEADS, gw, gw), lambda b, n: (0, 0, 0)),
            pl.BlockSpec((A_GROUP * WINDOW, gw), lambda b, n: (0, 0)),
        ],
        out_specs=pl.BlockSpec((W2, A_WIDTH), lambda b, n: (b * NB + n, 0)),
        out_shape=jax.ShapeDtypeStruct((T, A_WIDTH), BF16),
        scratch_shapes=[
            pltpu.VMEM((2, A_KV_HEADS, A_GROUP * WINDOW, gw), BF16),
            pltpu.VMEM((2, A_KV_HEADS, A_GROUP * WINDOW, gw), BF16),
        ],
        compiler_params=_params("parallel", "arbitrary"),
        name="swa",
    )(sinks, z, z, z, qa, qb, ka, kb, bd, rep, ones_bd)


def _merge_body(hm_ref, ha_ref, gm_ref, ga_ref, x_ref, wm_ref, wa_ref, wo_ref, gf_ref, rh_ref, rb_ref,
                x1_ref, hf_ref, lg_ref):
    pm = jnp.dot(hm_ref[...], wm_ref[...], preferred_element_type=F32)
    pa = jnp.dot(ha_ref[...], wa_ref[...], preferred_element_type=F32)
    mixed = _sigmoid(gm_ref[...].astype(F32)) * pm + _sigmoid(ga_ref[...].astype(F32)) * pa
    x1 = x_ref[...] + jnp.dot(mixed.astype(BF16), wo_ref[...], preferred_element_type=F32)
    x1_ref[...] = x1
    ms = jnp.mean(x1 * x1, axis=-1, keepdims=True)
    hf = (x1 * lax.rsqrt(ms + EPS)) * gf_ref[...]
    _store_token_rows(hf_ref, hf)
    lg_ref[...] = jnp.dot(hf.astype(BF16), rh_ref[...], preferred_element_type=F32) + rb_ref[...]


def _merge(hm, ha, z, x2, wm, wa, wo, gf, rh, rb, *, bm, col_gm, col_ga):
    T, D = x2.shape
    const = lambda shape: pl.BlockSpec(shape, lambda i: (0,) * len(shape), pipeline_mode=pl.Buffered(1))
    return pl.pallas_call(
        _merge_body,
        grid=(T // bm,),
        in_specs=[
            pl.BlockSpec((bm, M_WIDTH), lambda i: (i, 0)),
            pl.BlockSpec((bm, A_WIDTH), lambda i: (i, 0)),
            pl.BlockSpec((bm, D), lambda i: (i, col_gm)),
            pl.BlockSpec((bm, D), lambda i: (i, col_ga)),
            pl.BlockSpec((bm, D), lambda i: (i, 0)),
            const((M_WIDTH, D)), const((A_WIDTH, D)), const((D, D)), const((1, D)),
            const((D, LANES)), const((1, LANES)),
        ],
        out_specs=[
            pl.BlockSpec((bm, D), lambda i: (i, 0)),
            pl.BlockSpec((bm * (D // (2 * LANES)), LANES), lambda i: (i, 0)),
            pl.BlockSpec((bm, LANES), lambda i: (i, 0)),
        ],
        out_shape=[
            jax.ShapeDtypeStruct((T, D), F32),
            jax.ShapeDtypeStruct((T * (D // (2 * LANES)), LANES), jnp.uint32),
            jax.ShapeDtypeStruct((T, LANES), F32),
        ],
        compiler_params=_params("parallel"),
        name="merge",
    )(hm, ha, z, z, x2, wm, wa, wo, gf, rh, rb)


def _moe_body(blk_e_ref, first_ref, nxt_ref, nused_ref, tok_ref, hf_hbm, wg_hbm, wu_hbm, wd_hbm, y_hbm,
              xbuf, xs_ref, stage_g, stage_u, stage_d, wg_ref, wu_ref, wd_ref, ybuf, sem, wsem, ysem, *, R, CB):
    i = pl.program_id(0)
    nused = nused_ref[0]
    sub = xs_ref.shape[1] // (2 * LANES)

    nout = ybuf.shape[0]
    oslot = i % nout

    def out_copy(step, slot_):
        row0 = pl.multiple_of(step * (R * sub), R * sub)
        return pltpu.make_async_copy(ybuf.at[slot_], y_hbm.at[pl.ds(row0, R * sub)], ysem.at[slot_])

    def weight_copies(e):
        return (pltpu.make_async_copy(wg_hbm.at[e], stage_g, wsem.at[0]),
                pltpu.make_async_copy(wu_hbm.at[e], stage_u, wsem.at[1]),
                pltpu.make_async_copy(wd_hbm.at[e], stage_d, wsem.at[2]))

    def cast(src, dst):
        def rows(r, carry):
            sl = pl.ds(pl.multiple_of(r * CB, CB), CB)
            dst[sl, :] = src[sl, :].astype(BF16)
            return carry
        lax.fori_loop(0, src.shape[0] // CB, rows, 0)

    def start_gather(blk, slot):
        base = blk * R

        for r in range(R):
            src = pl.multiple_of(tok_ref[base + r] * sub, sub)
            pltpu.make_async_copy(hf_hbm.at[pl.ds(src, sub)], xbuf.at[slot, pl.ds(r * sub, sub)],
                                  sem.at[slot]).start(priority=r % 2)

    def wait_gather(slot):
        pltpu.make_async_copy(xbuf.at[slot], xbuf.at[slot], sem.at[slot]).wait()

    nslots = xbuf.shape[0]
    ahead = nslots - 1
    last_blk = pl.num_programs(0) - 1
    slot = i % nslots

    @pl.when(i == 0)
    def _():
        for a in range(ahead):
            start_gather(a, a)
        for c in weight_copies(blk_e_ref[0]):
            c.start(priority=1)

    @pl.when(i < nused)
    def _():
        @pl.when(first_ref[i] == 1)
        def _():
            for c in weight_copies(blk_e_ref[i]):
                c.wait()
            cast(stage_g, wg_ref)
            cast(stage_u, wu_ref)
            cast(stage_d, wd_ref)

            @pl.when(nxt_ref[i] >= 0)
            def _():
                for c in weight_copies(nxt_ref[i]):
                    c.start(priority=1)

        @pl.when(i >= nout)
        def _():
            out_copy(i - nout, oslot).wait()

        wait_gather(slot)
        for s in range(sub):
            lo, hi = _load_token_rows(xbuf.at[slot], R, sub, s)
            xs_ref[:, 2 * s * LANES:(2 * s + 1) * LANES] = lo.astype(BF16)
            xs_ref[:, (2 * s + 1) * LANES:(2 * s + 2) * LANES] = hi.astype(BF16)
        start_gather(jnp.minimum(i + ahead, last_blk), (i + ahead) % nslots)
        xv = xs_ref[...]
        g = jnp.dot(xv, wg_ref[...], preferred_element_type=F32)
        u = jnp.dot(xv, wu_ref[...], preferred_element_type=F32)
        hmid = ((g * _sigmoid(g)) * u).astype(BF16)
        _store_token_rows(ybuf.at[oslot], jnp.dot(hmid, wd_ref[...], preferred_element_type=F32))
        out_copy(i, oslot).start()

        @pl.when(i == nused - 1)
        def _():
            for a in range(1, nslots):
                wait_gather((i + a) % nslots)
            for a in range(nout):
                @pl.when(i - a >= 0)
                def _(a=a):
                    out_copy(i - a, (i - a) % nout).wait()


def _moe(blk_e, first, nxt, nused, row_tok, hf, wg, wu, wd, *, R):
    E, D, F = wg.shape
    sub = D // (2 * LANES)
    n_blocks = blk_e.shape[0]
    hbm = lambda: pl.BlockSpec(memory_space=pl.ANY)
    grid_spec = pltpu.PrefetchScalarGridSpec(
        num_scalar_prefetch=5,
        grid=(n_blocks,),
        in_specs=[hbm(), hbm(), hbm(), hbm()],
        out_specs=hbm(),
        scratch_shapes=[
            pltpu.VMEM((MOE_GATHER_SLOTS, R * sub, LANES), jnp.uint32), pltpu.VMEM((R, D), BF16),
            pltpu.VMEM((D, F), F32), pltpu.VMEM((D, F), F32), pltpu.VMEM((F, D), F32),
            pltpu.VMEM((D, F), BF16), pltpu.VMEM((D, F), BF16), pltpu.VMEM((F, D), BF16),
            pltpu.VMEM((MOE_OUT_SLOTS, R * sub, LANES), jnp.uint32),
            pltpu.SemaphoreType.DMA((MOE_GATHER_SLOTS,)), pltpu.SemaphoreType.DMA((3,)),
            pltpu.SemaphoreType.DMA((MOE_OUT_SLOTS,)),
        ],
    )
    return pl.pallas_call(
        functools.partial(_moe_body, R=R, CB=128),
        grid_spec=grid_spec,
        out_shape=jax.ShapeDtypeStruct((n_blocks * R * sub, LANES), jnp.uint32),
        compiler_params=_params("arbitrary"),
        name="moe",
    )(blk_e, first, nxt, nused, row_tok, hf, wg, wu, wd)


def _combine_body(dest_ref, x1_ref, w_ref, yr_hbm, out_ref, ybuf, sem, *, R):
    i = pl.program_id(0)
    nsteps = pl.num_programs(0)

    sub = out_ref.shape[1] // (2 * LANES)

    def start_gather(step, slot):
        base = step * (R * TOP_K)

        def row(r, carry):
            dst = pl.multiple_of(r * sub, sub)
            for k in range(TOP_K):
                src = pl.multiple_of(dest_ref[base + r * TOP_K + k] * sub, sub)
                pltpu.make_async_copy(yr_hbm.at[pl.ds(src, sub)], ybuf.at[slot, k, pl.ds(dst, sub)],
                                      sem.at[slot]).start(priority=k % 2)
            return carry
        lax.fori_loop(0, R, row, 0, unroll=4)

    slot = i % 2

    @pl.when(i == 0)
    def _():
        start_gather(0, 0)

    pltpu.make_async_copy(ybuf.at[slot], ybuf.at[slot], sem.at[slot]).wait()

    @pl.when(i + 1 < nsteps)
    def _():
        start_gather(i + 1, 1 - slot)

    w = w_ref[...]
    for s in range(sub):
        halves0 = _load_token_rows(ybuf.at[slot, 0], R, sub, s)
        halves1 = _load_token_rows(ybuf.at[slot, 1], R, sub, s)
        for half in range(2):
            cols = slice((2 * s + half) * LANES, (2 * s + half + 1) * LANES)
            out_ref[:, cols] = x1_ref[:, cols] + (halves0[half] * w[:, 0:1] + halves1[half] * w[:, 1:2])


def _combine(dest, x1, gate_w, yr, *, R):
    T, D = x1.shape
    grid_spec = pltpu.PrefetchScalarGridSpec(
        num_scalar_prefetch=1,
        grid=(T // R,),
        in_specs=[
            pl.BlockSpec((R, D), lambda i, d: (i, 0)),
            pl.BlockSpec((R, TOP_K), lambda i, d: (i, 0)),
            pl.BlockSpec(memory_space=pl.ANY),
        ],
        out_specs=pl.BlockSpec((R, D), lambda i, d: (i, 0)),
        scratch_shapes=[pltpu.VMEM((2, TOP_K, R * (D // (2 * LANES)), LANES), jnp.uint32),
                        pltpu.SemaphoreType.DMA((2,))],
    )
    return pl.pallas_call(
        functools.partial(_combine_body, R=R),
        grid_spec=grid_spec,
        out_shape=jax.ShapeDtypeStruct((T, D), F32),
        compiler_params=_params("arbitrary"),
        name="combine",
    )(dest, x1, gate_w, yr)


def _route_body(lg_ref, tri_ref, ids_ref, gate_ref, cnt_ref):
    @pl.when(pl.program_id(0) == 0)
    def _():
        cnt_ref[...] = jnp.zeros_like(cnt_ref)

    lg = lg_ref[...]
    shape = lg.shape
    lane = lax.broadcasted_iota(jnp.int32, shape, 1)
    big = jnp.int32(LANES)

    def softmax_masked(mask):
        v = jnp.where(mask, lg, -jnp.inf)
        u = jnp.exp(v - jnp.max(v, axis=1, keepdims=True))
        return jnp.where(mask, u / jnp.sum(u, axis=1, keepdims=True), -1.0)

    def top1(p):
        best = jnp.max(p, axis=1, keepdims=True)
        idx = jnp.min(jnp.where(p == best, lane, big), axis=1, keepdims=True)
        return best, idx

    g_p, g_lane = top1(softmax_masked(lane < N_GROUPS))
    grp_of_lane = lax.shift_right_arithmetic(lane - N_GROUPS, jnp.int32(3))
    in_grp = jnp.logical_and(lane >= N_GROUPS, grp_of_lane == g_lane)
    in_grp = jnp.logical_and(in_grp, lane < N_GROUPS + N_EXPERTS)
    pe = softmax_masked(in_grp)
    p1, l1 = top1(pe)
    p2, l2 = top1(jnp.where(lane == l1, -1.0, pe))
    tot = p1 + p2
    gate1 = g_p * (p1 / tot)
    gate2 = g_p * (p2 / tot)

    hot1 = lane == l1
    hot2 = lane == l2
    hot = jnp.logical_or(hot1, hot2)
    before = jnp.dot(tri_ref[...], jnp.where(hot, 1.0, 0.0).astype(BF16), preferred_element_type=F32) + cnt_ref[...]
    rank1 = jnp.sum(jnp.where(hot1, before, 0.0), axis=1, keepdims=True)
    rank2 = jnp.sum(jnp.where(hot2, before, 0.0), axis=1, keepdims=True)
    cnt_ref[...] = cnt_ref[...] + jnp.sum(jnp.where(hot, 1.0, 0.0), axis=0, keepdims=True)

    ids = jnp.where(lane == 0, l1 - N_GROUPS, jnp.where(lane == 1, l2 - N_GROUPS, 0))
    ids = jnp.where(lane == 2, rank1.astype(jnp.int32), jnp.where(lane == 3, rank2.astype(jnp.int32), ids))
    ids_ref[...] = ids
    gate_ref[...] = jnp.where(lane == 0, gate1, jnp.where(lane == 1, gate2, 0.0))


def _route_tokens(logits, *, tb):
    T = logits.shape[0]
    r = jnp.arange(tb)
    tri = (r[None, :] < r[:, None]).astype(BF16)
    return pl.pallas_call(
        _route_body,
        grid=(T // tb,),
        in_specs=[
            pl.BlockSpec((tb, LANES), lambda i: (i, 0)),
            pl.BlockSpec((tb, tb), lambda i: (0, 0)),
        ],
        out_specs=[
            pl.BlockSpec((tb, LANES), lambda i: (i, 0)),
            pl.BlockSpec((tb, LANES), lambda i: (i, 0)),
            pl.BlockSpec((1, LANES), lambda i: (0, 0)),
        ],
        out_shape=[
            jax.ShapeDtypeStruct((T, LANES), jnp.int32),
            jax.ShapeDtypeStruct((T, LANES), F32),
            jax.ShapeDtypeStruct((1, LANES), F32),
        ],
        compiler_params=_params("arbitrary"),
        name="route",
    )(logits, tri)


def _route(logits, R):
    T = logits.shape[0]
    ids, gates, cnt = _route_tokens(logits, tb=512 if T % 512 == 0 else T)
    gate = gates[:, 0:TOP_K]
    M = T * TOP_K
    eid_f = ids[:, 0:TOP_K].reshape(M)
    rank = ids[:, TOP_K:2 * TOP_K].reshape(M)
    counts = cnt[0, N_GROUPS:N_GROUPS + N_EXPERTS].astype(jnp.int32)
    padded = (counts + R - 1) // R * R
    pend = jnp.cumsum(padded)
    pstart = pend - padded
    dest = (pstart[eid_f] + rank).astype(jnp.int32)
    n_blocks = -(-M // R) + N_EXPERTS
    tok_f = jnp.arange(M, dtype=jnp.int32) // TOP_K
    row_tok = jnp.zeros((n_blocks * R,), jnp.int32).at[dest].set(
        tok_f, unique_indices=True, mode="promise_in_bounds")
    blk_start = jnp.arange(n_blocks, dtype=jnp.int32) * R
    blk_e = jnp.sum((pend[None, :] <= blk_start[:, None]).astype(jnp.int32), axis=1)
    nused = (pend[-1] // R).astype(jnp.int32)
    last_e = blk_e[jnp.maximum(nused - 1, 0)]
    blk_e = jnp.where(jnp.arange(n_blocks) < nused, blk_e, last_e)
    blk_e = jnp.minimum(blk_e, N_EXPERTS - 1).astype(jnp.int32)
    e_idx = jnp.arange(N_EXPERTS, dtype=jnp.int32)
    cand = jnp.where(counts > 0, e_idx, N_EXPERTS)
    sfx = lax.cummin(cand[::-1])[::-1]
    nxt_of_e = jnp.concatenate([sfx[1:], jnp.full((1,), N_EXPERTS, jnp.int32)])
    nxt_of_e = jnp.where(nxt_of_e < N_EXPERTS, nxt_of_e, -1)
    nxt = nxt_of_e[blk_e].astype(jnp.int32)
    changed = jnp.concatenate([jnp.ones((1,), bool), blk_e[1:] != blk_e[:-1]])
    first = jnp.logical_and(changed, jnp.arange(n_blocks) < nused).astype(jnp.int32)
    return gate, dest, row_tok, blk_e, first, nxt, nused.reshape(1)


def _rope_tables(S, gain, scale):
    half = A_HEAD_DIM // 2
    freqs = ROPE_THETA ** (-jnp.arange(half, dtype=F32) / half)
    ang = jnp.arange(S, dtype=F32)[:, None] * freqs[None, :]
    cos = jnp.cos(ang)
    sin = jnp.sin(ang)
    g1, g2 = gain[:half], gain[half:]
    a_head = jnp.concatenate([cos * g1, cos * g2], axis=1)
    b_head = jnp.concatenate([-sin * g2, sin * g1], axis=1)
    reps = LANES // A_HEAD_DIM
    return jnp.tile(a_head, (1, reps)) * scale, jnp.tile(b_head, (1, reps)) * scale


def kernel(x, g_mix, w_in, conv_qk, b_igate, b_fgate, g_mlstm, g_q, g_k, sinks, w_proj_m, w_proj_a, w_out,
           g_ffn, w_group, b_group, w_expert, b_expert, w_gate, w_up, w_down):
    B, S, D = x.shape
    T = B * S
    depth = g_mix.shape[0]
    xf = x.reshape(T, D)

    sizes = (M_WIDTH, M_WIDTH, M_WIDTH, M_WIDTH, M_HEADS, M_HEADS, A_WIDTH, A_KV_WIDTH, A_KV_WIDTH, D, D)
    offs = [0]
    for s_ in sizes:
        offs.append(offs[-1] + s_)
    seg = lambda w, idx: w[:, offs[idx]:offs[idx + 1]]
    order = (9, 10, 0, 1, 2, 3, 6, 7, 8)
    new_off = {}
    acc = 0
    for idx in order:
        new_off[idx] = acc
        acc += sizes[idx]
    dk = M_HEAD_DIM

    lane128 = jnp.arange(LANES)
    bd = (lane128[:, None] // A_HEAD_DIM == lane128[None, :] // A_HEAD_DIM).astype(BF16)
    gw = A_GROUP * A_HEAD_DIM
    r_idx = jnp.arange(gw)
    rep = jnp.stack([(r_idx[:, None] == (j * A_HEAD_DIM + r_idx[None, :] % A_HEAD_DIM)).astype(BF16)
                     for j in range(A_KV_HEADS)])
    ones_bd = (jnp.arange(A_GROUP * WINDOW)[:, None] // WINDOW == r_idx[None, :] // A_HEAD_DIM).astype(BF16)

    for l in range(depth):
        w16 = w_in[l].astype(BF16)
        w_rep = jnp.concatenate([seg(w16, idx) for idx in order], axis=1)
        w_gates = jnp.concatenate([seg(w16, 4), seg(w16, 5)], axis=1)
        w_gates = jnp.pad(w_gates, ((0, 0), (0, LANES - 2 * M_HEADS)))
        z, zg = _in_proj(xf, g_mix[l][None, :], w_rep, w_gates, bm=1024 if T % 1024 == 0 else T,
                         bn=w_rep.shape[1] // 4)

        gate_bias = jnp.pad(jnp.concatenate([b_igate[l], b_fgate[l]]), (0, LANES - 2 * M_HEADS))[None, :]
        hm = _mlstm(z, zg, conv_qk[l], gate_bias, g_mlstm[l][:, None, :], B=B, S=S,
                    col_q=new_off[0] // dk, col_k=new_off[1] // dk, col_v=new_off[2] // dk, col_o=new_off[3] // dk)

        qa, qb = _rope_tables(S, g_q[l], A_HEAD_DIM ** -0.5)
        ka, kb = _rope_tables(S, g_k[l], 1.0)
        ha = _swa(z, sinks[l], qa, qb, ka, kb, bd, rep, ones_bd, B=B, S=S,
                  col_q=new_off[6] // A_WIDTH, col_k=new_off[7] // A_KV_WIDTH, col_v=new_off[8] // A_KV_WIDTH)

        w_router = jnp.pad(jnp.concatenate([w_group[l], w_expert[l]], axis=1),
                           ((0, 0), (0, LANES - N_GROUPS - N_EXPERTS)))
        r_b = jnp.pad(jnp.concatenate([b_group[l], b_expert[l]]), (0, LANES - N_GROUPS - N_EXPERTS))[None, :]
        x1, hf, logits = _merge(hm, ha, z, xf, w_proj_m[l].astype(BF16), w_proj_a[l].astype(BF16),
                                w_out[l].astype(BF16), g_ffn[l][None, :], w_router.astype(BF16), r_b,
                                bm=256, col_gm=new_off[9] // D, col_ga=new_off[10] // D)

        gate, dest, row_tok, blk_e, first, nxt, nused = _route(logits, MOE_ROWS)
        yr = _moe(blk_e, first, nxt, nused, row_tok, hf, w_gate[l], w_up[l], w_down[l], R=MOE_ROWS)
        xf = _combine(dest, x1, gate, yr, R=COMBINE_ROWS)
    return xf.reshape(B, S, D)
```

```python
import functools

import jax
import jax.numpy as jnp
from jax import lax
from jax.experimental import pallas as pl
from jax.experimental.pallas import tpu as pltpu

F32 = jnp.float32
BF16 = jnp.bfloat16
EPS = 1e-6
LANES = 128
VMEM_LIMIT = 56 * 1024 * 1024

M_HEADS = 4
M_HEAD_DIM = 256
M_WIDTH = M_HEADS * M_HEAD_DIM
CONV_WIDTH = 4
A_HEADS = 16
A_KV_HEADS = 4
A_GROUP = A_HEADS // A_KV_HEADS
A_HEAD_DIM = 64
A_WIDTH = A_HEADS * A_HEAD_DIM
A_KV_WIDTH = A_KV_HEADS * A_HEAD_DIM
WINDOW = 128
ROPE_THETA = 10000.0
N_GROUPS = 8
EXPERTS_PER_GROUP = 8
N_EXPERTS = N_GROUPS * EXPERTS_PER_GROUP
TOP_K = 2

MLSTM_CHUNK = 128
MLSTM_TIME_BLOCK = 512
MOE_ROWS = 256
MOE_GATHER_SLOTS = 4
COMBINE_ROWS = 512


def _sigmoid(v):
    return 1.0 / (1.0 + jnp.exp(-v))


def _params(*sem):
    return pltpu.CompilerParams(dimension_semantics=sem, vmem_limit_bytes=VMEM_LIMIT)


def _pack_bf16_pair(lo, hi):
    lo_b = lax.bitcast_convert_type(lo.astype(BF16).astype(F32), jnp.uint32)
    hi_b = lax.bitcast_convert_type(hi.astype(BF16).astype(F32), jnp.uint32)
    return lax.shift_right_logical(lo_b, jnp.uint32(16)) | hi_b


def _unpack_bf16_pair(word):
    lo = lax.bitcast_convert_type(lax.shift_left(word, jnp.uint32(16)), F32)
    hi = lax.bitcast_convert_type(word & jnp.uint32(0xFFFF0000), F32)
    return lo, hi


def _store_token_rows(ref, val):
    n, d = val.shape
    sub = d // (2 * LANES)
    for s in range(sub):
        c = 2 * s * LANES
        ref[pl.ds(s, n, stride=sub), :] = _pack_bf16_pair(val[:, c:c + LANES], val[:, c + LANES:c + 2 * LANES])


def _load_token_rows(ref, n, sub, s):
    return _unpack_bf16_pair(ref[pl.ds(s, n, stride=sub), :])


def _inproj_body(x_ref, g_ref, w_ref, wgate_ref, z_ref, zg_ref, hn_ref, *, sub):
    bm = x_ref.shape[0]

    @pl.when(pl.program_id(1) == 0)
    def _():
        def rows(r, carry):
            sl = pl.ds(pl.multiple_of(r * sub, sub), sub)
            xv = x_ref[sl, :]
            ms = jnp.mean(xv * xv, axis=-1, keepdims=True)
            hn_ref[sl, :] = ((xv * lax.rsqrt(ms + EPS)) * g_ref[...]).astype(BF16)
            return carry
        lax.fori_loop(0, bm // sub, rows, 0)
        zg_ref[...] = jnp.dot(hn_ref[...], wgate_ref[...], preferred_element_type=F32)

    z_ref[...] = jnp.dot(hn_ref[...], w_ref[...], preferred_element_type=F32).astype(BF16)


def _in_proj(x2, g, w, wgate, *, bm, bn):
    T, D = x2.shape
    N = w.shape[1]
    return pl.pallas_call(
        functools.partial(_inproj_body, sub=128),
        grid=(T // bm, N // bn),
        in_specs=[
            pl.BlockSpec((bm, D), lambda i, j: (i, 0)),
            pl.BlockSpec((1, D), lambda i, j: (0, 0)),
            pl.BlockSpec((D, bn), lambda i, j: (0, j)),
            pl.BlockSpec((D, LANES), lambda i, j: (0, 0)),
        ],
        out_specs=[
            pl.BlockSpec((bm, bn), lambda i, j: (i, j)),
            pl.BlockSpec((bm, LANES), lambda i, j: (i, 0)),
        ],
        out_shape=[
            jax.ShapeDtypeStruct((T, N), BF16),
            jax.ShapeDtypeStruct((T, LANES), F32),
        ],
        scratch_shapes=[pltpu.VMEM((bm, D), BF16)],
        compiler_params=_params("parallel", "arbitrary"),
        name="in_proj",
    )(x2, g, w, wgate)


def _mlstm_body(q_ref, k_ref, v_ref, o_ref, zg_ref, cq_ref, ck_ref, bias_ref, gn_ref, out_ref,
                qs_ref, ks_ref, colli_ref, colb_ref, rowli_ref, rowb_ref, cli_ref, cb_ref, rli_ref, rb_ref,
                c_ref, n_ref, m_ref, hq_ref, hk_ref, *, L, CB, HP):
    S = q_ref.shape[0]
    dk = q_ref.shape[1] // HP

    @pl.when(pl.program_id(1) == 0)
    def _():
        c_ref[...] = jnp.zeros_like(c_ref)
        n_ref[...] = jnp.zeros_like(n_ref)
        m_ref[...] = jnp.zeros_like(m_ref)
        hq_ref[...] = jnp.zeros_like(hq_ref)
        hk_ref[...] = jnp.zeros_like(hk_ref)

    G = zg_ref[...] + bias_ref[...]
    lf = jnp.minimum(G, 0.0) - jnp.log1p(jnp.exp(-jnp.abs(G)))
    pos = lax.broadcasted_iota(jnp.int32, (S, LANES), 0) % L
    bc = lf
    sh = 1
    while sh < L:
        bc = bc + jnp.where(pos >= sh, pltpu.roll(bc, sh, axis=0), 0.0)
        sh *= 2
    colli_ref[...] = G
    colb_ref[...] = bc
    for p in range(S // LANES):
        sl = slice(p * LANES, (p + 1) * LANES)
        rowli_ref[:, sl] = G[sl, :].T[0:8, :]
        rowb_ref[:, sl] = bc[sl, :].T[0:8, :]

    lane = lax.broadcasted_iota(jnp.int32, (S, LANES), 1)
    sub = lax.broadcasted_iota(jnp.int32, (8, S), 0)
    for j in range(HP):
        cli_ref[j] = jnp.sum(jnp.where(lane == j, colli_ref[...], 0.0), axis=1, keepdims=True)
        cb_ref[j] = jnp.sum(jnp.where(lane == j + HP, colb_ref[...], 0.0), axis=1, keepdims=True)
        rli_ref[j] = jnp.sum(jnp.where(sub == j, rowli_ref[...], 0.0), axis=0, keepdims=True)
        rb_ref[j] = jnp.sum(jnp.where(sub == j + HP, rowb_ref[...], 0.0), axis=0, keepdims=True)

    def conv_silu(src_ref, w_ref, dst_ref, halo_ref, scale):
        w = w_ref[...]

        def taps(xs, wj):
            y = xs[3] * wj[0:1, :]
            y = y + xs[2] * wj[1:2, :]
            y = y + xs[1] * wj[2:3, :]
            y = y + xs[0] * wj[3:4, :]
            return ((y * _sigmoid(y)) * scale).astype(BF16)

        for j in range(HP):
            cj = slice(j * dk, (j + 1) * dk)
            wj = w[:, cj]

            def chunk(r, carry, cj=cj, wj=wj):
                r0 = pl.multiple_of(r * CB, CB)
                cur = src_ref[pl.ds(r0, CB), cj].astype(F32)
                dst_ref[pl.ds(r0, CB), cj] = taps([cur] + [pltpu.roll(cur, d, axis=0) for d in (1, 2, 3)], wj)
                p0 = pl.multiple_of(jnp.maximum(r0 - 16, 0), 16)
                prev = jnp.where(r > 0, src_ref[pl.ds(p0, 16), cj], halo_ref[:, cj]).astype(F32)[8:16, :]
                both = jnp.concatenate([prev, cur[0:16, :]], axis=0)
                dst_ref[pl.ds(r0, 16), cj] = taps(
                    [both[8:24, :]] + [pltpu.roll(both, d, axis=0)[8:24, :] for d in (1, 2, 3)], wj)
                return carry
            lax.fori_loop(0, S // CB, chunk, 0)
        halo_ref[...] = src_ref[S - 16:S, :]

    conv_silu(q_ref, cq_ref, qs_ref, hq_ref, 1.0)
    conv_silu(k_ref, ck_ref, ks_ref, hk_ref, float(dk) ** -0.5)

    t_idx = lax.broadcasted_iota(jnp.int32, (L, L), 0)
    s_idx = lax.broadcasted_iota(jnp.int32, (L, L), 1)
    causal = s_idx <= t_idx

    heads = range(HP)
    cols = [slice(j * dk, (j + 1) * dk) for j in heads]
    dims_nt = (((1,), (1,)), ((), ()))

    def chunk(c, carry):
        rows = pl.ds(pl.multiple_of(c * L, L), L)
        qc = [qs_ref[rows, cols[j]] for j in heads]
        kc = [ks_ref[rows, cols[j]] for j in heads]
        vc = [v_ref[rows, cols[j]] for j in heads]
        b_col = [cb_ref[j, rows, :] for j in heads]
        li_col = [cli_ref[j, rows, :] for j in heads]
        li_row = [rli_ref[j, :, rows] for j in heads]
        b_row = [rb_ref[j, :, rows] for j in heads]
        m = [m_ref[j] for j in heads]
        a = [b_col[j] + m[j] for j in heads]
        D = [jnp.where(causal, b_col[j] - b_row[j] + li_row[j], -jnp.inf) for j in heads]
        m_t = [jnp.maximum(a[j], jnp.max(D[j], axis=1, keepdims=True)) for j in heads]
        w_inter = [jnp.exp(a[j] - m_t[j]) for j in heads]
        s_qk = [lax.dot_general(qc[j], kc[j], dims_nt, preferred_element_type=F32) for j in heads]
        P = [s_qk[j] * jnp.exp(D[j] - m_t[j]) for j in heads]
        inter = [jnp.dot(qc[j], c_ref[j].astype(BF16), preferred_element_type=F32) for j in heads]
        intra = [jnp.dot(P[j].astype(BF16), vc[j], preferred_element_type=F32) for j in heads]
        num = [w_inter[j] * inter[j] + intra[j] for j in heads]
        qn = [w_inter[j] * jnp.sum(qc[j].astype(F32) * n_ref[j], axis=1, keepdims=True)
              + jnp.sum(P[j], axis=1, keepdims=True) for j in heads]
        den = [jnp.maximum(jnp.abs(qn[j]), jnp.exp(-m_t[j])) for j in heads]
        hh = [num[j] / den[j] for j in heads]
        ms = [jnp.mean(hh[j] * hh[j], axis=1, keepdims=True) for j in heads]
        hn = [(hh[j] * lax.rsqrt(ms[j] + EPS)) * gn_ref[j] for j in heads]
        for j in heads:
            out_ref[rows, cols[j]] = (hn[j] * _sigmoid(o_ref[rows, cols[j]].astype(F32))).astype(BF16)
        bL = [b_row[j][:, L - 1:L] for j in heads]
        g_col = [bL[j] - b_col[j] + li_col[j] for j in heads]
        m_new = [jnp.maximum(bL[j] + m[j], jnp.max(g_col[j], axis=0, keepdims=True)) for j in heads]
        decay = [jnp.exp(bL[j] + m[j] - m_new[j]) for j in heads]
        kw = [kc[j].astype(F32) * jnp.exp(g_col[j] - m_new[j]) for j in heads]
        upd = [jnp.dot(kw[j].T.astype(BF16), vc[j], preferred_element_type=F32) for j in heads]
        for j in heads:
            c_ref[j] = decay[j] * c_ref[j] + upd[j]
            n_ref[j] = decay[j] * n_ref[j] + jnp.sum(kw[j], axis=0, keepdims=True)
            m_ref[j] = m_new[j]
        return carry
    lax.fori_loop(0, S // L, chunk, 0)


def _mlstm(z, zg, conv_qk, gate_bias, g_mlstm3, *, B, S, col_q, col_k, col_v, col_o):
    T = B * S
    dk = M_HEAD_DIM
    L = MLSTM_CHUNK
    HP = M_HEADS
    wd = HP * dk
    TS = min(MLSTM_TIME_BLOCK, S)
    NT = S // TS
    zspec = lambda col: pl.BlockSpec((TS, wd), lambda b, t: (b * NT + t, col // HP))
    return pl.pallas_call(
        functools.partial(_mlstm_body, L=L, CB=min(128, TS), HP=HP),
        grid=(B, NT),
        in_specs=[
            zspec(col_q), zspec(col_k), zspec(col_v), zspec(col_o),
            pl.BlockSpec((TS, LANES), lambda b, t: (b * NT + t, 0)),
            pl.BlockSpec((CONV_WIDTH, wd), lambda b, t: (0, 0)),
            pl.BlockSpec((CONV_WIDTH, wd), lambda b, t: (0, 1)),
            pl.BlockSpec((1, LANES), lambda b, t: (0, 0)),
            pl.BlockSpec((HP, 1, dk), lambda b, t: (0, 0, 0)),
        ],
        out_specs=pl.BlockSpec((TS, wd), lambda b, t: (b * NT + t, 0)),
        out_shape=jax.ShapeDtypeStruct((T, M_WIDTH), BF16),
        scratch_shapes=[
            pltpu.VMEM((TS, wd), BF16), pltpu.VMEM((TS, wd), BF16),
            pltpu.VMEM((TS, LANES), F32), pltpu.VMEM((TS, LANES), F32),
            pltpu.VMEM((8, TS), F32), pltpu.VMEM((8, TS), F32),
            pltpu.VMEM((HP, TS, 1), F32), pltpu.VMEM((HP, TS, 1), F32),
            pltpu.VMEM((HP, 1, TS), F32), pltpu.VMEM((HP, 1, TS), F32),
            pltpu.VMEM((HP, dk, dk), F32), pltpu.VMEM((HP, 1, dk), F32), pltpu.VMEM((HP, 1, 1), F32),
            pltpu.VMEM((16, wd), BF16), pltpu.VMEM((16, wd), BF16),
        ],
        compiler_params=_params("parallel", "arbitrary"),
        name="mlstm",
    )(z, z, z, z, zg, conv_qk, conv_qk, gate_bias, g_mlstm3)


def _swa_body(sink_ref, q_ref, k_ref, v_ref, qa_ref, qb_ref, ka_ref, kb_ref, bd_ref, rep_ref, ones_ref, out_ref,
              kbd_ref, vbd_ref):
    n = pl.program_id(1)
    W = WINDOW
    hd = A_HEAD_DIM
    gw = A_GROUP * hd

    def norm_rope(x, ta, tb):
        x2 = x * x
        x2h = x2.astype(BF16)
        x2l = (x2 - x2h.astype(F32)).astype(BF16)
        ss = (jnp.dot(x2h, bd_ref[...], preferred_element_type=F32)
              + jnp.dot(x2l, bd_ref[...], preferred_element_type=F32))
        r = lax.rsqrt(ss * (1.0 / hd) + EPS)
        ln = lax.broadcasted_iota(jnp.int32, x.shape, 1) % hd
        swapped = jnp.where(ln < hd // 2, pltpu.roll(x, LANES - hd // 2, axis=1), pltpu.roll(x, hd // 2, axis=1))
        return r * (x * ta + swapped * tb)

    lane_head = lax.broadcasted_iota(jnp.int32, (W, gw), 1) // hd
    t_idx = lax.broadcasted_iota(jnp.int32, (W, W), 0)
    k_idx = lax.broadcasted_iota(jnp.int32, (W, W), 1)
    mask_cur = k_idx <= t_idx
    neg = -jnp.inf

    @pl.when(n == 0)
    def _():
        kbd_ref[1] = jnp.zeros(kbd_ref.shape[1:], BF16)
        vbd_ref[1] = jnp.zeros(vbd_ref.shape[1:], BF16)

    def prepare(half, cur_slot):
        rows = slice(half * W, (half + 1) * W)
        qa = qa_ref[rows, :]
        qb = qb_ref[rows, :]
        qp = jnp.concatenate(
            [norm_rope(q_ref[rows, t * LANES:(t + 1) * LANES].astype(F32), qa, qb) for t in range(A_WIDTH // LANES)],
            axis=1).astype(BF16)
        ka = ka_ref[rows, :]
        kb = kb_ref[rows, :]
        kp = jnp.concatenate(
            [norm_rope(k_ref[rows, t * LANES:(t + 1) * LANES].astype(F32), ka, kb)
             for t in range(A_KV_WIDTH // LANES)], axis=1).astype(BF16)
        vv = v_ref[rows, :]
        for j in range(A_KV_HEADS):
            krep = jnp.dot(kp, rep_ref[j], preferred_element_type=F32).astype(BF16)
            vrep = jnp.dot(vv, rep_ref[j], preferred_element_type=F32).astype(BF16)
            for i in range(A_GROUP):
                kbd_ref[cur_slot, j, i * W:(i + 1) * W, :] = jnp.where(lane_head == i, krep, jnp.zeros_like(krep))
                vbd_ref[cur_slot, j, i * W:(i + 1) * W, :] = jnp.where(lane_head == i, vrep, jnp.zeros_like(vrep))
        return qp

    upper = k_idx > t_idx
    blocks = ((0, 1, 0, jnp.logical_and(upper, n > 0)), (1, 0, 1, upper))

    dims_nt = (((1,), (1,)), ((), ()))
    for halves in ((blocks[0],), (blocks[1],)):
        qps = {h: prepare(h, cur) for h, _, cur, _ in halves}
        groups = tuple(range(A_KV_HEADS))
        gsel = [(h, j) for h, _, _, _ in halves for j in groups]
        pairs = [(h, j, i) for h, j in gsel for i in range(A_GROUP)]
        prev_of = {h: p for h, p, _, _ in halves}
        cur_of = {h: c for h, _, c, _ in halves}
        mprev_of = {h: m for h, _, _, m in halves}
        qg = {(h, j): qps[h][:, j * gw:(j + 1) * gw] for h, j in gsel}
        s_prev = {(h, j): lax.dot_general(qg[(h, j)], kbd_ref[prev_of[h], j], dims_nt, preferred_element_type=F32)
                  for h, j in gsel}
        s_cur = {(h, j): lax.dot_general(qg[(h, j)], kbd_ref[cur_of[h], j], dims_nt, preferred_element_type=F32)
                 for h, j in gsel}
        sp = {(h, j, i): jnp.where(mprev_of[h], s_prev[(h, j)][:, i * W:(i + 1) * W], neg) for h, j, i in pairs}
        sc = {(h, j, i): jnp.where(mask_cur, s_cur[(h, j)][:, i * W:(i + 1) * W], neg) for h, j, i in pairs}
        sink = {(h, j, i): sink_ref[j * A_GROUP + i] for h, j, i in pairs}
        mx = {k: jnp.maximum(jnp.max(jnp.maximum(sp[k], sc[k]), axis=1, keepdims=True), sink[k]) for k in pairs}
        pp = {k: jnp.exp(sp[k] - mx[k]).astype(BF16) for k in pairs}
        pc = {k: jnp.exp(sc[k] - mx[k]).astype(BF16) for k in pairs}
        es = {k: jnp.exp(sink[k] - mx[k]) for k in pairs}
        p_prev = {(h, j): jnp.concatenate([pp[(h, j, i)] for i in range(A_GROUP)], axis=1) for h, j in gsel}
        p_cur = {(h, j): jnp.concatenate([pc[(h, j, i)] for i in range(A_GROUP)], axis=1) for h, j in gsel}
        o = {(h, j): jnp.dot(p_prev[(h, j)], vbd_ref[prev_of[h], j], preferred_element_type=F32)
             + jnp.dot(p_cur[(h, j)], vbd_ref[cur_of[h], j], preferred_element_type=F32) for h, j in gsel}
        den = {(h, j): jnp.dot(p_prev[(h, j)], ones_ref[...], preferred_element_type=F32)
               + jnp.dot(p_cur[(h, j)], ones_ref[...], preferred_element_type=F32) for h, j in gsel}
        for h, j in gsel:
            sink_term = jnp.zeros((W, gw), F32)
            for i in range(A_GROUP):
                sink_term = jnp.where(lane_head == i, es[(h, j, i)], sink_term)
            out_ref[h * W:(h + 1) * W, j * gw:(j + 1) * gw] = (o[(h, j)] / (den[(h, j)] + sink_term)).astype(BF16)


def _swa(z, sinks, qa, qb, ka, kb, bd, rep, ones_bd, *, B, S, col_q, col_k, col_v):
    T = B * S
    W2 = 2 * WINDOW
    NB = S // W2
    gw = A_GROUP * A_HEAD_DIM
    tab = lambda: pl.BlockSpec((W2, LANES), lambda b, n: (n, 0))
    return pl.pallas_call(
        _swa_body,
        grid=(B, NB),
        in_specs=[
            pl.BlockSpec(memory_space=pltpu.SMEM),
            pl.BlockSpec((W2, A_WIDTH), lambda b, n: (b * NB + n, col_q)),
            pl.BlockSpec((W2, A_KV_WIDTH), lambda b, n: (b * NB + n, col_k)),
            pl.BlockSpec((W2, A_KV_WIDTH), lambda b, n: (b * NB + n, col_v)),
            tab(), tab(), tab(), tab(),
            pl.BlockSpec((LANES, LANES), lambda b, n: (0, 0)),
            pl.BlockSpec((A_KV_HEADS, gw, gw), lambda b, n: (0, 0, 0)),
            pl.BlockSpec((A_GROUP * WINDOW, gw), lambda b, n: (0, 0)),
        ],
        out_specs=pl.BlockSpec((W2, A_WIDTH), lambda b, n: (b * NB + n, 0)),
        out_shape=jax.ShapeDtypeStruct((T, A_WIDTH), BF16),
        scratch_shapes=[
            pltpu.VMEM((2, A_KV_HEADS, A_GROUP * WINDOW, gw), BF16),
            pltpu.VMEM((2, A_KV_HEADS, A_GROUP * WINDOW, gw), BF16),
        ],
        compiler_params=_params("parallel", "arbitrary"),
        name="swa",
    )(sinks, z, z, z, qa, qb, ka, kb, bd, rep, ones_bd)


def _merge_body(hm_ref, ha_ref, gm_ref, ga_ref, x_ref, wm_ref, wa_ref, wo_ref, gf_ref, rh_ref, rb_ref,
                x1_ref, hf_ref, lg_ref):
    pm = jnp.dot(hm_ref[...], wm_ref[...], preferred_element_type=F32)
    pa = jnp.dot(ha_ref[...], wa_ref[...], preferred_element_type=F32)
    mixed = _sigmoid(gm_ref[...].astype(F32)) * pm + _sigmoid(ga_ref[...].astype(F32)) * pa
    x1 = x_ref[...] + jnp.dot(mixed.astype(BF16), wo_ref[...], preferred_element_type=F32)
    x1_ref[...] = x1
    ms = jnp.mean(x1 * x1, axis=-1, keepdims=True)
    hf = (x1 * lax.rsqrt(ms + EPS)) * gf_ref[...]
    _store_token_rows(hf_ref, hf)
    lg_ref[...] = jnp.dot(hf.astype(BF16), rh_ref[...], preferred_element_type=F32) + rb_ref[...]


def _merge(hm, ha, z, x2, wm, wa, wo, gf, rh, rb, *, bm, col_gm, col_ga):
    T, D = x2.shape
    const = lambda shape: pl.BlockSpec(shape, lambda i: (0,) * len(shape), pipeline_mode=pl.Buffered(1))
    return pl.pallas_call(
        _merge_body,
        grid=(T // bm,),
        in_specs=[
            pl.BlockSpec((bm, M_WIDTH), lambda i: (i, 0)),
            pl.BlockSpec((bm, A_WIDTH), lambda i: (i, 0)),
            pl.BlockSpec((bm, D), lambda i: (i, col_gm)),
            pl.BlockSpec((bm, D), lambda i: (i, col_ga)),
            pl.BlockSpec((bm, D), lambda i: (i, 0)),
            const((M_WIDTH, D)), const((A_WIDTH, D)), const((D, D)), const((1, D)),
            const((D, LANES)), const((1, LANES)),
        ],
        out_specs=[
            pl.BlockSpec((bm, D), lambda i: (i, 0)),
            pl.BlockSpec((bm * (D // (2 * LANES)), LANES), lambda i: (i, 0)),
            pl.BlockSpec((bm, LANES), lambda i: (i, 0)),
        ],
        out_shape=[
            jax.ShapeDtypeStruct((T, D), F32),
            jax.ShapeDtypeStruct((T * (D // (2 * LANES)), LANES), jnp.uint32),
            jax.ShapeDtypeStruct((T, LANES), F32),
        ],
        compiler_params=_params("parallel"),
        name="merge",
    )(hm, ha, z, z, x2, wm, wa, wo, gf, rh, rb)


def _moe_body(blk_e_ref, first_ref, nxt_ref, nused_ref, tok_ref, hf_hbm, wg_hbm, wu_hbm, wd_hbm, y_ref,
              xbuf, xs_ref, stage_g, stage_u, stage_d, wg_ref, wu_ref, wd_ref, sem, wsem, *, R, CB):
    i = pl.program_id(0)
    nused = nused_ref[0]
    sub = xs_ref.shape[1] // (2 * LANES)

    def weight_copies(e):
        return (pltpu.make_async_copy(wg_hbm.at[e], stage_g, wsem.at[0]),
                pltpu.make_async_copy(wu_hbm.at[e], stage_u, wsem.at[1]),
                pltpu.make_async_copy(wd_hbm.at[e], stage_d, wsem.at[2]))

    def cast(src, dst):
        def rows(r, carry):
            sl = pl.ds(pl.multiple_of(r * CB, CB), CB)
            dst[sl, :] = src[sl, :].astype(BF16)
            return carry
        lax.fori_loop(0, src.shape[0] // CB, rows, 0)

    def start_gather(blk, slot):
        base = blk * R

        for r in range(R):
            src = pl.multiple_of(tok_ref[base + r] * sub, sub)
            pltpu.make_async_copy(hf_hbm.at[pl.ds(src, sub)], xbuf.at[slot, pl.ds(r * sub, sub)],
                                  sem.at[slot]).start(priority=r % 2)

    def wait_gather(slot):
        pltpu.make_async_copy(xbuf.at[slot], xbuf.at[slot], sem.at[slot]).wait()

    nslots = xbuf.shape[0]
    ahead = nslots - 1
    last_blk = pl.num_programs(0) - 1
    slot = i % nslots

    @pl.when(i == 0)
    def _():
        for a in range(ahead):
            start_gather(a, a)
        for c in weight_copies(blk_e_ref[0]):
            c.start(priority=1)

    @pl.when(i < nused)
    def _():
        @pl.when(first_ref[i] == 1)
        def _():
            for c in weight_copies(blk_e_ref[i]):
                c.wait()
            cast(stage_g, wg_ref)
            cast(stage_u, wu_ref)
            cast(stage_d, wd_ref)

            @pl.when(nxt_ref[i] >= 0)
            def _():
                for c in weight_copies(nxt_ref[i]):
                    c.start(priority=1)

        wait_gather(slot)
        for s in range(sub):
            lo, hi = _load_token_rows(xbuf.at[slot], R, sub, s)
            xs_ref[:, 2 * s * LANES:(2 * s + 1) * LANES] = lo.astype(BF16)
            xs_ref[:, (2 * s + 1) * LANES:(2 * s + 2) * LANES] = hi.astype(BF16)
        start_gather(jnp.minimum(i + ahead, last_blk), (i + ahead) % nslots)
        xv = xs_ref[...]
        g = jnp.dot(xv, wg_ref[...], preferred_element_type=F32)
        u = jnp.dot(xv, wu_ref[...], preferred_element_type=F32)
        hmid = ((g * _sigmoid(g)) * u).astype(BF16)
        _store_token_rows(y_ref, jnp.dot(hmid, wd_ref[...], preferred_element_type=F32))

        @pl.when(i == nused - 1)
        def _():
            for a in range(1, nslots):
                wait_gather((i + a) % nslots)


def _moe(blk_e, first, nxt, nused, row_tok, hf, wg, wu, wd, *, R):
    E, D, F = wg.shape
    sub = D // (2 * LANES)
    n_blocks = blk_e.shape[0]
    hbm = lambda: pl.BlockSpec(memory_space=pl.ANY)
    grid_spec = pltpu.PrefetchScalarGridSpec(
        num_scalar_prefetch=5,
        grid=(n_blocks,),
        in_specs=[hbm(), hbm(), hbm(), hbm()],
        out_specs=pl.BlockSpec((R * sub, LANES), lambda i, be, fi, nx, nu, tk: (jnp.minimum(i, nu[0] - 1), 0)),
        scratch_shapes=[
            pltpu.VMEM((MOE_GATHER_SLOTS, R * sub, LANES), jnp.uint32), pltpu.VMEM((R, D), BF16),
            pltpu.VMEM((D, F), F32), pltpu.VMEM((D, F), F32), pltpu.VMEM((F, D), F32),
            pltpu.VMEM((D, F), BF16), pltpu.VMEM((D, F), BF16), pltpu.VMEM((F, D), BF16),
            pltpu.SemaphoreType.DMA((MOE_GATHER_SLOTS,)), pltpu.SemaphoreType.DMA((3,)),
        ],
    )
    return pl.pallas_call(
        functools.partial(_moe_body, R=R, CB=128),
        grid_spec=grid_spec,
        out_shape=jax.ShapeDtypeStruct((n_blocks * R * sub, LANES), jnp.uint32),
        compiler_params=_params("arbitrary"),
        name="moe",
    )(blk_e, first, nxt, nused, row_tok, hf, wg, wu, wd)


def _combine_body(dest_ref, x1_ref, w_ref, yr_hbm, out_ref, ybuf, sem, *, R):
    i = pl.program_id(0)
    nsteps = pl.num_programs(0)

    sub = out_ref.shape[1] // (2 * LANES)

    def start_gather(step, slot):
        base = step * (R * TOP_K)

        def row(r, carry):
            dst = pl.multiple_of(r * sub, sub)
            for k in range(TOP_K):
                src = pl.multiple_of(dest_ref[base + r * TOP_K + k] * sub, sub)
                pltpu.make_async_copy(yr_hbm.at[pl.ds(src, sub)], ybuf.at[slot, k, pl.ds(dst, sub)],
                                      sem.at[slot]).start(priority=k % 2)
            return carry
        lax.fori_loop(0, R, row, 0, unroll=4)

    slot = i % 2

    @pl.when(i == 0)
    def _():
        start_gather(0, 0)

    pltpu.make_async_copy(ybuf.at[slot], ybuf.at[slot], sem.at[slot]).wait()

    @pl.when(i + 1 < nsteps)
    def _():
        start_gather(i + 1, 1 - slot)

    w = w_ref[...]
    for s in range(sub):
        halves0 = _load_token_rows(ybuf.at[slot, 0], R, sub, s)
        halves1 = _load_token_rows(ybuf.at[slot, 1], R, sub, s)
        for half in range(2):
            cols = slice((2 * s + half) * LANES, (2 * s + half + 1) * LANES)
            out_ref[:, cols] = x1_ref[:, cols] + (halves0[half] * w[:, 0:1] + halves1[half] * w[:, 1:2])


def _combine(dest, x1, gate_w, yr, *, R):
    T, D = x1.shape
    grid_spec = pltpu.PrefetchScalarGridSpec(
        num_scalar_prefetch=1,
        grid=(T // R,),
        in_specs=[
            pl.BlockSpec((R, D), lambda i, d: (i, 0)),
            pl.BlockSpec((R, TOP_K), lambda i, d: (i, 0)),
            pl.BlockSpec(memory_space=pl.ANY),
        ],
        out_specs=pl.BlockSpec((R, D), lambda i, d: (i, 0)),
        scratch_shapes=[pltpu.VMEM((2, TOP_K, R * (D // (2 * LANES)), LANES), jnp.uint32),
                        pltpu.SemaphoreType.DMA((2,))],
    )
    return pl.pallas_call(
        functools.partial(_combine_body, R=R),
        grid_spec=grid_spec,
        out_shape=jax.ShapeDtypeStruct((T, D), F32),
        compiler_params=_params("arbitrary"),
        name="combine",
    )(dest, x1, gate_w, yr)


def _route_body(lg_ref, tri_ref, ids_ref, gate_ref, cnt_ref):
    @pl.when(pl.program_id(0) == 0)
    def _():
        cnt_ref[...] = jnp.zeros_like(cnt_ref)

    lg = lg_ref[...]
    shape = lg.shape
    lane = lax.broadcasted_iota(jnp.int32, shape, 1)
    big = jnp.int32(LANES)

    def softmax_masked(mask):
        v = jnp.where(mask, lg, -jnp.inf)
        u = jnp.exp(v - jnp.max(v, axis=1, keepdims=True))
        return jnp.where(mask, u / jnp.sum(u, axis=1, keepdims=True), -1.0)

    def top1(p):
        best = jnp.max(p, axis=1, keepdims=True)
        idx = jnp.min(jnp.where(p == best, lane, big), axis=1, keepdims=True)
        return best, idx

    g_p, g_lane = top1(softmax_masked(lane < N_GROUPS))
    grp_of_lane = lax.shift_right_arithmetic(lane - N_GROUPS, jnp.int32(3))
    in_grp = jnp.logical_and(lane >= N_GROUPS, grp_of_lane == g_lane)
    in_grp = jnp.logical_and(in_grp, lane < N_GROUPS + N_EXPERTS)
    pe = softmax_masked(in_grp)
    p1, l1 = top1(pe)
    p2, l2 = top1(jnp.where(lane == l1, -1.0, pe))
    tot = p1 + p2
    gate1 = g_p * (p1 / tot)
    gate2 = g_p * (p2 / tot)

    hot1 = lane == l1
    hot2 = lane == l2
    hot = jnp.logical_or(hot1, hot2)
    before = jnp.dot(tri_ref[...], jnp.where(hot, 1.0, 0.0).astype(BF16), preferred_element_type=F32) + cnt_ref[...]
    rank1 = jnp.sum(jnp.where(hot1, before, 0.0), axis=1, keepdims=True)
    rank2 = jnp.sum(jnp.where(hot2, before, 0.0), axis=1, keepdims=True)
    cnt_ref[...] = cnt_ref[...] + jnp.sum(jnp.where(hot, 1.0, 0.0), axis=0, keepdims=True)

    ids = jnp.where(lane == 0, l1 - N_GROUPS, jnp.where(lane == 1, l2 - N_GROUPS, 0))
    ids = jnp.where(lane == 2, rank1.astype(jnp.int32), jnp.where(lane == 3, rank2.astype(jnp.int32), ids))
    ids_ref[...] = ids
    gate_ref[...] = jnp.where(lane == 0, gate1, jnp.where(lane == 1, gate2, 0.0))


def _route_tokens(logits, *, tb):
    T = logits.shape[0]
    r = jnp.arange(tb)
    tri = (r[None, :] < r[:, None]).astype(BF16)
    return pl.pallas_call(
        _route_body,
        grid=(T // tb,),
        in_specs=[
            pl.BlockSpec((tb, LANES), lambda i: (i, 0)),
            pl.BlockSpec((tb, tb), lambda i: (0, 0)),
        ],
        out_specs=[
            pl.BlockSpec((tb, LANES), lambda i: (i, 0)),
            pl.BlockSpec((tb, LANES), lambda i: (i, 0)),
            pl.BlockSpec((1, LANES), lambda i: (0, 0)),
        ],
        out_shape=[
            jax.ShapeDtypeStruct((T, LANES), jnp.int32),
            jax.ShapeDtypeStruct((T, LANES), F32),
            jax.ShapeDtypeStruct((1, LANES), F32),
        ],
        compiler_params=_params("arbitrary"),
        name="route",
    )(logits, tri)


def _route(logits, R):
    T = logits.shape[0]
    ids, gates, cnt = _route_tokens(logits, tb=512 if T % 512 == 0 else T)
    gate = gates[:, 0:TOP_K]
    M = T * TOP_K
    eid_f = ids[:, 0:TOP_K].reshape(M)
    rank = ids[:, TOP_K:2 * TOP_K].reshape(M)
    counts = cnt[0, N_GROUPS:N_GROUPS + N_EXPERTS].astype(jnp.int32)
    padded = (counts + R - 1) // R * R
    pend = jnp.cumsum(padded)
    pstart = pend - padded
    dest = (pstart[eid_f] + rank).astype(jnp.int32)
    n_blocks = -(-M // R) + N_EXPERTS
    tok_f = jnp.arange(M, dtype=jnp.int32) // TOP_K
    row_tok = jnp.zeros((n_blocks * R,), jnp.int32).at[dest].set(
        tok_f, unique_indices=True, mode="promise_in_bounds")
    blk_start = jnp.arange(n_blocks, dtype=jnp.int32) * R
    blk_e = jnp.sum((pend[None, :] <= blk_start[:, None]).astype(jnp.int32), axis=1)
    nused = (pend[-1] // R).astype(jnp.int32)
    last_e = blk_e[jnp.maximum(nused - 1, 0)]
    blk_e = jnp.where(jnp.arange(n_blocks) < nused, blk_e, last_e)
    blk_e = jnp.minimum(blk_e, N_EXPERTS - 1).astype(jnp.int32)
    e_idx = jnp.arange(N_EXPERTS, dtype=jnp.int32)
    cand = jnp.where(counts > 0, e_idx, N_EXPERTS)
    sfx = lax.cummin(cand[::-1])[::-1]
    nxt_of_e = jnp.concatenate([sfx[1:], jnp.full((1,), N_EXPERTS, jnp.int32)])
    nxt_of_e = jnp.where(nxt_of_e < N_EXPERTS, nxt_of_e, -1)
    nxt = nxt_of_e[blk_e].astype(jnp.int32)
    changed = jnp.concatenate([jnp.ones((1,), bool), blk_e[1:] != blk_e[:-1]])
    first = jnp.logical_and(changed, jnp.arange(n_blocks) < nused).astype(jnp.int32)
    return gate, dest, row_tok, blk_e, first, nxt, nused.reshape(1)


def _rope_tables(S, gain, scale):
    half = A_HEAD_DIM // 2
    freqs = ROPE_THETA ** (-jnp.arange(half, dtype=F32) / half)
    ang = jnp.arange(S, dtype=F32)[:, None] * freqs[None, :]
    cos = jnp.cos(ang)
    sin = jnp.sin(ang)
    g1, g2 = gain[:half], gain[half:]
    a_head = jnp.concatenate([cos * g1, cos * g2], axis=1)
    b_head = jnp.concatenate([-sin * g2, sin * g1], axis=1)
    reps = LANES // A_HEAD_DIM
    return jnp.tile(a_head, (1, reps)) * scale, jnp.tile(b_head, (1, reps)) * scale


def kernel(x, g_mix, w_in, conv_qk, b_igate, b_fgate, g_mlstm, g_q, g_k, sinks, w_proj_m, w_proj_a, w_out,
           g_ffn, w_group, b_group, w_expert, b_expert, w_gate, w_up, w_down):
    B, S, D = x.shape
    T = B * S
    depth = g_mix.shape[0]
    xf = x.reshape(T, D)

    sizes = (M_WIDTH, M_WIDTH, M_WIDTH, M_WIDTH, M_HEADS, M_HEADS, A_WIDTH, A_KV_WIDTH, A_KV_WIDTH, D, D)
    offs = [0]
    for s_ in sizes:
        offs.append(offs[-1] + s_)
    seg = lambda w, idx: w[:, offs[idx]:offs[idx + 1]]
    order = (9, 10, 0, 1, 2, 3, 6, 7, 8)
    new_off = {}
    acc = 0
    for idx in order:
        new_off[idx] = acc
        acc += sizes[idx]
    dk = M_HEAD_DIM

    lane128 = jnp.arange(LANES)
    bd = (lane128[:, None] // A_HEAD_DIM == lane128[None, :] // A_HEAD_DIM).astype(BF16)
    gw = A_GROUP * A_HEAD_DIM
    r_idx = jnp.arange(gw)
    rep = jnp.stack([(r_idx[:, None] == (j * A_HEAD_DIM + r_idx[None, :] % A_HEAD_DIM)).astype(BF16)
                     for j in range(A_KV_HEADS)])
    ones_bd = (jnp.arange(A_GROUP * WINDOW)[:, None] // WINDOW == r_idx[None, :] // A_HEAD_DIM).astype(BF16)

    for l in range(depth):
        w16 = w_in[l].astype(BF16)
        w_rep = jnp.concatenate([seg(w16, idx) for idx in order], axis=1)
        w_gates = jnp.concatenate([seg(w16, 4), seg(w16, 5)], axis=1)
        w_gates = jnp.pad(w_gates, ((0, 0), (0, LANES - 2 * M_HEADS)))
        z, zg = _in_proj(xf, g_mix[l][None, :], w_rep, w_gates, bm=1024 if T % 1024 == 0 else T,
                         bn=w_rep.shape[1] // 4)

        gate_bias = jnp.pad(jnp.concatenate([b_igate[l], b_fgate[l]]), (0, LANES - 2 * M_HEADS))[None, :]
        hm = _mlstm(z, zg, conv_qk[l], gate_bias, g_mlstm[l][:, None, :], B=B, S=S,
                    col_q=new_off[0] // dk, col_k=new_off[1] // dk, col_v=new_off[2] // dk, col_o=new_off[3] // dk)

        qa, qb = _rope_tables(S, g_q[l], A_HEAD_DIM ** -0.5)
        ka, kb = _rope_tables(S, g_k[l], 1.0)
        ha = _swa(z, sinks[l], qa, qb, ka, kb, bd, rep, ones_bd, B=B, S=S,
                  col_q=new_off[6] // A_WIDTH, col_k=new_off[7] // A_KV_WIDTH, col_v=new_off[8] // A_KV_WIDTH)

        w_router = jnp.pad(jnp.concatenate([w_group[l], w_expert[l]], axis=1),
                           ((0, 0), (0, LANES - N_GROUPS - N_EXPERTS)))
        r_b = jnp.pad(jnp.concatenate([b_group[l], b_expert[l]]), (0, LANES - N_GROUPS - N_EXPERTS))[None, :]
        x1, hf, logits = _merge(hm, ha, z, xf, w_proj_m[l].astype(BF16), w_proj_a[l].astype(BF16),
                                w_out[l].astype(BF16), g_ffn[l][None, :], w_router.astype(BF16), r_b,
                                bm=256, col_gm=new_off[9] // D, col_ga=new_off[10] // D)

        gate, dest, row_tok, blk_e, first, nxt, nused = _route(logits, MOE_ROWS)
        yr = _moe(blk_e, first, nxt, nused, row_tok, hf, w_gate[l], w_up[l], w_down[l], R=MOE_ROWS)
        xf = _combine(dest, x1, gate, yr, R=COMBINE_ROWS)
    return xf.reshape(B, S, D)
```

```python
import functools

import jax
import jax.numpy as jnp
from jax import lax
from jax.experimental import pallas as pl
from jax.experimental.pallas import tpu as pltpu

F32 = jnp.float32
BF16 = jnp.bfloat16
EPS = 1e-6
LANES = 128
VMEM_LIMIT = 56 * 1024 * 1024

M_HEADS = 4
M_HEAD_DIM = 256
M_WIDTH = M_HEADS * M_HEAD_DIM
CONV_WIDTH = 4
A_HEADS = 16
A_KV_HEADS = 4
A_GROUP = A_HEADS // A_KV_HEADS
A_HEAD_DIM = 64
A_WIDTH = A_HEADS * A_HEAD_DIM
A_KV_WIDTH = A_KV_HEADS * A_HEAD_DIM
WINDOW = 128
ROPE_THETA = 10000.0
N_GROUPS = 8
EXPERTS_PER_GROUP = 8
N_EXPERTS = N_GROUPS * EXPERTS_PER_GROUP
TOP_K = 2

MLSTM_CHUNK = 128
MLSTM_TIME_BLOCK = 512
MOE_ROWS = 256
MOE_GATHER_SLOTS = 6
COMBINE_ROWS = 256


def _sigmoid(v):
    return 1.0 / (1.0 + jnp.exp(-v))


def _params(*sem):
    return pltpu.CompilerParams(dimension_semantics=sem, vmem_limit_bytes=VMEM_LIMIT)


def _pack_bf16_pair(lo, hi):
    lo_b = lax.bitcast_convert_type(lo.astype(BF16).astype(F32), jnp.uint32)
    hi_b = lax.bitcast_convert_type(hi.astype(BF16).astype(F32), jnp.uint32)
    return lax.shift_right_logical(lo_b, jnp.uint32(16)) | hi_b


def _unpack_bf16_pair(word):
    lo = lax.bitcast_convert_type(lax.shift_left(word, jnp.uint32(16)), F32)
    hi = lax.bitcast_convert_type(word & jnp.uint32(0xFFFF0000), F32)
    return lo, hi


def _store_token_rows(ref, val):
    n, d = val.shape
    sub = d // (2 * LANES)
    for s in range(sub):
        c = 2 * s * LANES
        ref[pl.ds(s, n, stride=sub), :] = _pack_bf16_pair(val[:, c:c + LANES], val[:, c + LANES:c + 2 * LANES])


def _load_token_rows(ref, n, sub, s):
    return _unpack_bf16_pair(ref[pl.ds(s, n, stride=sub), :])


def _inproj_body(x_ref, g_ref, w_ref, wgate_ref, z_ref, zg_ref, hn_ref, *, sub):
    bm = x_ref.shape[0]

    @pl.when(pl.program_id(1) == 0)
    def _():
        def rows(r, carry):
            sl = pl.ds(pl.multiple_of(r * sub, sub), sub)
            xv = x_ref[sl, :]
            ms = jnp.mean(xv * xv, axis=-1, keepdims=True)
            hn_ref[sl, :] = ((xv * lax.rsqrt(ms + EPS)) * g_ref[...]).astype(BF16)
            return carry
        lax.fori_loop(0, bm // sub, rows, 0)
        zg_ref[...] = jnp.dot(hn_ref[...], wgate_ref[...], preferred_element_type=F32)

    z_ref[...] = jnp.dot(hn_ref[...], w_ref[...], preferred_element_type=F32).astype(BF16)


def _in_proj(x2, g, w, wgate, *, bm, bn):
    T, D = x2.shape
    N = w.shape[1]
    return pl.pallas_call(
        functools.partial(_inproj_body, sub=128),
        grid=(T // bm, N // bn),
        in_specs=[
            pl.BlockSpec((bm, D), lambda i, j: (i, 0)),
            pl.BlockSpec((1, D), lambda i, j: (0, 0)),
            pl.BlockSpec((D, bn), lambda i, j: (0, j)),
            pl.BlockSpec((D, LANES), lambda i, j: (0, 0)),
        ],
        out_specs=[
            pl.BlockSpec((bm, bn), lambda i, j: (i, j)),
            pl.BlockSpec((bm, LANES), lambda i, j: (i, 0)),
        ],
        out_shape=[
            jax.ShapeDtypeStruct((T, N), BF16),
            jax.ShapeDtypeStruct((T, LANES), F32),
        ],
        scratch_shapes=[pltpu.VMEM((bm, D), BF16)],
        compiler_params=_params("parallel", "arbitrary"),
        name="in_proj",
    )(x2, g, w, wgate)


def _mlstm_body(q_ref, k_ref, v_ref, o_ref, zg_ref, cq_ref, ck_ref, bias_ref, gn_ref, out_ref,
                qs_ref, ks_ref, colli_ref, colb_ref, rowli_ref, rowb_ref, cli_ref, cb_ref, rli_ref, rb_ref,
                c_ref, n_ref, m_ref, hq_ref, hk_ref, *, L, CB, HP):
    S = q_ref.shape[0]
    dk = q_ref.shape[1] // HP

    @pl.when(pl.program_id(1) == 0)
    def _():
        c_ref[...] = jnp.zeros_like(c_ref)
        n_ref[...] = jnp.zeros_like(n_ref)
        m_ref[...] = jnp.zeros_like(m_ref)
        hq_ref[...] = jnp.zeros_like(hq_ref)
        hk_ref[...] = jnp.zeros_like(hk_ref)

    G = zg_ref[...] + bias_ref[...]
    lf = jnp.minimum(G, 0.0) - jnp.log1p(jnp.exp(-jnp.abs(G)))
    pos = lax.broadcasted_iota(jnp.int32, (S, LANES), 0) % L
    bc = lf
    sh = 1
    while sh < L:
        bc = bc + jnp.where(pos >= sh, pltpu.roll(bc, sh, axis=0), 0.0)
        sh *= 2
    colli_ref[...] = G
    colb_ref[...] = bc
    for p in range(S // LANES):
        sl = slice(p * LANES, (p + 1) * LANES)
        rowli_ref[:, sl] = G[sl, :].T[0:8, :]
        rowb_ref[:, sl] = bc[sl, :].T[0:8, :]

    lane = lax.broadcasted_iota(jnp.int32, (S, LANES), 1)
    sub = lax.broadcasted_iota(jnp.int32, (8, S), 0)
    for j in range(HP):
        cli_ref[j] = jnp.sum(jnp.where(lane == j, colli_ref[...], 0.0), axis=1, keepdims=True)
        cb_ref[j] = jnp.sum(jnp.where(lane == j + HP, colb_ref[...], 0.0), axis=1, keepdims=True)
        rli_ref[j] = jnp.sum(jnp.where(sub == j, rowli_ref[...], 0.0), axis=0, keepdims=True)
        rb_ref[j] = jnp.sum(jnp.where(sub == j + HP, rowb_ref[...], 0.0), axis=0, keepdims=True)

    def conv_silu(src_ref, w_ref, dst_ref, halo_ref, scale):
        w = w_ref[...]

        def taps(xs, wj):
            y = xs[3] * wj[0:1, :]
            y = y + xs[2] * wj[1:2, :]
            y = y + xs[1] * wj[2:3, :]
            y = y + xs[0] * wj[3:4, :]
            return ((y * _sigmoid(y)) * scale).astype(BF16)

        for j in range(HP):
            cj = slice(j * dk, (j + 1) * dk)
            wj = w[:, cj]

            def chunk(r, carry, cj=cj, wj=wj):
                r0 = pl.multiple_of(r * CB, CB)
                cur = src_ref[pl.ds(r0, CB), cj].astype(F32)
                dst_ref[pl.ds(r0, CB), cj] = taps([cur] + [pltpu.roll(cur, d, axis=0) for d in (1, 2, 3)], wj)
                p0 = pl.multiple_of(jnp.maximum(r0 - 16, 0), 16)
                prev = jnp.where(r > 0, src_ref[pl.ds(p0, 16), cj], halo_ref[:, cj]).astype(F32)[8:16, :]
                both = jnp.concatenate([prev, cur[0:16, :]], axis=0)
                dst_ref[pl.ds(r0, 16), cj] = taps(
                    [both[8:24, :]] + [pltpu.roll(both, d, axis=0)[8:24, :] for d in (1, 2, 3)], wj)
                return carry
            lax.fori_loop(0, S // CB, chunk, 0)
        halo_ref[...] = src_ref[S - 16:S, :]

    conv_silu(q_ref, cq_ref, qs_ref, hq_ref, 1.0)
    conv_silu(k_ref, ck_ref, ks_ref, hk_ref, float(dk) ** -0.5)

    t_idx = lax.broadcasted_iota(jnp.int32, (L, L), 0)
    s_idx = lax.broadcasted_iota(jnp.int32, (L, L), 1)
    causal = s_idx <= t_idx

    heads = range(HP)
    cols = [slice(j * dk, (j + 1) * dk) for j in heads]
    dims_nt = (((1,), (1,)), ((), ()))

    def chunk(c, carry):
        rows = pl.ds(pl.multiple_of(c * L, L), L)
        qc = [qs_ref[rows, cols[j]] for j in heads]
        kc = [ks_ref[rows, cols[j]] for j in heads]
        vc = [v_ref[rows, cols[j]] for j in heads]
        b_col = [cb_ref[j, rows, :] for j in heads]
        li_col = [cli_ref[j, rows, :] for j in heads]
        li_row = [rli_ref[j, :, rows] for j in heads]
        b_row = [rb_ref[j, :, rows] for j in heads]
        m = [m_ref[j] for j in heads]
        a = [b_col[j] + m[j] for j in heads]
        D = [jnp.where(causal, b_col[j] - b_row[j] + li_row[j], -jnp.inf) for j in heads]
        m_t = [jnp.maximum(a[j], jnp.max(D[j], axis=1, keepdims=True)) for j in heads]
        w_inter = [jnp.exp(a[j] - m_t[j]) for j in heads]
        s_qk = [lax.dot_general(qc[j], kc[j], dims_nt, preferred_element_type=F32) for j in heads]
        P = [s_qk[j] * jnp.exp(D[j] - m_t[j]) for j in heads]
        inter = [jnp.dot(qc[j], c_ref[j].astype(BF16), preferred_element_type=F32) for j in heads]
        intra = [jnp.dot(P[j].astype(BF16), vc[j], preferred_element_type=F32) for j in heads]
        num = [w_inter[j] * inter[j] + intra[j] for j in heads]
        qn = [w_inter[j] * jnp.sum(qc[j].astype(F32) * n_ref[j], axis=1, keepdims=True)
              + jnp.sum(P[j], axis=1, keepdims=True) for j in heads]
        den = [jnp.maximum(jnp.abs(qn[j]), jnp.exp(-m_t[j])) for j in heads]
        hh = [num[j] / den[j] for j in heads]
        ms = [jnp.mean(hh[j] * hh[j], axis=1, keepdims=True) for j in heads]
        hn = [(hh[j] * lax.rsqrt(ms[j] + EPS)) * gn_ref[j] for j in heads]
        for j in heads:
            out_ref[rows, cols[j]] = (hn[j] * _sigmoid(o_ref[rows, cols[j]].astype(F32))).astype(BF16)
        bL = [b_row[j][:, L - 1:L] for j in heads]
        g_col = [bL[j] - b_col[j] + li_col[j] for j in heads]
        m_new = [jnp.maximum(bL[j] + m[j], jnp.max(g_col[j], axis=0, keepdims=True)) for j in heads]
        decay = [jnp.exp(bL[j] + m[j] - m_new[j]) for j in heads]
        kw = [kc[j].astype(F32) * jnp.exp(g_col[j] - m_new[j]) for j in heads]
        upd = [jnp.dot(kw[j].T.astype(BF16), vc[j], preferred_element_type=F32) for j in heads]
        for j in heads:
            c_ref[j] = decay[j] * c_ref[j] + upd[j]
            n_ref[j] = decay[j] * n_ref[j] + jnp.sum(kw[j], axis=0, keepdims=True)
            m_ref[j] = m_new[j]
        return carry
    lax.fori_loop(0, S // L, chunk, 0)


def _mlstm(z, zg, conv_qk, gate_bias, g_mlstm3, *, B, S, col_q, col_k, col_v, col_o):
    T = B * S
    dk = M_HEAD_DIM
    L = MLSTM_CHUNK
    HP = M_HEADS
    wd = HP * dk
    TS = min(MLSTM_TIME_BLOCK, S)
    NT = S // TS
    zspec = lambda col: pl.BlockSpec((TS, wd), lambda b, t: (b * NT + t, col // HP))
    return pl.pallas_call(
        functools.partial(_mlstm_body, L=L, CB=min(128, TS), HP=HP),
        grid=(B, NT),
        in_specs=[
            zspec(col_q), zspec(col_k), zspec(col_v), zspec(col_o),
            pl.BlockSpec((TS, LANES), lambda b, t: (b * NT + t, 0)),
            pl.BlockSpec((CONV_WIDTH, wd), lambda b, t: (0, 0)),
            pl.BlockSpec((CONV_WIDTH, wd), lambda b, t: (0, 1)),
            pl.BlockSpec((1, LANES), lambda b, t: (0, 0)),
            pl.BlockSpec((HP, 1, dk), lambda b, t: (0, 0, 0)),
        ],
        out_specs=pl.BlockSpec((TS, wd), lambda b, t: (b * NT + t, 0)),
        out_shape=jax.ShapeDtypeStruct((T, M_WIDTH), BF16),
        scratch_shapes=[
            pltpu.VMEM((TS, wd), BF16), pltpu.VMEM((TS, wd), BF16),
            pltpu.VMEM((TS, LANES), F32), pltpu.VMEM((TS, LANES), F32),
            pltpu.VMEM((8, TS), F32), pltpu.VMEM((8, TS), F32),
            pltpu.VMEM((HP, TS, 1), F32), pltpu.VMEM((HP, TS, 1), F32),
            pltpu.VMEM((HP, 1, TS), F32), pltpu.VMEM((HP, 1, TS), F32),
            pltpu.VMEM((HP, dk, dk), F32), pltpu.VMEM((HP, 1, dk), F32), pltpu.VMEM((HP, 1, 1), F32),
            pltpu.VMEM((16, wd), BF16), pltpu.VMEM((16, wd), BF16),
        ],
        compiler_params=_params("parallel", "arbitrary"),
        name="mlstm",
    )(z, z, z, z, zg, conv_qk, conv_qk, gate_bias, g_mlstm3)


def _swa_body(sink_ref, q_ref, k_ref, v_ref, qa_ref, qb_ref, ka_ref, kb_ref, bd_ref, rep_ref, ones_ref, out_ref,
              kbd_ref, vbd_ref):
    n = pl.program_id(1)
    W = WINDOW
    hd = A_HEAD_DIM
    gw = A_GROUP * hd

    def norm_rope(x, ta, tb):
        x2 = x * x
        x2h = x2.astype(BF16)
        x2l = (x2 - x2h.astype(F32)).astype(BF16)
        ss = (jnp.dot(x2h, bd_ref[...], preferred_element_type=F32)
              + jnp.dot(x2l, bd_ref[...], preferred_element_type=F32))
        r = lax.rsqrt(ss * (1.0 / hd) + EPS)
        ln = lax.broadcasted_iota(jnp.int32, x.shape, 1) % hd
        swapped = jnp.where(ln < hd // 2, pltpu.roll(x, LANES - hd // 2, axis=1), pltpu.roll(x, hd // 2, axis=1))
        return r * (x * ta + swapped * tb)

    lane_head = lax.broadcasted_iota(jnp.int32, (W, gw), 1) // hd
    t_idx = lax.broadcasted_iota(jnp.int32, (W, W), 0)
    k_idx = lax.broadcasted_iota(jnp.int32, (W, W), 1)
    mask_cur = k_idx <= t_idx
    neg = -jnp.inf

    @pl.when(n == 0)
    def _():
        kbd_ref[1] = jnp.zeros(kbd_ref.shape[1:], BF16)
        vbd_ref[1] = jnp.zeros(vbd_ref.shape[1:], BF16)

    def prepare(half, cur_slot):
        rows = slice(half * W, (half + 1) * W)
        qa = qa_ref[rows, :]
        qb = qb_ref[rows, :]
        qp = jnp.concatenate(
            [norm_rope(q_ref[rows, t * LANES:(t + 1) * LANES].astype(F32), qa, qb) for t in range(A_WIDTH // LANES)],
            axis=1).astype(BF16)
        ka = ka_ref[rows, :]
        kb = kb_ref[rows, :]
        kp = jnp.concatenate(
            [norm_rope(k_ref[rows, t * LANES:(t + 1) * LANES].astype(F32), ka, kb)
             for t in range(A_KV_WIDTH // LANES)], axis=1).astype(BF16)
        vv = v_ref[rows, :]
        for j in range(A_KV_HEADS):
            krep = jnp.dot(kp, rep_ref[j], preferred_element_type=F32).astype(BF16)
            vrep = jnp.dot(vv, rep_ref[j], preferred_element_type=F32).astype(BF16)
            for i in range(A_GROUP):
                kbd_ref[cur_slot, j, i * W:(i + 1) * W, :] = jnp.where(lane_head == i, krep, jnp.zeros_like(krep))
                vbd_ref[cur_slot, j, i * W:(i + 1) * W, :] = jnp.where(lane_head == i, vrep, jnp.zeros_like(vrep))
        return qp

    upper = k_idx > t_idx
    blocks = ((0, 1, 0, jnp.logical_and(upper, n > 0)), (1, 0, 1, upper))

    dims_nt = (((1,), (1,)), ((), ()))
    for halves in ((blocks[0],), (blocks[1],)):
        qps = {h: prepare(h, cur) for h, _, cur, _ in halves}
        groups = tuple(range(A_KV_HEADS))
        gsel = [(h, j) for h, _, _, _ in halves for j in groups]
        pairs = [(h, j, i) for h, j in gsel for i in range(A_GROUP)]
        prev_of = {h: p for h, p, _, _ in halves}
        cur_of = {h: c for h, _, c, _ in halves}
        mprev_of = {h: m for h, _, _, m in halves}
        qg = {(h, j): qps[h][:, j * gw:(j + 1) * gw] for h, j in gsel}
        s_prev = {(h, j): lax.dot_general(qg[(h, j)], kbd_ref[prev_of[h], j], dims_nt, preferred_element_type=F32)
                  for h, j in gsel}
        s_cur = {(h, j): lax.dot_general(qg[(h, j)], kbd_ref[cur_of[h], j], dims_nt, preferred_element_type=F32)
                 for h, j in gsel}
        sp = {(h, j, i): jnp.where(mprev_of[h], s_prev[(h, j)][:, i * W:(i + 1) * W], neg) for h, j, i in pairs}
        sc = {(h, j, i): jnp.where(mask_cur, s_cur[(h, j)][:, i * W:(i + 1) * W], neg) for h, j, i in pairs}
        sink = {(h, j, i): sink_ref[j * A_GROUP + i] for h, j, i in pairs}
        mx = {k: jnp.maximum(jnp.max(jnp.maximum(sp[k], sc[k]), axis=1, keepdims=True), sink[k]) for k in pairs}
        pp = {k: jnp.exp(sp[k] - mx[k]).astype(BF16) for k in pairs}
        pc = {k: jnp.exp(sc[k] - mx[k]).astype(BF16) for k in pairs}
        es = {k: jnp.exp(sink[k] - mx[k]) for k in pairs}
        p_prev = {(h, j): jnp.concatenate([pp[(h, j, i)] for i in range(A_GROUP)], axis=1) for h, j in gsel}
        p_cur = {(h, j): jnp.concatenate([pc[(h, j, i)] for i in range(A_GROUP)], axis=1) for h, j in gsel}
        o = {(h, j): jnp.dot(p_prev[(h, j)], vbd_ref[prev_of[h], j], preferred_element_type=F32)
             + jnp.dot(p_cur[(h, j)], vbd_ref[cur_of[h], j], preferred_element_type=F32) for h, j in gsel}
        den = {(h, j): jnp.dot(p_prev[(h, j)], ones_ref[...], preferred_element_type=F32)
               + jnp.dot(p_cur[(h, j)], ones_ref[...], preferred_element_type=F32) for h, j in gsel}
        for h, j in gsel:
            sink_term = jnp.zeros((W, gw), F32)
            for i in range(A_GROUP):
                sink_term = jnp.where(lane_head == i, es[(h, j, i)], sink_term)
            out_ref[h * W:(h + 1) * W, j * gw:(j + 1) * gw] = (o[(h, j)] / (den[(h, j)] + sink_term)).astype(BF16)


def _swa(z, sinks, qa, qb, ka, kb, bd, rep, ones_bd, *, B, S, col_q, col_k, col_v):
    T = B * S
    W2 = 2 * WINDOW
    NB = S // W2
    gw = A_GROUP * A_HEAD_DIM
    tab = lambda: pl.BlockSpec((W2, LANES), lambda b, n: (n, 0))
    return pl.pallas_call(
        _swa_body,
        grid=(B, NB),
        in_specs=[
            pl.BlockSpec(memory_space=pltpu.SMEM),
            pl.BlockSpec((W2, A_WIDTH), lambda b, n: (b * NB + n, col_q)),
            pl.BlockSpec((W2, A_KV_WIDTH), lambda b, n: (b * NB + n, col_k)),
            pl.BlockSpec((W2, A_KV_WIDTH), lambda b, n: (b * NB + n, col_v)),
            tab(), tab(), tab(), tab(),
            pl.BlockSpec((LANES, LANES), lambda b, n: (0, 0)),
            pl.BlockSpec((A_KV_HEADS, gw, gw), lambda b, n: (0, 0, 0)),
            pl.BlockSpec((A_GROUP * WINDOW, gw), lambda b, n: (0, 0)),
        ],
        out_specs=pl.BlockSpec((W2, A_WIDTH), lambda b, n: (b * NB + n, 0)),
        out_shape=jax.ShapeDtypeStruct((T, A_WIDTH), BF16),
        scratch_shapes=[
            pltpu.VMEM((2, A_KV_HEADS, A_GROUP * WINDOW, gw), BF16),
            pltpu.VMEM((2, A_KV_HEADS, A_GROUP * WINDOW, gw), BF16),
        ],
        compiler_params=_params("parallel", "arbitrary"),
        name="swa",
    )(sinks, z, z, z, qa, qb, ka, kb, bd, rep, ones_bd)


def _merge_body(hm_ref, ha_ref, gm_ref, ga_ref, x_ref, wm_ref, wa_ref, wo_ref, gf_ref, rh_ref, rb_ref,
                x1_ref, hf_ref, lg_ref):
    pm = jnp.dot(hm_ref[...], wm_ref[...], preferred_element_type=F32)
    pa = jnp.dot(ha_ref[...], wa_ref[...], preferred_element_type=F32)
    mixed = _sigmoid(gm_ref[...].astype(F32)) * pm + _sigmoid(ga_ref[...].astype(F32)) * pa
    x1 = x_ref[...] + jnp.dot(mixed.astype(BF16), wo_ref[...], preferred_element_type=F32)
    x1_ref[...] = x1
    ms = jnp.mean(x1 * x1, axis=-1, keepdims=True)
    hf = (x1 * lax.rsqrt(ms + EPS)) * gf_ref[...]
    _store_token_rows(hf_ref, hf)
    lg_ref[...] = jnp.dot(hf.astype(BF16), rh_ref[...], preferred_element_type=F32) + rb_ref[...]


def _merge(hm, ha, z, x2, wm, wa, wo, gf, rh, rb, *, bm, col_gm, col_ga):
    T, D = x2.shape
    const = lambda shape: pl.BlockSpec(shape, lambda i: (0,) * len(shape), pipeline_mode=pl.Buffered(1))
    return pl.pallas_call(
        _merge_body,
        grid=(T // bm,),
        in_specs=[
            pl.BlockSpec((bm, M_WIDTH), lambda i: (i, 0)),
            pl.BlockSpec((bm, A_WIDTH), lambda i: (i, 0)),
            pl.BlockSpec((bm, D), lambda i: (i, col_gm)),
            pl.BlockSpec((bm, D), lambda i: (i, col_ga)),
            pl.BlockSpec((bm, D), lambda i: (i, 0)),
            const((M_WIDTH, D)), const((A_WIDTH, D)), const((D, D)), const((1, D)),
            const((D, LANES)), const((1, LANES)),
        ],
        out_specs=[
            pl.BlockSpec((bm, D), lambda i: (i, 0)),
            pl.BlockSpec((bm * (D // (2 * LANES)), LANES), lambda i: (i, 0)),
            pl.BlockSpec((bm, LANES), lambda i: (i, 0)),
        ],
        out_shape=[
            jax.ShapeDtypeStruct((T, D), F32),
            jax.ShapeDtypeStruct((T * (D // (2 * LANES)), LANES), jnp.uint32),
            jax.ShapeDtypeStruct((T, LANES), F32),
        ],
        compiler_params=_params("parallel"),
        name="merge",
    )(hm, ha, z, z, x2, wm, wa, wo, gf, rh, rb)


def _moe_body(blk_e_ref, first_ref, nxt_ref, nused_ref, tok_ref, hf_hbm, wg_hbm, wu_hbm, wd_hbm, y_ref,
              xbuf, xs_ref, stage_g, stage_u, stage_d, wg_ref, wu_ref, wd_ref, sem, wsem, *, R, CB):
    i = pl.program_id(0)
    nused = nused_ref[0]
    sub = xs_ref.shape[1] // (2 * LANES)

    def weight_copies(e):
        return (pltpu.make_async_copy(wg_hbm.at[e], stage_g, wsem.at[0]),
                pltpu.make_async_copy(wu_hbm.at[e], stage_u, wsem.at[1]),
                pltpu.make_async_copy(wd_hbm.at[e], stage_d, wsem.at[2]))

    def cast(src, dst):
        def rows(r, carry):
            sl = pl.ds(pl.multiple_of(r * CB, CB), CB)
            dst[sl, :] = src[sl, :].astype(BF16)
            return carry
        lax.fori_loop(0, src.shape[0] // CB, rows, 0)

    def start_gather(blk, slot):
        base = blk * R

        for r in range(R):
            src = pl.multiple_of(tok_ref[base + r] * sub, sub)
            pltpu.make_async_copy(hf_hbm.at[pl.ds(src, sub)], xbuf.at[slot, pl.ds(r * sub, sub)],
                                  sem.at[slot]).start(priority=r % 2)

    def wait_gather(slot):
        pltpu.make_async_copy(xbuf.at[slot], xbuf.at[slot], sem.at[slot]).wait()

    nslots = xbuf.shape[0]
    ahead = nslots - 1
    last_blk = pl.num_programs(0) - 1
    slot = i % nslots

    @pl.when(i == 0)
    def _():
        for a in range(ahead):
            start_gather(a, a)
        for c in weight_copies(blk_e_ref[0]):
            c.start(priority=1)

    @pl.when(i < nused)
    def _():
        @pl.when(first_ref[i] == 1)
        def _():
            for c in weight_copies(blk_e_ref[i]):
                c.wait()
            cast(stage_g, wg_ref)
            cast(stage_u, wu_ref)
            cast(stage_d, wd_ref)

            @pl.when(nxt_ref[i] >= 0)
            def _():
                for c in weight_copies(nxt_ref[i]):
                    c.start(priority=1)

        wait_gather(slot)
        for s in range(sub):
            lo, hi = _load_token_rows(xbuf.at[slot], R, sub, s)
            xs_ref[:, 2 * s * LANES:(2 * s + 1) * LANES] = lo.astype(BF16)
            xs_ref[:, (2 * s + 1) * LANES:(2 * s + 2) * LANES] = hi.astype(BF16)
        start_gather(jnp.minimum(i + ahead, last_blk), (i + ahead) % nslots)
        xv = xs_ref[...]
        g = jnp.dot(xv, wg_ref[...], preferred_element_type=F32)
        u = jnp.dot(xv, wu_ref[...], preferred_element_type=F32)
        hmid = ((g * _sigmoid(g)) * u).astype(BF16)
        _store_token_rows(y_ref, jnp.dot(hmid, wd_ref[...], preferred_element_type=F32))

        @pl.when(i == nused - 1)
        def _():
            for a in range(1, nslots):
                wait_gather((i + a) % nslots)


def _moe(blk_e, first, nxt, nused, row_tok, hf, wg, wu, wd, *, R):
    E, D, F = wg.shape
    sub = D // (2 * LANES)
    n_blocks = blk_e.shape[0]
    hbm = lambda: pl.BlockSpec(memory_space=pl.ANY)
    grid_spec = pltpu.PrefetchScalarGridSpec(
        num_scalar_prefetch=5,
        grid=(n_blocks,),
        in_specs=[hbm(), hbm(), hbm(), hbm()],
        out_specs=pl.BlockSpec((R * sub, LANES), lambda i, be, fi, nx, nu, tk: (jnp.minimum(i, nu[0] - 1), 0)),
        scratch_shapes=[
            pltpu.VMEM((MOE_GATHER_SLOTS, R * sub, LANES), jnp.uint32), pltpu.VMEM((R, D), BF16),
            pltpu.VMEM((D, F), F32), pltpu.VMEM((D, F), F32), pltpu.VMEM((F, D), F32),
            pltpu.VMEM((D, F), BF16), pltpu.VMEM((D, F), BF16), pltpu.VMEM((F, D), BF16),
            pltpu.SemaphoreType.DMA((MOE_GATHER_SLOTS,)), pltpu.SemaphoreType.DMA((3,)),
        ],
    )
    return pl.pallas_call(
        functools.partial(_moe_body, R=R, CB=128),
        grid_spec=grid_spec,
        out_shape=jax.ShapeDtypeStruct((n_blocks * R * sub, LANES), jnp.uint32),
        compiler_params=_params("arbitrary"),
        name="moe",
    )(blk_e, first, nxt, nused, row_tok, hf, wg, wu, wd)


def _combine_body(dest_ref, x1_ref, w_ref, yr_hbm, out_ref, ybuf, sem, *, R):
    i = pl.program_id(0)
    nsteps = pl.num_programs(0)

    sub = out_ref.shape[1] // (2 * LANES)

    def start_gather(step, slot):
        base = step * (R * TOP_K)

        def row(r, carry):
            dst = pl.multiple_of(r * sub, sub)
            for k in range(TOP_K):
                src = pl.multiple_of(dest_ref[base + r * TOP_K + k] * sub, sub)
                pltpu.make_async_copy(yr_hbm.at[pl.ds(src, sub)], ybuf.at[slot, k, pl.ds(dst, sub)],
                                      sem.at[slot]).start(priority=k % 2)
            return carry
        lax.fori_loop(0, R, row, 0, unroll=4)

    slot = i % 2

    @pl.when(i == 0)
    def _():
        start_gather(0, 0)

    pltpu.make_async_copy(ybuf.at[slot], ybuf.at[slot], sem.at[slot]).wait()

    @pl.when(i + 1 < nsteps)
    def _():
        start_gather(i + 1, 1 - slot)

    w = w_ref[...]
    for s in range(sub):
        halves0 = _load_token_rows(ybuf.at[slot, 0], R, sub, s)
        halves1 = _load_token_rows(ybuf.at[slot, 1], R, sub, s)
        for half in range(2):
            cols = slice((2 * s + half) * LANES, (2 * s + half + 1) * LANES)
            out_ref[:, cols] = x1_ref[:, cols] + (halves0[half] * w[:, 0:1] + halves1[half] * w[:, 1:2])


def _combine(dest, x1, gate_w, yr, *, R):
    T, D = x1.shape
    grid_spec = pltpu.PrefetchScalarGridSpec(
        num_scalar_prefetch=1,
        grid=(T // R,),
        in_specs=[
            pl.BlockSpec((R, D), lambda i, d: (i, 0)),
            pl.BlockSpec((R, TOP_K), lambda i, d: (i, 0)),
            pl.BlockSpec(memory_space=pl.ANY),
        ],
        out_specs=pl.BlockSpec((R, D), lambda i, d: (i, 0)),
        scratch_shapes=[pltpu.VMEM((2, TOP_K, R * (D // (2 * LANES)), LANES), jnp.uint32),
                        pltpu.SemaphoreType.DMA((2,))],
    )
    return pl.pallas_call(
        functools.partial(_combine_body, R=R),
        grid_spec=grid_spec,
        out_shape=jax.ShapeDtypeStruct((T, D), F32),
        compiler_params=_params("arbitrary"),
        name="combine",
    )(dest, x1, gate_w, yr)


def _route_body(lg_ref, tri_ref, ids_ref, gate_ref, cnt_ref):
    @pl.when(pl.program_id(0) == 0)
    def _():
        cnt_ref[...] = jnp.zeros_like(cnt_ref)

    lg = lg_ref[...]
    shape = lg.shape
    lane = lax.broadcasted_iota(jnp.int32, shape, 1)
    big = jnp.int32(LANES)

    def softmax_masked(mask):
        v = jnp.where(mask, lg, -jnp.inf)
        u = jnp.exp(v - jnp.max(v, axis=1, keepdims=True))
        return jnp.where(mask, u / jnp.sum(u, axis=1, keepdims=True), -1.0)

    def top1(p):
        best = jnp.max(p, axis=1, keepdims=True)
        idx = jnp.min(jnp.where(p == best, lane, big), axis=1, keepdims=True)
        return best, idx

    g_p, g_lane = top1(softmax_masked(lane < N_GROUPS))
    grp_of_lane = lax.shift_right_arithmetic(lane - N_GROUPS, jnp.int32(3))
    in_grp = jnp.logical_and(lane >= N_GROUPS, grp_of_lane == g_lane)
    in_grp = jnp.logical_and(in_grp, lane < N_GROUPS + N_EXPERTS)
    pe = softmax_masked(in_grp)
    p1, l1 = top1(pe)
    p2, l2 = top1(jnp.where(lane == l1, -1.0, pe))
    tot = p1 + p2
    gate1 = g_p * (p1 / tot)
    gate2 = g_p * (p2 / tot)

    hot1 = lane == l1
    hot2 = lane == l2
    hot = jnp.logical_or(hot1, hot2)
    before = jnp.dot(tri_ref[...], jnp.where(hot, 1.0, 0.0).astype(BF16), preferred_element_type=F32) + cnt_ref[...]
    rank1 = jnp.sum(jnp.where(hot1, before, 0.0), axis=1, keepdims=True)
    rank2 = jnp.sum(jnp.where(hot2, before, 0.0), axis=1, keepdims=True)
    cnt_ref[...] = cnt_ref[...] + jnp.sum(jnp.where(hot, 1.0, 0.0), axis=0, keepdims=True)

    ids = jnp.where(lane == 0, l1 - N_GROUPS, jnp.where(lane == 1, l2 - N_GROUPS, 0))
    ids = jnp.where(lane == 2, rank1.astype(jnp.int32), jnp.where(lane == 3, rank2.astype(jnp.int32), ids))
    ids_ref[...] = ids
    gate_ref[...] = jnp.where(lane == 0, gate1, jnp.where(lane == 1, gate2, 0.0))


def _route_tokens(logits, *, tb):
    T = logits.shape[0]
    r = jnp.arange(tb)
    tri = (r[None, :] < r[:, None]).astype(BF16)
    return pl.pallas_call(
        _route_body,
        grid=(T // tb,),
        in_specs=[
            pl.BlockSpec((tb, LANES), lambda i: (i, 0)),
            pl.BlockSpec((tb, tb), lambda i: (0, 0)),
        ],
        out_specs=[
            pl.BlockSpec((tb, LANES), lambda i: (i, 0)),
            pl.BlockSpec((tb, LANES), lambda i: (i, 0)),
            pl.BlockSpec((1, LANES), lambda i: (0, 0)),
        ],
        out_shape=[
            jax.ShapeDtypeStruct((T, LANES), jnp.int32),
            jax.ShapeDtypeStruct((T, LANES), F32),
            jax.ShapeDtypeStruct((1, LANES), F32),
        ],
        compiler_params=_params("arbitrary"),
        name="route",
    )(logits, tri)


def _route(logits, R):
    T = logits.shape[0]
    ids, gates, cnt = _route_tokens(logits, tb=512 if T % 512 == 0 else T)
    gate = gates[:, 0:TOP_K]
    M = T * TOP_K
    eid_f = ids[:, 0:TOP_K].reshape(M)
    rank = ids[:, TOP_K:2 * TOP_K].reshape(M)
    counts = cnt[0, N_GROUPS:N_GROUPS + N_EXPERTS].astype(jnp.int32)
    padded = (counts + R - 1) // R * R
    pend = jnp.cumsum(padded)
    pstart = pend - padded
    dest = (pstart[eid_f] + rank).astype(jnp.int32)
    n_blocks = -(-M // R) + N_EXPERTS
    tok_f = jnp.arange(M, dtype=jnp.int32) // TOP_K
    row_tok = jnp.zeros((n_blocks * R,), jnp.int32).at[dest].set(
        tok_f, unique_indices=True, mode="promise_in_bounds")
    blk_start = jnp.arange(n_blocks, dtype=jnp.int32) * R
    blk_e = jnp.sum((pend[None, :] <= blk_start[:, None]).astype(jnp.int32), axis=1)
    nused = (pend[-1] // R).astype(jnp.int32)
    last_e = blk_e[jnp.maximum(nused - 1, 0)]
    blk_e = jnp.where(jnp.arange(n_blocks) < nused, blk_e, last_e)
    blk_e = jnp.minimum(blk_e, N_EXPERTS - 1).astype(jnp.int32)
    e_idx = jnp.arange(N_EXPERTS, dtype=jnp.int32)
    cand = jnp.where(counts > 0, e_idx, N_EXPERTS)
    sfx = lax.cummin(cand[::-1])[::-1]
    nxt_of_e = jnp.concatenate([sfx[1:], jnp.full((1,), N_EXPERTS, jnp.int32)])
    nxt_of_e = jnp.where(nxt_of_e < N_EXPERTS, nxt_of_e, -1)
    nxt = nxt_of_e[blk_e].astype(jnp.int32)
    changed = jnp.concatenate([jnp.ones((1,), bool), blk_e[1:] != blk_e[:-1]])
    first = jnp.logical_and(changed, jnp.arange(n_blocks) < nused).astype(jnp.int32)
    return gate, dest, row_tok, blk_e, first, nxt, nused.reshape(1)


def _rope_tables(S, gain, scale):
    half = A_HEAD_DIM // 2
    freqs = ROPE_THETA ** (-jnp.arange(half, dtype=F32) / half)
    ang = jnp.arange(S, dtype=F32)[:, None] * freqs[None, :]
    cos = jnp.cos(ang)
    sin = jnp.sin(ang)
    g1, g2 = gain[:half], gain[half:]
    a_head = jnp.concatenate([cos * g1, cos * g2], axis=1)
    b_head = jnp.concatenate([-sin * g2, sin * g1], axis=1)
    reps = LANES // A_HEAD_DIM
    return jnp.tile(a_head, (1, reps)) * scale, jnp.tile(b_head, (1, reps)) * scale


def kernel(x, g_mix, w_in, conv_qk, b_igate, b_fgate, g_mlstm, g_q, g_k, sinks, w_proj_m, w_proj_a, w_out,
           g_ffn, w_group, b_group, w_expert, b_expert, w_gate, w_up, w_down):
    B, S, D = x.shape
    T = B * S
    depth = g_mix.shape[0]
    xf = x.reshape(T, D)

    sizes = (M_WIDTH, M_WIDTH, M_WIDTH, M_WIDTH, M_HEADS, M_HEADS, A_WIDTH, A_KV_WIDTH, A_KV_WIDTH, D, D)
    offs = [0]
    for s_ in sizes:
        offs.append(offs[-1] + s_)
    seg = lambda w, idx: w[:, offs[idx]:offs[idx + 1]]
    order = (9, 10, 0, 1, 2, 3, 6, 7, 8)
    new_off = {}
    acc = 0
    for idx in order:
        new_off[idx] = acc
        acc += sizes[idx]
    dk = M_HEAD_DIM

    lane128 = jnp.arange(LANES)
    bd = (lane128[:, None] // A_HEAD_DIM == lane128[None, :] // A_HEAD_DIM).astype(BF16)
    gw = A_GROUP * A_HEAD_DIM
    r_idx = jnp.arange(gw)
    rep = jnp.stack([(r_idx[:, None] == (j * A_HEAD_DIM + r_idx[None, :] % A_HEAD_DIM)).astype(BF16)
                     for j in range(A_KV_HEADS)])
    ones_bd = (jnp.arange(A_GROUP * WINDOW)[:, None] // WINDOW == r_idx[None, :] // A_HEAD_DIM).astype(BF16)

    for l in range(depth):
        w16 = w_in[l].astype(BF16)
        w_rep = jnp.concatenate([seg(w16, idx) for idx in order], axis=1)
        w_gates = jnp.concatenate([seg(w16, 4), seg(w16, 5)], axis=1)
        w_gates = jnp.pad(w_gates, ((0, 0), (0, LANES - 2 * M_HEADS)))
        z, zg = _in_proj(xf, g_mix[l][None, :], w_rep, w_gates, bm=1024 if T % 1024 == 0 else T,
                         bn=w_rep.shape[1] // 4)

        gate_bias = jnp.pad(jnp.concatenate([b_igate[l], b_fgate[l]]), (0, LANES - 2 * M_HEADS))[None, :]
        hm = _mlstm(z, zg, conv_qk[l], gate_bias, g_mlstm[l][:, None, :], B=B, S=S,
                    col_q=new_off[0] // dk, col_k=new_off[1] // dk, col_v=new_off[2] // dk, col_o=new_off[3] // dk)

        qa, qb = _rope_tables(S, g_q[l], A_HEAD_DIM ** -0.5)
        ka, kb = _rope_tables(S, g_k[l], 1.0)
        ha = _swa(z, sinks[l], qa, qb, ka, kb, bd, rep, ones_bd, B=B, S=S,
                  col_q=new_off[6] // A_WIDTH, col_k=new_off[7] // A_KV_WIDTH, col_v=new_off[8] // A_KV_WIDTH)

        w_router = jnp.pad(jnp.concatenate([w_group[l], w_expert[l]], axis=1),
                           ((0, 0), (0, LANES - N_GROUPS - N_EXPERTS)))
        r_b = jnp.pad(jnp.concatenate([b_group[l], b_expert[l]]), (0, LANES - N_GROUPS - N_EXPERTS))[None, :]
        x1, hf, logits = _merge(hm, ha, z, xf, w_proj_m[l].astype(BF16), w_proj_a[l].astype(BF16),
                                w_out[l].astype(BF16), g_ffn[l][None, :], w_router.astype(BF16), r_b,
                                bm=256, col_gm=new_off[9] // D, col_ga=new_off[10] // D)

        gate, dest, row_tok, blk_e, first, nxt, nused = _route(logits, MOE_ROWS)
        yr = _moe(blk_e, first, nxt, nused, row_tok, hf, w_gate[l], w_up[l], w_down[l], R=MOE_ROWS)
        xf = _combine(dest, x1, gate, yr, R=COMBINE_ROWS)
    return xf.reshape(B, S, D)
```

```python
import functools

import jax
import jax.numpy as jnp
from jax import lax
from jax.experimental import pallas as pl
from jax.experimental.pallas import tpu as pltpu

F32 = jnp.float32
BF16 = jnp.bfloat16
EPS = 1e-6
LANES = 128
VMEM_LIMIT = 56 * 1024 * 1024

M_HEADS = 4
M_HEAD_DIM = 256
M_WIDTH = M_HEADS * M_HEAD_DIM
CONV_WIDTH = 4
A_HEADS = 16
A_KV_HEADS = 4
A_GROUP = A_HEADS // A_KV_HEADS
A_HEAD_DIM = 64
A_WIDTH = A_HEADS * A_HEAD_DIM
A_KV_WIDTH = A_KV_HEADS * A_HEAD_DIM
WINDOW = 128
ROPE_THETA = 10000.0
N_GROUPS = 8
EXPERTS_PER_GROUP = 8
N_EXPERTS = N_GROUPS * EXPERTS_PER_GROUP
TOP_K = 2

MLSTM_CHUNK = 128
MLSTM_TIME_BLOCK = 512
MOE_ROWS = 256
MOE_GATHER_SLOTS = 4
COMBINE_ROWS = 256


def _sigmoid(v):
    return 1.0 / (1.0 + jnp.exp(-v))


def _params(*sem):
    return pltpu.CompilerParams(dimension_semantics=sem, vmem_limit_bytes=VMEM_LIMIT)


def _pack_bf16_pair(lo, hi):
    lo_b = lax.bitcast_convert_type(lo.astype(BF16).astype(F32), jnp.uint32)
    hi_b = lax.bitcast_convert_type(hi.astype(BF16).astype(F32), jnp.uint32)
    return lax.shift_right_logical(lo_b, jnp.uint32(16)) | hi_b


def _unpack_bf16_pair(word):
    lo = lax.bitcast_convert_type(lax.shift_left(word, jnp.uint32(16)), F32)
    hi = lax.bitcast_convert_type(word & jnp.uint32(0xFFFF0000), F32)
    return lo, hi


def _store_token_rows(ref, val):
    n, d = val.shape
    sub = d // (2 * LANES)
    for s in range(sub):
        c = 2 * s * LANES
        ref[pl.ds(s, n, stride=sub), :] = _pack_bf16_pair(val[:, c:c + LANES], val[:, c + LANES:c + 2 * LANES])


def _load_token_rows(ref, n, sub, s):
    return _unpack_bf16_pair(ref[pl.ds(s, n, stride=sub), :])


def _inproj_body(x_ref, g_ref, w_ref, wgate_ref, z_ref, zg_ref, hn_ref, *, sub):
    bm = x_ref.shape[0]

    @pl.when(pl.program_id(1) == 0)
    def _():
        def rows(r, carry):
            sl = pl.ds(pl.multiple_of(r * sub, sub), sub)
            xv = x_ref[sl, :]
            ms = jnp.mean(xv * xv, axis=-1, keepdims=True)
            hn_ref[sl, :] = ((xv * lax.rsqrt(ms + EPS)) * g_ref[...]).astype(BF16)
            return carry
        lax.fori_loop(0, bm // sub, rows, 0)
        zg_ref[...] = jnp.dot(hn_ref[...], wgate_ref[...], preferred_element_type=F32)

    z_ref[...] = jnp.dot(hn_ref[...], w_ref[...], preferred_element_type=F32).astype(BF16)


def _in_proj(x2, g, w, wgate, *, bm, bn):
    T, D = x2.shape
    N = w.shape[1]
    return pl.pallas_call(
        functools.partial(_inproj_body, sub=128),
        grid=(T // bm, N // bn),
        in_specs=[
            pl.BlockSpec((bm, D), lambda i, j: (i, 0)),
            pl.BlockSpec((1, D), lambda i, j: (0, 0)),
            pl.BlockSpec((D, bn), lambda i, j: (0, j)),
            pl.BlockSpec((D, LANES), lambda i, j: (0, 0)),
        ],
        out_specs=[
            pl.BlockSpec((bm, bn), lambda i, j: (i, j)),
            pl.BlockSpec((bm, LANES), lambda i, j: (i, 0)),
        ],
        out_shape=[
            jax.ShapeDtypeStruct((T, N), BF16),
            jax.ShapeDtypeStruct((T, LANES), F32),
        ],
        scratch_shapes=[pltpu.VMEM((bm, D), BF16)],
        compiler_params=_params("parallel", "arbitrary"),
        name="in_proj",
    )(x2, g, w, wgate)


def _mlstm_body(q_ref, k_ref, v_ref, o_ref, zg_ref, cq_ref, ck_ref, bias_ref, gn_ref, out_ref,
                qs_ref, ks_ref, colli_ref, colb_ref, rowli_ref, rowb_ref, cli_ref, cb_ref, rli_ref, rb_ref,
                c_ref, n_ref, m_ref, hq_ref, hk_ref, *, L, CB, HP):
    S = q_ref.shape[0]
    dk = q_ref.shape[1] // HP

    @pl.when(pl.program_id(1) == 0)
    def _():
        c_ref[...] = jnp.zeros_like(c_ref)
        n_ref[...] = jnp.zeros_like(n_ref)
        m_ref[...] = jnp.zeros_like(m_ref)
        hq_ref[...] = jnp.zeros_like(hq_ref)
        hk_ref[...] = jnp.zeros_like(hk_ref)

    G = zg_ref[...] + bias_ref[...]
    lf = jnp.minimum(G, 0.0) - jnp.log1p(jnp.exp(-jnp.abs(G)))
    pos = lax.broadcasted_iota(jnp.int32, (S, LANES), 0) % L
    bc = lf
    sh = 1
    while sh < L:
        bc = bc + jnp.where(pos >= sh, pltpu.roll(bc, sh, axis=0), 0.0)
        sh *= 2
    colli_ref[...] = G
    colb_ref[...] = bc
    for p in range(S // LANES):
        sl = slice(p * LANES, (p + 1) * LANES)
        rowli_ref[:, sl] = G[sl, :].T[0:8, :]
        rowb_ref[:, sl] = bc[sl, :].T[0:8, :]

    lane = lax.broadcasted_iota(jnp.int32, (S, LANES), 1)
    sub = lax.broadcasted_iota(jnp.int32, (8, S), 0)
    for j in range(HP):
        cli_ref[j] = jnp.sum(jnp.where(lane == j, colli_ref[...], 0.0), axis=1, keepdims=True)
        cb_ref[j] = jnp.sum(jnp.where(lane == j + HP, colb_ref[...], 0.0), axis=1, keepdims=True)
        rli_ref[j] = jnp.sum(jnp.where(sub == j, rowli_ref[...], 0.0), axis=0, keepdims=True)
        rb_ref[j] = jnp.sum(jnp.where(sub == j + HP, rowb_ref[...], 0.0), axis=0, keepdims=True)

    def conv_silu(src_ref, w_ref, dst_ref, halo_ref, scale):
        w = w_ref[...]

        def taps(xs, wj):
            y = xs[3] * wj[0:1, :]
            y = y + xs[2] * wj[1:2, :]
            y = y + xs[1] * wj[2:3, :]
            y = y + xs[0] * wj[3:4, :]
            return ((y * _sigmoid(y)) * scale).astype(BF16)

        for j in range(HP):
            cj = slice(j * dk, (j + 1) * dk)
            wj = w[:, cj]

            def chunk(r, carry, cj=cj, wj=wj):
                r0 = pl.multiple_of(r * CB, CB)
                cur = src_ref[pl.ds(r0, CB), cj].astype(F32)
                dst_ref[pl.ds(r0, CB), cj] = taps([cur] + [pltpu.roll(cur, d, axis=0) for d in (1, 2, 3)], wj)
                p0 = pl.multiple_of(jnp.maximum(r0 - 16, 0), 16)
                prev = jnp.where(r > 0, src_ref[pl.ds(p0, 16), cj], halo_ref[:, cj]).astype(F32)[8:16, :]
                both = jnp.concatenate([prev, cur[0:16, :]], axis=0)
                dst_ref[pl.ds(r0, 16), cj] = taps(
                    [both[8:24, :]] + [pltpu.roll(both, d, axis=0)[8:24, :] for d in (1, 2, 3)], wj)
                return carry
            lax.fori_loop(0, S // CB, chunk, 0)
        halo_ref[...] = src_ref[S - 16:S, :]

    conv_silu(q_ref, cq_ref, qs_ref, hq_ref, 1.0)
    conv_silu(k_ref, ck_ref, ks_ref, hk_ref, float(dk) ** -0.5)

    t_idx = lax.broadcasted_iota(jnp.int32, (L, L), 0)
    s_idx = lax.broadcasted_iota(jnp.int32, (L, L), 1)
    causal = s_idx <= t_idx

    heads = range(HP)
    cols = [slice(j * dk, (j + 1) * dk) for j in heads]
    dims_nt = (((1,), (1,)), ((), ()))

    def chunk(c, carry):
        rows = pl.ds(pl.multiple_of(c * L, L), L)
        qc = [qs_ref[rows, cols[j]] for j in heads]
        kc = [ks_ref[rows, cols[j]] for j in heads]
        vc = [v_ref[rows, cols[j]] for j in heads]
        b_col = [cb_ref[j, rows, :] for j in heads]
        li_col = [cli_ref[j, rows, :] for j in heads]
        li_row = [rli_ref[j, :, rows] for j in heads]
        b_row = [rb_ref[j, :, rows] for j in heads]
        m = [m_ref[j] for j in heads]
        a = [b_col[j] + m[j] for j in heads]
        D = [jnp.where(causal, b_col[j] - b_row[j] + li_row[j], -jnp.inf) for j in heads]
        m_t = [jnp.maximum(a[j], jnp.max(D[j], axis=1, keepdims=True)) for j in heads]
        w_inter = [jnp.exp(a[j] - m_t[j]) for j in heads]
        s_qk = [lax.dot_general(qc[j], kc[j], dims_nt, preferred_element_type=F32) for j in heads]
        P = [s_qk[j] * jnp.exp(D[j] - m_t[j]) for j in heads]
        inter = [jnp.dot(qc[j], c_ref[j].astype(BF16), preferred_element_type=F32) for j in heads]
        intra = [jnp.dot(P[j].astype(BF16), vc[j], preferred_element_type=F32) for j in heads]
        num = [w_inter[j] * inter[j] + intra[j] for j in heads]
        qn = [w_inter[j] * jnp.sum(qc[j].astype(F32) * n_ref[j], axis=1, keepdims=True)
              + jnp.sum(P[j], axis=1, keepdims=True) for j in heads]
        den = [jnp.maximum(jnp.abs(qn[j]), jnp.exp(-m_t[j])) for j in heads]
        hh = [num[j] / den[j] for j in heads]
        ms = [jnp.mean(hh[j] * hh[j], axis=1, keepdims=True) for j in heads]
        hn = [(hh[j] * lax.rsqrt(ms[j] + EPS)) * gn_ref[j] for j in heads]
        for j in heads:
            out_ref[rows, cols[j]] = (hn[j] * _sigmoid(o_ref[rows, cols[j]].astype(F32))).astype(BF16)
        bL = [b_row[j][:, L - 1:L] for j in heads]
        g_col = [bL[j] - b_col[j] + li_col[j] for j in heads]
        m_new = [jnp.maximum(bL[j] + m[j], jnp.max(g_col[j], axis=0, keepdims=True)) for j in heads]
        decay = [jnp.exp(bL[j] + m[j] - m_new[j]) for j in heads]
        kw = [kc[j].astype(F32) * jnp.exp(g_col[j] - m_new[j]) for j in heads]
        upd = [jnp.dot(kw[j].T.astype(BF16), vc[j], preferred_element_type=F32) for j in heads]
        for j in heads:
            c_ref[j] = decay[j] * c_ref[j] + upd[j]
            n_ref[j] = decay[j] * n_ref[j] + jnp.sum(kw[j], axis=0, keepdims=True)
            m_ref[j] = m_new[j]
        return carry
    lax.fori_loop(0, S // L, chunk, 0)


def _mlstm(z, zg, conv_qk, gate_bias, g_mlstm3, *, B, S, col_q, col_k, col_v, col_o):
    T = B * S
    dk = M_HEAD_DIM
    L = MLSTM_CHUNK
    HP = M_HEADS
    wd = HP * dk
    TS = min(MLSTM_TIME_BLOCK, S)
    NT = S // TS
    zspec = lambda col: pl.BlockSpec((TS, wd), lambda b, t: (b * NT + t, col // HP))
    return pl.pallas_call(
        functools.partial(_mlstm_body, L=L, CB=min(128, TS), HP=HP),
        grid=(B, NT),
        in_specs=[
            zspec(col_q), zspec(col_k), zspec(col_v), zspec(col_o),
            pl.BlockSpec((TS, LANES), lambda b, t: (b * NT + t, 0)),
            pl.BlockSpec((CONV_WIDTH, wd), lambda b, t: (0, 0)),
            pl.BlockSpec((CONV_WIDTH, wd), lambda b, t: (0, 1)),
            pl.BlockSpec((1, LANES), lambda b, t: (0, 0)),
            pl.BlockSpec((HP, 1, dk), lambda b, t: (0, 0, 0)),
        ],
        out_specs=pl.BlockSpec((TS, wd), lambda b, t: (b * NT + t, 0)),
        out_shape=jax.ShapeDtypeStruct((T, M_WIDTH), BF16),
        scratch_shapes=[
            pltpu.VMEM((TS, wd), BF16), pltpu.VMEM((TS, wd), BF16),
            pltpu.VMEM((TS, LANES), F32), pltpu.VMEM((TS, LANES), F32),
            pltpu.VMEM((8, TS), F32), pltpu.VMEM((8, TS), F32),
            pltpu.VMEM((HP, TS, 1), F32), pltpu.VMEM((HP, TS, 1), F32),
            pltpu.VMEM((HP, 1, TS), F32), pltpu.VMEM((HP, 1, TS), F32),
            pltpu.VMEM((HP, dk, dk), F32), pltpu.VMEM((HP, 1, dk), F32), pltpu.VMEM((HP, 1, 1), F32),
            pltpu.VMEM((16, wd), BF16), pltpu.VMEM((16, wd), BF16),
        ],
        compiler_params=_params("parallel", "arbitrary"),
        name="mlstm",
    )(z, z, z, z, zg, conv_qk, conv_qk, gate_bias, g_mlstm3)


def _swa_body(sink_ref, q_ref, k_ref, v_ref, qa_ref, qb_ref, ka_ref, kb_ref, bd_ref, rep_ref, ones_ref, out_ref,
              kbd_ref, vbd_ref):
    n = pl.program_id(1)
    W = WINDOW
    hd = A_HEAD_DIM
    gw = A_GROUP * hd

    def norm_rope(x, ta, tb):
        x2 = x * x
        x2h = x2.astype(BF16)
        x2l = (x2 - x2h.astype(F32)).astype(BF16)
        ss = (jnp.dot(x2h, bd_ref[...], preferred_element_type=F32)
              + jnp.dot(x2l, bd_ref[...], preferred_element_type=F32))
        r = lax.rsqrt(ss * (1.0 / hd) + EPS)
        ln = lax.broadcasted_iota(jnp.int32, x.shape, 1) % hd
        swapped = jnp.where(ln < hd // 2, pltpu.roll(x, LANES - hd // 2, axis=1), pltpu.roll(x, hd // 2, axis=1))
        return r * (x * ta + swapped * tb)

    lane_head = lax.broadcasted_iota(jnp.int32, (W, gw), 1) // hd
    t_idx = lax.broadcasted_iota(jnp.int32, (W, W), 0)
    k_idx = lax.broadcasted_iota(jnp.int32, (W, W), 1)
    mask_cur = k_idx <= t_idx
    neg = -jnp.inf

    @pl.when(n == 0)
    def _():
        kbd_ref[1] = jnp.zeros(kbd_ref.shape[1:], BF16)
        vbd_ref[1] = jnp.zeros(vbd_ref.shape[1:], BF16)

    def prepare(half, cur_slot):
        rows = slice(half * W, (half + 1) * W)
        qa = qa_ref[rows, :]
        qb = qb_ref[rows, :]
        qp = jnp.concatenate(
            [norm_rope(q_ref[rows, t * LANES:(t + 1) * LANES].astype(F32), qa, qb) for t in range(A_WIDTH // LANES)],
            axis=1).astype(BF16)
        ka = ka_ref[rows, :]
        kb = kb_ref[rows, :]
        kp = jnp.concatenate(
            [norm_rope(k_ref[rows, t * LANES:(t + 1) * LANES].astype(F32), ka, kb)
             for t in range(A_KV_WIDTH // LANES)], axis=1).astype(BF16)
        vv = v_ref[rows, :]
        for j in range(A_KV_HEADS):
            krep = jnp.dot(kp, rep_ref[j], preferred_element_type=F32).astype(BF16)
            vrep = jnp.dot(vv, rep_ref[j], preferred_element_type=F32).astype(BF16)
            for i in range(A_GROUP):
                kbd_ref[cur_slot, j, i * W:(i + 1) * W, :] = jnp.where(lane_head == i, krep, jnp.zeros_like(krep))
                vbd_ref[cur_slot, j, i * W:(i + 1) * W, :] = jnp.where(lane_head == i, vrep, jnp.zeros_like(vrep))
        return qp

    upper = k_idx > t_idx
    blocks = ((0, 1, 0, jnp.logical_and(upper, n > 0)), (1, 0, 1, upper))

    dims_nt = (((1,), (1,)), ((), ()))
    for halves in ((blocks[0],), (blocks[1],)):
        qps = {h: prepare(h, cur) for h, _, cur, _ in halves}
        groups = tuple(range(A_KV_HEADS))
        gsel = [(h, j) for h, _, _, _ in halves for j in groups]
        pairs = [(h, j, i) for h, j in gsel for i in range(A_GROUP)]
        prev_of = {h: p for h, p, _, _ in halves}
        cur_of = {h: c for h, _, c, _ in halves}
        mprev_of = {h: m for h, _, _, m in halves}
        qg = {(h, j): qps[h][:, j * gw:(j + 1) * gw] for h, j in gsel}
        s_prev = {(h, j): lax.dot_general(qg[(h, j)], kbd_ref[prev_of[h], j], dims_nt, preferred_element_type=F32)
                  for h, j in gsel}
        s_cur = {(h, j): lax.dot_general(qg[(h, j)], kbd_ref[cur_of[h], j], dims_nt, preferred_element_type=F32)
                 for h, j in gsel}
        sp = {(h, j, i): jnp.where(mprev_of[h], s_prev[(h, j)][:, i * W:(i + 1) * W], neg) for h, j, i in pairs}
        sc = {(h, j, i): jnp.where(mask_cur, s_cur[(h, j)][:, i * W:(i + 1) * W], neg) for h, j, i in pairs}
        sink = {(h, j, i): sink_ref[j * A_GROUP + i] for h, j, i in pairs}
        mx = {k: jnp.maximum(jnp.max(jnp.maximum(sp[k], sc[k]), axis=1, keepdims=True), sink[k]) for k in pairs}
        pp = {k: jnp.exp(sp[k] - mx[k]).astype(BF16) for k in pairs}
        pc = {k: jnp.exp(sc[k] - mx[k]).astype(BF16) for k in pairs}
        es = {k: jnp.exp(sink[k] - mx[k]) for k in pairs}
        p_prev = {(h, j): jnp.concatenate([pp[(h, j, i)] for i in range(A_GROUP)], axis=1) for h, j in gsel}
        p_cur = {(h, j): jnp.concatenate([pc[(h, j, i)] for i in range(A_GROUP)], axis=1) for h, j in gsel}
        o = {(h, j): jnp.dot(p_prev[(h, j)], vbd_ref[prev_of[h], j], preferred_element_type=F32)
             + jnp.dot(p_cur[(h, j)], vbd_ref[cur_of[h], j], preferred_element_type=F32) for h, j in gsel}
        den = {(h, j): jnp.dot(p_prev[(h, j)], ones_ref[...], preferred_element_type=F32)
               + jnp.dot(p_cur[(h, j)], ones_ref[...], preferred_element_type=F32) for h, j in gsel}
        for h, j in gsel:
            sink_term = jnp.zeros((W, gw), F32)
            for i in range(A_GROUP):
                sink_term = jnp.where(lane_head == i, es[(h, j, i)], sink_term)
            out_ref[h * W:(h + 1) * W, j * gw:(j + 1) * gw] = (o[(h, j)] / (den[(h, j)] + sink_term)).astype(BF16)


def _swa(z, sinks, qa, qb, ka, kb, bd, rep, ones_bd, *, B, S, col_q, col_k, col_v):
    T = B * S
    W2 = 2 * WINDOW
    NB = S // W2
    gw = A_GROUP * A_HEAD_DIM
    tab = lambda: pl.BlockSpec((W2, LANES), lambda b, n: (n, 0))
    return pl.pallas_call(
        _swa_body,
        grid=(B, NB),
        in_specs=[
            pl.BlockSpec(memory_space=pltpu.SMEM),
            pl.BlockSpec((W2, A_WIDTH), lambda b, n: (b * NB + n, col_q)),
            pl.BlockSpec((W2, A_KV_WIDTH), lambda b, n: (b * NB + n, col_k)),
            pl.BlockSpec((W2, A_KV_WIDTH), lambda b, n: (b * NB + n, col_v)),
            tab(), tab(), tab(), tab(),
            pl.BlockSpec((LANES, LANES), lambda b, n: (0, 0)),
            pl.BlockSpec((A_KV_HEADS, gw, gw), lambda b, n: (0, 0, 0)),
            pl.BlockSpec((A_GROUP * WINDOW, gw), lambda b, n: (0, 0)),
        ],
        out_specs=pl.BlockSpec((W2, A_WIDTH), lambda b, n: (b * NB + n, 0)),
        out_shape=jax.ShapeDtypeStruct((T, A_WIDTH), BF16),
        scratch_shapes=[
            pltpu.VMEM((2, A_KV_HEADS, A_GROUP * WINDOW, gw), BF16),
            pltpu.VMEM((2, A_KV_HEADS, A_GROUP * WINDOW, gw), BF16),
        ],
        compiler_params=_params("parallel", "arbitrary"),
        name="swa",
    )(sinks, z, z, z, qa, qb, ka, kb, bd, rep, ones_bd)


def _merge_body(hm_ref, ha_ref, gm_ref, ga_ref, x_ref, wm_ref, wa_ref, wo_ref, gf_ref, rh_ref, rb_ref,
                x1_ref, hf_ref, lg_ref):
    pm = jnp.dot(hm_ref[...], wm_ref[...], preferred_element_type=F32)
    pa = jnp.dot(ha_ref[...], wa_ref[...], preferred_element_type=F32)
    mixed = _sigmoid(gm_ref[...].astype(F32)) * pm + _sigmoid(ga_ref[...].astype(F32)) * pa
    x1 = x_ref[...] + jnp.dot(mixed.astype(BF16), wo_ref[...], preferred_element_type=F32)
    x1_ref[...] = x1
    ms = jnp.mean(x1 * x1, axis=-1, keepdims=True)
    hf = (x1 * lax.rsqrt(ms + EPS)) * gf_ref[...]
    _store_token_rows(hf_ref, hf)
    lg_ref[...] = jnp.dot(hf.astype(BF16), rh_ref[...], preferred_element_type=F32) + rb_ref[...]


def _merge(hm, ha, z, x2, wm, wa, wo, gf, rh, rb, *, bm, col_gm, col_ga):
    T, D = x2.shape
    const = lambda shape: pl.BlockSpec(shape, lambda i: (0,) * len(shape), pipeline_mode=pl.Buffered(1))
    return pl.pallas_call(
        _merge_body,
        grid=(T // bm,),
        in_specs=[
            pl.BlockSpec((bm, M_WIDTH), lambda i: (i, 0)),
            pl.BlockSpec((bm, A_WIDTH), lambda i: (i, 0)),
            pl.BlockSpec((bm, D), lambda i: (i, col_gm)),
            pl.BlockSpec((bm, D), lambda i: (i, col_ga)),
            pl.BlockSpec((bm, D), lambda i: (i, 0)),
            const((M_WIDTH, D)), const((A_WIDTH, D)), const((D, D)), const((1, D)),
            const((D, LANES)), const((1, LANES)),
        ],
        out_specs=[
            pl.BlockSpec((bm, D), lambda i: (i, 0)),
            pl.BlockSpec((bm * (D // (2 * LANES)), LANES), lambda i: (i, 0)),
            pl.BlockSpec((bm, LANES), lambda i: (i, 0)),
        ],
        out_shape=[
            jax.ShapeDtypeStruct((T, D), F32),
            jax.ShapeDtypeStruct((T * (D // (2 * LANES)), LANES), jnp.uint32),
            jax.ShapeDtypeStruct((T, LANES), F32),
        ],
        compiler_params=_params("parallel"),
        name="merge",
    )(hm, ha, z, z, x2, wm, wa, wo, gf, rh, rb)


def _moe_body(blk_e_ref, first_ref, nxt_ref, nused_ref, tok_ref, hf_hbm, wg_hbm, wu_hbm, wd_hbm, y_ref,
              xbuf, xs_ref, stage_g, stage_u, stage_d, wg_ref, wu_ref, wd_ref, sem, wsem, *, R, CB):
    i = pl.program_id(0)
    nused = nused_ref[0]
    sub = xs_ref.shape[1] // (2 * LANES)

    def weight_copies(e):
        return (pltpu.make_async_copy(wg_hbm.at[e], stage_g, wsem.at[0]),
                pltpu.make_async_copy(wu_hbm.at[e], stage_u, wsem.at[1]),
                pltpu.make_async_copy(wd_hbm.at[e], stage_d, wsem.at[2]))

    def cast(src, dst):
        def rows(r, carry):
            sl = pl.ds(pl.multiple_of(r * CB, CB), CB)
            dst[sl, :] = src[sl, :].astype(BF16)
            return carry
        lax.fori_loop(0, src.shape[0] // CB, rows, 0)

    def start_gather(blk, slot):
        base = blk * R

        for r in range(R):
            src = pl.multiple_of(tok_ref[base + r] * sub, sub)
            pltpu.make_async_copy(hf_hbm.at[pl.ds(src, sub)], xbuf.at[slot, pl.ds(r * sub, sub)],
                                  sem.at[slot]).start(priority=r % 2)

    def wait_gather(slot):
        pltpu.make_async_copy(xbuf.at[slot], xbuf.at[slot], sem.at[slot]).wait()

    nslots = xbuf.shape[0]
    ahead = nslots - 1
    last_blk = pl.num_programs(0) - 1
    slot = i % nslots

    @pl.when(i == 0)
    def _():
        for a in range(ahead):
            start_gather(a, a)
        for c in weight_copies(blk_e_ref[0]):
            c.start(priority=1)

    @pl.when(i < nused)
    def _():
        @pl.when(first_ref[i] == 1)
        def _():
            for c in weight_copies(blk_e_ref[i]):
                c.wait()
            cast(stage_g, wg_ref)
            cast(stage_u, wu_ref)
            cast(stage_d, wd_ref)

            @pl.when(nxt_ref[i] >= 0)
            def _():
                for c in weight_copies(nxt_ref[i]):
                    c.start(priority=1)

        wait_gather(slot)
        for s in range(sub):
            lo, hi = _load_token_rows(xbuf.at[slot], R, sub, s)
            xs_ref[:, 2 * s * LANES:(2 * s + 1) * LANES] = lo.astype(BF16)
            xs_ref[:, (2 * s + 1) * LANES:(2 * s + 2) * LANES] = hi.astype(BF16)
        start_gather(jnp.minimum(i + ahead, last_blk), (i + ahead) % nslots)
        xv = xs_ref[...]
        g = jnp.dot(xv, wg_ref[...], preferred_element_type=F32)
        u = jnp.dot(xv, wu_ref[...], preferred_element_type=F32)
        hmid = ((g * _sigmoid(g)) * u).astype(BF16)
        _store_token_rows(y_ref, jnp.dot(hmid, wd_ref[...], preferred_element_type=F32))

        @pl.when(i == nused - 1)
        def _():
            for a in range(1, nslots):
                wait_gather((i + a) % nslots)


def _moe(blk_e, first, nxt, nused, row_tok, hf, wg, wu, wd, *, R):
    E, D, F = wg.shape
    sub = D // (2 * LANES)
    n_blocks = blk_e.shape[0]
    hbm = lambda: pl.BlockSpec(memory_space=pl.ANY)
    grid_spec = pltpu.PrefetchScalarGridSpec(
        num_scalar_prefetch=5,
        grid=(n_blocks,),
        in_specs=[hbm(), hbm(), hbm(), hbm()],
        out_specs=pl.BlockSpec((R * sub, LANES), lambda i, be, fi, nx, nu, tk: (jnp.minimum(i, nu[0] - 1), 0)),
        scratch_shapes=[
            pltpu.VMEM((MOE_GATHER_SLOTS, R * sub, LANES), jnp.uint32), pltpu.VMEM((R, D), BF16),
            pltpu.VMEM((D, F), F32), pltpu.VMEM((D, F), F32), pltpu.VMEM((F, D), F32),
            pltpu.VMEM((D, F), BF16), pltpu.VMEM((D, F), BF16), pltpu.VMEM((F, D), BF16),
            pltpu.SemaphoreType.DMA((MOE_GATHER_SLOTS,)), pltpu.SemaphoreType.DMA((3,)),
        ],
    )
    return pl.pallas_call(
        functools.partial(_moe_body, R=R, CB=128),
        grid_spec=grid_spec,
        out_shape=jax.ShapeDtypeStruct((n_blocks * R * sub, LANES), jnp.uint32),
        compiler_params=_params("arbitrary"),
        name="moe",
    )(blk_e, first, nxt, nused, row_tok, hf, wg, wu, wd)


def _combine_body(dest_ref, x1_ref, w_ref, yr_hbm, out_ref, ybuf, sem, *, R):
    i = pl.program_id(0)
    nsteps = pl.num_programs(0)

    sub = out_ref.shape[1] // (2 * LANES)

    def start_gather(step, slot):
        base = step * (R * TOP_K)

        def row(r, carry):
            dst = pl.multiple_of(r * sub, sub)
            for k in range(TOP_K):
                src = pl.multiple_of(dest_ref[base + r * TOP_K + k] * sub, sub)
                pltpu.make_async_copy(yr_hbm.at[pl.ds(src, sub)], ybuf.at[slot, k, pl.ds(dst, sub)],
                                      sem.at[slot]).start(priority=k % 2)
            return carry
        lax.fori_loop(0, R, row, 0, unroll=4)

    slot = i % 2

    @pl.when(i == 0)
    def _():
        start_gather(0, 0)

    pltpu.make_async_copy(ybuf.at[slot], ybuf.at[slot], sem.at[slot]).wait()

    @pl.when(i + 1 < nsteps)
    def _():
        start_gather(i + 1, 1 - slot)

    w = w_ref[...]
    for s in range(sub):
        halves0 = _load_token_rows(ybuf.at[slot, 0], R, sub, s)
        halves1 = _load_token_rows(ybuf.at[slot, 1], R, sub, s)
        for half in range(2):
            cols = slice((2 * s + half) * LANES, (2 * s + half + 1) * LANES)
            out_ref[:, cols] = x1_ref[:, cols] + (halves0[half] * w[:, 0:1] + halves1[half] * w[:, 1:2])


def _combine(dest, x1, gate_w, yr, *, R):
    T, D = x1.shape
    grid_spec = pltpu.PrefetchScalarGridSpec(
        num_scalar_prefetch=1,
        grid=(T // R,),
        in_specs=[
            pl.BlockSpec((R, D), lambda i, d: (i, 0)),
            pl.BlockSpec((R, TOP_K), lambda i, d: (i, 0)),
            pl.BlockSpec(memory_space=pl.ANY),
        ],
        out_specs=pl.BlockSpec((R, D), lambda i, d: (i, 0)),
        scratch_shapes=[pltpu.VMEM((2, TOP_K, R * (D // (2 * LANES)), LANES), jnp.uint32),
                        pltpu.SemaphoreType.DMA((2,))],
    )
    return pl.pallas_call(
        functools.partial(_combine_body, R=R),
        grid_spec=grid_spec,
        out_shape=jax.ShapeDtypeStruct((T, D), F32),
        compiler_params=_params("arbitrary"),
        name="combine",
    )(dest, x1, gate_w, yr)


def _route_body(lg_ref, tri_ref, ids_ref, gate_ref, cnt_ref):
    @pl.when(pl.program_id(0) == 0)
    def _():
        cnt_ref[...] = jnp.zeros_like(cnt_ref)

    lg = lg_ref[...]
    shape = lg.shape
    lane = lax.broadcasted_iota(jnp.int32, shape, 1)
    big = jnp.int32(LANES)

    def softmax_masked(mask):
        v = jnp.where(mask, lg, -jnp.inf)
        u = jnp.exp(v - jnp.max(v, axis=1, keepdims=True))
        return jnp.where(mask, u / jnp.sum(u, axis=1, keepdims=True), -1.0)

    def top1(p):
        best = jnp.max(p, axis=1, keepdims=True)
        idx = jnp.min(jnp.where(p == best, lane, big), axis=1, keepdims=True)
        return best, idx

    g_p, g_lane = top1(softmax_masked(lane < N_GROUPS))
    grp_of_lane = lax.shift_right_arithmetic(lane - N_GROUPS, jnp.int32(3))
    in_grp = jnp.logical_and(lane >= N_GROUPS, grp_of_lane == g_lane)
    in_grp = jnp.logical_and(in_grp, lane < N_GROUPS + N_EXPERTS)
    pe = softmax_masked(in_grp)
    p1, l1 = top1(pe)
    p2, l2 = top1(jnp.where(lane == l1, -1.0, pe))
    tot = p1 + p2
    gate1 = g_p * (p1 / tot)
    gate2 = g_p * (p2 / tot)

    hot1 = lane == l1
    hot2 = lane == l2
    hot = jnp.logical_or(hot1, hot2)
    before = jnp.dot(tri_ref[...], jnp.where(hot, 1.0, 0.0).astype(BF16), preferred_element_type=F32) + cnt_ref[...]
    rank1 = jnp.sum(jnp.where(hot1, before, 0.0), axis=1, keepdims=True)
    rank2 = jnp.sum(jnp.where(hot2, before, 0.0), axis=1, keepdims=True)
    cnt_ref[...] = cnt_ref[...] + jnp.sum(jnp.where(hot, 1.0, 0.0), axis=0, keepdims=True)

    ids = jnp.where(lane == 0, l1 - N_GROUPS, jnp.where(lane == 1, l2 - N_GROUPS, 0))
    ids = jnp.where(lane == 2, rank1.astype(jnp.int32), jnp.where(lane == 3, rank2.astype(jnp.int32), ids))
    ids_ref[...] = ids
    gate_ref[...] = jnp.where(lane == 0, gate1, jnp.where(lane == 1, gate2, 0.0))


def _route_tokens(logits, *, tb):
    T = logits.shape[0]
    r = jnp.arange(tb)
    tri = (r[None, :] < r[:, None]).astype(BF16)
    return pl.pallas_call(
        _route_body,
        grid=(T // tb,),
        in_specs=[
            pl.BlockSpec((tb, LANES), lambda i: (i, 0)),
            pl.BlockSpec((tb, tb), lambda i: (0, 0)),
        ],
        out_specs=[
            pl.BlockSpec((tb, LANES), lambda i: (i, 0)),
            pl.BlockSpec((tb, LANES), lambda i: (i, 0)),
            pl.BlockSpec((1, LANES), lambda i: (0, 0)),
        ],
        out_shape=[
            jax.ShapeDtypeStruct((T, LANES), jnp.int32),
            jax.ShapeDtypeStruct((T, LANES), F32),
            jax.ShapeDtypeStruct((1, LANES), F32),
        ],
        compiler_params=_params("arbitrary"),
        name="route",
    )(logits, tri)


def _route(logits, R):
    T = logits.shape[0]
    ids, gates, cnt = _route_tokens(logits, tb=512 if T % 512 == 0 else T)
    gate = gates[:, 0:TOP_K]
    M = T * TOP_K
    eid_f = ids[:, 0:TOP_K].reshape(M)
    rank = ids[:, TOP_K:2 * TOP_K].reshape(M)
    counts = cnt[0, N_GROUPS:N_GROUPS + N_EXPERTS].astype(jnp.int32)
    padded = (counts + R - 1) // R * R
    pend = jnp.cumsum(padded)
    pstart = pend - padded
    dest = (pstart[eid_f] + rank).astype(jnp.int32)
    n_blocks = -(-M // R) + N_EXPERTS
    tok_f = jnp.arange(M, dtype=jnp.int32) // TOP_K
    row_tok = jnp.zeros((n_blocks * R,), jnp.int32).at[dest].set(
        tok_f, unique_indices=True, mode="promise_in_bounds")
    blk_start = jnp.arange(n_blocks, dtype=jnp.int32) * R
    blk_e = jnp.sum((pend[None, :] <= blk_start[:, None]).astype(jnp.int32), axis=1)
    nused = (pend[-1] // R).astype(jnp.int32)
    last_e = blk_e[jnp.maximum(nused - 1, 0)]
    blk_e = jnp.where(jnp.arange(n_blocks) < nused, blk_e, last_e)
    blk_e = jnp.minimum(blk_e, N_EXPERTS - 1).astype(jnp.int32)
    e_idx = jnp.arange(N_EXPERTS, dtype=jnp.int32)
    cand = jnp.where(counts > 0, e_idx, N_EXPERTS)
    sfx = lax.cummin(cand[::-1])[::-1]
    nxt_of_e = jnp.concatenate([sfx[1:], jnp.full((1,), N_EXPERTS, jnp.int32)])
    nxt_of_e = jnp.where(nxt_of_e < N_EXPERTS, nxt_of_e, -1)
    nxt = nxt_of_e[blk_e].astype(jnp.int32)
    changed = jnp.concatenate([jnp.ones((1,), bool), blk_e[1:] != blk_e[:-1]])
    first = jnp.logical_and(changed, jnp.arange(n_blocks) < nused).astype(jnp.int32)
    return gate, dest, row_tok, blk_e, first, nxt, nused.reshape(1)


def _rope_tables(S, gain, scale):
    half = A_HEAD_DIM // 2
    freqs = ROPE_THETA ** (-jnp.arange(half, dtype=F32) / half)
    ang = jnp.arange(S, dtype=F32)[:, None] * freqs[None, :]
    cos = jnp.cos(ang)
    sin = jnp.sin(ang)
    g1, g2 = gain[:half], gain[half:]
    a_head = jnp.concatenate([cos * g1, cos * g2], axis=1)
    b_head = jnp.concatenate([-sin * g2, sin * g1], axis=1)
    reps = LANES // A_HEAD_DIM
    return jnp.tile(a_head, (1, reps)) * scale, jnp.tile(b_head, (1, reps)) * scale


def kernel(x, g_mix, w_in, conv_qk, b_igate, b_fgate, g_mlstm, g_q, g_k, sinks, w_proj_m, w_proj_a, w_out,
           g_ffn, w_group, b_group, w_expert, b_expert, w_gate, w_up, w_down):
    B, S, D = x.shape
    T = B * S
    depth = g_mix.shape[0]
    xf = x.reshape(T, D)

    sizes = (M_WIDTH, M_WIDTH, M_WIDTH, M_WIDTH, M_HEADS, M_HEADS, A_WIDTH, A_KV_WIDTH, A_KV_WIDTH, D, D)
    offs = [0]
    for s_ in sizes:
        offs.append(offs[-1] + s_)
    seg = lambda w, idx: w[:, offs[idx]:offs[idx + 1]]
    order = (9, 10, 0, 1, 2, 3, 6, 7, 8)
    new_off = {}
    acc = 0
    for idx in order:
        new_off[idx] = acc
        acc += sizes[idx]
    dk = M_HEAD_DIM

    lane128 = jnp.arange(LANES)
    bd = (lane128[:, None] // A_HEAD_DIM == lane128[None, :] // A_HEAD_DIM).astype(BF16)
    gw = A_GROUP * A_HEAD_DIM
    r_idx = jnp.arange(gw)
    rep = jnp.stack([(r_idx[:, None] == (j * A_HEAD_DIM + r_idx[None, :] % A_HEAD_DIM)).astype(BF16)
                     for j in range(A_KV_HEADS)])
    ones_bd = (jnp.arange(A_GROUP * WINDOW)[:, None] // WINDOW == r_idx[None, :] // A_HEAD_DIM).astype(BF16)

    for l in range(depth):
        w16 = w_in[l].astype(BF16)
        w_rep = jnp.concatenate([seg(w16, idx) for idx in order], axis=1)
        w_gates = jnp.concatenate([seg(w16, 4), seg(w16, 5)], axis=1)
        w_gates = jnp.pad(w_gates, ((0, 0), (0, LANES - 2 * M_HEADS)))
        z, zg = _in_proj(xf, g_mix[l][None, :], w_rep, w_gates, bm=1024 if T % 1024 == 0 else T,
                         bn=w_rep.shape[1] // 4)

        gate_bias = jnp.pad(jnp.concatenate([b_igate[l], b_fgate[l]]), (0, LANES - 2 * M_HEADS))[None, :]
        hm = _mlstm(z, zg, conv_qk[l], gate_bias, g_mlstm[l][:, None, :], B=B, S=S,
                    col_q=new_off[0] // dk, col_k=new_off[1] // dk, col_v=new_off[2] // dk, col_o=new_off[3] // dk)

        qa, qb = _rope_tables(S, g_q[l], A_HEAD_DIM ** -0.5)
        ka, kb = _rope_tables(S, g_k[l], 1.0)
        ha = _swa(z, sinks[l], qa, qb, ka, kb, bd, rep, ones_bd, B=B, S=S,
                  col_q=new_off[6] // A_WIDTH, col_k=new_off[7] // A_KV_WIDTH, col_v=new_off[8] // A_KV_WIDTH)

        w_router = jnp.pad(jnp.concatenate([w_group[l], w_expert[l]], axis=1),
                           ((0, 0), (0, LANES - N_GROUPS - N_EXPERTS)))
        r_b = jnp.pad(jnp.concatenate([b_group[l], b_expert[l]]), (0, LANES - N_GROUPS - N_EXPERTS))[None, :]
        x1, hf, logits = _merge(hm, ha, z, xf, w_proj_m[l].astype(BF16), w_proj_a[l].astype(BF16),
                                w_out[l].astype(BF16), g_ffn[l][None, :], w_router.astype(BF16), r_b,
                                bm=256, col_gm=new_off[9] // D, col_ga=new_off[10] // D)

        gate, dest, row_tok, blk_e, first, nxt, nused = _route(logits, MOE_ROWS)
        yr = _moe(blk_e, first, nxt, nused, row_tok, hf, w_gate[l], w_up[l], w_down[l], R=MOE_ROWS)
        xf = _combine(dest, x1, gate, yr, R=COMBINE_ROWS)
    return xf.reshape(B, S, D)
```

```python
import functools

import jax
import jax.numpy as jnp
from jax import lax
from jax.experimental import pallas as pl
from jax.experimental.pallas import tpu as pltpu

F32 = jnp.float32
BF16 = jnp.bfloat16
EPS = 1e-6
LANES = 128
VMEM_LIMIT = 56 * 1024 * 1024

M_HEADS = 4
M_HEAD_DIM = 256
M_WIDTH = M_HEADS * M_HEAD_DIM
CONV_WIDTH = 4
A_HEADS = 16
A_KV_HEADS = 4
A_GROUP = A_HEADS // A_KV_HEADS
A_HEAD_DIM = 64
A_WIDTH = A_HEADS * A_HEAD_DIM
A_KV_WIDTH = A_KV_HEADS * A_HEAD_DIM
WINDOW = 128
ROPE_THETA = 10000.0
N_GROUPS = 8
EXPERTS_PER_GROUP = 8
N_EXPERTS = N_GROUPS * EXPERTS_PER_GROUP
TOP_K = 2

MLSTM_CHUNK = 128
MLSTM_TIME_BLOCK = 512
MOE_ROWS = 256
MOE_GATHER_SLOTS = 4
COMBINE_ROWS = 256


def _sigmoid(v):
    return 1.0 / (1.0 + jnp.exp(-v))


def _params(*sem):
    return pltpu.CompilerParams(dimension_semantics=sem, vmem_limit_bytes=VMEM_LIMIT)


def _pack_bf16_pair(lo, hi):
    lo_b = lax.bitcast_convert_type(lo.astype(BF16).astype(F32), jnp.uint32)
    hi_b = lax.bitcast_convert_type(hi.astype(BF16).astype(F32), jnp.uint32)
    return lax.shift_right_logical(lo_b, jnp.uint32(16)) | hi_b


def _unpack_bf16_pair(word):
    lo = lax.bitcast_convert_type(lax.shift_left(word, jnp.uint32(16)), F32)
    hi = lax.bitcast_convert_type(word & jnp.uint32(0xFFFF0000), F32)
    return lo, hi


def _store_token_rows(ref, val):
    n, d = val.shape
    sub = d // (2 * LANES)
    for s in range(sub):
        c = 2 * s * LANES
        ref[pl.ds(s, n, stride=sub), :] = _pack_bf16_pair(val[:, c:c + LANES], val[:, c + LANES:c + 2 * LANES])


def _load_token_rows(ref, n, sub, s):
    return _unpack_bf16_pair(ref[pl.ds(s, n, stride=sub), :])


def _inproj_body(x_ref, g_ref, w_ref, wgate_ref, z_ref, zg_ref, hn_ref, *, sub):
    bm = x_ref.shape[0]

    @pl.when(pl.program_id(1) == 0)
    def _():
        def rows(r, carry):
            sl = pl.ds(pl.multiple_of(r * sub, sub), sub)
            xv = x_ref[sl, :]
            ms = jnp.mean(xv * xv, axis=-1, keepdims=True)
            hn_ref[sl, :] = ((xv * lax.rsqrt(ms + EPS)) * g_ref[...]).astype(BF16)
            return carry
        lax.fori_loop(0, bm // sub, rows, 0)
        zg_ref[...] = jnp.dot(hn_ref[...], wgate_ref[...], preferred_element_type=F32)

    z_ref[...] = jnp.dot(hn_ref[...], w_ref[...], preferred_element_type=F32).astype(BF16)


def _in_proj(x2, g, w, wgate, *, bm, bn):
    T, D = x2.shape
    N = w.shape[1]
    return pl.pallas_call(
        functools.partial(_inproj_body, sub=128),
        grid=(T // bm, N // bn),
        in_specs=[
            pl.BlockSpec((bm, D), lambda i, j: (i, 0)),
            pl.BlockSpec((1, D), lambda i, j: (0, 0)),
            pl.BlockSpec((D, bn), lambda i, j: (0, j)),
            pl.BlockSpec((D, LANES), lambda i, j: (0, 0)),
        ],
        out_specs=[
            pl.BlockSpec((bm, bn), lambda i, j: (i, j)),
            pl.BlockSpec((bm, LANES), lambda i, j: (i, 0)),
        ],
        out_shape=[
            jax.ShapeDtypeStruct((T, N), BF16),
            jax.ShapeDtypeStruct((T, LANES), F32),
        ],
        scratch_shapes=[pltpu.VMEM((bm, D), BF16)],
        compiler_params=_params("parallel", "arbitrary"),
        name="in_proj",
    )(x2, g, w, wgate)


def _mlstm_body(q_ref, k_ref, v_ref, o_ref, zg_ref, cq_ref, ck_ref, bias_ref, gn_ref, out_ref,
                qs_ref, ks_ref, colli_ref, colb_ref, rowli_ref, rowb_ref, cli_ref, cb_ref, rli_ref, rb_ref,
                c_ref, n_ref, m_ref, hq_ref, hk_ref, *, L, CB, HP):
    S = q_ref.shape[0]
    dk = q_ref.shape[1] // HP

    @pl.when(pl.program_id(1) == 0)
    def _():
        c_ref[...] = jnp.zeros_like(c_ref)
        n_ref[...] = jnp.zeros_like(n_ref)
        m_ref[...] = jnp.zeros_like(m_ref)
        hq_ref[...] = jnp.zeros_like(hq_ref)
        hk_ref[...] = jnp.zeros_like(hk_ref)

    G = zg_ref[...] + bias_ref[...]
    lf = jnp.minimum(G, 0.0) - jnp.log1p(jnp.exp(-jnp.abs(G)))
    pos = lax.broadcasted_iota(jnp.int32, (S, LANES), 0) % L
    bc = lf
    sh = 1
    while sh < L:
        bc = bc + jnp.where(pos >= sh, pltpu.roll(bc, sh, axis=0), 0.0)
        sh *= 2
    colli_ref[...] = G
    colb_ref[...] = bc
    for p in range(S // LANES):
        sl = slice(p * LANES, (p + 1) * LANES)
        rowli_ref[:, sl] = G[sl, :].T[0:8, :]
        rowb_ref[:, sl] = bc[sl, :].T[0:8, :]

    lane = lax.broadcasted_iota(jnp.int32, (S, LANES), 1)
    sub = lax.broadcasted_iota(jnp.int32, (8, S), 0)
    for j in range(HP):
        cli_ref[j] = jnp.sum(jnp.where(lane == j, colli_ref[...], 0.0), axis=1, keepdims=True)
        cb_ref[j] = jnp.sum(jnp.where(lane == j + HP, colb_ref[...], 0.0), axis=1, keepdims=True)
        rli_ref[j] = jnp.sum(jnp.where(sub == j, rowli_ref[...], 0.0), axis=0, keepdims=True)
        rb_ref[j] = jnp.sum(jnp.where(sub == j + HP, rowb_ref[...], 0.0), axis=0, keepdims=True)

    def conv_silu(src_ref, w_ref, dst_ref, halo_ref, scale):
        w = w_ref[...]

        def taps(xs, wj):
            y = xs[3] * wj[0:1, :]
            y = y + xs[2] * wj[1:2, :]
            y = y + xs[1] * wj[2:3, :]
            y = y + xs[0] * wj[3:4, :]
            return ((y * _sigmoid(y)) * scale).astype(BF16)

        for j in range(HP):
            cj = slice(j * dk, (j + 1) * dk)
            wj = w[:, cj]

            def chunk(r, carry, cj=cj, wj=wj):
                r0 = pl.multiple_of(r * CB, CB)
                cur = src_ref[pl.ds(r0, CB), cj].astype(F32)
                dst_ref[pl.ds(r0, CB), cj] = taps([cur] + [pltpu.roll(cur, d, axis=0) for d in (1, 2, 3)], wj)
                p0 = pl.multiple_of(jnp.maximum(r0 - 16, 0), 16)
                prev = jnp.where(r > 0, src_ref[pl.ds(p0, 16), cj], halo_ref[:, cj]).astype(F32)[8:16, :]
                both = jnp.concatenate([prev, cur[0:16, :]], axis=0)
                dst_ref[pl.ds(r0, 16), cj] = taps(
                    [both[8:24, :]] + [pltpu.roll(both, d, axis=0)[8:24, :] for d in (1, 2, 3)], wj)
                return carry
            lax.fori_loop(0, S // CB, chunk, 0)
        halo_ref[...] = src_ref[S - 16:S, :]

    conv_silu(q_ref, cq_ref, qs_ref, hq_ref, 1.0)
    conv_silu(k_ref, ck_ref, ks_ref, hk_ref, float(dk) ** -0.5)

    t_idx = lax.broadcasted_iota(jnp.int32, (L, L), 0)
    s_idx = lax.broadcasted_iota(jnp.int32, (L, L), 1)
    causal = s_idx <= t_idx

    heads = range(HP)
    cols = [slice(j * dk, (j + 1) * dk) for j in heads]
    dims_nt = (((1,), (1,)), ((), ()))

    def chunk(c, carry):
        rows = pl.ds(pl.multiple_of(c * L, L), L)
        qc = [qs_ref[rows, cols[j]] for j in heads]
        kc = [ks_ref[rows, cols[j]] for j in heads]
        vc = [v_ref[rows, cols[j]] for j in heads]
        b_col = [cb_ref[j, rows, :] for j in heads]
        li_col = [cli_ref[j, rows, :] for j in heads]
        li_row = [rli_ref[j, :, rows] for j in heads]
        b_row = [rb_ref[j, :, rows] for j in heads]
        m = [m_ref[j] for j in heads]
        a = [b_col[j] + m[j] for j in heads]
        D = [jnp.where(causal, b_col[j] - b_row[j] + li_row[j], -jnp.inf) for j in heads]
        m_t = [jnp.maximum(a[j], jnp.max(D[j], axis=1, keepdims=True)) for j in heads]
        w_inter = [jnp.exp(a[j] - m_t[j]) for j in heads]
        s_qk = [lax.dot_general(qc[j], kc[j], dims_nt, preferred_element_type=F32) for j in heads]
        P = [s_qk[j] * jnp.exp(D[j] - m_t[j]) for j in heads]
        inter = [jnp.dot(qc[j], c_ref[j].astype(BF16), preferred_element_type=F32) for j in heads]
        intra = [jnp.dot(P[j].astype(BF16), vc[j], preferred_element_type=F32) for j in heads]
        num = [w_inter[j] * inter[j] + intra[j] for j in heads]
        qn = [w_inter[j] * jnp.sum(qc[j].astype(F32) * n_ref[j], axis=1, keepdims=True)
              + jnp.sum(P[j], axis=1, keepdims=True) for j in heads]
        den = [jnp.maximum(jnp.abs(qn[j]), jnp.exp(-m_t[j])) for j in heads]
        hh = [num[j] / den[j] for j in heads]
        ms = [jnp.mean(hh[j] * hh[j], axis=1, keepdims=True) for j in heads]
        hn = [(hh[j] * lax.rsqrt(ms[j] + EPS)) * gn_ref[j] for j in heads]
        for j in heads:
            out_ref[rows, cols[j]] = (hn[j] * _sigmoid(o_ref[rows, cols[j]].astype(F32))).astype(BF16)
        bL = [b_row[j][:, L - 1:L] for j in heads]
        g_col = [bL[j] - b_col[j] + li_col[j] for j in heads]
        m_new = [jnp.maximum(bL[j] + m[j], jnp.max(g_col[j], axis=0, keepdims=True)) for j in heads]
        decay = [jnp.exp(bL[j] + m[j] - m_new[j]) for j in heads]
        kw = [kc[j].astype(F32) * jnp.exp(g_col[j] - m_new[j]) for j in heads]
        upd = [jnp.dot(kw[j].T.astype(BF16), vc[j], preferred_element_type=F32) for j in heads]
        for j in heads:
            c_ref[j] = decay[j] * c_ref[j] + upd[j]
            n_ref[j] = decay[j] * n_ref[j] + jnp.sum(kw[j], axis=0, keepdims=True)
            m_ref[j] = m_new[j]
        return carry
    lax.fori_loop(0, S // L, chunk, 0)


def _mlstm(z, zg, conv_qk, gate_bias, g_mlstm3, *, B, S, col_q, col_k, col_v, col_o):
    T = B * S
    dk = M_HEAD_DIM
    L = MLSTM_CHUNK
    HP = M_HEADS
    wd = HP * dk
    TS = min(MLSTM_TIME_BLOCK, S)
    NT = S // TS
    zspec = lambda col: pl.BlockSpec((TS, wd), lambda b, t: (b * NT + t, col // HP))
    return pl.pallas_call(
        functools.partial(_mlstm_body, L=L, CB=min(128, TS), HP=HP),
        grid=(B, NT),
        in_specs=[
            zspec(col_q), zspec(col_k), zspec(col_v), zspec(col_o),
            pl.BlockSpec((TS, LANES), lambda b, t: (b * NT + t, 0)),
            pl.BlockSpec((CONV_WIDTH, wd), lambda b, t: (0, 0)),
            pl.BlockSpec((CONV_WIDTH, wd), lambda b, t: (0, 1)),
            pl.BlockSpec((1, LANES), lambda b, t: (0, 0)),
            pl.BlockSpec((HP, 1, dk), lambda b, t: (0, 0, 0)),
        ],
        out_specs=pl.BlockSpec((TS, wd), lambda b, t: (b * NT + t, 0)),
        out_shape=jax.ShapeDtypeStruct((T, M_WIDTH), BF16),
        scratch_shapes=[
            pltpu.VMEM((TS, wd), BF16), pltpu.VMEM((TS, wd), BF16),
            pltpu.VMEM((TS, LANES), F32), pltpu.VMEM((TS, LANES), F32),
            pltpu.VMEM((8, TS), F32), pltpu.VMEM((8, TS), F32),
            pltpu.VMEM((HP, TS, 1), F32), pltpu.VMEM((HP, TS, 1), F32),
            pltpu.VMEM((HP, 1, TS), F32), pltpu.VMEM((HP, 1, TS), F32),
            pltpu.VMEM((HP, dk, dk), F32), pltpu.VMEM((HP, 1, dk), F32), pltpu.VMEM((HP, 1, 1), F32),
            pltpu.VMEM((16, wd), BF16), pltpu.VMEM((16, wd), BF16),
        ],
        compiler_params=_params("parallel", "arbitrary"),
        name="mlstm",
    )(z, z, z, z, zg, conv_qk, conv_qk, gate_bias, g_mlstm3)


def _swa_body(sink_ref, q_ref, k_ref, v_ref, qa_ref, qb_ref, ka_ref, kb_ref, bd_ref, rep_ref, ones_ref, out_ref,
              kbd_ref, vbd_ref):
    n = pl.program_id(1)
    W = WINDOW
    hd = A_HEAD_DIM
    gw = A_GROUP * hd

    def norm_rope(x, ta, tb):
        x2 = x * x
        x2h = x2.astype(BF16)
        x2l = (x2 - x2h.astype(F32)).astype(BF16)
        ss = (jnp.dot(x2h, bd_ref[...], preferred_element_type=F32)
              + jnp.dot(x2l, bd_ref[...], preferred_element_type=F32))
        r = lax.rsqrt(ss * (1.0 / hd) + EPS)
        ln = lax.broadcasted_iota(jnp.int32, x.shape, 1) % hd
        swapped = jnp.where(ln < hd // 2, pltpu.roll(x, LANES - hd // 2, axis=1), pltpu.roll(x, hd // 2, axis=1))
        return r * (x * ta + swapped * tb)

    lane_head = lax.broadcasted_iota(jnp.int32, (W, gw), 1) // hd
    t_idx = lax.broadcasted_iota(jnp.int32, (W, W), 0)
    k_idx = lax.broadcasted_iota(jnp.int32, (W, W), 1)
    mask_cur = k_idx <= t_idx
    neg = -jnp.inf

    @pl.when(n == 0)
    def _():
        kbd_ref[1] = jnp.zeros(kbd_ref.shape[1:], BF16)
        vbd_ref[1] = jnp.zeros(vbd_ref.shape[1:], BF16)

    def prepare(half, cur_slot):
        rows = slice(half * W, (half + 1) * W)
        qa = qa_ref[rows, :]
        qb = qb_ref[rows, :]
        qp = jnp.concatenate(
            [norm_rope(q_ref[rows, t * LANES:(t + 1) * LANES].astype(F32), qa, qb) for t in range(A_WIDTH // LANES)],
            axis=1).astype(BF16)
        ka = ka_ref[rows, :]
        kb = kb_ref[rows, :]
        kp = jnp.concatenate(
            [norm_rope(k_ref[rows, t * LANES:(t + 1) * LANES].astype(F32), ka, kb)
             for t in range(A_KV_WIDTH // LANES)], axis=1).astype(BF16)
        vv = v_ref[rows, :]
        for j in range(A_KV_HEADS):
            krep = jnp.dot(kp, rep_ref[j], preferred_element_type=F32).astype(BF16)
            vrep = jnp.dot(vv, rep_ref[j], preferred_element_type=F32).astype(BF16)
            for i in range(A_GROUP):
                kbd_ref[cur_slot, j, i * W:(i + 1) * W, :] = jnp.where(lane_head == i, krep, jnp.zeros_like(krep))
                vbd_ref[cur_slot, j, i * W:(i + 1) * W, :] = jnp.where(lane_head == i, vrep, jnp.zeros_like(vrep))
        return qp

    upper = k_idx > t_idx
    blocks = ((0, 1, 0, jnp.logical_and(upper, n > 0)), (1, 0, 1, upper))

    dims_nt = (((1,), (1,)), ((), ()))
    for halves in ((blocks[0],), (blocks[1],)):
        qps = {h: prepare(h, cur) for h, _, cur, _ in halves}
        groups = tuple(range(A_KV_HEADS))
        gsel = [(h, j) for h, _, _, _ in halves for j in groups]
        pairs = [(h, j, i) for h, j in gsel for i in range(A_GROUP)]
        prev_of = {h: p for h, p, _, _ in halves}
        cur_of = {h: c for h, _, c, _ in halves}
        mprev_of = {h: m for h, _, _, m in halves}
        qg = {(h, j): qps[h][:, j * gw:(j + 1) * gw] for h, j in gsel}
        s_prev = {(h, j): lax.dot_general(qg[(h, j)], kbd_ref[prev_of[h], j], dims_nt, preferred_element_type=F32)
                  for h, j in gsel}
        s_cur = {(h, j): lax.dot_general(qg[(h, j)], kbd_ref[cur_of[h], j], dims_nt, preferred_element_type=F32)
                 for h, j in gsel}
        sp = {(h, j, i): jnp.where(mprev_of[h], s_prev[(h, j)][:, i * W:(i + 1) * W], neg) for h, j, i in pairs}
        sc = {(h, j, i): jnp.where(mask_cur, s_cur[(h, j)][:, i * W:(i + 1) * W], neg) for h, j, i in pairs}
        sink = {(h, j, i): sink_ref[j * A_GROUP + i] for h, j, i in pairs}
        mx = {k: jnp.maximum(jnp.max(jnp.maximum(sp[k], sc[k]), axis=1, keepdims=True), sink[k]) for k in pairs}
        pp = {k: jnp.exp(sp[k] - mx[k]).astype(BF16) for k in pairs}
        pc = {k: jnp.exp(sc[k] - mx[k]).astype(BF16) for k in pairs}
        es = {k: jnp.exp(sink[k] - mx[k]) for k in pairs}
        p_prev = {(h, j): jnp.concatenate([pp[(h, j, i)] for i in range(A_GROUP)], axis=1) for h, j in gsel}
        p_cur = {(h, j): jnp.concatenate([pc[(h, j, i)] for i in range(A_GROUP)], axis=1) for h, j in gsel}
        o = {(h, j): jnp.dot(p_prev[(h, j)], vbd_ref[prev_of[h], j], preferred_element_type=F32)
             + jnp.dot(p_cur[(h, j)], vbd_ref[cur_of[h], j], preferred_element_type=F32) for h, j in gsel}
        den = {(h, j): jnp.dot(p_prev[(h, j)], ones_ref[...], preferred_element_type=F32)
               + jnp.dot(p_cur[(h, j)], ones_ref[...], preferred_element_type=F32) for h, j in gsel}
        for h, j in gsel:
            sink_term = jnp.zeros((W, gw), F32)
            for i in range(A_GROUP):
                sink_term = jnp.where(lane_head == i, es[(h, j, i)], sink_term)
            out_ref[h * W:(h + 1) * W, j * gw:(j + 1) * gw] = (o[(h, j)] / (den[(h, j)] + sink_term)).astype(BF16)


def _swa(z, sinks, qa, qb, ka, kb, bd, rep, ones_bd, *, B, S, col_q, col_k, col_v):
    T = B * S
    W2 = 2 * WINDOW
    NB = S // W2
    gw = A_GROUP * A_HEAD_DIM
    tab = lambda: pl.BlockSpec((W2, LANES), lambda b, n: (n, 0))
    return pl.pallas_call(
        _swa_body,
        grid=(B, NB),
        in_specs=[
            pl.BlockSpec(memory_space=pltpu.SMEM),
            pl.BlockSpec((W2, A_WIDTH), lambda b, n: (b * NB + n, col_q)),
            pl.BlockSpec((W2, A_KV_WIDTH), lambda b, n: (b * NB + n, col_k)),
            pl.BlockSpec((W2, A_KV_WIDTH), lambda b, n: (b * NB + n, col_v)),
            tab(), tab(), tab(), tab(),
            pl.BlockSpec((LANES, LANES), lambda b, n: (0, 0)),
            pl.BlockSpec((A_KV_HEADS, gw, gw), lambda b, n: (0, 0, 0)),
            pl.BlockSpec((A_GROUP * WINDOW, gw), lambda b, n: (0, 0)),
        ],
        out_specs=pl.BlockSpec((W2, A_WIDTH), lambda b, n: (b * NB + n, 0)),
        out_shape=jax.ShapeDtypeStruct((T, A_WIDTH), BF16),
        scratch_shapes=[
            pltpu.VMEM((2, A_KV_HEADS, A_GROUP * WINDOW, gw), BF16),
            pltpu.VMEM((2, A_KV_HEADS, A_GROUP * WINDOW, gw), BF16),
        ],
        compiler_params=_params("parallel", "arbitrary"),
        name="swa",
    )(sinks, z, z, z, qa, qb, ka, kb, bd, rep, ones_bd)


def _merge_body(hm_ref, ha_ref, gm_ref, ga_ref, x_ref, wm_ref, wa_ref, wo_ref, gf_ref, rh_ref, rb_ref,
                x1_ref, hf_ref, lg_ref):
    pm = jnp.dot(hm_ref[...], wm_ref[...], preferred_element_type=F32)
    pa = jnp.dot(ha_ref[...], wa_ref[...], preferred_element_type=F32)
    mixed = _sigmoid(gm_ref[...].astype(F32)) * pm + _sigmoid(ga_ref[...].astype(F32)) * pa
    x1 = x_ref[...] + jnp.dot(mixed.astype(BF16), wo_ref[...], preferred_element_type=F32)
    x1_ref[...] = x1
    ms = jnp.mean(x1 * x1, axis=-1, keepdims=True)
    hf = (x1 * lax.rsqrt(ms + EPS)) * gf_ref[...]
    _store_token_rows(hf_ref, hf)
    lg_ref[...] = jnp.dot(hf.astype(BF16), rh_ref[...], preferred_element_type=F32) + rb_ref[...]


def _merge(hm, ha, z, x2, wm, wa, wo, gf, rh, rb, *, bm, col_gm, col_ga):
    T, D = x2.shape
    const = lambda shape: pl.BlockSpec(shape, lambda i: (0,) * len(shape), pipeline_mode=pl.Buffered(1))
    return pl.pallas_call(
        _merge_body,
        grid=(T // bm,),
        in_specs=[
            pl.BlockSpec((bm, M_WIDTH), lambda i: (i, 0)),
            pl.BlockSpec((bm, A_WIDTH), lambda i: (i, 0)),
            pl.BlockSpec((bm, D), lambda i: (i, col_gm)),
            pl.BlockSpec((bm, D), lambda i: (i, col_ga)),
            pl.BlockSpec((bm, D), lambda i: (i, 0)),
            const((M_WIDTH, D)), const((A_WIDTH, D)), const((D, D)), const((1, D)),
            const((D, LANES)), const((1, LANES)),
        ],
        out_specs=[
            pl.BlockSpec((bm, D), lambda i: (i, 0)),
            pl.BlockSpec((bm * (D // (2 * LANES)), LANES), lambda i: (i, 0)),
            pl.BlockSpec((bm, LANES), lambda i: (i, 0)),
        ],
        out_shape=[
            jax.ShapeDtypeStruct((T, D), F32),
            jax.ShapeDtypeStruct((T * (D // (2 * LANES)), LANES), jnp.uint32),
            jax.ShapeDtypeStruct((T, LANES), F32),
        ],
        compiler_params=_params("parallel"),
        name="merge",
    )(hm, ha, z, z, x2, wm, wa, wo, gf, rh, rb)


def _moe_body(blk_e_ref, first_ref, nxt_ref, nused_ref, tok_ref, hf_hbm, wg_hbm, wu_hbm, wd_hbm, y_ref,
              xbuf, xs_ref, stage_g, stage_u, stage_d, wg_ref, wu_ref, wd_ref, sem, wsem, *, R, CB):
    i = pl.program_id(0)
    nused = nused_ref[0]
    sub = xs_ref.shape[1] // (2 * LANES)

    def weight_copies(e):
        return (pltpu.make_async_copy(wg_hbm.at[e], stage_g, wsem.at[0]),
                pltpu.make_async_copy(wu_hbm.at[e], stage_u, wsem.at[1]),
                pltpu.make_async_copy(wd_hbm.at[e], stage_d, wsem.at[2]))

    def cast(src, dst):
        def rows(r, carry):
            sl = pl.ds(pl.multiple_of(r * CB, CB), CB)
            dst[sl, :] = src[sl, :].astype(BF16)
            return carry
        lax.fori_loop(0, src.shape[0] // CB, rows, 0)

    def start_gather(blk, slot):
        base = blk * R

        for r in range(R):
            src = pl.multiple_of(tok_ref[base + r] * sub, sub)
            pltpu.make_async_copy(hf_hbm.at[pl.ds(src, sub)], xbuf.at[slot, pl.ds(r * sub, sub)],
                                  sem.at[slot]).start(priority=r % 2)

    def wait_gather(slot):
        pltpu.make_async_copy(xbuf.at[slot], xbuf.at[slot], sem.at[slot]).wait()

    nslots = xbuf.shape[0]
    ahead = nslots - 1
    last_blk = pl.num_programs(0) - 1
    slot = i % nslots

    @pl.when(i == 0)
    def _():
        for a in range(ahead):
            start_gather(a, a)
        for c in weight_copies(blk_e_ref[0]):
            c.start(priority=1)

    @pl.when(i < nused)
    def _():
        @pl.when(first_ref[i] == 1)
        def _():
            for c in weight_copies(blk_e_ref[i]):
                c.wait()
            cast(stage_g, wg_ref)
            cast(stage_u, wu_ref)
            cast(stage_d, wd_ref)

            @pl.when(nxt_ref[i] >= 0)
            def _():
                for c in weight_copies(nxt_ref[i]):
                    c.start(priority=1)

        wait_gather(slot)
        for s in range(sub):
            lo, hi = _load_token_rows(xbuf.at[slot], R, sub, s)
            xs_ref[:, 2 * s * LANES:(2 * s + 1) * LANES] = lo.astype(BF16)
            xs_ref[:, (2 * s + 1) * LANES:(2 * s + 2) * LANES] = hi.astype(BF16)
        xv = xs_ref[...]
        g = jnp.dot(xv, wg_ref[...], preferred_element_type=F32)
        u = jnp.dot(xv, wu_ref[...], preferred_element_type=F32)
        hmid = ((g * _sigmoid(g)) * u).astype(BF16)
        _store_token_rows(y_ref, jnp.dot(hmid, wd_ref[...], preferred_element_type=F32))
        start_gather(jnp.minimum(i + ahead, last_blk), (i + ahead) % nslots)

        @pl.when(i == nused - 1)
        def _():
            for a in range(1, nslots):
                wait_gather((i + a) % nslots)


def _moe(blk_e, first, nxt, nused, row_tok, hf, wg, wu, wd, *, R):
    E, D, F = wg.shape
    sub = D // (2 * LANES)
    n_blocks = blk_e.shape[0]
    hbm = lambda: pl.BlockSpec(memory_space=pl.ANY)
    grid_spec = pltpu.PrefetchScalarGridSpec(
        num_scalar_prefetch=5,
        grid=(n_blocks,),
        in_specs=[hbm(), hbm(), hbm(), hbm()],
        out_specs=pl.BlockSpec((R * sub, LANES), lambda i, be, fi, nx, nu, tk: (jnp.minimum(i, nu[0] - 1), 0)),
        scratch_shapes=[
            pltpu.VMEM((MOE_GATHER_SLOTS, R * sub, LANES), jnp.uint32), pltpu.VMEM((R, D), BF16),
            pltpu.VMEM((D, F), F32), pltpu.VMEM((D, F), F32), pltpu.VMEM((F, D), F32),
            pltpu.VMEM((D, F), BF16), pltpu.VMEM((D, F), BF16), pltpu.VMEM((F, D), BF16),
            pltpu.SemaphoreType.DMA((MOE_GATHER_SLOTS,)), pltpu.SemaphoreType.DMA((3,)),
        ],
    )
    return pl.pallas_call(
        functools.partial(_moe_body, R=R, CB=128),
        grid_spec=grid_spec,
        out_shape=jax.ShapeDtypeStruct((n_blocks * R * sub, LANES), jnp.uint32),
        compiler_params=_params("arbitrary"),
        name="moe",
    )(blk_e, first, nxt, nused, row_tok, hf, wg, wu, wd)


def _combine_body(dest_ref, x1_ref, w_ref, yr_hbm, out_ref, ybuf, sem, *, R):
    i = pl.program_id(0)
    nsteps = pl.num_programs(0)

    sub = out_ref.shape[1] // (2 * LANES)

    def start_gather(step, slot):
        base = step * (R * TOP_K)

        def row(r, carry):
            dst = pl.multiple_of(r * sub, sub)
            for k in range(TOP_K):
                src = pl.multiple_of(dest_ref[base + r * TOP_K + k] * sub, sub)
                pltpu.make_async_copy(yr_hbm.at[pl.ds(src, sub)], ybuf.at[slot, k, pl.ds(dst, sub)],
                                      sem.at[slot]).start(priority=k % 2)
            return carry
        lax.fori_loop(0, R, row, 0, unroll=4)

    slot = i % 2

    @pl.when(i == 0)
    def _():
        start_gather(0, 0)

    pltpu.make_async_copy(ybuf.at[slot], ybuf.at[slot], sem.at[slot]).wait()

    @pl.when(i + 1 < nsteps)
    def _():
        start_gather(i + 1, 1 - slot)

    w = w_ref[...]
    for s in range(sub):
        halves0 = _load_token_rows(ybuf.at[slot, 0], R, sub, s)
        halves1 = _load_token_rows(ybuf.at[slot, 1], R, sub, s)
        for half in range(2):
            cols = slice((2 * s + half) * LANES, (2 * s + half + 1) * LANES)
            out_ref[:, cols] = x1_ref[:, cols] + (halves0[half] * w[:, 0:1] + halves1[half] * w[:, 1:2])


def _combine(dest, x1, gate_w, yr, *, R):
    T, D = x1.shape
    grid_spec = pltpu.PrefetchScalarGridSpec(
        num_scalar_prefetch=1,
        grid=(T // R,),
        in_specs=[
            pl.BlockSpec((R, D), lambda i, d: (i, 0)),
            pl.BlockSpec((R, TOP_K), lambda i, d: (i, 0)),
            pl.BlockSpec(memory_space=pl.ANY),
        ],
        out_specs=pl.BlockSpec((R, D), lambda i, d: (i, 0)),
        scratch_shapes=[pltpu.VMEM((2, TOP_K, R * (D // (2 * LANES)), LANES), jnp.uint32),
                        pltpu.SemaphoreType.DMA((2,))],
    )
    return pl.pallas_call(
        functools.partial(_combine_body, R=R),
        grid_spec=grid_spec,
        out_shape=jax.ShapeDtypeStruct((T, D), F32),
        compiler_params=_params("arbitrary"),
        name="combine",
    )(dest, x1, gate_w, yr)


def _route_body(lg_ref, tri_ref, ids_ref, gate_ref, cnt_ref):
    @pl.when(pl.program_id(0) == 0)
    def _():
        cnt_ref[...] = jnp.zeros_like(cnt_ref)

    lg = lg_ref[...]
    shape = lg.shape
    lane = lax.broadcasted_iota(jnp.int32, shape, 1)
    big = jnp.int32(LANES)

    def softmax_masked(mask):
        v = jnp.where(mask, lg, -jnp.inf)
        u = jnp.exp(v - jnp.max(v, axis=1, keepdims=True))
        return jnp.where(mask, u / jnp.sum(u, axis=1, keepdims=True), -1.0)

    def top1(p):
        best = jnp.max(p, axis=1, keepdims=True)
        idx = jnp.min(jnp.where(p == best, lane, big), axis=1, keepdims=True)
        return best, idx

    g_p, g_lane = top1(softmax_masked(lane < N_GROUPS))
    grp_of_lane = lax.shift_right_arithmetic(lane - N_GROUPS, jnp.int32(3))
    in_grp = jnp.logical_and(lane >= N_GROUPS, grp_of_lane == g_lane)
    in_grp = jnp.logical_and(in_grp, lane < N_GROUPS + N_EXPERTS)
    pe = softmax_masked(in_grp)
    p1, l1 = top1(pe)
    p2, l2 = top1(jnp.where(lane == l1, -1.0, pe))
    tot = p1 + p2
    gate1 = g_p * (p1 / tot)
    gate2 = g_p * (p2 / tot)

    hot1 = lane == l1
    hot2 = lane == l2
    hot = jnp.logical_or(hot1, hot2)
    before = jnp.dot(tri_ref[...], jnp.where(hot, 1.0, 0.0).astype(BF16), preferred_element_type=F32) + cnt_ref[...]
    rank1 = jnp.sum(jnp.where(hot1, before, 0.0), axis=1, keepdims=True)
    rank2 = jnp.sum(jnp.where(hot2, before, 0.0), axis=1, keepdims=True)
    cnt_ref[...] = cnt_ref[...] + jnp.sum(jnp.where(hot, 1.0, 0.0), axis=0, keepdims=True)

    ids = jnp.where(lane == 0, l1 - N_GROUPS, jnp.where(lane == 1, l2 - N_GROUPS, 0))
    ids = jnp.where(lane == 2, rank1.astype(jnp.int32), jnp.where(lane == 3, rank2.astype(jnp.int32), ids))
    ids_ref[...] = ids
    gate_ref[...] = jnp.where(lane == 0, gate1, jnp.where(lane == 1, gate2, 0.0))


def _route_tokens(logits, *, tb):
    T = logits.shape[0]
    r = jnp.arange(tb)
    tri = (r[None, :] < r[:, None]).astype(BF16)
    return pl.pallas_call(
        _route_body,
        grid=(T // tb,),
        in_specs=[
            pl.BlockSpec((tb, LANES), lambda i: (i, 0)),
            pl.BlockSpec((tb, tb), lambda i: (0, 0)),
        ],
        out_specs=[
            pl.BlockSpec((tb, LANES), lambda i: (i, 0)),
            pl.BlockSpec((tb, LANES), lambda i: (i, 0)),
            pl.BlockSpec((1, LANES), lambda i: (0, 0)),
        ],
        out_shape=[
            jax.ShapeDtypeStruct((T, LANES), jnp.int32),
            jax.ShapeDtypeStruct((T, LANES), F32),
            jax.ShapeDtypeStruct((1, LANES), F32),
        ],
        compiler_params=_params("arbitrary"),
        name="route",
    )(logits, tri)


def _route(logits, R):
    T = logits.shape[0]
    ids, gates, cnt = _route_tokens(logits, tb=512 if T % 512 == 0 else T)
    gate = gates[:, 0:TOP_K]
    M = T * TOP_K
    eid_f = ids[:, 0:TOP_K].reshape(M)
    rank = ids[:, TOP_K:2 * TOP_K].reshape(M)
    counts = cnt[0, N_GROUPS:N_GROUPS + N_EXPERTS].astype(jnp.int32)
    padded = (counts + R - 1) // R * R
    pend = jnp.cumsum(padded)
    pstart = pend - padded
    dest = (pstart[eid_f] + rank).astype(jnp.int32)
    n_blocks = -(-M // R) + N_EXPERTS
    tok_f = jnp.arange(M, dtype=jnp.int32) // TOP_K
    row_tok = jnp.zeros((n_blocks * R,), jnp.int32).at[dest].set(
        tok_f, unique_indices=True, mode="promise_in_bounds")
    blk_start = jnp.arange(n_blocks, dtype=jnp.int32) * R
    blk_e = jnp.sum((pend[None, :] <= blk_start[:, None]).astype(jnp.int32), axis=1)
    nused = (pend[-1] // R).astype(jnp.int32)
    last_e = blk_e[jnp.maximum(nused - 1, 0)]
    blk_e = jnp.where(jnp.arange(n_blocks) < nused, blk_e, last_e)
    blk_e = jnp.minimum(blk_e, N_EXPERTS - 1).astype(jnp.int32)
    e_idx = jnp.arange(N_EXPERTS, dtype=jnp.int32)
    cand = jnp.where(counts > 0, e_idx, N_EXPERTS)
    sfx = lax.cummin(cand[::-1])[::-1]
    nxt_of_e = jnp.concatenate([sfx[1:], jnp.full((1,), N_EXPERTS, jnp.int32)])
    nxt_of_e = jnp.where(nxt_of_e < N_EXPERTS, nxt_of_e, -1)
    nxt = nxt_of_e[blk_e].astype(jnp.int32)
    changed = jnp.concatenate([jnp.ones((1,), bool), blk_e[1:] != blk_e[:-1]])
    first = jnp.logical_and(changed, jnp.arange(n_blocks) < nused).astype(jnp.int32)
    return gate, dest, row_tok, blk_e, first, nxt, nused.reshape(1)


def _rope_tables(S, gain, scale):
    half = A_HEAD_DIM // 2
    freqs = ROPE_THETA ** (-jnp.arange(half, dtype=F32) / half)
    ang = jnp.arange(S, dtype=F32)[:, None] * freqs[None, :]
    cos = jnp.cos(ang)
    sin = jnp.sin(ang)
    g1, g2 = gain[:half], gain[half:]
    a_head = jnp.concatenate([cos * g1, cos * g2], axis=1)
    b_head = jnp.concatenate([-sin * g2, sin * g1], axis=1)
    reps = LANES // A_HEAD_DIM
    return jnp.tile(a_head, (1, reps)) * scale, jnp.tile(b_head, (1, reps)) * scale


def kernel(x, g_mix, w_in, conv_qk, b_igate, b_fgate, g_mlstm, g_q, g_k, sinks, w_proj_m, w_proj_a, w_out,
           g_ffn, w_group, b_group, w_expert, b_expert, w_gate, w_up, w_down):
    B, S, D = x.shape
    T = B * S
    depth = g_mix.shape[0]
    xf = x.reshape(T, D)

    sizes = (M_WIDTH, M_WIDTH, M_WIDTH, M_WIDTH, M_HEADS, M_HEADS, A_WIDTH, A_KV_WIDTH, A_KV_WIDTH, D, D)
    offs = [0]
    for s_ in sizes:
        offs.append(offs[-1] + s_)
    seg = lambda w, idx: w[:, offs[idx]:offs[idx + 1]]
    order = (9, 10, 0, 1, 2, 3, 6, 7, 8)
    new_off = {}
    acc = 0
    for idx in order:
        new_off[idx] = acc
        acc += sizes[idx]
    dk = M_HEAD_DIM

    lane128 = jnp.arange(LANES)
    bd = (lane128[:, None] // A_HEAD_DIM == lane128[None, :] // A_HEAD_DIM).astype(BF16)
    gw = A_GROUP * A_HEAD_DIM
    r_idx = jnp.arange(gw)
    rep = jnp.stack([(r_idx[:, None] == (j * A_HEAD_DIM + r_idx[None, :] % A_HEAD_DIM)).astype(BF16)
                     for j in range(A_KV_HEADS)])
    ones_bd = (jnp.arange(A_GROUP * WINDOW)[:, None] // WINDOW == r_idx[None, :] // A_HEAD_DIM).astype(BF16)

    for l in range(depth):
        w16 = w_in[l].astype(BF16)
        w_rep = jnp.concatenate([seg(w16, idx) for idx in order], axis=1)
        w_gates = jnp.concatenate([seg(w16, 4), seg(w16, 5)], axis=1)
        w_gates = jnp.pad(w_gates, ((0, 0), (0, LANES - 2 * M_HEADS)))
        z, zg = _in_proj(xf, g_mix[l][None, :], w_rep, w_gates, bm=1024 if T % 1024 == 0 else T,
                         bn=w_rep.shape[1] // 4)

        gate_bias = jnp.pad(jnp.concatenate([b_igate[l], b_fgate[l]]), (0, LANES - 2 * M_HEADS))[None, :]
        hm = _mlstm(z, zg, conv_qk[l], gate_bias, g_mlstm[l][:, None, :], B=B, S=S,
                    col_q=new_off[0] // dk, col_k=new_off[1] // dk, col_v=new_off[2] // dk, col_o=new_off[3] // dk)

        qa, qb = _rope_tables(S, g_q[l], A_HEAD_DIM ** -0.5)
        ka, kb = _rope_tables(S, g_k[l], 1.0)
        ha = _swa(z, sinks[l], qa, qb, ka, kb, bd, rep, ones_bd, B=B, S=S,
                  col_q=new_off[6] // A_WIDTH, col_k=new_off[7] // A_KV_WIDTH, col_v=new_off[8] // A_KV_WIDTH)

        w_router = jnp.pad(jnp.concatenate([w_group[l], w_expert[l]], axis=1),
                           ((0, 0), (0, LANES - N_GROUPS - N_EXPERTS)))
        r_b = jnp.pad(jnp.concatenate([b_group[l], b_expert[l]]), (0, LANES - N_GROUPS - N_EXPERTS))[None, :]
        x1, hf, logits = _merge(hm, ha, z, xf, w_proj_m[l].astype(BF16), w_proj_a[l].astype(BF16),
                                w_out[l].astype(BF16), g_ffn[l][None, :], w_router.astype(BF16), r_b,
                                bm=256, col_gm=new_off[9] // D, col_ga=new_off[10] // D)

        gate, dest, row_tok, blk_e, first, nxt, nused = _route(logits, MOE_ROWS)
        yr = _moe(blk_e, first, nxt, nused, row_tok, hf, w_gate[l], w_up[l], w_down[l], R=MOE_ROWS)
        xf = _combine(dest, x1, gate, yr, R=COMBINE_ROWS)
    return xf.reshape(B, S, D)
```
